```python
import jax, jax.numpy as jnp
from jax import lax
import numpy as np

D_MODEL = 1024
BATCH = 4
SEQ = 4096
DEPTH = 1

N_ATT_HEADS = 8
HEAD_DIM = 64
ATT_WIDTH = N_ATT_HEADS * HEAD_DIM
Q_BLOCK = 128
CONV_WIDTH = 512
CONV_K = 3
N_GROUPS = 4
EXPERTS_PER_GROUP = 4
N_EXPERTS = N_GROUPS * EXPERTS_PER_GROUP
TOP_K_IN_GROUP = 2
D_EXPERT = 256
PLE_DIM = 256
EPS = 1e-6
NEG_INF = -1e30

PROJ_SIZES = (ATT_WIDTH, ATT_WIDTH, ATT_WIDTH, N_ATT_HEADS,
              CONV_WIDTH, CONV_WIDTH, CONV_WIDTH, D_MODEL, D_MODEL)
PROJ_WIDTH = 3 * ATT_WIDTH + N_ATT_HEADS + 3 * CONV_WIDTH + 2 * D_MODEL

kernel_name = "hybrid_fox_shortconv_hmoe_block"


def rms_norm(x, g):
    xf = x.astype(jnp.float32)
    y = xf * lax.rsqrt(jnp.mean(xf * xf, axis=-1, keepdims=True) + EPS)
    return (y * g.astype(jnp.float32)).astype(x.dtype)


def split_offsets():
    offs, acc = [], 0
    for s in PROJ_SIZES[:-1]:
        acc += s
        offs.append(acc)
    return offs


def forgetting_attention(q, k, v, log_f):
    b, s, h, d = q.shape
    n_blocks = s // Q_BLOCK
    c = jnp.cumsum(log_f, axis=1).transpose(0, 2, 1)
    qh = q.transpose(0, 2, 1, 3)
    kh = k.transpose(0, 2, 1, 3)
    vh = v.transpose(0, 2, 1, 3)
    q_blocks = jnp.moveaxis(qh.reshape(b, h, n_blocks, Q_BLOCK, d), 2, 0)
    c_blocks = jnp.moveaxis(c.reshape(b, h, n_blocks, Q_BLOCK), 2, 0)
    key_pos = jnp.arange(s)
    scale = HEAD_DIM ** -0.5

    def one_block(args):
        qb, cqb, blk = args
        q_pos = blk * Q_BLOCK + jnp.arange(Q_BLOCK)
        logits = jnp.einsum('bhqd,bhkd->bhqk', qb, kh).astype(jnp.float32) * scale
        logits = logits + (cqb[..., :, None] - c[..., None, :])
        causal = key_pos[None, :] <= q_pos[:, None]
        logits = jnp.where(causal, logits, NEG_INF)
        probs = jax.nn.softmax(logits, axis=-1)
        return jnp.einsum('bhqk,bhkd->bhqd', probs.astype(vh.dtype), vh)

    out = lax.map(one_block, (q_blocks, c_blocks, jnp.arange(n_blocks)))
    out = out.transpose(1, 0, 3, 2, 4).reshape(b, s, h * d)
    return out


def causal_short_conv(u, w):
    s = u.shape[1]
    u_pad = jnp.pad(u, ((0, 0), (CONV_K - 1, 0), (0, 0)))
    y = w[0] * u_pad[:, 0:s]
    for j in range(1, CONV_K):
        y = y + w[j] * u_pad[:, j:j + s]
    return y


def hierarchical_moe(h, w_rg, b_rg, w_re, b_re, w_gate, w_up, w_down):
    b, s, d = h.shape
    ht = h.reshape(b * s, d)
    g_logits = (ht @ w_rg + b_rg).astype(jnp.float32)
    g_prob = jax.nn.softmax(g_logits, axis=-1)
    g_val, g_idx = lax.top_k(g_prob, 1)
    e_logits = (ht @ w_re + b_re).astype(jnp.float32)
    e_logits = e_logits.reshape(-1, N_GROUPS, EXPERTS_PER_GROUP)
    e_in = jnp.take_along_axis(e_logits, g_idx[:, :, None], axis=1)[:, 0]
    e_prob = jax.nn.softmax(e_in, axis=-1)
    e_val, e_idx = lax.top_k(e_prob, TOP_K_IN_GROUP)
    e_val = e_val / jnp.sum(e_val, axis=-1, keepdims=True)
    gate = g_val * e_val
    global_idx = g_idx * EXPERTS_PER_GROUP + e_idx
    combine = jnp.sum(jax.nn.one_hot(global_idx, N_EXPERTS, dtype=jnp.float32)
                      * gate[..., None], axis=1)
    a = jnp.einsum('td,edf->tef', ht, w_gate)
    u = jnp.einsum('td,edf->tef', ht, w_up)
    hid = jax.nn.silu(a) * u * combine[..., None].astype(ht.dtype)
    out = jnp.einsum('tef,efd->td', hid, w_down)
    return out.reshape(b, s, d)


def setup_inputs(seed: int = 0) -> dict:
    key = jax.random.key(seed)
    ks = jax.random.split(key, 24)
    L, D = DEPTH, D_MODEL
    f32 = jnp.float32

    def nrm(k, shape, scale):
        return jax.random.normal(k, shape, f32) * scale

    def gain(k, shape):
        return 1.0 + 0.05 * jax.random.normal(k, shape, f32)

    return {
        "x": nrm(ks[0], (BATCH, SEQ, D), 1.0),
        "p": nrm(ks[1], (DEPTH, BATCH, SEQ, PLE_DIM), 1.0),
        "attn_norm_g": gain(ks[2], (L, D)),
        "w_in": nrm(ks[3], (L, D, PROJ_WIDTH), D ** -0.5),
        "b_f": 1.0 + 3.0 * jax.random.uniform(ks[4], (L, N_ATT_HEADS), f32),
        "q_norm_g": gain(ks[5], (L, HEAD_DIM)),
        "k_norm_g": gain(ks[6], (L, HEAD_DIM)),
        "conv_w": nrm(ks[7], (L, CONV_K, CONV_WIDTH), CONV_K ** -0.5),
        "w_out_att": nrm(ks[8], (L, ATT_WIDTH, D), ATT_WIDTH ** -0.5),
        "w_out_conv": nrm(ks[9], (L, CONV_WIDTH, D), CONV_WIDTH ** -0.5),
        "w_o": nrm(ks[10], (L, D, D), D ** -0.5),
        "ffn_norm_g": gain(ks[11], (L, D)),
        "w_rg": nrm(ks[12], (L, D, N_GROUPS), D ** -0.5),
        "b_rg": nrm(ks[13], (L, N_GROUPS), 0.01),
        "w_re": nrm(ks[14], (L, D, N_EXPERTS), D ** -0.5),
        "b_re": nrm(ks[15], (L, N_EXPERTS), 0.01),
        "w_gate": nrm(ks[16], (L, N_EXPERTS, D, D_EXPERT), D ** -0.5),
        "w_up": nrm(ks[17], (L, N_EXPERTS, D, D_EXPERT), D ** -0.5),
        "w_down": nrm(ks[18], (L, N_EXPERTS, D_EXPERT, D), D_EXPERT ** -0.5),
        "ple_norm_g": gain(ks[19], (L, D)),
        "w_pg": nrm(ks[20], (L, D, D), D ** -0.5),
        "w_ple": nrm(ks[21], (L, PLE_DIM, D), PLE_DIM ** -0.5),
    }


def reference(x, p, attn_norm_g, w_in, b_f, q_norm_g, k_norm_g, conv_w, w_out_att,
              w_out_conv, w_o, ffn_norm_g, w_rg, b_rg, w_re, b_re, w_gate, w_up, w_down,
              ple_norm_g, w_pg, w_ple):
    b, s, _ = x.shape
    offs = split_offsets()
    for i in range(DEPTH):
        h = rms_norm(x, attn_norm_g[i])
        proj = h @ w_in[i]
        q, k, v, f_logit, cb, cc, cu, ga, gb = jnp.split(proj, offs, axis=-1)

        q = rms_norm(q.reshape(b, s, N_ATT_HEADS, HEAD_DIM), q_norm_g[i])
        k = rms_norm(k.reshape(b, s, N_ATT_HEADS, HEAD_DIM), k_norm_g[i])
        v = v.reshape(b, s, N_ATT_HEADS, HEAD_DIM)
        log_f = jax.nn.log_sigmoid(f_logit.astype(jnp.float32) + b_f[i].astype(jnp.float32))
        y_att = forgetting_attention(q, k, v, log_f)

        y_conv = cb * causal_short_conv(cc * cu, conv_w[i])

        merged = (jax.nn.sigmoid(ga) * (y_att @ w_out_att[i])
                  + jax.nn.sigmoid(gb) * (y_conv @ w_out_conv[i]))
        x = x + merged @ w_o[i]

        h2 = rms_norm(x, ffn_norm_g[i])
        x = x + hierarchical_moe(h2, w_rg[i], b_rg[i], w_re[i], b_re[i],
                                 w_gate[i], w_up[i], w_down[i])

        h3 = rms_norm(x, ple_norm_g[i])
        x = x + jax.nn.sigmoid(h3 @ w_pg[i]) * (p[i] @ w_ple[i])
    return x
```

```python
import functools

import numpy as np
import jax
import jax.numpy as jnp
from jax import lax
from jax.experimental import pallas as pl
from jax.experimental.pallas import tpu as pltpu

D_MODEL = 1024
N_HEADS = 8
HEAD_DIM = 64
ATT_WIDTH = N_HEADS * HEAD_DIM
CONV_WIDTH = 512
CONV_K = 3
N_GROUPS = 4
EXPERTS_PER_GROUP = 4
N_EXPERTS = 16
D_EXPERT = 256
PLE_DIM = 256
EPS = 1e-6
NEG_INF = -1e30

LANES = 128
C_PIECES = 3
QC_LANE = HEAD_DIM
KC_LANE = HEAD_DIM + C_PIECES
VMEM_LIMIT = 56 * 1024 * 1024

F32 = jnp.float32
BF16 = jnp.bfloat16


def _rms(xf, g):
    return xf * lax.rsqrt(jnp.mean(xf * xf, axis=-1, keepdims=True) + EPS) * g


def _log_sigmoid(z):
    return jnp.minimum(z, 0.0) - jnp.log1p(jnp.exp(-jnp.abs(z)))


def _bdot(a, b):
    return jnp.dot(a, b, preferred_element_type=F32)


def _qkv_kernel(x_ref, g_ref, w_ref, bf_ref, qg_ref, kg_ref, pq_ref, pk_ref,
                q_ref, k_ref, v_ref, carry_ref, *, tm):
    st = pl.program_id(1)

    @pl.when(st == 0)
    def _():
        carry_ref[...] = jnp.zeros_like(carry_ref)

    h = _rms(x_ref[0], g_ref[...]).astype(BF16)
    proj = _bdot(h, w_ref[...])
    v_ref[0] = proj[:, 2 * ATT_WIDTH:3 * ATT_WIDTH].astype(BF16)

    lane = lax.broadcasted_iota(jnp.int32, (tm, LANES), 1)
    row = lax.broadcasted_iota(jnp.int32, (tm, LANES), 0)

    f3 = proj[:, 3 * ATT_WIDTH:3 * ATT_WIDTH + LANES] + bf_ref[...]
    c = jnp.where(lane < C_PIECES * N_HEADS, _log_sigmoid(f3), 0.0)
    sh = 1
    while sh < tm:
        c = c + jnp.where(row >= sh, pltpu.roll(c, sh, axis=0), 0.0)
        sh *= 2
    c = c + carry_ref[...]
    carry_ref[...] = c[tm - 1:tm, :]
    hi = c.astype(BF16).astype(F32)
    r1 = c - hi
    mid = r1.astype(BF16).astype(F32)
    lo = r1 - mid
    piece = jnp.where(lane < N_HEADS, hi, jnp.where(lane < 2 * N_HEADS, mid, lo)).astype(BF16)
    qc = _bdot(piece, pq_ref[...])
    kc = _bdot(piece, pk_ref[...])

    q_ones = jnp.where((lane >= KC_LANE) & (lane < KC_LANE + C_PIECES), 1.0, 0.0)
    k_ones = jnp.where((lane >= QC_LANE) & (lane < QC_LANE + C_PIECES), 1.0, 0.0)
    low = lane < HEAD_DIM

    def heads(base, gain_ref, cmat, ones, out_ref):
        gain = gain_ref[...]
        for j in range(N_HEADS // 2):
            pair = proj[:, base + LANES * j: base + LANES * (j + 1)]
            sq = pair * pair
            ss_lo = jnp.sum(jnp.where(low, sq, 0.0), axis=1, keepdims=True)
            ss_hi = jnp.sum(jnp.where(low, 0.0, sq), axis=1, keepdims=True)
            n_lo = pair * lax.rsqrt(ss_lo * (1.0 / HEAD_DIM) + EPS) * gain
            n_hi = pltpu.roll(pair, HEAD_DIM, axis=1) * lax.rsqrt(ss_hi * (1.0 / HEAD_DIM) + EPS) * gain
            for hh, nrm in ((2 * j, n_lo), (2 * j + 1, n_hi)):
                aug = jnp.where(low, nrm, 0.0) + cmat[:, LANES * hh: LANES * (hh + 1)] + ones
                out_ref[0, hh] = aug.astype(BF16)

    heads(0, qg_ref, qc, q_ones, q_ref)
    heads(ATT_WIDTH, kg_ref, kc, k_ones, k_ref)


def _qkv_proj(x, g, w, bf3, qg, kg, pq, pk, *, tm):
    b, s, d = x.shape
    grid = (b, s // tm)
    full = lambda shape: pl.BlockSpec(shape, lambda i, j: (0,) * len(shape))
    aug_spec = pl.BlockSpec((1, N_HEADS, tm, LANES), lambda i, j: (i, 0, j, 0))
    return pl.pallas_call(
        functools.partial(_qkv_kernel, tm=tm),
        grid=grid,
        in_specs=[
            pl.BlockSpec((1, tm, d), lambda i, j: (i, j, 0)),
            full(g.shape), full(w.shape), full(bf3.shape), full(qg.shape), full(kg.shape),
            full(pq.shape), full(pk.shape),
        ],
        out_specs=[aug_spec, aug_spec, pl.BlockSpec((1, tm, ATT_WIDTH), lambda i, j: (i, j, 0))],
        out_shape=[
            jax.ShapeDtypeStruct((b, N_HEADS, s, LANES), BF16),
            jax.ShapeDtypeStruct((b, N_HEADS, s, LANES), BF16),
            jax.ShapeDtypeStruct((b, s, ATT_WIDTH), BF16),
        ],
        scratch_shapes=[pltpu.VMEM((1, LANES), F32)],
        compiler_params=pltpu.CompilerParams(
            dimension_semantics=("arbitrary", "arbitrary"), vmem_limit_bytes=VMEM_LIMIT),
        name="qkv_proj",
    )(x, g, w, bf3, qg, kg, pq, pk)


def _attn_kernel(q_ref, k_ref, v_ref, o_ref, *, tq, tk):
    qi = pl.program_id(2)
    row = lax.broadcasted_iota(jnp.int32, (tq, tk), 0)
    col = lax.broadcasted_iota(jnp.int32, (tq, tk), 1)
    causal = row >= col

    def step(j, carry, masked):
        start = pl.multiple_of(j * tk, tk)
        vblk = v_ref[0, pl.ds(start, tk), :]
        out = []
        for hh in range(2):
            m, l, acc = carry[hh]
            s = lax.dot_general(q_ref[0, hh], k_ref[0, hh, pl.ds(start, tk), :],
                                (((1,), (1,)), ((), ())), preferred_element_type=F32)
            if masked:
                s = jnp.where(causal, s, NEG_INF)
            m_new = jnp.maximum(m, jnp.max(s, axis=1, keepdims=True))
            alpha = jnp.exp(m - m_new)
            p = jnp.exp(s - m_new)
            l = alpha * l + jnp.sum(p, axis=1, keepdims=True)
            acc = alpha * acc + _bdot(p.astype(BF16), vblk)
            out.append((m_new, l, acc))
        return tuple(out)

    init = tuple((jnp.full((tq, 1), NEG_INF, F32), jnp.zeros((tq, 1), F32),
                  jnp.zeros((tq, LANES), F32)) for _ in range(2))
    carry = lax.fori_loop(0, qi, lambda j, c: step(j, c, False), init)
    carry = step(qi, carry, True)
    lane = lax.broadcasted_iota(jnp.int32, (tq, LANES), 1)
    (_, l0, a0), (_, l1, a1) = carry
    o_ref[0] = jnp.where(lane < HEAD_DIM, a0 / l0, a1 / l1).astype(BF16)


def _attention(q_aug, k_aug, v, *, tq):
    b, nh, s, _ = q_aug.shape
    grid = (b, nh // 2, s // tq)
    return pl.pallas_call(
        functools.partial(_attn_kernel, tq=tq, tk=tq),
        grid=grid,
        in_specs=[
            pl.BlockSpec((1, 2, tq, LANES), lambda i, h, j: (i, h, j, 0)),
            pl.BlockSpec((1, 2, s, LANES), lambda i, h, j: (i, h, 0, 0)),
            pl.BlockSpec((1, s, LANES), lambda i, h, j: (i, 0, h)),
        ],
        out_specs=pl.BlockSpec((1, tq, LANES), lambda i, h, j: (i, j, h)),
        out_shape=jax.ShapeDtypeStruct((b, s, ATT_WIDTH), BF16),
        compiler_params=pltpu.CompilerParams(
            dimension_semantics=("arbitrary", "arbitrary", "arbitrary"),
            vmem_limit_bytes=VMEM_LIMIT),
        name="fox_attention",
    )(q_aug, k_aug, v)


def _mix_kernel(x_ref, ya_ref, g_ref, wcg_ref, cw_ref, wa_ref, wb_ref, wo_ref, g2_ref,
                wr_ref, br_ref, x1_ref, h2_ref, comb_ref, carry_ref, *, tm):
    st = pl.program_id(1)

    @pl.when(st == 0)
    def _():
        carry_ref[...] = jnp.zeros_like(carry_ref)

    x = x_ref[0]
    h = _rms(x, g_ref[...]).astype(BF16)
    pc = _bdot(h, wcg_ref[...])
    cw = CONV_WIDTH
    cb = pc[:, 0:cw]
    prod = pc[:, cw:2 * cw] * pc[:, 2 * cw:3 * cw]
    ga = pc[:, 3 * cw:3 * cw + D_MODEL]
    gb = pc[:, 3 * cw + D_MODEL:3 * cw + 2 * D_MODEL]

    prev = carry_ref[...]
    crow = lax.broadcasted_iota(jnp.int32, (tm, cw), 0)
    m1 = jnp.where(crow == 0, prev[7:8, :], pltpu.roll(prod, 1, axis=0))
    m2 = jnp.where(crow == 0, prev[6:7, :],
                   jnp.where(crow == 1, prev[7:8, :], pltpu.roll(prod, 2, axis=0)))
    carry_ref[...] = prod[tm - 8:tm, :]
    w = cw_ref[...]
    y_conv = cb * (w[0:1, :] * m2 + w[1:2, :] * m1 + w[2:3, :] * prod)

    a = _bdot(ya_ref[0], wa_ref[...])
    bb = _bdot(y_conv.astype(BF16), wb_ref[...])
    merged = jax.nn.sigmoid(ga) * a + jax.nn.sigmoid(gb) * bb
    x1 = x + _bdot(merged.astype(BF16), wo_ref[...])
    x1_ref[0] = x1

    h2 = _rms(x1, g2_ref[...]).astype(BF16)
    h2_ref[0] = h2

    logits = _bdot(h2, wr_ref[...]) + br_ref[...]
    lane = lax.broadcasted_iota(jnp.int32, (tm, LANES), 1)
    lanef = lane.astype(F32)
    is_g = (lane >= N_EXPERTS) & (lane < N_EXPERTS + N_GROUPS)
    gl = jnp.where(is_g, logits, NEG_INF)
    gmax = jnp.max(gl, axis=1, keepdims=True)
    gsum = jnp.sum(jnp.exp(gl - gmax), axis=1, keepdims=True)
    g_val = 1.0 / gsum
    g_lane = jnp.min(jnp.where(gl == gmax, lanef, float(LANES)), axis=1, keepdims=True)
    e_lo = (g_lane - float(N_EXPERTS)) * float(EXPERTS_PER_GROUP)
    in_grp = (lanef >= e_lo) & (lanef < e_lo + float(EXPERTS_PER_GROUP))
    el = jnp.where(in_grp, logits, NEG_INF)
    t1 = jnp.max(el, axis=1, keepdims=True)
    i1 = jnp.min(jnp.where(el == t1, lanef, float(LANES)), axis=1, keepdims=True)
    el2 = jnp.where(lanef == i1, NEG_INF, el)
    t2 = jnp.max(el2, axis=1, keepdims=True)
    i2 = jnp.min(jnp.where(el2 == t2, lanef, float(LANES)), axis=1, keepdims=True)
    e2 = jnp.exp(t2 - t1)
    w1 = g_val / (1.0 + e2)
    w2 = g_val * e2 / (1.0 + e2)
    comb_ref[0] = jnp.where(lanef == i1, w1, 0.0) + jnp.where(lanef == i2, w2, 0.0)


def _mix(x, y_att, g, wcg, conv_w, wa, wb, wo, g2, wr, br, *, tm):
    b, s, d = x.shape
    grid = (b, s // tm)
    full = lambda a: pl.BlockSpec(a.shape, lambda i, j: (0,) * a.ndim)
    row = lambda width: pl.BlockSpec((1, tm, width), lambda i, j: (i, j, 0))
    return pl.pallas_call(
        functools.partial(_mix_kernel, tm=tm),
        grid=grid,
        in_specs=[row(d), row(ATT_WIDTH), full(g), full(wcg), full(conv_w), full(wa), full(wb),
                  full(wo), full(g2), full(wr), full(br)],
        out_specs=[row(d), row(d), row(LANES)],
        out_shape=[
            jax.ShapeDtypeStruct((b, s, d), F32),
            jax.ShapeDtypeStruct((b, s, d), BF16),
            jax.ShapeDtypeStruct((b, s, LANES), F32),
        ],
        scratch_shapes=[pltpu.VMEM((8, CONV_WIDTH), F32)],
        compiler_params=pltpu.CompilerParams(
            dimension_semantics=("arbitrary", "arbitrary"), vmem_limit_bytes=VMEM_LIMIT),
        name="mix",
    )(x, y_att, g, wcg, conv_w, wa, wb, wo, g2, wr, br)


def _ffn_kernel(h2_ref, comb_ref, wg_ref, wu_ref, wd_ref, x1_ref, p_ref, g3_ref, wpg_ref,
                wple_ref, o_ref, acc_ref, *, tm):
    e = pl.program_id(1)

    @pl.when(e == 0)
    def _():
        acc_ref[...] = jnp.zeros_like(acc_ref)

    h2 = h2_ref[...]
    a = _bdot(h2, wg_ref[0])
    u = _bdot(h2, wu_ref[0])
    lane = lax.broadcasted_iota(jnp.int32, (tm, LANES), 1)
    ce = jnp.sum(jnp.where(lane == e, comb_ref[...], 0.0), axis=1, keepdims=True)
    hid = (a * jax.nn.sigmoid(a)) * u * ce
    acc_ref[...] += _bdot(hid.astype(BF16), wd_ref[0])

    @pl.when(e == N_EXPERTS - 1)
    def _():
        x2 = x1_ref[...] + acc_ref[...]
        h3 = _rms(x2, g3_ref[...]).astype(BF16)
        gate = jax.nn.sigmoid(_bdot(h3, wpg_ref[...]))
        emb = _bdot(p_ref[...].astype(BF16), wple_ref[...])
        o_ref[...] = x2 + gate * emb


def _ffn(h2, comb, wg, wu, wd, x1, p, g3, wpg, wple, *, tm):
    t, d = x1.shape
    grid = (t // tm, N_EXPERTS)
    full = lambda a: pl.BlockSpec(a.shape, lambda i, e: (0,) * a.ndim)
    row = lambda width: pl.BlockSpec((tm, width), lambda i, e: (i, 0))
    return pl.pallas_call(
        functools.partial(_ffn_kernel, tm=tm),
        grid=grid,
        in_specs=[
            row(d), row(LANES),
            pl.BlockSpec((1, d, D_EXPERT), lambda i, e: (e, 0, 0)),
            pl.BlockSpec((1, d, D_EXPERT), lambda i, e: (e, 0, 0)),
            pl.BlockSpec((1, D_EXPERT, d), lambda i, e: (e, 0, 0)),
            row(d), row(PLE_DIM), full(g3), full(wpg), full(wple),
        ],
        out_specs=row(d),
        out_shape=jax.ShapeDtypeStruct((t, d), F32),
        scratch_shapes=[pltpu.VMEM((tm, d), F32)],
        compiler_params=pltpu.CompilerParams(
            dimension_semantics=("arbitrary", "arbitrary"), vmem_limit_bytes=VMEM_LIMIT),
        name="ffn",
    )(h2, comb, wg, wu, wd, x1, p, g3, wpg, wple)


def _c_select_matrices():
    pq = np.zeros((LANES, N_HEADS * LANES), np.float32)
    pk = np.zeros((LANES, N_HEADS * LANES), np.float32)
    for idx in range(C_PIECES):
        for h in range(N_HEADS):
            pq[idx * N_HEADS + h, h * LANES + QC_LANE + idx] = 1.0
            pk[idx * N_HEADS + h, h * LANES + KC_LANE + idx] = -1.0
    return jnp.asarray(pq, BF16), jnp.asarray(pk, BF16)


def kernel(x, p, attn_norm_g, w_in, b_f, q_norm_g, k_norm_g, conv_w, w_out_att, w_out_conv, w_o,
           ffn_norm_g, w_rg, b_rg, w_re, b_re, w_gate, w_up, w_down, ple_norm_g, w_pg, w_ple):
    b, s, d = x.shape
    t = b * s
    aw = ATT_WIDTH
    for i in range(w_in.shape[0]):
        wi = w_in[i]
        wf = wi[:, 3 * aw:3 * aw + N_HEADS]
        w_qkvf = jnp.concatenate(
            [wi[:, :3 * aw], wf, wf, wf, jnp.zeros((d, LANES - C_PIECES * N_HEADS), F32)],
            axis=1).astype(BF16)
        w_cg = wi[:, 3 * aw + N_HEADS:].astype(BF16)
        bf3 = jnp.concatenate([b_f[i]] * C_PIECES + [jnp.zeros((LANES - C_PIECES * N_HEADS,), F32)])[None, :]
        scale = HEAD_DIM ** -0.5
        qg = jnp.tile(q_norm_g[i] * scale, 2)[None, :]
        kg = jnp.tile(k_norm_g[i], 2)[None, :]
        pq, pk = _c_select_matrices()
        q_aug, k_aug, v = _qkv_proj(x, attn_norm_g[i][None, :], w_qkvf, bf3, qg, kg, pq, pk, tm=512)
        y_att = _attention(q_aug, k_aug, v, tq=512)

        w_r = jnp.concatenate(
            [w_re[i], w_rg[i], jnp.zeros((d, LANES - N_EXPERTS - N_GROUPS), F32)], axis=1).astype(BF16)
        b_r = jnp.concatenate(
            [b_re[i], b_rg[i], jnp.zeros((LANES - N_EXPERTS - N_GROUPS,), F32)])[None, :]
        x1, h2, comb = _mix(x, y_att, attn_norm_g[i][None, :], w_cg, conv_w[i],
                            w_out_att[i].astype(BF16), w_out_conv[i].astype(BF16),
                            w_o[i].astype(BF16), ffn_norm_g[i][None, :], w_r, b_r, tm=512)

        x = _ffn(h2.reshape(t, d), comb.reshape(t, LANES), w_gate[i].astype(BF16),
                 w_up[i].astype(BF16), w_down[i].astype(BF16), x1.reshape(t, d),
                 p[i].reshape(t, PLE_DIM), ple_norm_g[i][None, :], w_pg[i].astype(BF16),
                 w_ple[i].astype(BF16), tm=1024).reshape(b, s, d)
    return x
```

```python
import functools

import numpy as np
import jax
import jax.numpy as jnp
from jax import lax
from jax.experimental import pallas as pl
from jax.experimental.pallas import tpu as pltpu

D_MODEL = 1024
N_HEADS = 8
HEAD_DIM = 64
ATT_WIDTH = N_HEADS * HEAD_DIM
CONV_WIDTH = 512
CONV_K = 3
N_GROUPS = 4
EXPERTS_PER_GROUP = 4
N_EXPERTS = 16
D_EXPERT = 256
PLE_DIM = 256
EPS = 1e-6
NEG_INF = -1e30

LANES = 128
C_PIECES = 3
QC_LANE = HEAD_DIM
KC_LANE = HEAD_DIM + C_PIECES
GID_LANE = N_EXPERTS
RANK_LANE = N_EXPERTS + 1
XE_WIDTH = D_MODEL + LANES
VMEM_LIMIT = 56 * 1024 * 1024

F32 = jnp.float32
BF16 = jnp.bfloat16


def _rms(xf, g):
    return xf * lax.rsqrt(jnp.mean(xf * xf, axis=-1, keepdims=True) + EPS) * g


def _log_sigmoid(z):
    return jnp.minimum(z, 0.0) - jnp.log1p(jnp.exp(-jnp.abs(z)))


def _bdot(a, b):
    return jnp.dot(a, b, preferred_element_type=F32)


def _qkv_kernel(x_ref, g_ref, w_ref, bf_ref, qg_ref, kg_ref, pq_ref, pk_ref,
                q_ref, k_ref, v_ref, carry_ref, *, tm):
    st = pl.program_id(1)

    @pl.when(st == 0)
    def _():
        carry_ref[...] = jnp.zeros_like(carry_ref)

    h = _rms(x_ref[0], g_ref[...]).astype(BF16)
    proj = _bdot(h, w_ref[...])
    v_ref[0] = proj[:, 2 * ATT_WIDTH:3 * ATT_WIDTH].astype(BF16)

    lane = lax.broadcasted_iota(jnp.int32, (tm, LANES), 1)
    row = lax.broadcasted_iota(jnp.int32, (tm, LANES), 0)

    f3 = proj[:, 3 * ATT_WIDTH:3 * ATT_WIDTH + LANES] + bf_ref[...]
    c = jnp.where(lane < C_PIECES * N_HEADS, _log_sigmoid(f3), 0.0)
    sh = 1
    while sh < tm:
        c = c + jnp.where(row >= sh, pltpu.roll(c, sh, axis=0), 0.0)
        sh *= 2
    c = c + carry_ref[...]
    carry_ref[...] = c[tm - 1:tm, :]
    hi = c.astype(BF16).astype(F32)
    r1 = c - hi
    mid = r1.astype(BF16).astype(F32)
    lo = r1 - mid
    piece = jnp.where(lane < N_HEADS, hi, jnp.where(lane < 2 * N_HEADS, mid, lo)).astype(BF16)
    qc = _bdot(piece, pq_ref[...])
    kc = _bdot(piece, pk_ref[...])

    q_ones = jnp.where((lane >= KC_LANE) & (lane < KC_LANE + C_PIECES), 1.0, 0.0)
    k_ones = jnp.where((lane >= QC_LANE) & (lane < QC_LANE + C_PIECES), 1.0, 0.0)
    low = lane < HEAD_DIM

    def heads(base, gain_ref, cmat, ones, out_ref):
        gain = gain_ref[...]
        for j in range(N_HEADS // 2):
            pair = proj[:, base + LANES * j: base + LANES * (j + 1)]
            sq = pair * pair
            ss_lo = jnp.sum(jnp.where(low, sq, 0.0), axis=1, keepdims=True)
            ss_hi = jnp.sum(jnp.where(low, 0.0, sq), axis=1, keepdims=True)
            n_lo = pair * lax.rsqrt(ss_lo * (1.0 / HEAD_DIM) + EPS) * gain
            n_hi = pltpu.roll(pair, HEAD_DIM, axis=1) * lax.rsqrt(ss_hi * (1.0 / HEAD_DIM) + EPS) * gain
            for hh, nrm in ((2 * j, n_lo), (2 * j + 1, n_hi)):
                aug = jnp.where(low, nrm, 0.0) + cmat[:, LANES * hh: LANES * (hh + 1)] + ones
                out_ref[0, hh] = aug.astype(BF16)

    heads(0, qg_ref, qc, q_ones, q_ref)
    heads(ATT_WIDTH, kg_ref, kc, k_ones, k_ref)


def _qkv_proj(x, g, w, bf3, qg, kg, pq, pk, *, tm):
    b, s, d = x.shape
    grid = (b, s // tm)
    full = lambda shape: pl.BlockSpec(shape, lambda i, j: (0,) * len(shape))
    aug_spec = pl.BlockSpec((1, N_HEADS, tm, LANES), lambda i, j: (i, 0, j, 0))
    return pl.pallas_call(
        functools.partial(_qkv_kernel, tm=tm),
        grid=grid,
        in_specs=[
            pl.BlockSpec((1, tm, d), lambda i, j: (i, j, 0)),
            full(g.shape), full(w.shape), full(bf3.shape), full(qg.shape), full(kg.shape),
            full(pq.shape), full(pk.shape),
        ],
        out_specs=[aug_spec, aug_spec, pl.BlockSpec((1, tm, ATT_WIDTH), lambda i, j: (i, j, 0))],
        out_shape=[
            jax.ShapeDtypeStruct((b, N_HEADS, s, LANES), BF16),
            jax.ShapeDtypeStruct((b, N_HEADS, s, LANES), BF16),
            jax.ShapeDtypeStruct((b, s, ATT_WIDTH), BF16),
        ],
        scratch_shapes=[pltpu.VMEM((1, LANES), F32)],
        compiler_params=pltpu.CompilerParams(
            dimension_semantics=("arbitrary", "arbitrary"), vmem_limit_bytes=VMEM_LIMIT),
        name="qkv_proj",
    )(x, g, w, bf3, qg, kg, pq, pk)


def _attn_kernel(q_ref, k_ref, v_ref, o_ref, *, tq, tk):
    qi = pl.program_id(2)
    row = lax.broadcasted_iota(jnp.int32, (tq, tk), 0)
    col = lax.broadcasted_iota(jnp.int32, (tq, tk), 1)
    causal = row >= col

    def step(j, carry, masked):
        start = pl.multiple_of(j * tk, tk)
        vblk = v_ref[0, pl.ds(start, tk), :]
        out = []
        for hh in range(2):
            m, l, acc = carry[hh]
            s = lax.dot_general(q_ref[0, hh], k_ref[0, hh, pl.ds(start, tk), :],
                                (((1,), (1,)), ((), ())), preferred_element_type=F32)
            if masked:
                s = jnp.where(causal, s, NEG_INF)
            m_new = jnp.maximum(m, jnp.max(s, axis=1, keepdims=True))
            alpha = jnp.exp(m - m_new)
            p = jnp.exp(s - m_new)
            l = alpha * l + jnp.sum(p, axis=1, keepdims=True)
            acc = alpha * acc + _bdot(p.astype(BF16), vblk)
            out.append((m_new, l, acc))
        return tuple(out)

    init = tuple((jnp.full((tq, 1), NEG_INF, F32), jnp.zeros((tq, 1), F32),
                  jnp.zeros((tq, LANES), F32)) for _ in range(2))
    carry = lax.fori_loop(0, qi, lambda j, c: step(j, c, False), init)
    carry = step(qi, carry, True)
    lane = lax.broadcasted_iota(jnp.int32, (tq, LANES), 1)
    (_, l0, a0), (_, l1, a1) = carry
    o_ref[0] = jnp.where(lane < HEAD_DIM, a0 / l0, a1 / l1).astype(BF16)


def _attention(q_aug, k_aug, v, *, tq):
    b, nh, s, _ = q_aug.shape
    grid = (b, nh // 2, s // tq)
    return pl.pallas_call(
        functools.partial(_attn_kernel, tq=tq, tk=tq),
        grid=grid,
        in_specs=[
            pl.BlockSpec((1, 2, tq, LANES), lambda i, h, j: (i, h, j, 0)),
            pl.BlockSpec((1, 2, s, LANES), lambda i, h, j: (i, h, 0, 0)),
            pl.BlockSpec((1, s, LANES), lambda i, h, j: (i, 0, h)),
        ],
        out_specs=pl.BlockSpec((1, tq, LANES), lambda i, h, j: (i, j, h)),
        out_shape=jax.ShapeDtypeStruct((b, s, ATT_WIDTH), BF16),
        compiler_params=pltpu.CompilerParams(
            dimension_semantics=("arbitrary", "arbitrary", "arbitrary"),
            vmem_limit_bytes=VMEM_LIMIT),
        name="fox_attention",
    )(q_aug, k_aug, v)


def _mix_kernel(x_ref, ya_ref, g_ref, wcg_ref, cw_ref, wa_ref, wb_ref, wo_ref, g2_ref,
                wr_ref, br_ref, ltri_ref, xe_ref, cnt_ref, carry_ref, *, tm):
    st = pl.program_id(1)

    @pl.when(st == 0)
    def _():
        carry_ref[...] = jnp.zeros_like(carry_ref)

    @pl.when((st == 0) & (pl.program_id(0) == 0))
    def _():
        cnt_ref[...] = jnp.zeros_like(cnt_ref)

    x = x_ref[0]
    h = _rms(x, g_ref[...]).astype(BF16)
    pc = _bdot(h, wcg_ref[...])
    cw = CONV_WIDTH
    cb = pc[:, 0:cw]
    prod = pc[:, cw:2 * cw] * pc[:, 2 * cw:3 * cw]
    ga = pc[:, 3 * cw:3 * cw + D_MODEL]
    gb = pc[:, 3 * cw + D_MODEL:3 * cw + 2 * D_MODEL]

    prev = carry_ref[...]
    crow = lax.broadcasted_iota(jnp.int32, (tm, cw), 0)
    m1 = jnp.where(crow == 0, prev[7:8, :], pltpu.roll(prod, 1, axis=0))
    m2 = jnp.where(crow == 0, prev[6:7, :],
                   jnp.where(crow == 1, prev[7:8, :], pltpu.roll(prod, 2, axis=0)))
    carry_ref[...] = prod[tm - 8:tm, :]
    w = cw_ref[...]
    y_conv = cb * (w[0:1, :] * m2 + w[1:2, :] * m1 + w[2:3, :] * prod)

    a = _bdot(ya_ref[0], wa_ref[...])
    bb = _bdot(y_conv.astype(BF16), wb_ref[...])
    merged = jax.nn.sigmoid(ga) * a + jax.nn.sigmoid(gb) * bb
    x1 = x + _bdot(merged.astype(BF16), wo_ref[...])
    xe_ref[0, :, 0:D_MODEL] = x1

    h2 = _rms(x1, g2_ref[...]).astype(BF16)

    logits = _bdot(h2, wr_ref[...]) + br_ref[...]
    lane = lax.broadcasted_iota(jnp.int32, (tm, LANES), 1)
    lanef = lane.astype(F32)
    is_g = (lane >= N_EXPERTS) & (lane < N_EXPERTS + N_GROUPS)
    gl = jnp.where(is_g, logits, NEG_INF)
    gmax = jnp.max(gl, axis=1, keepdims=True)
    gsum = jnp.sum(jnp.exp(gl - gmax), axis=1, keepdims=True)
    g_val = 1.0 / gsum
    g_lane = jnp.min(jnp.where(gl == gmax, lanef, float(LANES)), axis=1, keepdims=True)
    e_lo = (g_lane - float(N_EXPERTS)) * float(EXPERTS_PER_GROUP)
    in_grp = (lanef >= e_lo) & (lanef < e_lo + float(EXPERTS_PER_GROUP))
    el = jnp.where(in_grp, logits, NEG_INF)
    t1 = jnp.max(el, axis=1, keepdims=True)
    i1 = jnp.min(jnp.where(el == t1, lanef, float(LANES)), axis=1, keepdims=True)
    el2 = jnp.where(lanef == i1, NEG_INF, el)
    t2 = jnp.max(el2, axis=1, keepdims=True)
    i2 = jnp.min(jnp.where(el2 == t2, lanef, float(LANES)), axis=1, keepdims=True)
    e2 = jnp.exp(t2 - t1)
    w1 = g_val / (1.0 + e2)
    w2 = g_val * e2 / (1.0 + e2)
    comb = jnp.where(lanef == i1, w1, 0.0) + jnp.where(lanef == i2, w2, 0.0)

    onehot = jnp.where(is_g & (lanef == g_lane), 1.0, 0.0)
    running = cnt_ref[...]
    prefix = _bdot(ltri_ref[...], onehot.astype(BF16)) + running
    rank = jnp.sum(onehot * prefix, axis=1, keepdims=True)
    cnt_ref[...] = running + jnp.sum(onehot, axis=0, keepdims=True)
    route = (comb + jnp.where(lane == GID_LANE, g_lane - float(N_EXPERTS), 0.0)
             + jnp.where(lane == RANK_LANE, rank, 0.0))
    xe_ref[0, :, D_MODEL:XE_WIDTH] = route


def _mix(x, y_att, g, wcg, conv_w, wa, wb, wo, g2, wr, br, ltri, *, tm):
    b, s, d = x.shape
    grid = (b, s // tm)
    full = lambda a: pl.BlockSpec(a.shape, lambda i, j: (0,) * a.ndim)
    row = lambda width: pl.BlockSpec((1, tm, width), lambda i, j: (i, j, 0))
    return pl.pallas_call(
        functools.partial(_mix_kernel, tm=tm),
        grid=grid,
        in_specs=[row(d), row(ATT_WIDTH), full(g), full(wcg), full(conv_w), full(wa), full(wb),
                  full(wo), full(g2), full(wr), full(br), full(ltri)],
        out_specs=[row(XE_WIDTH), pl.BlockSpec((1, LANES), lambda i, j: (0, 0))],
        out_shape=[
            jax.ShapeDtypeStruct((b, s, XE_WIDTH), F32),
            jax.ShapeDtypeStruct((1, LANES), F32),
        ],
        scratch_shapes=[pltpu.VMEM((8, CONV_WIDTH), F32)],
        compiler_params=pltpu.CompilerParams(
            dimension_semantics=("arbitrary", "arbitrary"), vmem_limit_bytes=VMEM_LIMIT),
        name="mix",
    )(x, y_att, g, wcg, conv_w, wa, wb, wo, g2, wr, br, ltri)


def _group_offsets(cnt_ref, tile):
    offs = [jnp.int32(0)]
    for g in range(N_GROUPS - 1):
        offs.append(offs[-1] + pl.cdiv(cnt_ref[g], tile) * tile)
    return offs


def _sorted_pos(gid_ref, rank_ref, offs, t):
    g = gid_ref[t]
    base = jnp.where(g == 0, offs[0], jnp.where(g == 1, offs[1], jnp.where(g == 2, offs[2], offs[3])))
    return base + rank_ref[t]


def _dispatch_kernel(gid_ref, rank_ref, cnt_ref, xe_ref, xs_ref, zero_ref, sem, *, tm, tile):
    i = pl.program_id(0)
    offs = _group_offsets(cnt_ref, tile)

    @pl.when(i == 0)
    def _():
        zero_ref[...] = jnp.zeros_like(zero_ref)
        for g in range(N_GROUPS):
            @pl.when(cnt_ref[g] > 0)
            def _():
                last = offs[g] + (pl.cdiv(cnt_ref[g], tile) - 1) * tile
                cp = pltpu.make_async_copy(zero_ref, xs_ref.at[pl.ds(last, tile)], sem)
                cp.start()
                cp.wait()

    def row_copy(r, pos):
        return pltpu.make_async_copy(xe_ref.at[pl.ds(r, 1)], xs_ref.at[pl.ds(pos, 1)], sem)

    def issue(r, c):
        row_copy(r, _sorted_pos(gid_ref, rank_ref, offs, i * tm + r)).start()
        return c

    def drain(r, c):
        row_copy(0, 0).wait()
        return c

    lax.fori_loop(0, tm, issue, 0, unroll=8)
    lax.fori_loop(0, tm, drain, 0, unroll=8)


def _dispatch(gid, rank, cnt, xe, *, tm, tile, n_rows):
    t, w = xe.shape
    return pl.pallas_call(
        functools.partial(_dispatch_kernel, tm=tm, tile=tile),
        grid_spec=pltpu.PrefetchScalarGridSpec(
            num_scalar_prefetch=3,
            grid=(t // tm,),
            in_specs=[pl.BlockSpec((tm, w), lambda i, *_: (i, 0))],
            out_specs=pl.BlockSpec(memory_space=pl.ANY),
            scratch_shapes=[pltpu.VMEM((tile, w), F32), pltpu.SemaphoreType.DMA(())],
        ),
        out_shape=jax.ShapeDtypeStruct((n_rows, w), F32),
        compiler_params=pltpu.CompilerParams(
            dimension_semantics=("arbitrary",), vmem_limit_bytes=VMEM_LIMIT),
        name="dispatch",
    )(gid, rank, cnt, xe)


def _tile_group(i, cnt_ref, tile):
    end = jnp.int32(0)
    g = jnp.int32(0)
    for k in range(N_GROUPS):
        end = end + pl.cdiv(cnt_ref[k], tile)
        if k < N_GROUPS - 1:
            g = g + (i >= end).astype(jnp.int32)
    return g, i < end


def _experts_kernel(cnt_ref, xs_ref, g2_ref, wgu_ref, wd_ref, ys_ref, *, tm):
    i = pl.program_id(0)
    g, used = _tile_group(i, cnt_ref, tm)

    @pl.when(used)
    def _():
        x1 = xs_ref[:, 0:D_MODEL]
        route = xs_ref[:, D_MODEL:XE_WIDTH]
        h2 = _rms(x1, g2_ref[...]).astype(BF16)
        gu = _bdot(h2, wgu_ref[0])
        width = EXPERTS_PER_GROUP * D_EXPERT
        a = gu[:, 0:width]
        u = gu[:, width:2 * width]
        hid = (a * jax.nn.sigmoid(a)) * u
        lane = lax.broadcasted_iota(jnp.int32, (tm, LANES), 1)
        parts = []
        for e in range(EXPERTS_PER_GROUP):
            ce = jnp.sum(jnp.where(lane == g * EXPERTS_PER_GROUP + e, route, 0.0), axis=1, keepdims=True)
            parts.append((hid[:, e * D_EXPERT:(e + 1) * D_EXPERT] * ce).astype(BF16))
        ys_ref[...] = x1 + _bdot(jnp.concatenate(parts, axis=1), wd_ref[0])

    @pl.when(jnp.logical_not(used))
    def _():
        ys_ref[...] = jnp.zeros_like(ys_ref)


def _experts(cnt, xs, g2, wgu, wd, *, tm):
    n_rows, w = xs.shape
    d = D_MODEL

    def grp(i, cnt_ref):
        return _tile_group(i, cnt_ref, tm)[0]

    def xs_map(i, cnt_ref):
        total = sum(pl.cdiv(cnt_ref[k], tm) for k in range(N_GROUPS))
        return (jnp.minimum(i, total - 1), 0)

    return pl.pallas_call(
        functools.partial(_experts_kernel, tm=tm),
        grid_spec=pltpu.PrefetchScalarGridSpec(
            num_scalar_prefetch=1,
            grid=(n_rows // tm,),
            in_specs=[
                pl.BlockSpec((tm, w), xs_map),
                pl.BlockSpec(g2.shape, lambda i, c: (0, 0)),
                pl.BlockSpec((1, d, 2 * EXPERTS_PER_GROUP * D_EXPERT), lambda i, c: (grp(i, c), 0, 0)),
                pl.BlockSpec((1, EXPERTS_PER_GROUP * D_EXPERT, d), lambda i, c: (grp(i, c), 0, 0)),
            ],
            out_specs=pl.BlockSpec((tm, d), lambda i, c: (i, 0)),
        ),
        out_shape=jax.ShapeDtypeStruct((n_rows, d), F32),
        compiler_params=pltpu.CompilerParams(
            dimension_semantics=("arbitrary",), vmem_limit_bytes=VMEM_LIMIT),
        name="experts",
    )(cnt, xs, g2, wgu, wd)


def _final_kernel(gid_ref, rank_ref, cnt_ref, p_ref, g3_ref, wpg_ref, wple_ref, ys_ref,
                  o_ref, buf_ref, sem, *, tm, tile):
    i = pl.program_id(0)
    n = pl.num_programs(0)
    offs = _group_offsets(cnt_ref, tile)
    slot = i % 2

    def row_copy(s, r, pos):
        return pltpu.make_async_copy(ys_ref.at[pl.ds(pos, 1)], buf_ref.at[s, pl.ds(r, 1)], sem.at[s])

    def issue(tile_idx, s):
        def body(r, c):
            row_copy(s, r, _sorted_pos(gid_ref, rank_ref, offs, tile_idx * tm + r)).start()
            return c
        lax.fori_loop(0, tm, body, 0, unroll=8)

    @pl.when(i == 0)
    def _():
        issue(0, 0)

    @pl.when(i + 1 < n)
    def _():
        issue(i + 1, 1 - slot)

    def drain(r, c):
        row_copy(slot, 0, 0).wait()
        return c

    lax.fori_loop(0, tm, drain, 0, unroll=8)

    x2 = buf_ref[slot]
    h3 = _rms(x2, g3_ref[...]).astype(BF16)
    gate = jax.nn.sigmoid(_bdot(h3, wpg_ref[...]))
    emb = _bdot(p_ref[...].astype(BF16), wple_ref[...])
    o_ref[...] = x2 + gate * emb


def _final(gid, rank, cnt, p, g3, wpg, wple, ys, *, tm, tile):
    t, d = p.shape[0], ys.shape[1]
    full = lambda a: pl.BlockSpec(a.shape, lambda i, *_: (0,) * a.ndim)
    row = lambda width: pl.BlockSpec((tm, width), lambda i, *_: (i, 0))
    return pl.pallas_call(
        functools.partial(_final_kernel, tm=tm, tile=tile),
        grid_spec=pltpu.PrefetchScalarGridSpec(
            num_scalar_prefetch=3,
            grid=(t // tm,),
            in_specs=[row(PLE_DIM), full(g3), full(wpg), full(wple),
                      pl.BlockSpec(memory_space=pl.ANY)],
            out_specs=row(d),
            scratch_shapes=[pltpu.VMEM((2, tm, d), F32), pltpu.SemaphoreType.DMA((2,))],
        ),
        out_shape=jax.ShapeDtypeStruct((t, d), F32),
        compiler_params=pltpu.CompilerParams(
            dimension_semantics=("arbitrary",), vmem_limit_bytes=VMEM_LIMIT),
        name="final",
    )(gid, rank, cnt, p, g3, wpg, wple, ys)


def _c_select_matrices():
    pq = np.zeros((LANES, N_HEADS * LANES), np.float32)
    pk = np.zeros((LANES, N_HEADS * LANES), np.float32)
    for idx in range(C_PIECES):
        for h in range(N_HEADS):
            pq[idx * N_HEADS + h, h * LANES + QC_LANE + idx] = 1.0
            pk[idx * N_HEADS + h, h * LANES + KC_LANE + idx] = -1.0
    return jnp.asarray(pq, BF16), jnp.asarray(pk, BF16)


def kernel(x, p, attn_norm_g, w_in, b_f, q_norm_g, k_norm_g, conv_w, w_out_att, w_out_conv, w_o,
           ffn_norm_g, w_rg, b_rg, w_re, b_re, w_gate, w_up, w_down, ple_norm_g, w_pg, w_ple):
    b, s, d = x.shape
    t = b * s
    aw = ATT_WIDTH
    for i in range(w_in.shape[0]):
        wi = w_in[i]
        wf = wi[:, 3 * aw:3 * aw + N_HEADS]
        w_qkvf = jnp.concatenate(
            [wi[:, :3 * aw], wf, wf, wf, jnp.zeros((d, LANES - C_PIECES * N_HEADS), F32)],
            axis=1).astype(BF16)
        w_cg = wi[:, 3 * aw + N_HEADS:].astype(BF16)
        bf3 = jnp.concatenate([b_f[i]] * C_PIECES + [jnp.zeros((LANES - C_PIECES * N_HEADS,), F32)])[None, :]
        scale = HEAD_DIM ** -0.5
        qg = jnp.tile(q_norm_g[i] * scale, 2)[None, :]
        kg = jnp.tile(k_norm_g[i], 2)[None, :]
        pq, pk = _c_select_matrices()
        q_aug, k_aug, v = _qkv_proj(x, attn_norm_g[i][None, :], w_qkvf, bf3, qg, kg, pq, pk, tm=512)
        y_att = _attention(q_aug, k_aug, v, tq=512)

        w_r = jnp.concatenate(
            [w_re[i], w_rg[i], jnp.zeros((d, LANES - N_EXPERTS - N_GROUPS), F32)], axis=1).astype(BF16)
        b_r = jnp.concatenate(
            [b_re[i], b_rg[i], jnp.zeros((LANES - N_EXPERTS - N_GROUPS,), F32)])[None, :]
        tm_mix = 512
        ltri = jnp.asarray(np.tril(np.ones((tm_mix, tm_mix), np.float32), -1), BF16)
        g_ffn = ffn_norm_g[i][None, :]
        xe, counts = _mix(x, y_att, attn_norm_g[i][None, :], w_cg, conv_w[i],
                          w_out_att[i].astype(BF16), w_out_conv[i].astype(BF16),
                          w_o[i].astype(BF16), g_ffn, w_r, b_r, ltri, tm=tm_mix)
        gid = xe[..., D_MODEL + GID_LANE].astype(jnp.int32).reshape(t)
        rank = xe[..., D_MODEL + RANK_LANE].astype(jnp.int32).reshape(t)
        cnt = counts[0, N_EXPERTS:N_EXPERTS + N_GROUPS].astype(jnp.int32)

        tile = 512
        n_rows = (t // tile + N_GROUPS - 1) * tile
        xs = _dispatch(gid, rank, cnt, xe.reshape(t, XE_WIDTH), tm=1024, tile=tile, n_rows=n_rows)

        def by_group(w):
            return w.reshape(N_GROUPS, EXPERTS_PER_GROUP, d, D_EXPERT).transpose(0, 2, 1, 3).reshape(
                N_GROUPS, d, EXPERTS_PER_GROUP * D_EXPERT)
        w_gu = jnp.concatenate([by_group(w_gate[i]), by_group(w_up[i])], axis=2).astype(BF16)
        w_dn = w_down[i].reshape(N_GROUPS, EXPERTS_PER_GROUP * D_EXPERT, d).astype(BF16)
        x2s = _experts(cnt, xs, g_ffn, w_gu, w_dn, tm=tile)

        x = _final(gid, rank, cnt, p[i].reshape(t, PLE_DIM), ple_norm_g[i][None, :],
                   w_pg[i].astype(BF16), w_ple[i].astype(BF16), x2s,
                   tm=512, tile=tile).reshape(b, s, d)
    return x
```

```python
import functools

import numpy as np
import jax
import jax.numpy as jnp
from jax import lax
from jax.experimental import pallas as pl
from jax.experimental.pallas import tpu as pltpu

D_MODEL = 1024
N_HEADS = 8
HEAD_DIM = 64
ATT_WIDTH = N_HEADS * HEAD_DIM
CONV_WIDTH = 512
CONV_K = 3
N_GROUPS = 4
EXPERTS_PER_GROUP = 4
N_EXPERTS = 16
D_EXPERT = 256
PLE_DIM = 256
EPS = 1e-6
NEG_INF = -1e30

LANES = 128
C_PIECES = 3
QC_LANE = HEAD_DIM
KC_LANE = HEAD_DIM + C_PIECES
V_ROWS = HEAD_DIM + 16
LOG2E = 1.4426950408889634
GID_LANE = N_EXPERTS
RANK_LANE = N_EXPERTS + 1
XE_WIDTH = D_MODEL + LANES
VMEM_LIMIT = 56 * 1024 * 1024

F32 = jnp.float32
BF16 = jnp.bfloat16


def _rms(xf, g):
    return xf * lax.rsqrt(jnp.mean(xf * xf, axis=-1, keepdims=True) + EPS) * g


def _log_sigmoid(z):
    return jnp.minimum(z, 0.0) - jnp.log1p(jnp.exp(-jnp.abs(z)))


def _bdot(a, b):
    return jnp.dot(a, b, preferred_element_type=F32)


def _qkv_kernel(x_ref, g_ref, w_ref, bf_ref, qg_ref, kg_ref, pq_ref, pk_ref,
                qt_ref, k_ref, vt_ref, carry_ref, *, tm):
    st = pl.program_id(1)

    @pl.when(st == 0)
    def _():
        carry_ref[...] = jnp.zeros_like(carry_ref)

    h = _rms(x_ref[0], g_ref[...]).astype(BF16)
    proj = _bdot(h, w_ref[...])

    ext_row = lax.broadcasted_iota(jnp.int32, (V_ROWS - HEAD_DIM, tm), 0)
    v_ext = jnp.where(ext_row == 0, 1.0, 0.0)
    for j in range(N_HEADS // 2):
        pair_t = proj[:, 2 * ATT_WIDTH + LANES * j: 2 * ATT_WIDTH + LANES * (j + 1)].T
        for hh in range(2):
            vt_ref[0, 2 * j + hh] = jnp.concatenate(
                [pair_t[HEAD_DIM * hh: HEAD_DIM * (hh + 1)], v_ext], axis=0).astype(BF16)

    lane = lax.broadcasted_iota(jnp.int32, (tm, LANES), 1)
    row = lax.broadcasted_iota(jnp.int32, (tm, LANES), 0)

    f3 = proj[:, 3 * ATT_WIDTH:3 * ATT_WIDTH + LANES] + bf_ref[...]
    c = jnp.where(lane < C_PIECES * N_HEADS, _log_sigmoid(f3), 0.0)
    sh = 1
    while sh < tm:
        c = c + jnp.where(row >= sh, pltpu.roll(c, sh, axis=0), 0.0)
        sh *= 2
    c = c + carry_ref[...]
    carry_ref[...] = c[tm - 1:tm, :]
    c = c * LOG2E
    hi = c.astype(BF16).astype(F32)
    r1 = c - hi
    mid = r1.astype(BF16).astype(F32)
    lo = r1 - mid
    piece = jnp.where(lane < N_HEADS, hi, jnp.where(lane < 2 * N_HEADS, mid, lo)).astype(BF16)
    qc = _bdot(piece, pq_ref[...])
    kc = _bdot(piece, pk_ref[...])

    q_ones = jnp.where((lane >= KC_LANE) & (lane < KC_LANE + C_PIECES), 1.0, 0.0)
    k_ones = jnp.where((lane >= QC_LANE) & (lane < QC_LANE + C_PIECES), 1.0, 0.0)
    low = lane < HEAD_DIM

    def heads(base, gain_ref, cmat, ones, out_ref, transposed):
        gain = gain_ref[...]
        for j in range(N_HEADS // 2):
            pair = proj[:, base + LANES * j: base + LANES * (j + 1)]
            sq = pair * pair
            ss_lo = jnp.sum(jnp.where(low, sq, 0.0), axis=1, keepdims=True)
            ss_hi = jnp.sum(jnp.where(low, 0.0, sq), axis=1, keepdims=True)
            n_lo = pair * lax.rsqrt(ss_lo * (1.0 / HEAD_DIM) + EPS) * gain
            n_hi = pltpu.roll(pair, HEAD_DIM, axis=1) * lax.rsqrt(ss_hi * (1.0 / HEAD_DIM) + EPS) * gain
            for hh, nrm in ((2 * j, n_lo), (2 * j + 1, n_hi)):
                aug = jnp.where(low, nrm, 0.0) + cmat[:, LANES * hh: LANES * (hh + 1)] + ones
                out_ref[0, hh] = (aug.T if transposed else aug).astype(BF16)

    heads(0, qg_ref, qc, q_ones, qt_ref, True)
    heads(ATT_WIDTH, kg_ref, kc, k_ones, k_ref, False)


def _qkv_proj(x, g, w, bf3, qg, kg, pq, pk, *, tm):
    b, s, d = x.shape
    grid = (b, s // tm)
    full = lambda shape: pl.BlockSpec(shape, lambda i, j: (0,) * len(shape))
    aug_spec = pl.BlockSpec((1, N_HEADS, tm, LANES), lambda i, j: (i, 0, j, 0))
    return pl.pallas_call(
        functools.partial(_qkv_kernel, tm=tm),
        grid=grid,
        in_specs=[
            pl.BlockSpec((1, tm, d), lambda i, j: (i, j, 0)),
            full(g.shape), full(w.shape), full(bf3.shape), full(qg.shape), full(kg.shape),
            full(pq.shape), full(pk.shape),
        ],
        out_specs=[pl.BlockSpec((1, N_HEADS, LANES, tm), lambda i, j: (i, 0, 0, j)), aug_spec,
                   pl.BlockSpec((1, N_HEADS, V_ROWS, tm), lambda i, j: (i, 0, 0, j))],
        out_shape=[
            jax.ShapeDtypeStruct((b, N_HEADS, LANES, s), BF16),
            jax.ShapeDtypeStruct((b, N_HEADS, s, LANES), BF16),
            jax.ShapeDtypeStruct((b, N_HEADS, V_ROWS, s), BF16),
        ],
        scratch_shapes=[pltpu.VMEM((1, LANES), F32)],
        compiler_params=pltpu.CompilerParams(
            dimension_semantics=("arbitrary", "arbitrary"), vmem_limit_bytes=VMEM_LIMIT),
        name="qkv_proj",
    )(x, g, w, bf3, qg, kg, pq, pk)


def _attn_kernel(qt_ref, k_ref, vt_ref, bias_ref, o_ref, sa_ref, sb_ref, *, tq, tk):
    qi = pl.program_id(2)
    last = 2 * qi + 1

    def produce(j, s_ref):
        jc = jnp.minimum(j, last)
        start = pl.multiple_of(jc * tk, tk)
        bias = bias_ref[jnp.clip(jc - last + 2, 0, 2)]
        for hh in range(2):
            s_ref[hh] = _bdot(k_ref[0, hh, pl.ds(start, tk), :], qt_ref[0, hh]) + bias

    def consume(j, s_ref, carry):
        start = pl.multiple_of(j * tk, tk)
        out = []
        for hh in range(2):
            m, acc = carry[hh]
            m_new = jnp.maximum(m, jnp.max(s_ref[hh], axis=0, keepdims=True))
            alpha = jnp.exp2(m - m_new)
            p = jnp.exp2(s_ref[hh] - m_new).astype(BF16)
            acc = alpha * acc + _bdot(vt_ref[0, hh, :, pl.ds(start, tk)], p)
            out.append((m_new, acc))
        return tuple(out)

    def pair(r, carry):
        produce(2 * r + 1, sb_ref)
        carry = consume(2 * r, sa_ref, carry)
        produce(2 * r + 2, sa_ref)
        return consume(2 * r + 1, sb_ref, carry)

    produce(0, sa_ref)
    init = tuple((jnp.full((1, tq), NEG_INF, F32), jnp.zeros((V_ROWS, tq), F32)) for _ in range(2))
    carry = lax.fori_loop(0, qi + 1, pair, init)
    y_t = jnp.concatenate([acc[0:HEAD_DIM] / acc[HEAD_DIM:HEAD_DIM + 1] for _, acc in carry], axis=0)
    o_ref[0] = y_t.T.astype(BF16)


def _attention(qt_aug, k_aug, vt, *, tq):
    b, nh, s, _ = k_aug.shape
    tk = tq // 2
    future = np.arange(tq)[:, None] > np.arange(tq)[None, :]
    masks = np.where(future, np.float32(NEG_INF), np.float32(0.0)).reshape(2, tk, tq)
    bias = jnp.asarray(np.concatenate([np.zeros((1, tk, tq), np.float32), masks]))
    grid = (b, nh // 2, s // tq)
    return pl.pallas_call(
        functools.partial(_attn_kernel, tq=tq, tk=tk),
        grid=grid,
        in_specs=[
            pl.BlockSpec((1, 2, LANES, tq), lambda i, h, j: (i, h, 0, j)),
            pl.BlockSpec((1, 2, s, LANES), lambda i, h, j: (i, h, 0, 0)),
            pl.BlockSpec((1, 2, V_ROWS, s), lambda i, h, j: (i, h, 0, 0)),
            pl.BlockSpec((3, tk, tq), lambda i, h, j: (0, 0, 0)),
        ],
        out_specs=pl.BlockSpec((1, tq, LANES), lambda i, h, j: (i, j, h)),
        out_shape=jax.ShapeDtypeStruct((b, s, ATT_WIDTH), BF16),
        scratch_shapes=[pltpu.VMEM((2, tk, tq), F32)] * 2,
        compiler_params=pltpu.CompilerParams(
            dimension_semantics=("arbitrary", "arbitrary", "arbitrary"),
            vmem_limit_bytes=VMEM_LIMIT),
        name="fox_attention",
    )(qt_aug, k_aug, vt, bias)


def _mix_kernel(x_ref, ya_ref, g_ref, wcg_ref, cw_ref, wa_ref, wb_ref, wo_ref, g2_ref,
                wr_ref, br_ref, ltri_ref, xe_ref, cnt_ref, carry_ref, *, tm):
    st = pl.program_id(1)

    @pl.when(st == 0)
    def _():
        carry_ref[...] = jnp.zeros_like(carry_ref)

    @pl.when((st == 0) & (pl.program_id(0) == 0))
    def _():
        cnt_ref[...] = jnp.zeros_like(cnt_ref)

    x = x_ref[0]
    h = _rms(x, g_ref[...]).astype(BF16)
    pc = _bdot(h, wcg_ref[...])
    cw = CONV_WIDTH
    cb = pc[:, 0:cw]
    prod = pc[:, cw:2 * cw] * pc[:, 2 * cw:3 * cw]
    ga = pc[:, 3 * cw:3 * cw + D_MODEL]
    gb = pc[:, 3 * cw + D_MODEL:3 * cw + 2 * D_MODEL]

    prev = carry_ref[...]
    crow = lax.broadcasted_iota(jnp.int32, (tm, cw), 0)
    m1 = jnp.where(crow == 0, prev[7:8, :], pltpu.roll(prod, 1, axis=0))
    m2 = jnp.where(crow == 0, prev[6:7, :],
                   jnp.where(crow == 1, prev[7:8, :], pltpu.roll(prod, 2, axis=0)))
    carry_ref[...] = prod[tm - 8:tm, :]
    w = cw_ref[...]
    y_conv = cb * (w[0:1, :] * m2 + w[1:2, :] * m1 + w[2:3, :] * prod)

    a = _bdot(ya_ref[0], wa_ref[...])
    bb = _bdot(y_conv.astype(BF16), wb_ref[...])
    merged = jax.nn.sigmoid(ga) * a + jax.nn.sigmoid(gb) * bb
    x1 = x + _bdot(merged.astype(BF16), wo_ref[...])
    xe_ref[0, :, 0:D_MODEL] = x1

    h2 = _rms(x1, g2_ref[...]).astype(BF16)

    logits = _bdot(h2, wr_ref[...]) + br_ref[...]
    lane = lax.broadcasted_iota(jnp.int32, (tm, LANES), 1)
    lanef = lane.astype(F32)
    is_g = (lane >= N_EXPERTS) & (lane < N_EXPERTS + N_GROUPS)
    gl = jnp.where(is_g, logits, NEG_INF)
    gmax = jnp.max(gl, axis=1, keepdims=True)
    gsum = jnp.sum(jnp.exp(gl - gmax), axis=1, keepdims=True)
    g_val = 1.0 / gsum
    g_lane = jnp.min(jnp.where(gl == gmax, lanef, float(LANES)), axis=1, keepdims=True)
    e_lo = (g_lane - float(N_EXPERTS)) * float(EXPERTS_PER_GROUP)
    in_grp = (lanef >= e_lo) & (lanef < e_lo + float(EXPERTS_PER_GROUP))
    el = jnp.where(in_grp, logits, NEG_INF)
    t1 = jnp.max(el, axis=1, keepdims=True)
    i1 = jnp.min(jnp.where(el == t1, lanef, float(LANES)), axis=1, keepdims=True)
    el2 = jnp.where(lanef == i1, NEG_INF, el)
    t2 = jnp.max(el2, axis=1, keepdims=True)
    i2 = jnp.min(jnp.where(el2 == t2, lanef, float(LANES)), axis=1, keepdims=True)
    e2 = jnp.exp(t2 - t1)
    w1 = g_val / (1.0 + e2)
    w2 = g_val * e2 / (1.0 + e2)
    comb = jnp.where(lanef == i1, w1, 0.0) + jnp.where(lanef == i2, w2, 0.0)

    onehot = jnp.where(is_g & (lanef == g_lane), 1.0, 0.0)
    running = cnt_ref[...]
    prefix = _bdot(ltri_ref[...], onehot.astype(BF16)) + running
    rank = jnp.sum(onehot * prefix, axis=1, keepdims=True)
    cnt_ref[...] = running + jnp.sum(onehot, axis=0, keepdims=True)
    route = (comb + jnp.where(lane == GID_LANE, g_lane - float(N_EXPERTS), 0.0)
             + jnp.where(lane == RANK_LANE, rank, 0.0))
    xe_ref[0, :, D_MODEL:XE_WIDTH] = route


def _mix(x, y_att, g, wcg, conv_w, wa, wb, wo, g2, wr, br, ltri, *, tm):
    b, s, d = x.shape
    grid = (b, s // tm)
    full = lambda a: pl.BlockSpec(a.shape, lambda i, j: (0,) * a.ndim)
    row = lambda width: pl.BlockSpec((1, tm, width), lambda i, j: (i, j, 0))
    return pl.pallas_call(
        functools.partial(_mix_kernel, tm=tm),
        grid=grid,
        in_specs=[row(d), row(ATT_WIDTH), full(g), full(wcg), full(conv_w), full(wa), full(wb),
                  full(wo), full(g2), full(wr), full(br), full(ltri)],
        out_specs=[row(XE_WIDTH), pl.BlockSpec((1, LANES), lambda i, j: (0, 0))],
        out_shape=[
            jax.ShapeDtypeStruct((b, s, XE_WIDTH), F32),
            jax.ShapeDtypeStruct((1, LANES), F32),
        ],
        scratch_shapes=[pltpu.VMEM((8, CONV_WIDTH), F32)],
        compiler_params=pltpu.CompilerParams(
            dimension_semantics=("arbitrary", "arbitrary"), vmem_limit_bytes=VMEM_LIMIT),
        name="mix",
    )(x, y_att, g, wcg, conv_w, wa, wb, wo, g2, wr, br, ltri)


def _group_offsets(cnt_ref, tile):
    offs = [jnp.int32(0)]
    for g in range(N_GROUPS - 1):
        offs.append(offs[-1] + pl.cdiv(cnt_ref[g], tile) * tile)
    return offs


def _sorted_pos(gid_ref, rank_ref, offs, t):
    g = gid_ref[t]
    base = jnp.where(g == 0, offs[0], jnp.where(g == 1, offs[1], jnp.where(g == 2, offs[2], offs[3])))
    return base + rank_ref[t]


def _dispatch_kernel(gid_ref, rank_ref, cnt_ref, xe_ref, xs_ref, zero_ref, sem, *, tm, tile):
    i = pl.program_id(0)
    offs = _group_offsets(cnt_ref, tile)

    @pl.when(i == 0)
    def _():
        zero_ref[...] = jnp.zeros_like(zero_ref)
        for g in range(N_GROUPS):
            @pl.when(cnt_ref[g] > 0)
            def _():
                last = offs[g] + (pl.cdiv(cnt_ref[g], tile) - 1) * tile
                cp = pltpu.make_async_copy(zero_ref, xs_ref.at[pl.ds(last, tile)], sem)
                cp.start()
                cp.wait()

    def row_copy(r, pos):
        return pltpu.make_async_copy(xe_ref.at[pl.ds(r, 1)], xs_ref.at[pl.ds(pos, 1)], sem)

    def issue(r, c):
        row_copy(r, _sorted_pos(gid_ref, rank_ref, offs, i * tm + r)).start()
        return c

    def drain(r, c):
        row_copy(0, 0).wait()
        return c

    lax.fori_loop(0, tm, issue, 0, unroll=8)
    lax.fori_loop(0, tm, drain, 0, unroll=8)


def _dispatch(gid, rank, cnt, xe, *, tm, tile, n_rows):
    t, w = xe.shape
    return pl.pallas_call(
        functools.partial(_dispatch_kernel, tm=tm, tile=tile),
        grid_spec=pltpu.PrefetchScalarGridSpec(
            num_scalar_prefetch=3,
            grid=(t // tm,),
            in_specs=[pl.BlockSpec((tm, w), lambda i, *_: (i, 0))],
            out_specs=pl.BlockSpec(memory_space=pl.ANY),
            scratch_shapes=[pltpu.VMEM((tile, w), F32), pltpu.SemaphoreType.DMA(())],
        ),
        out_shape=jax.ShapeDtypeStruct((n_rows, w), F32),
        compiler_params=pltpu.CompilerParams(
            dimension_semantics=("arbitrary",), vmem_limit_bytes=VMEM_LIMIT),
        name="dispatch",
    )(gid, rank, cnt, xe)


def _tile_group(i, cnt_ref, tile):
    end = jnp.int32(0)
    g = jnp.int32(0)
    for k in range(N_GROUPS):
        end = end + pl.cdiv(cnt_ref[k], tile)
        if k < N_GROUPS - 1:
            g = g + (i >= end).astype(jnp.int32)
    return g, i < end


def _experts_kernel(cnt_ref, xs_ref, g2_ref, wgu_ref, wd_ref, ys_ref, *, tm):
    i = pl.program_id(0)
    g, used = _tile_group(i, cnt_ref, tm)

    @pl.when(used)
    def _():
        x1 = xs_ref[:, 0:D_MODEL]
        route = xs_ref[:, D_MODEL:XE_WIDTH]
        h2 = _rms(x1, g2_ref[...]).astype(BF16)
        gu = _bdot(h2, wgu_ref[0])
        width = EXPERTS_PER_GROUP * D_EXPERT
        a = gu[:, 0:width]
        u = gu[:, width:2 * width]
        hid = (a * jax.nn.sigmoid(a)) * u
        lane = lax.broadcasted_iota(jnp.int32, (tm, LANES), 1)
        parts = []
        for e in range(EXPERTS_PER_GROUP):
            ce = jnp.sum(jnp.where(lane == g * EXPERTS_PER_GROUP + e, route, 0.0), axis=1, keepdims=True)
            parts.append((hid[:, e * D_EXPERT:(e + 1) * D_EXPERT] * ce).astype(BF16))
        ys_ref[...] = x1 + _bdot(jnp.concatenate(parts, axis=1), wd_ref[0])

    @pl.when(jnp.logical_not(used))
    def _():
        ys_ref[...] = jnp.zeros_like(ys_ref)


def _experts(cnt, xs, g2, wgu, wd, *, tm):
    n_rows, w = xs.shape
    d = D_MODEL

    def grp(i, cnt_ref):
        return _tile_group(i, cnt_ref, tm)[0]

    def xs_map(i, cnt_ref):
        total = sum(pl.cdiv(cnt_ref[k], tm) for k in range(N_GROUPS))
        return (jnp.minimum(i, total - 1), 0)

    return pl.pallas_call(
        functools.partial(_experts_kernel, tm=tm),
        grid_spec=pltpu.PrefetchScalarGridSpec(
            num_scalar_prefetch=1,
            grid=(n_rows // tm,),
            in_specs=[
                pl.BlockSpec((tm, w), xs_map),
                pl.BlockSpec(g2.shape, lambda i, c: (0, 0)),
                pl.BlockSpec((1, d, 2 * EXPERTS_PER_GROUP * D_EXPERT), lambda i, c: (grp(i, c), 0, 0)),
                pl.BlockSpec((1, EXPERTS_PER_GROUP * D_EXPERT, d), lambda i, c: (grp(i, c), 0, 0)),
            ],
            out_specs=pl.BlockSpec((tm, d), lambda i, c: (i, 0)),
        ),
        out_shape=jax.ShapeDtypeStruct((n_rows, d), F32),
        compiler_params=pltpu.CompilerParams(
            dimension_semantics=("arbitrary",), vmem_limit_bytes=VMEM_LIMIT),
        name="experts",
    )(cnt, xs, g2, wgu, wd)


def _final_kernel(gid_ref, rank_ref, cnt_ref, p_ref, g3_ref, wpg_ref, wple_ref, ys_ref,
                  o_ref, buf_ref, sem, *, tm, tile):
    i = pl.program_id(0)
    n = pl.num_programs(0)
    offs = _group_offsets(cnt_ref, tile)
    slot = i % 2

    def row_copy(s, r, pos):
        return pltpu.make_async_copy(ys_ref.at[pl.ds(pos, 1)], buf_ref.at[s, pl.ds(r, 1)], sem.at[s])

    def issue(tile_idx, s):
        def body(r, c):
            row_copy(s, r, _sorted_pos(gid_ref, rank_ref, offs, tile_idx * tm + r)).start()
            return c
        lax.fori_loop(0, tm, body, 0, unroll=8)

    @pl.when(i == 0)
    def _():
        issue(0, 0)

    @pl.when(i + 1 < n)
    def _():
        issue(i + 1, 1 - slot)

    def drain(r, c):
        row_copy(slot, 0, 0).wait()
        return c

    lax.fori_loop(0, tm, drain, 0, unroll=8)

    x2 = buf_ref[slot]
    h3 = _rms(x2, g3_ref[...]).astype(BF16)
    gate = jax.nn.sigmoid(_bdot(h3, wpg_ref[...]))
    emb = _bdot(p_ref[...].astype(BF16), wple_ref[...])
    o_ref[...] = x2 + gate * emb


def _final(gid, rank, cnt, p, g3, wpg, wple, ys, *, tm, tile):
    t, d = p.shape[0], ys.shape[1]
    full = lambda a: pl.BlockSpec(a.shape, lambda i, *_: (0,) * a.ndim)
    row = lambda width: pl.BlockSpec((tm, width), lambda i, *_: (i, 0))
    return pl.pallas_call(
        functools.partial(_final_kernel, tm=tm, tile=tile),
        grid_spec=pltpu.PrefetchScalarGridSpec(
            num_scalar_prefetch=3,
            grid=(t // tm,),
            in_specs=[row(PLE_DIM), full(g3), full(wpg), full(wple),
                      pl.BlockSpec(memory_space=pl.ANY)],
            out_specs=row(d),
            scratch_shapes=[pltpu.VMEM((2, tm, d), F32), pltpu.SemaphoreType.DMA((2,))],
        ),
        out_shape=jax.ShapeDtypeStruct((t, d), F32),
        compiler_params=pltpu.CompilerParams(
            dimension_semantics=("arbitrary",), vmem_limit_bytes=VMEM_LIMIT),
        name="final",
    )(gid, rank, cnt, p, g3, wpg, wple, ys)


def _c_select_matrices():
    pq = np.zeros((LANES, N_HEADS * LANES), np.float32)
    pk = np.zeros((LANES, N_HEADS * LANES), np.float32)
    for idx in range(C_PIECES):
        for h in range(N_HEADS):
            pq[idx * N_HEADS + h, h * LANES + QC_LANE + idx] = 1.0
            pk[idx * N_HEADS + h, h * LANES + KC_LANE + idx] = -1.0
    return jnp.asarray(pq, BF16), jnp.asarray(pk, BF16)


def kernel(x, p, attn_norm_g, w_in, b_f, q_norm_g, k_norm_g, conv_w, w_out_att, w_out_conv, w_o,
           ffn_norm_g, w_rg, b_rg, w_re, b_re, w_gate, w_up, w_down, ple_norm_g, w_pg, w_ple):
    b, s, d = x.shape
    t = b * s
    aw = ATT_WIDTH
    for i in range(w_in.shape[0]):
        wi = w_in[i]
        wf = wi[:, 3 * aw:3 * aw + N_HEADS]
        w_qkvf = jnp.concatenate(
            [wi[:, :3 * aw], wf, wf, wf, jnp.zeros((d, LANES - C_PIECES * N_HEADS), F32)],
            axis=1).astype(BF16)
        w_cg = wi[:, 3 * aw + N_HEADS:].astype(BF16)
        bf3 = jnp.concatenate([b_f[i]] * C_PIECES + [jnp.zeros((LANES - C_PIECES * N_HEADS,), F32)])[None, :]
        scale = HEAD_DIM ** -0.5 * LOG2E
        qg = jnp.tile(q_norm_g[i] * scale, 2)[None, :]
        kg = jnp.tile(k_norm_g[i], 2)[None, :]
        pq, pk = _c_select_matrices()
        qt_aug, k_aug, vt = _qkv_proj(x, attn_norm_g[i][None, :], w_qkvf, bf3, qg, kg, pq, pk, tm=512)
        y_att = _attention(qt_aug, k_aug, vt, tq=512)

        w_r = jnp.concatenate(
            [w_re[i], w_rg[i], jnp.zeros((d, LANES - N_EXPERTS - N_GROUPS), F32)], axis=1).astype(BF16)
        b_r = jnp.concatenate(
            [b_re[i], b_rg[i], jnp.zeros((LANES - N_EXPERTS - N_GROUPS,), F32)])[None, :]
        tm_mix = 512
        ltri = jnp.asarray(np.tril(np.ones((tm_mix, tm_mix), np.float32), -1), BF16)
        g_ffn = ffn_norm_g[i][None, :]
        xe, counts = _mix(x, y_att, attn_norm_g[i][None, :], w_cg, conv_w[i],
                          w_out_att[i].astype(BF16), w_out_conv[i].astype(BF16),
                          w_o[i].astype(BF16), g_ffn, w_r, b_r, ltri, tm=tm_mix)
        gid = xe[..., D_MODEL + GID_LANE].astype(jnp.int32).reshape(t)
        rank = xe[..., D_MODEL + RANK_LANE].astype(jnp.int32).reshape(t)
        cnt = counts[0, N_EXPERTS:N_EXPERTS + N_GROUPS].astype(jnp.int32)

        tile = 512
        n_rows = (t // tile + N_GROUPS - 1) * tile
        xs = _dispatch(gid, rank, cnt, xe.reshape(t, XE_WIDTH), tm=1024, tile=tile, n_rows=n_rows)

        def by_group(w):
            return w.reshape(N_GROUPS, EXPERTS_PER_GROUP, d, D_EXPERT).transpose(0, 2, 1, 3).reshape(
                N_GROUPS, d, EXPERTS_PER_GROUP * D_EXPERT)
        w_gu = jnp.concatenate([by_group(w_gate[i]), by_group(w_up[i])], axis=2).astype(BF16)
        w_dn = w_down[i].reshape(N_GROUPS, EXPERTS_PER_GROUP * D_EXPERT, d).astype(BF16)
        x2s = _experts(cnt, xs, g_ffn, w_gu, w_dn, tm=tile)

        x = _final(gid, rank, cnt, p[i].reshape(t, PLE_DIM), ple_norm_g[i][None, :],
                   w_pg[i].astype(BF16), w_ple[i].astype(BF16), x2s,
                   tm=512, tile=tile).reshape(b, s, d)
    return x
```

```python
import functools

import numpy as np
import jax
import jax.numpy as jnp
from jax import lax
from jax.experimental import pallas as pl
from jax.experimental.pallas import tpu as pltpu

D_MODEL = 1024
N_HEADS = 8
HEAD_DIM = 64
ATT_WIDTH = N_HEADS * HEAD_DIM
CONV_WIDTH = 512
CONV_K = 3
N_GROUPS = 4
EXPERTS_PER_GROUP = 4
N_EXPERTS = 16
D_EXPERT = 256
GROUP_WIDTH = EXPERTS_PER_GROUP * D_EXPERT
PLE_DIM = 256
EPS = 1e-6
NEG_INF = -1e30
LOG2E = 1.4426950408889634

LANES = 128
BF16_ROWS = 16
C_PIECES = 3
QC_LANE = HEAD_DIM
KC_LANE = HEAD_DIM + C_PIECES
V_ROWS = HEAD_DIM + BF16_ROWS
GID_LANE = N_EXPERTS
ROUTE_PITCH = 32
XE_WIDTH = D_MODEL + LANES
VMEM_LIMIT = 56 * 1024 * 1024

TM_QKV = 512
TQ_ATT = 512
TM_MIX = 512
TM_MOE = TM_MIX
MOE_SORTED = -(-(TM_MOE + N_GROUPS * BF16_ROWS) // LANES) * LANES
MOE_CHUNK = 160
MOE_ROWS = MOE_SORTED + MOE_CHUNK

F32 = jnp.float32
BF16 = jnp.bfloat16


def _rms(xf, g):
    return xf * lax.rsqrt(jnp.mean(xf * xf, axis=-1, keepdims=True) + EPS) * g


def _log_sigmoid(z):
    return jnp.minimum(z, 0.0) - jnp.log1p(jnp.exp(-jnp.abs(z)))


def _bdot(a, b):
    return jnp.dot(a, b, preferred_element_type=F32)


def _bdot_nt(a, b):
    return lax.dot_general(a, b, (((1,), (1,)), ((), ())), preferred_element_type=F32)


def _bf16_pieces(x):
    hi = x.astype(BF16)
    r1 = x - hi.astype(F32)
    mid = r1.astype(BF16)
    lo = (r1 - mid.astype(F32)).astype(BF16)
    return hi, mid, lo


def _qkv_kernel(x_ref, g_ref, w_ref, bf_ref, qg_ref, kg_ref, pq_ref, pk_ref,
                qt_ref, k_ref, vt_ref, carry_ref, *, tm):
    st = pl.program_id(1)

    @pl.when(st == 0)
    def _():
        carry_ref[...] = jnp.zeros_like(carry_ref)

    h = _rms(x_ref[0], g_ref[...]).astype(BF16)
    proj = _bdot(h, w_ref[...])

    ext_row = lax.broadcasted_iota(jnp.int32, (V_ROWS - HEAD_DIM, tm), 0)
    v_ext = jnp.where(ext_row == 0, 1.0, 0.0)
    for j in range(N_HEADS // 2):
        pair_t = proj[:, 2 * ATT_WIDTH + LANES * j: 2 * ATT_WIDTH + LANES * (j + 1)].T
        for hh in range(2):
            vt_ref[0, 2 * j + hh] = jnp.concatenate(
                [pair_t[HEAD_DIM * hh: HEAD_DIM * (hh + 1)], v_ext], axis=0).astype(BF16)

    lane = lax.broadcasted_iota(jnp.int32, (tm, LANES), 1)
    row = lax.broadcasted_iota(jnp.int32, (tm, LANES), 0)

    f3 = proj[:, 3 * ATT_WIDTH:3 * ATT_WIDTH + LANES] + bf_ref[...]
    c = jnp.where(lane < C_PIECES * N_HEADS, _log_sigmoid(f3), 0.0)
    sh = 1
    while sh < tm:
        c = c + jnp.where(row >= sh, pltpu.roll(c, sh, axis=0), 0.0)
        sh *= 2
    c = c + carry_ref[...]
    carry_ref[...] = c[tm - 1:tm, :]
    hi, mid, lo = _bf16_pieces(c * LOG2E)
    piece = jnp.where(lane < N_HEADS, hi, jnp.where(lane < 2 * N_HEADS, mid, lo))
    qc = _bdot(piece, pq_ref[...])
    kc = _bdot(piece, pk_ref[...])

    q_ones = jnp.where((lane >= KC_LANE) & (lane < KC_LANE + C_PIECES), 1.0, 0.0)
    k_ones = jnp.where((lane >= QC_LANE) & (lane < QC_LANE + C_PIECES), 1.0, 0.0)
    low = lane < HEAD_DIM

    def heads(base, gain_ref, cmat, ones, out_ref, transposed):
        gain = gain_ref[...]
        for j in range(N_HEADS // 2):
            pair = proj[:, base + LANES * j: base + LANES * (j + 1)]
            sq = pair * pair
            ss_lo = jnp.sum(jnp.where(low, sq, 0.0), axis=1, keepdims=True)
            ss_hi = jnp.sum(jnp.where(low, 0.0, sq), axis=1, keepdims=True)
            n_lo = pair * lax.rsqrt(ss_lo * (1.0 / HEAD_DIM) + EPS) * gain
            n_hi = pltpu.roll(pair, HEAD_DIM, axis=1) * lax.rsqrt(ss_hi * (1.0 / HEAD_DIM) + EPS) * gain
            for hh, nrm in ((2 * j, n_lo), (2 * j + 1, n_hi)):
                aug = jnp.where(low, nrm, 0.0) + cmat[:, LANES * hh: LANES * (hh + 1)] + ones
                out_ref[0, hh] = (aug.T if transposed else aug).astype(BF16)

    heads(0, qg_ref, qc, q_ones, qt_ref, True)
    heads(ATT_WIDTH, kg_ref, kc, k_ones, k_ref, False)


def _qkv_proj(x, g, w, bf3, qg, kg, pq, pk, *, tm):
    b, s, d = x.shape
    grid = (b, s // tm)
    full = lambda shape: pl.BlockSpec(shape, lambda i, j: (0,) * len(shape))
    return pl.pallas_call(
        functools.partial(_qkv_kernel, tm=tm),
        grid=grid,
        in_specs=[
            pl.BlockSpec((1, tm, d), lambda i, j: (i, j, 0)),
            full(g.shape), full(w.shape), full(bf3.shape), full(qg.shape), full(kg.shape),
            full(pq.shape), full(pk.shape),
        ],
        out_specs=[pl.BlockSpec((1, N_HEADS, LANES, tm), lambda i, j: (i, 0, 0, j)),
                   pl.BlockSpec((1, N_HEADS, tm, LANES), lambda i, j: (i, 0, j, 0)),
                   pl.BlockSpec((1, N_HEADS, V_ROWS, tm), lambda i, j: (i, 0, 0, j))],
        out_shape=[
            jax.ShapeDtypeStruct((b, N_HEADS, LANES, s), BF16),
            jax.ShapeDtypeStruct((b, N_HEADS, s, LANES), BF16),
            jax.ShapeDtypeStruct((b, N_HEADS, V_ROWS, s), BF16),
        ],
        scratch_shapes=[pltpu.VMEM((1, LANES), F32)],
        compiler_params=pltpu.CompilerParams(
            dimension_semantics=("arbitrary", "arbitrary"), vmem_limit_bytes=VMEM_LIMIT),
        name="qkv_proj",
    )(x, g, w, bf3, qg, kg, pq, pk)


def _attn_kernel(qt_ref, k_ref, vt_ref, bias_ref, o_ref, sa_ref, sb_ref, *, tq, tk):
    qi = pl.program_id(2)
    last = 2 * qi + 1

    def produce(j, s_ref):
        jc = jnp.minimum(j, last)
        start = pl.multiple_of(jc * tk, tk)
        bias = bias_ref[jnp.clip(jc - last + 2, 0, 2)]
        for hh in range(2):
            s_ref[hh] = _bdot(k_ref[0, hh, pl.ds(start, tk), :], qt_ref[0, hh]) + bias

    def consume(j, s_ref, carry):
        start = pl.multiple_of(j * tk, tk)
        out = []
        for hh in range(2):
            m, acc = carry[hh]
            m_new = jnp.maximum(m, jnp.max(s_ref[hh], axis=0, keepdims=True))
            alpha = jnp.exp2(m - m_new)
            p = jnp.exp2(s_ref[hh] - m_new).astype(BF16)
            acc = alpha * acc + _bdot(vt_ref[0, hh, :, pl.ds(start, tk)], p)
            out.append((m_new, acc))
        return tuple(out)

    def pair(r, carry):
        produce(2 * r + 1, sb_ref)
        carry = consume(2 * r, sa_ref, carry)
        produce(2 * r + 2, sa_ref)
        return consume(2 * r + 1, sb_ref, carry)

    produce(0, sa_ref)
    init = tuple((jnp.full((1, tq), NEG_INF, F32), jnp.zeros((V_ROWS, tq), F32)) for _ in range(2))
    carry = lax.fori_loop(0, qi + 1, pair, init)
    y_t = jnp.concatenate([acc[0:HEAD_DIM] / acc[HEAD_DIM:HEAD_DIM + 1] for _, acc in carry], axis=0)
    o_ref[0] = y_t.T.astype(BF16)


def _attention(qt_aug, k_aug, vt, *, tq):
    b, nh, s, _ = k_aug.shape
    tk = tq // 2
    future = np.arange(tq)[:, None] > np.arange(tq)[None, :]
    masks = np.where(future, np.float32(NEG_INF), np.float32(0.0)).reshape(2, tk, tq)
    bias = jnp.asarray(np.concatenate([np.zeros((1, tk, tq), np.float32), masks]))
    grid = (b, nh // 2, s // tq)
    return pl.pallas_call(
        functools.partial(_attn_kernel, tq=tq, tk=tk),
        grid=grid,
        in_specs=[
            pl.BlockSpec((1, 2, LANES, tq), lambda i, h, j: (i, h, 0, j)),
            pl.BlockSpec((1, 2, s, LANES), lambda i, h, j: (i, h, 0, 0)),
            pl.BlockSpec((1, 2, V_ROWS, s), lambda i, h, j: (i, h, 0, 0)),
            pl.BlockSpec((3, tk, tq), lambda i, h, j: (0, 0, 0)),
        ],
        out_specs=pl.BlockSpec((1, tq, LANES), lambda i, h, j: (i, j, h)),
        out_shape=jax.ShapeDtypeStruct((b, s, ATT_WIDTH), BF16),
        scratch_shapes=[pltpu.VMEM((2, tk, tq), F32)] * 2,
        compiler_params=pltpu.CompilerParams(
            dimension_semantics=("arbitrary", "arbitrary", "arbitrary"),
            vmem_limit_bytes=VMEM_LIMIT),
        name="fox_attention",
    )(qt_aug, k_aug, vt, bias)


def _mix_kernel(x_ref, ya_ref, g_ref, wcg_ref, cw_ref, wa_ref, wb_ref, wo_ref, g2_ref,
                wr_ref, br_ref, xe_ref, cnt_ref, carry_ref, *, tm):
    st = pl.program_id(1)

    @pl.when(st == 0)
    def _():
        carry_ref[...] = jnp.zeros_like(carry_ref)

    x = x_ref[0]
    h = _rms(x, g_ref[...]).astype(BF16)
    pc = _bdot(h, wcg_ref[...])
    cw = CONV_WIDTH
    cb = pc[:, 0:cw]
    prod = pc[:, cw:2 * cw] * pc[:, 2 * cw:3 * cw]
    ga = pc[:, 3 * cw:3 * cw + D_MODEL]
    gb = pc[:, 3 * cw + D_MODEL:3 * cw + 2 * D_MODEL]

    prev = carry_ref[...]
    crow = lax.broadcasted_iota(jnp.int32, (tm, cw), 0)
    m1 = jnp.where(crow == 0, prev[7:8, :], pltpu.roll(prod, 1, axis=0))
    m2 = jnp.where(crow == 0, prev[6:7, :],
                   jnp.where(crow == 1, prev[7:8, :], pltpu.roll(prod, 2, axis=0)))
    carry_ref[...] = prod[tm - 8:tm, :]
    w = cw_ref[...]
    y_conv = cb * (w[0:1, :] * m2 + w[1:2, :] * m1 + w[2:3, :] * prod)

    a = _bdot(ya_ref[0], wa_ref[...])
    bb = _bdot(y_conv.astype(BF16), wb_ref[...])
    merged = jax.nn.sigmoid(ga) * a + jax.nn.sigmoid(gb) * bb
    x1 = x + _bdot(merged.astype(BF16), wo_ref[...])
    xe_ref[0, :, 0:D_MODEL] = x1

    h2 = _rms(x1, g2_ref[...]).astype(BF16)

    logits = _bdot(h2, wr_ref[...]) + br_ref[...]
    lane = lax.broadcasted_iota(jnp.int32, (tm, LANES), 1)
    lanef = lane.astype(F32)
    is_g = (lane >= N_EXPERTS) & (lane < N_EXPERTS + N_GROUPS)
    gl = jnp.where(is_g, logits, NEG_INF)
    gmax = jnp.max(gl, axis=1, keepdims=True)
    gsum = jnp.sum(jnp.exp(gl - gmax), axis=1, keepdims=True)
    g_val = 1.0 / gsum
    g_lane = jnp.min(jnp.where(gl == gmax, lanef, float(LANES)), axis=1, keepdims=True)
    e_lo = (g_lane - float(N_EXPERTS)) * float(EXPERTS_PER_GROUP)
    in_grp = (lanef >= e_lo) & (lanef < e_lo + float(EXPERTS_PER_GROUP))
    el = jnp.where(in_grp, logits, NEG_INF)
    t1 = jnp.max(el, axis=1, keepdims=True)
    i1 = jnp.min(jnp.where(el == t1, lanef, float(LANES)), axis=1, keepdims=True)
    el2 = jnp.where(lanef == i1, NEG_INF, el)
    t2 = jnp.max(el2, axis=1, keepdims=True)
    i2 = jnp.min(jnp.where(el2 == t2, lanef, float(LANES)), axis=1, keepdims=True)
    e2 = jnp.exp(t2 - t1)
    w1 = g_val / (1.0 + e2)
    w2 = g_val * e2 / (1.0 + e2)
    comb = jnp.where(lanef == i1, w1, 0.0) + jnp.where(lanef == i2, w2, 0.0)

    cnt_ref[0] = jnp.sum(jnp.where(is_g & (lanef == g_lane), 1.0, 0.0), axis=0, keepdims=True)
    xe_ref[0, :, D_MODEL:XE_WIDTH] = comb + jnp.where(lane == GID_LANE, g_lane - float(N_EXPERTS), 0.0)


def _mix(x, y_att, g, wcg, conv_w, wa, wb, wo, g2, wr, br, *, tm):
    b, s, d = x.shape
    grid = (b, s // tm)
    full = lambda a: pl.BlockSpec(a.shape, lambda i, j: (0,) * a.ndim)
    row = lambda width: pl.BlockSpec((1, tm, width), lambda i, j: (i, j, 0))
    return pl.pallas_call(
        functools.partial(_mix_kernel, tm=tm),
        grid=grid,
        in_specs=[row(d), row(ATT_WIDTH), full(g), full(wcg), full(conv_w), full(wa), full(wb),
                  full(wo), full(g2), full(wr), full(br)],
        out_specs=[row(XE_WIDTH),
                   pl.BlockSpec((1, 1, LANES), lambda i, j: (i * (s // tm) + j, 0, 0))],
        out_shape=[
            jax.ShapeDtypeStruct((b, s, XE_WIDTH), F32),
            jax.ShapeDtypeStruct((b * s // tm, 1, LANES), F32),
        ],
        scratch_shapes=[pltpu.VMEM((8, CONV_WIDTH), F32)],
        compiler_params=pltpu.CompilerParams(
            dimension_semantics=("arbitrary", "arbitrary"), vmem_limit_bytes=VMEM_LIMIT),
        name="mix",
    )(x, y_att, g, wcg, conv_w, wa, wb, wo, g2, wr, br)


def _moe_kernel(tcnt_ref, xe_ref, p_ref, g2_ref, wg_ref, wu_ref, wd_ref, g3_ref, wpg_ref, wple_ref,
                o_ref, hs_ref, rs_ref, ys_ref, *, tm):
    i = pl.program_id(0)
    n = [tcnt_ref[i * N_GROUPS + k] for k in range(N_GROUPS)]
    starts = [jnp.int32(0)]
    for k in range(N_GROUPS - 1):
        starts.append(starts[-1] + pl.cdiv(n[k], BF16_ROWS) * BF16_ROWS)

    x1 = xe_ref[:, 0:D_MODEL]
    route = xe_ref[:, D_MODEL:XE_WIDTH]
    h2 = _rms(x1, g2_ref[...]).astype(BF16)

    lane = lax.broadcasted_iota(jnp.int32, (tm, LANES), 1)
    lanef = lane.astype(F32)
    gid = jnp.sum(jnp.where(lane == GID_LANE, route, 0.0), axis=1, keepdims=True)
    onehot = jnp.where((lanef == gid) & (lane < N_GROUPS), 1.0, 0.0)
    earlier = (lax.broadcasted_iota(jnp.int32, (tm, tm), 0)
               > lax.broadcasted_iota(jnp.int32, (tm, tm), 1)).astype(BF16)
    rank = jnp.sum(onehot * _bdot(earlier, onehot.astype(BF16)), axis=1, keepdims=True)
    base = starts[N_GROUPS - 1].astype(F32)
    for k in range(N_GROUPS - 2, -1, -1):
        base = jnp.where(gid == float(k), starts[k].astype(F32), base)
    pos = base + rank

    unsort = (pos == lax.broadcasted_iota(jnp.int32, (tm, MOE_SORTED), 1).astype(F32)).astype(BF16)
    digit_hi = jnp.floor(pos * (1.0 / 32.0))
    digits = jnp.where(lane == 0, digit_hi, jnp.where(lane == 1, pos - 32.0 * digit_hi, 0.0))
    sel_lane = lax.broadcasted_iota(jnp.int32, (BF16_ROWS, LANES), 1)
    sel = jnp.where(sel_lane == 0, 32.0, jnp.where(sel_lane == 1, 1.0, 0.0)).astype(BF16)
    pos_row = _bdot_nt(sel, digits.astype(BF16))[0:1, :]
    sort = (lax.broadcasted_iota(jnp.int32, (MOE_SORTED, tm), 0).astype(F32) == pos_row).astype(BF16)

    hs_ref[0:MOE_SORTED] = _bdot(sort, h2).astype(BF16)
    r_hi, r_mid, r_lo = _bf16_pieces(route)
    packed = (r_hi.astype(F32) + pltpu.roll(r_mid.astype(F32), ROUTE_PITCH, axis=1)
              + pltpu.roll(r_lo.astype(F32), 2 * ROUTE_PITCH, axis=1)).astype(BF16)
    rsorted = _bdot(sort, packed)
    rs_ref[0:MOE_SORTED] = (rsorted + pltpu.roll(rsorted, LANES - ROUTE_PITCH, axis=1)
                            + pltpu.roll(rsorted, LANES - 2 * ROUTE_PITCH, axis=1))
    hs_ref[MOE_SORTED:MOE_ROWS] = jnp.zeros((MOE_CHUNK, D_MODEL), BF16)
    rs_ref[MOE_SORTED:MOE_ROWS] = jnp.zeros((MOE_CHUNK, LANES), F32)
    ys_ref[...] = jnp.zeros_like(ys_ref)

    clane = lax.broadcasted_iota(jnp.int32, (MOE_CHUNK, LANES), 1)
    for g in range(N_GROUPS):
        def chunk(c, _, g=g):
            r0 = pl.multiple_of(starts[g] + c * MOE_CHUNK, BF16_ROWS)
            hrows = hs_ref[pl.ds(r0, MOE_CHUNK), :]
            rt = rs_ref[pl.ds(r0, MOE_CHUNK), :]
            parts = []
            for e in range(EXPERTS_PER_GROUP):
                ex = g * EXPERTS_PER_GROUP + e
                a = _bdot(hrows, wg_ref[ex])
                u = _bdot(hrows, wu_ref[ex])
                ce = jnp.sum(jnp.where(clane == ex, rt, 0.0), axis=1, keepdims=True)
                parts.append(((a * jax.nn.sigmoid(a)) * u * ce).astype(BF16))
            ys_ref[pl.ds(r0, MOE_CHUNK), :] = _bdot(jnp.concatenate(parts, axis=1), wd_ref[g]).astype(BF16)
            return 0
        lax.fori_loop(0, pl.cdiv(n[g], MOE_CHUNK), chunk, 0)

    x2 = x1 + _bdot(unsort, ys_ref[0:MOE_SORTED])
    h3 = _rms(x2, g3_ref[...]).astype(BF16)
    gate = jax.nn.sigmoid(_bdot(h3, wpg_ref[...]))
    emb = _bdot(p_ref[...].astype(BF16), wple_ref[...])
    o_ref[...] = x2 + gate * emb


def _moe(tcnt, xe, p, g2, wg, wu, wd, g3, wpg, wple, *, tm):
    t = xe.shape[0]
    d = D_MODEL
    once = lambda a: pl.BlockSpec(a.shape, lambda i, c: (0,) * a.ndim, pipeline_mode=pl.Buffered(1))
    row = lambda width: pl.BlockSpec((tm, width), lambda i, c: (i, 0))
    return pl.pallas_call(
        functools.partial(_moe_kernel, tm=tm),
        grid_spec=pltpu.PrefetchScalarGridSpec(
            num_scalar_prefetch=1,
            grid=(t // tm,),
            in_specs=[row(XE_WIDTH), row(PLE_DIM), once(g2), once(wg), once(wu), once(wd), once(g3),
                      once(wpg), once(wple)],
            out_specs=row(d),
            scratch_shapes=[pltpu.VMEM((MOE_ROWS, d), BF16), pltpu.VMEM((MOE_ROWS, LANES), F32),
                            pltpu.VMEM((MOE_ROWS, d), BF16)],
        ),
        out_shape=jax.ShapeDtypeStruct((t, d), F32),
        compiler_params=pltpu.CompilerParams(
            dimension_semantics=("arbitrary",), vmem_limit_bytes=VMEM_LIMIT),
        name="moe",
    )(tcnt, xe, p, g2, wg, wu, wd, g3, wpg, wple)


def _c_select_matrices():
    pq = np.zeros((LANES, N_HEADS * LANES), np.float32)
    pk = np.zeros((LANES, N_HEADS * LANES), np.float32)
    for idx in range(C_PIECES):
        for h in range(N_HEADS):
            pq[idx * N_HEADS + h, h * LANES + QC_LANE + idx] = 1.0
            pk[idx * N_HEADS + h, h * LANES + KC_LANE + idx] = -1.0
    return jnp.asarray(pq, BF16), jnp.asarray(pk, BF16)


def kernel(x, p, attn_norm_g, w_in, b_f, q_norm_g, k_norm_g, conv_w, w_out_att, w_out_conv, w_o,
           ffn_norm_g, w_rg, b_rg, w_re, b_re, w_gate, w_up, w_down, ple_norm_g, w_pg, w_ple):
    b, s, d = x.shape
    t = b * s
    aw = ATT_WIDTH
    for i in range(w_in.shape[0]):
        wi = w_in[i]
        wf = wi[:, 3 * aw:3 * aw + N_HEADS]
        w_qkvf = jnp.concatenate(
            [wi[:, :3 * aw], wf, wf, wf, jnp.zeros((d, LANES - C_PIECES * N_HEADS), F32)],
            axis=1).astype(BF16)
        w_cg = wi[:, 3 * aw + N_HEADS:].astype(BF16)
        bf3 = jnp.concatenate([b_f[i]] * C_PIECES + [jnp.zeros((LANES - C_PIECES * N_HEADS,), F32)])[None, :]
        scale = HEAD_DIM ** -0.5 * LOG2E
        qg = jnp.tile(q_norm_g[i] * scale, 2)[None, :]
        kg = jnp.tile(k_norm_g[i], 2)[None, :]
        pq, pk = _c_select_matrices()
        qt_aug, k_aug, vt = _qkv_proj(x, attn_norm_g[i][None, :], w_qkvf, bf3, qg, kg, pq, pk, tm=TM_QKV)
        y_att = _attention(qt_aug, k_aug, vt, tq=TQ_ATT)

        w_r = jnp.concatenate(
            [w_re[i], w_rg[i], jnp.zeros((d, LANES - N_EXPERTS - N_GROUPS), F32)], axis=1).astype(BF16)
        b_r = jnp.concatenate(
            [b_re[i], b_rg[i], jnp.zeros((LANES - N_EXPERTS - N_GROUPS,), F32)])[None, :]
        g_ffn = ffn_norm_g[i][None, :]
        xe, counts = _mix(x, y_att, attn_norm_g[i][None, :], w_cg, conv_w[i],
                          w_out_att[i].astype(BF16), w_out_conv[i].astype(BF16),
                          w_o[i].astype(BF16), g_ffn, w_r, b_r, tm=TM_MIX)
        tcnt = counts[:, 0, N_EXPERTS:N_EXPERTS + N_GROUPS].astype(jnp.int32).reshape(-1)

        w_dn = w_down[i].reshape(N_GROUPS, GROUP_WIDTH, d).astype(BF16)
        x = _moe(tcnt, xe.reshape(t, XE_WIDTH), p[i].reshape(t, PLE_DIM), g_ffn,
                 w_gate[i].astype(BF16), w_up[i].astype(BF16), w_dn,
                 ple_norm_g[i][None, :], w_pg[i].astype(BF16), w_ple[i].astype(BF16),
                 tm=TM_MOE).reshape(b, s, d)
    return x
```

```python
import functools

import numpy as np
import jax
import jax.numpy as jnp
from jax import lax
from jax.experimental import pallas as pl
from jax.experimental.pallas import tpu as pltpu

D_MODEL = 1024
N_HEADS = 8
HEAD_DIM = 64
ATT_WIDTH = N_HEADS * HEAD_DIM
CONV_WIDTH = 512
CONV_K = 3
N_GROUPS = 4
EXPERTS_PER_GROUP = 4
N_EXPERTS = 16
D_EXPERT = 256
GROUP_WIDTH = EXPERTS_PER_GROUP * D_EXPERT
PLE_DIM = 256
EPS = 1e-6
NEG_INF = -1e30
LOG2E = 1.4426950408889634

LANES = 128
BF16_ROWS = 16
C_PIECES = 3
QC_LANE = HEAD_DIM
KC_LANE = HEAD_DIM + C_PIECES
V_ROWS = HEAD_DIM + BF16_ROWS
GID_LANE = N_EXPERTS
ROUTE_PITCH = 32
XE_WIDTH = D_MODEL + LANES
VMEM_LIMIT = 56 * 1024 * 1024

TM_QKV = 512
TQ_ATT = 1024
TK_ATT = 256
TM_MIX = 512
TM_MOE = TM_MIX
MOE_SORTED = -(-(TM_MOE + N_GROUPS * BF16_ROWS) // LANES) * LANES
MOE_CHUNK = 160
MOE_ROWS = MOE_SORTED + MOE_CHUNK

F32 = jnp.float32
BF16 = jnp.bfloat16


def _rms(xf, g):
    return xf * lax.rsqrt(jnp.mean(xf * xf, axis=-1, keepdims=True) + EPS) * g


def _log_sigmoid(z):
    return jnp.minimum(z, 0.0) - jnp.log1p(jnp.exp(-jnp.abs(z)))


def _bdot(a, b):
    return jnp.dot(a, b, preferred_element_type=F32)


def _bdot_nt(a, b):
    return lax.dot_general(a, b, (((1,), (1,)), ((), ())), preferred_element_type=F32)


def _bf16_pieces(x):
    hi = x.astype(BF16)
    r1 = x - hi.astype(F32)
    mid = r1.astype(BF16)
    lo = (r1 - mid.astype(F32)).astype(BF16)
    return hi, mid, lo


def _qkv_kernel(x_ref, g_ref, w_ref, bf_ref, qg_ref, kg_ref, pq_ref, pk_ref,
                qt_ref, k_ref, vt_ref, carry_ref, *, tm):
    st = pl.program_id(1)

    @pl.when(st == 0)
    def _():
        carry_ref[...] = jnp.zeros_like(carry_ref)

    h = _rms(x_ref[0], g_ref[...]).astype(BF16)
    proj = _bdot(h, w_ref[...])

    ext_row = lax.broadcasted_iota(jnp.int32, (V_ROWS - HEAD_DIM, tm), 0)
    v_ext = jnp.where(ext_row == 0, 1.0, 0.0)
    for j in range(N_HEADS // 2):
        pair_t = proj[:, 2 * ATT_WIDTH + LANES * j: 2 * ATT_WIDTH + LANES * (j + 1)].T
        for hh in range(2):
            vt_ref[0, 2 * j + hh] = jnp.concatenate(
                [pair_t[HEAD_DIM * hh: HEAD_DIM * (hh + 1)], v_ext], axis=0).astype(BF16)

    lane = lax.broadcasted_iota(jnp.int32, (tm, LANES), 1)
    row = lax.broadcasted_iota(jnp.int32, (tm, LANES), 0)

    f3 = proj[:, 3 * ATT_WIDTH:3 * ATT_WIDTH + LANES] + bf_ref[...]
    c = jnp.where(lane < C_PIECES * N_HEADS, _log_sigmoid(f3), 0.0)
    sh = 1
    while sh < tm:
        c = c + jnp.where(row >= sh, pltpu.roll(c, sh, axis=0), 0.0)
        sh *= 2
    c = c + carry_ref[...]
    carry_ref[...] = c[tm - 1:tm, :]
    hi, mid, lo = _bf16_pieces(c * LOG2E)
    piece = jnp.where(lane < N_HEADS, hi, jnp.where(lane < 2 * N_HEADS, mid, lo))
    qc = _bdot(piece, pq_ref[...])
    kc = _bdot(piece, pk_ref[...])

    q_ones = jnp.where((lane >= KC_LANE) & (lane < KC_LANE + C_PIECES), 1.0, 0.0)
    k_ones = jnp.where((lane >= QC_LANE) & (lane < QC_LANE + C_PIECES), 1.0, 0.0)
    low = lane < HEAD_DIM

    def heads(base, gain_ref, cmat, ones, out_ref, transposed):
        gain = gain_ref[...]
        for j in range(N_HEADS // 2):
            pair = proj[:, base + LANES * j: base + LANES * (j + 1)]
            sq = pair * pair
            ss_lo = jnp.sum(jnp.where(low, sq, 0.0), axis=1, keepdims=True)
            ss_hi = jnp.sum(jnp.where(low, 0.0, sq), axis=1, keepdims=True)
            n_lo = pair * lax.rsqrt(ss_lo * (1.0 / HEAD_DIM) + EPS) * gain
            n_hi = pltpu.roll(pair, HEAD_DIM, axis=1) * lax.rsqrt(ss_hi * (1.0 / HEAD_DIM) + EPS) * gain
            for hh, nrm in ((2 * j, n_lo), (2 * j + 1, n_hi)):
                aug = jnp.where(low, nrm, 0.0) + cmat[:, LANES * hh: LANES * (hh + 1)] + ones
                out_ref[0, hh] = (aug.T if transposed else aug).astype(BF16)

    heads(0, qg_ref, qc, q_ones, qt_ref, True)
    heads(ATT_WIDTH, kg_ref, kc, k_ones, k_ref, False)


def _qkv_proj(x, g, w, bf3, qg, kg, pq, pk, *, tm):
    b, s, d = x.shape
    grid = (b, s // tm)
    full = lambda shape: pl.BlockSpec(shape, lambda i, j: (0,) * len(shape))
    return pl.pallas_call(
        functools.partial(_qkv_kernel, tm=tm),
        grid=grid,
        in_specs=[
            pl.BlockSpec((1, tm, d), lambda i, j: (i, j, 0)),
            full(g.shape), full(w.shape), full(bf3.shape), full(qg.shape), full(kg.shape),
            full(pq.shape), full(pk.shape),
        ],
        out_specs=[pl.BlockSpec((1, N_HEADS, LANES, tm), lambda i, j: (i, 0, 0, j)),
                   pl.BlockSpec((1, N_HEADS, tm, LANES), lambda i, j: (i, 0, j, 0)),
                   pl.BlockSpec((1, N_HEADS, V_ROWS, tm), lambda i, j: (i, 0, 0, j))],
        out_shape=[
            jax.ShapeDtypeStruct((b, N_HEADS, LANES, s), BF16),
            jax.ShapeDtypeStruct((b, N_HEADS, s, LANES), BF16),
            jax.ShapeDtypeStruct((b, N_HEADS, V_ROWS, s), BF16),
        ],
        scratch_shapes=[pltpu.VMEM((1, LANES), F32)],
        compiler_params=pltpu.CompilerParams(
            dimension_semantics=("arbitrary", "arbitrary"), vmem_limit_bytes=VMEM_LIMIT),
        name="qkv_proj",
    )(x, g, w, bf3, qg, kg, pq, pk)


def _attn_kernel(qt_ref, k_ref, vt_ref, bias_ref, o_ref, sa_ref, sb_ref, *, tq, tk):
    qi = pl.program_id(2)
    n_sub = tq // tk
    bufs = (sa_ref, sb_ref)

    def produce(j, s_ref, lo=0):
        start = pl.multiple_of(j * tk, tk)
        for hh in range(2):
            s_ref[hh, :, lo:tq] = _bdot(k_ref[0, hh, pl.ds(start, tk), :], qt_ref[0, hh, :, lo:tq])

    def consume(j, s_ref, carry, lo=0, masked=False):
        start = pl.multiple_of(j * tk, tk)
        out = []
        for hh in range(2):
            m, acc = carry[hh]

            def read():
                if not masked:
                    return s_ref[hh, :, lo:tq]
                tri = s_ref[hh, :, lo:lo + tk] + bias_ref[...]
                return tri if lo + tk == tq else jnp.concatenate([tri, s_ref[hh, :, lo + tk:tq]], axis=1)

            m_new = jnp.maximum(m[:, lo:tq], jnp.max(read(), axis=0, keepdims=True))
            alpha = jnp.exp2(m[:, lo:tq] - m_new)
            p = jnp.exp2(read() - m_new).astype(BF16)
            acc_new = alpha * acc[:, lo:tq] + _bdot(vt_ref[0, hh, :, pl.ds(start, tk)], p)
            if lo:
                m_new = jnp.concatenate([m[:, 0:lo], m_new], axis=1)
                acc_new = jnp.concatenate([acc[:, 0:lo], acc_new], axis=1)
            out.append((m_new, acc_new))
        return tuple(out)

    def trip(r, carry):
        for t in range(n_sub):
            j = r * n_sub + t
            produce(j + 1, bufs[(t + 1) % 2])
            carry = consume(j, bufs[t % 2], carry)
        return carry

    produce(0, sa_ref)
    init = tuple((jnp.full((1, tq), NEG_INF, F32), jnp.zeros((V_ROWS, tq), F32)) for _ in range(2))
    carry = lax.fori_loop(0, qi, trip, init)
    for t in range(n_sub):
        j = qi * n_sub + t
        if t + 1 < n_sub:
            produce(j + 1, bufs[(t + 1) % 2], lo=(t + 1) * tk)
        carry = consume(j, bufs[t % 2], carry, lo=t * tk, masked=True)
    y_t = jnp.concatenate([acc[0:HEAD_DIM] / acc[HEAD_DIM:HEAD_DIM + 1] for _, acc in carry], axis=0)
    o_ref[0] = y_t.T.astype(BF16)


def _attention(qt_aug, k_aug, vt, *, tq, tk):
    b, nh, s, _ = k_aug.shape
    assert tq % (2 * tk) == 0 and s % tq == 0
    future = np.arange(tk)[:, None] > np.arange(tk)[None, :]
    bias = jnp.asarray(np.where(future, np.float32(NEG_INF), np.float32(0.0)))
    grid = (b, nh // 2, s // tq)
    return pl.pallas_call(
        functools.partial(_attn_kernel, tq=tq, tk=tk),
        grid=grid,
        in_specs=[
            pl.BlockSpec((1, 2, LANES, tq), lambda i, h, j: (i, h, 0, j)),
            pl.BlockSpec((1, 2, s, LANES), lambda i, h, j: (i, h, 0, 0)),
            pl.BlockSpec((1, 2, V_ROWS, s), lambda i, h, j: (i, h, 0, 0)),
            pl.BlockSpec((tk, tk), lambda i, h, j: (0, 0)),
        ],
        out_specs=pl.BlockSpec((1, tq, LANES), lambda i, h, j: (i, j, h)),
        out_shape=jax.ShapeDtypeStruct((b, s, ATT_WIDTH), BF16),
        scratch_shapes=[pltpu.VMEM((2, tk, tq), F32)] * 2,
        compiler_params=pltpu.CompilerParams(
            dimension_semantics=("arbitrary", "arbitrary", "arbitrary"),
            vmem_limit_bytes=VMEM_LIMIT),
        name="fox_attention",
    )(qt_aug, k_aug, vt, bias)


def _mix_kernel(x_ref, ya_ref, g_ref, wcg_ref, cw_ref, wa_ref, wb_ref, wo_ref, g2_ref,
                wr_ref, br_ref, xe_ref, cnt_ref, carry_ref, *, tm):
    st = pl.program_id(1)

    @pl.when(st == 0)
    def _():
        carry_ref[...] = jnp.zeros_like(carry_ref)

    x = x_ref[0]
    h = _rms(x, g_ref[...]).astype(BF16)
    pc = _bdot(h, wcg_ref[...])
    cw = CONV_WIDTH
    cb = pc[:, 0:cw]
    prod = pc[:, cw:2 * cw] * pc[:, 2 * cw:3 * cw]
    ga = pc[:, 3 * cw:3 * cw + D_MODEL]
    gb = pc[:, 3 * cw + D_MODEL:3 * cw + 2 * D_MODEL]

    prev = carry_ref[...]
    crow = lax.broadcasted_iota(jnp.int32, (tm, cw), 0)
    m1 = jnp.where(crow == 0, prev[7:8, :], pltpu.roll(prod, 1, axis=0))
    m2 = jnp.where(crow == 0, prev[6:7, :],
                   jnp.where(crow == 1, prev[7:8, :], pltpu.roll(prod, 2, axis=0)))
    carry_ref[...] = prod[tm - 8:tm, :]
    w = cw_ref[...]
    y_conv = cb * (w[0:1, :] * m2 + w[1:2, :] * m1 + w[2:3, :] * prod)

    a = _bdot(ya_ref[0], wa_ref[...])
    bb = _bdot(y_conv.astype(BF16), wb_ref[...])
    merged = jax.nn.sigmoid(ga) * a + jax.nn.sigmoid(gb) * bb
    x1 = x + _bdot(merged.astype(BF16), wo_ref[...])
    xe_ref[0, :, 0:D_MODEL] = x1

    h2 = _rms(x1, g2_ref[...]).astype(BF16)

    logits = _bdot(h2, wr_ref[...]) + br_ref[...]
    lane = lax.broadcasted_iota(jnp.int32, (tm, LANES), 1)
    lanef = lane.astype(F32)
    is_g = (lane >= N_EXPERTS) & (lane < N_EXPERTS + N_GROUPS)
    gl = jnp.where(is_g, logits, NEG_INF)
    gmax = jnp.max(gl, axis=1, keepdims=True)
    gsum = jnp.sum(jnp.exp(gl - gmax), axis=1, keepdims=True)
    g_val = 1.0 / gsum
    g_lane = jnp.min(jnp.where(gl == gmax, lanef, float(LANES)), axis=1, keepdims=True)
    e_lo = (g_lane - float(N_EXPERTS)) * float(EXPERTS_PER_GROUP)
    in_grp = (lanef >= e_lo) & (lanef < e_lo + float(EXPERTS_PER_GROUP))
    el = jnp.where(in_grp, logits, NEG_INF)
    t1 = jnp.max(el, axis=1, keepdims=True)
    i1 = jnp.min(jnp.where(el == t1, lanef, float(LANES)), axis=1, keepdims=True)
    el2 = jnp.where(lanef == i1, NEG_INF, el)
    t2 = jnp.max(el2, axis=1, keepdims=True)
    i2 = jnp.min(jnp.where(el2 == t2, lanef, float(LANES)), axis=1, keepdims=True)
    e2 = jnp.exp(t2 - t1)
    w1 = g_val / (1.0 + e2)
    w2 = g_val * e2 / (1.0 + e2)
    comb = jnp.where(lanef == i1, w1, 0.0) + jnp.where(lanef == i2, w2, 0.0)

    cnt_ref[0] = jnp.sum(jnp.where(is_g & (lanef == g_lane), 1.0, 0.0), axis=0, keepdims=True)
    xe_ref[0, :, D_MODEL:XE_WIDTH] = comb + jnp.where(lane == GID_LANE, g_lane - float(N_EXPERTS), 0.0)


def _mix(x, y_att, g, wcg, conv_w, wa, wb, wo, g2, wr, br, *, tm):
    b, s, d = x.shape
    grid = (b, s // tm)
    full = lambda a: pl.BlockSpec(a.shape, lambda i, j: (0,) * a.ndim)
    row = lambda width: pl.BlockSpec((1, tm, width), lambda i, j: (i, j, 0))
    return pl.pallas_call(
        functools.partial(_mix_kernel, tm=tm),
        grid=grid,
        in_specs=[row(d), row(ATT_WIDTH), full(g), full(wcg), full(conv_w), full(wa), full(wb),
                  full(wo), full(g2), full(wr), full(br)],
        out_specs=[row(XE_WIDTH),
                   pl.BlockSpec((1, 1, LANES), lambda i, j: (i * (s // tm) + j, 0, 0))],
        out_shape=[
            jax.ShapeDtypeStruct((b, s, XE_WIDTH), F32),
            jax.ShapeDtypeStruct((b * s // tm, 1, LANES), F32),
        ],
        scratch_shapes=[pltpu.VMEM((8, CONV_WIDTH), F32)],
        compiler_params=pltpu.CompilerParams(
            dimension_semantics=("arbitrary", "arbitrary"), vmem_limit_bytes=VMEM_LIMIT),
        name="mix",
    )(x, y_att, g, wcg, conv_w, wa, wb, wo, g2, wr, br)


def _moe_kernel(tcnt_ref, xe_ref, p_ref, g2_ref, wg_ref, wu_ref, wd_ref, g3_ref, wpg_ref, wple_ref,
                o_ref, hs_ref, rs_ref, ys_ref, *, tm):
    i = pl.program_id(0)
    n = [tcnt_ref[i * N_GROUPS + k] for k in range(N_GROUPS)]
    starts = [jnp.int32(0)]
    for k in range(N_GROUPS - 1):
        starts.append(starts[-1] + pl.cdiv(n[k], BF16_ROWS) * BF16_ROWS)

    x1 = xe_ref[:, 0:D_MODEL]
    route = xe_ref[:, D_MODEL:XE_WIDTH]
    h2 = _rms(x1, g2_ref[...]).astype(BF16)

    lane = lax.broadcasted_iota(jnp.int32, (tm, LANES), 1)
    lanef = lane.astype(F32)
    gid = jnp.sum(jnp.where(lane == GID_LANE, route, 0.0), axis=1, keepdims=True)
    onehot = jnp.where((lanef == gid) & (lane < N_GROUPS), 1.0, 0.0)
    earlier = (lax.broadcasted_iota(jnp.int32, (tm, tm), 0)
               > lax.broadcasted_iota(jnp.int32, (tm, tm), 1)).astype(BF16)
    rank = jnp.sum(onehot * _bdot(earlier, onehot.astype(BF16)), axis=1, keepdims=True)
    base = starts[N_GROUPS - 1].astype(F32)
    for k in range(N_GROUPS - 2, -1, -1):
        base = jnp.where(gid == float(k), starts[k].astype(F32), base)
    pos = base + rank

    unsort = (pos == lax.broadcasted_iota(jnp.int32, (tm, MOE_SORTED), 1).astype(F32)).astype(BF16)
    digit_hi = jnp.floor(pos * (1.0 / 32.0))
    digits = jnp.where(lane == 0, digit_hi, jnp.where(lane == 1, pos - 32.0 * digit_hi, 0.0))
    sel_lane = lax.broadcasted_iota(jnp.int32, (BF16_ROWS, LANES), 1)
    sel = jnp.where(sel_lane == 0, 32.0, jnp.where(sel_lane == 1, 1.0, 0.0)).astype(BF16)
    pos_row = _bdot_nt(sel, digits.astype(BF16))[0:1, :]
    sort = (lax.broadcasted_iota(jnp.int32, (MOE_SORTED, tm), 0).astype(F32) == pos_row).astype(BF16)

    hs_ref[0:MOE_SORTED] = _bdot(sort, h2).astype(BF16)
    r_hi, r_mid, r_lo = _bf16_pieces(route)
    packed = (r_hi.astype(F32) + pltpu.roll(r_mid.astype(F32), ROUTE_PITCH, axis=1)
              + pltpu.roll(r_lo.astype(F32), 2 * ROUTE_PITCH, axis=1)).astype(BF16)
    rsorted = _bdot(sort, packed)
    rs_ref[0:MOE_SORTED] = (rsorted + pltpu.roll(rsorted, LANES - ROUTE_PITCH, axis=1)
                            + pltpu.roll(rsorted, LANES - 2 * ROUTE_PITCH, axis=1))
    hs_ref[MOE_SORTED:MOE_ROWS] = jnp.zeros((MOE_CHUNK, D_MODEL), BF16)
    rs_ref[MOE_SORTED:MOE_ROWS] = jnp.zeros((MOE_CHUNK, LANES), F32)
    ys_ref[...] = jnp.zeros_like(ys_ref)

    clane = lax.broadcasted_iota(jnp.int32, (MOE_CHUNK, LANES), 1)
    for g in range(N_GROUPS):
        def chunk(c, _, g=g):
            r0 = pl.multiple_of(starts[g] + c * MOE_CHUNK, BF16_ROWS)
            hrows = hs_ref[pl.ds(r0, MOE_CHUNK), :]
            rt = rs_ref[pl.ds(r0, MOE_CHUNK), :]
            parts = []
            for e in range(EXPERTS_PER_GROUP):
                ex = g * EXPERTS_PER_GROUP + e
                a = _bdot(hrows, wg_ref[ex])
                u = _bdot(hrows, wu_ref[ex])
                ce = jnp.sum(jnp.where(clane == ex, rt, 0.0), axis=1, keepdims=True)
                parts.append(((a * jax.nn.sigmoid(a)) * u * ce).astype(BF16))
            ys_ref[pl.ds(r0, MOE_CHUNK), :] = _bdot(jnp.concatenate(parts, axis=1), wd_ref[g]).astype(BF16)
            return 0
        lax.fori_loop(0, pl.cdiv(n[g], MOE_CHUNK), chunk, 0)

    x2 = x1 + _bdot(unsort, ys_ref[0:MOE_SORTED])
    h3 = _rms(x2, g3_ref[...]).astype(BF16)
    gate = jax.nn.sigmoid(_bdot(h3, wpg_ref[...]))
    emb = _bdot(p_ref[...].astype(BF16), wple_ref[...])
    o_ref[...] = x2 + gate * emb


def _moe(tcnt, xe, p, g2, wg, wu, wd, g3, wpg, wple, *, tm):
    t = xe.shape[0]
    d = D_MODEL
    once = lambda a: pl.BlockSpec(a.shape, lambda i, c: (0,) * a.ndim, pipeline_mode=pl.Buffered(1))
    row = lambda width: pl.BlockSpec((tm, width), lambda i, c: (i, 0))
    return pl.pallas_call(
        functools.partial(_moe_kernel, tm=tm),
        grid_spec=pltpu.PrefetchScalarGridSpec(
            num_scalar_prefetch=1,
            grid=(t // tm,),
            in_specs=[row(XE_WIDTH), row(PLE_DIM), once(g2), once(wg), once(wu), once(wd), once(g3),
                      once(wpg), once(wple)],
            out_specs=row(d),
            scratch_shapes=[pltpu.VMEM((MOE_ROWS, d), BF16), pltpu.VMEM((MOE_ROWS, LANES), F32),
                            pltpu.VMEM((MOE_ROWS, d), BF16)],
        ),
        out_shape=jax.ShapeDtypeStruct((t, d), F32),
        compiler_params=pltpu.CompilerParams(
            dimension_semantics=("arbitrary",), vmem_limit_bytes=VMEM_LIMIT),
        name="moe",
    )(tcnt, xe, p, g2, wg, wu, wd, g3, wpg, wple)


def _c_select_matrices():
    pq = np.zeros((LANES, N_HEADS * LANES), np.float32)
    pk = np.zeros((LANES, N_HEADS * LANES), np.float32)
    for idx in range(C_PIECES):
        for h in range(N_HEADS):
            pq[idx * N_HEADS + h, h * LANES + QC_LANE + idx] = 1.0
            pk[idx * N_HEADS + h, h * LANES + KC_LANE + idx] = -1.0
    return jnp.asarray(pq, BF16), jnp.asarray(pk, BF16)


def kernel(x, p, attn_norm_g, w_in, b_f, q_norm_g, k_norm_g, conv_w, w_out_att, w_out_conv, w_o,
           ffn_norm_g, w_rg, b_rg, w_re, b_re, w_gate, w_up, w_down, ple_norm_g, w_pg, w_ple):
    b, s, d = x.shape
    t = b * s
    aw = ATT_WIDTH
    for i in range(w_in.shape[0]):
        wi = w_in[i]
        wf = wi[:, 3 * aw:3 * aw + N_HEADS]
        w_qkvf = jnp.concatenate(
            [wi[:, :3 * aw], wf, wf, wf, jnp.zeros((d, LANES - C_PIECES * N_HEADS), F32)],
            axis=1).astype(BF16)
        w_cg = wi[:, 3 * aw + N_HEADS:].astype(BF16)
        bf3 = jnp.concatenate([b_f[i]] * C_PIECES + [jnp.zeros((LANES - C_PIECES * N_HEADS,), F32)])[None, :]
        scale = HEAD_DIM ** -0.5 * LOG2E
        qg = jnp.tile(q_norm_g[i] * scale, 2)[None, :]
        kg = jnp.tile(k_norm_g[i], 2)[None, :]
        pq, pk = _c_select_matrices()
        qt_aug, k_aug, vt = _qkv_proj(x, attn_norm_g[i][None, :], w_qkvf, bf3, qg, kg, pq, pk, tm=TM_QKV)
        y_att = _attention(qt_aug, k_aug, vt, tq=TQ_ATT, tk=TK_ATT)

        w_r = jnp.concatenate(
            [w_re[i], w_rg[i], jnp.zeros((d, LANES - N_EXPERTS - N_GROUPS), F32)], axis=1).astype(BF16)
        b_r = jnp.concatenate(
            [b_re[i], b_rg[i], jnp.zeros((LANES - N_EXPERTS - N_GROUPS,), F32)])[None, :]
        g_ffn = ffn_norm_g[i][None, :]
        xe, counts = _mix(x, y_att, attn_norm_g[i][None, :], w_cg, conv_w[i],
                          w_out_att[i].astype(BF16), w_out_conv[i].astype(BF16),
                          w_o[i].astype(BF16), g_ffn, w_r, b_r, tm=TM_MIX)
        tcnt = counts[:, 0, N_EXPERTS:N_EXPERTS + N_GROUPS].astype(jnp.int32).reshape(-1)

        w_dn = w_down[i].reshape(N_GROUPS, GROUP_WIDTH, d).astype(BF16)
        x = _moe(tcnt, xe.reshape(t, XE_WIDTH), p[i].reshape(t, PLE_DIM), g_ffn,
                 w_gate[i].astype(BF16), w_up[i].astype(BF16), w_dn,
                 ple_norm_g[i][None, :], w_pg[i].astype(BF16), w_ple[i].astype(BF16),
                 tm=TM_MOE).reshape(b, s, d)
    return x
```

```python
import functools

import numpy as np
import jax
import jax.numpy as jnp
from jax import lax
from jax.experimental import pallas as pl
from jax.experimental.pallas import tpu as pltpu

D_MODEL = 1024
N_HEADS = 8
HEAD_DIM = 64
ATT_WIDTH = N_HEADS * HEAD_DIM
CONV_WIDTH = 512
CONV_K = 3
N_GROUPS = 4
EXPERTS_PER_GROUP = 4
N_EXPERTS = 16
D_EXPERT = 256
GROUP_WIDTH = EXPERTS_PER_GROUP * D_EXPERT
PLE_DIM = 256
EPS = 1e-6
NEG_INF = -1e30
LOG2E = 1.4426950408889634

LANES = 128
BF16_ROWS = 16
C_PIECES = 3
QC_LANE = HEAD_DIM
KC_LANE = HEAD_DIM + C_PIECES
V_ROWS = HEAD_DIM + BF16_ROWS
QKV_WIDTH = 3 * ATT_WIDTH + LANES
GID_LANE = N_EXPERTS
ROUTE_PITCH = 32
XE_WIDTH = D_MODEL + LANES
VMEM_LIMIT = 56 * 1024 * 1024

TM_QKV = 512
TQ_ATT = 1024
TK_ATT = 256
TM_MIX = 512
TM_MOE = TM_MIX
MOE_SORTED = -(-(TM_MOE + N_GROUPS * BF16_ROWS) // LANES) * LANES
MOE_CHUNK = 160
MOE_ROWS = MOE_SORTED + MOE_CHUNK

F32 = jnp.float32
BF16 = jnp.bfloat16


def _rms(xf, g):
    return xf * lax.rsqrt(jnp.mean(xf * xf, axis=-1, keepdims=True) + EPS) * g


def _log_sigmoid(z):
    return jnp.minimum(z, 0.0) - jnp.log1p(jnp.exp(-jnp.abs(z)))


def _bdot(a, b):
    return jnp.dot(a, b, preferred_element_type=F32)


def _bdot_nt(a, b):
    return lax.dot_general(a, b, (((1,), (1,)), ((), ())), preferred_element_type=F32)


def _bf16_pieces(x):
    hi = x.astype(BF16)
    r1 = x - hi.astype(F32)
    mid = r1.astype(BF16)
    lo = (r1 - mid.astype(F32)).astype(BF16)
    return hi, mid, lo


def _qkv_kernel(x0_ref, xa_ref, xb_ref, g_ref, w_ref, bf_ref, qg_ref, kg_ref, pq_ref, pk_ref,
                qt_ref, k_ref, vt_ref, pa_ref, pb_ref, carry_ref, *, tm, tiles_per_seq):
    step = pl.program_id(0)

    def project(x_ref, dst_ref):
        dst_ref[...] = _bdot(_rms(x_ref[...], g_ref[...]).astype(BF16), w_ref[...])

    @pl.when(step == 0)
    def _():
        carry_ref[...] = jnp.zeros_like(carry_ref)
        project(x0_ref, pa_ref)

    lane = lax.broadcasted_iota(jnp.int32, (tm, LANES), 1)
    row = lax.broadcasted_iota(jnp.int32, (tm, LANES), 0)
    q_ones = jnp.where((lane >= KC_LANE) & (lane < KC_LANE + C_PIECES), 1.0, 0.0)
    k_ones = jnp.where((lane >= QC_LANE) & (lane < QC_LANE + C_PIECES), 1.0, 0.0)
    low = lane < HEAD_DIM
    ext_row = lax.broadcasted_iota(jnp.int32, (V_ROWS - HEAD_DIM, tm), 0)
    v_ext = jnp.where(ext_row == 0, 1.0, 0.0)

    def finish(p_ref, tile, half):
        rows = slice(half * tm, (half + 1) * tm)
        for j in range(N_HEADS // 2):
            pair_t = p_ref[:, 2 * ATT_WIDTH + LANES * j: 2 * ATT_WIDTH + LANES * (j + 1)].T
            for hh in range(2):
                vt_ref[0, 2 * j + hh, :, rows] = jnp.concatenate(
                    [pair_t[HEAD_DIM * hh: HEAD_DIM * (hh + 1)], v_ext], axis=0).astype(BF16)

        f3 = p_ref[:, 3 * ATT_WIDTH:QKV_WIDTH] + bf_ref[...]
        c = jnp.where(lane < C_PIECES * N_HEADS, _log_sigmoid(f3), 0.0)
        sh = 1
        while sh < tm:
            c = c + jnp.where(row >= sh, pltpu.roll(c, sh, axis=0), 0.0)
            sh *= 2
        c = c + jnp.where(tile % tiles_per_seq == 0, 0.0, carry_ref[...])
        carry_ref[...] = c[tm - 1:tm, :]
        hi, mid, lo = _bf16_pieces(c * LOG2E)
        piece = jnp.where(lane < N_HEADS, hi, jnp.where(lane < 2 * N_HEADS, mid, lo))
        qc = _bdot(piece, pq_ref[...])
        kc = _bdot(piece, pk_ref[...])

        def heads(base, gain_ref, cmat, ones, store):
            gain = gain_ref[...]
            for j in range(N_HEADS // 2):
                pair = p_ref[:, base + LANES * j: base + LANES * (j + 1)]
                sq = pair * pair
                ss_lo = jnp.sum(jnp.where(low, sq, 0.0), axis=1, keepdims=True)
                ss_hi = jnp.sum(jnp.where(low, 0.0, sq), axis=1, keepdims=True)
                n_lo = pair * lax.rsqrt(ss_lo * (1.0 / HEAD_DIM) + EPS) * gain
                n_hi = pltpu.roll(pair, HEAD_DIM, axis=1) * lax.rsqrt(ss_hi * (1.0 / HEAD_DIM) + EPS) * gain
                for hh, nrm in ((2 * j, n_lo), (2 * j + 1, n_hi)):
                    store(hh, jnp.where(low, nrm, 0.0) + cmat[:, LANES * hh: LANES * (hh + 1)] + ones)

        def store_q(hh, aug):
            qt_ref[0, hh, :, rows] = aug.T.astype(BF16)

        def store_k(hh, aug):
            k_ref[0, hh, rows, :] = aug.astype(BF16)

        heads(0, qg_ref, qc, q_ones, store_q)
        heads(ATT_WIDTH, kg_ref, kc, k_ones, store_k)

    finish(pa_ref, 2 * step, 0)
    project(xa_ref, pb_ref)
    finish(pb_ref, 2 * step + 1, 1)
    project(xb_ref, pa_ref)


def _qkv_proj(x, g, w, bf3, qg, kg, pq, pk, *, tm):
    b, s, d = x.shape
    x2 = x.reshape(b * s, d)
    n_tiles = b * s // tm
    per_seq = s // (2 * tm)
    full = lambda shape: pl.BlockSpec(shape, lambda i: (0,) * len(shape))
    return pl.pallas_call(
        functools.partial(_qkv_kernel, tm=tm, tiles_per_seq=s // tm),
        grid=(n_tiles // 2,),
        in_specs=[
            pl.BlockSpec((tm, d), lambda i: (0, 0)),
            pl.BlockSpec((tm, d), lambda i: (2 * i + 1, 0)),
            pl.BlockSpec((tm, d), lambda i: (jnp.minimum(2 * i + 2, n_tiles - 1), 0)),
            full(g.shape), full(w.shape), full(bf3.shape), full(qg.shape), full(kg.shape),
            full(pq.shape), full(pk.shape),
        ],
        out_specs=[pl.BlockSpec((1, N_HEADS, LANES, 2 * tm), lambda i: (i // per_seq, 0, 0, i % per_seq)),
                   pl.BlockSpec((1, N_HEADS, 2 * tm, LANES), lambda i: (i // per_seq, 0, i % per_seq, 0)),
                   pl.BlockSpec((1, N_HEADS, V_ROWS, 2 * tm), lambda i: (i // per_seq, 0, 0, i % per_seq))],
        out_shape=[
            jax.ShapeDtypeStruct((b, N_HEADS, LANES, s), BF16),
            jax.ShapeDtypeStruct((b, N_HEADS, s, LANES), BF16),
            jax.ShapeDtypeStruct((b, N_HEADS, V_ROWS, s), BF16),
        ],
        scratch_shapes=[pltpu.VMEM((tm, QKV_WIDTH), F32), pltpu.VMEM((tm, QKV_WIDTH), F32),
                        pltpu.VMEM((1, LANES), F32)],
        compiler_params=pltpu.CompilerParams(
            dimension_semantics=("arbitrary",), vmem_limit_bytes=VMEM_LIMIT),
        name="qkv_proj",
    )(x2, x2, x2, g, w, bf3, qg, kg, pq, pk)


def _attn_kernel(qt_ref, k_ref, vt_ref, bias_ref, o_ref, sa_ref, sb_ref, *, tq, tk):
    qi = pl.program_id(2)
    n_sub = tq // tk
    bufs = (sa_ref, sb_ref)

    def produce(j, s_ref, lo=0):
        start = pl.multiple_of(j * tk, tk)
        for hh in range(2):
            s_ref[hh, :, lo:tq] = _bdot(k_ref[0, hh, pl.ds(start, tk), :], qt_ref[0, hh, :, lo:tq])

    def consume(j, s_ref, carry, lo=0, masked=False):
        start = pl.multiple_of(j * tk, tk)
        out = []
        for hh in range(2):
            m, acc = carry[hh]

            def read():
                if not masked:
                    return s_ref[hh, :, lo:tq]
                tri = s_ref[hh, :, lo:lo + tk] + bias_ref[...]
                return tri if lo + tk == tq else jnp.concatenate([tri, s_ref[hh, :, lo + tk:tq]], axis=1)

            m_new = jnp.maximum(m[:, lo:tq], jnp.max(read(), axis=0, keepdims=True))
            alpha = jnp.exp2(m[:, lo:tq] - m_new)
            p = jnp.exp2(read() - m_new).astype(BF16)
            acc_new = alpha * acc[:, lo:tq] + _bdot(vt_ref[0, hh, :, pl.ds(start, tk)], p)
            if lo:
                m_new = jnp.concatenate([m[:, 0:lo], m_new], axis=1)
                acc_new = jnp.concatenate([acc[:, 0:lo], acc_new], axis=1)
            out.append((m_new, acc_new))
        return tuple(out)

    def trip(r, carry):
        for t in range(n_sub):
            j = r * n_sub + t
            produce(j + 1, bufs[(t + 1) % 2])
            carry = consume(j, bufs[t % 2], carry)
        return carry

    produce(0, sa_ref)
    init = tuple((jnp.full((1, tq), NEG_INF, F32), jnp.zeros((V_ROWS, tq), F32)) for _ in range(2))
    carry = lax.fori_loop(0, qi, trip, init)
    for t in range(n_sub):
        j = qi * n_sub + t
        if t + 1 < n_sub:
            produce(j + 1, bufs[(t + 1) % 2], lo=(t + 1) * tk)
        carry = consume(j, bufs[t % 2], carry, lo=t * tk, masked=True)
    y_t = jnp.concatenate([acc[0:HEAD_DIM] / acc[HEAD_DIM:HEAD_DIM + 1] for _, acc in carry], axis=0)
    o_ref[0] = y_t.T.astype(BF16)


def _attention(qt_aug, k_aug, vt, *, tq, tk):
    b, nh, s, _ = k_aug.shape
    assert tq % (2 * tk) == 0 and s % tq == 0
    future = np.arange(tk)[:, None] > np.arange(tk)[None, :]
    bias = jnp.asarray(np.where(future, np.float32(NEG_INF), np.float32(0.0)))
    grid = (b, nh // 2, s // tq)
    return pl.pallas_call(
        functools.partial(_attn_kernel, tq=tq, tk=tk),
        grid=grid,
        in_specs=[
            pl.BlockSpec((1, 2, LANES, tq), lambda i, h, j: (i, h, 0, j)),
            pl.BlockSpec((1, 2, s, LANES), lambda i, h, j: (i, h, 0, 0)),
            pl.BlockSpec((1, 2, V_ROWS, s), lambda i, h, j: (i, h, 0, 0)),
            pl.BlockSpec((tk, tk), lambda i, h, j: (0, 0)),
        ],
        out_specs=pl.BlockSpec((1, tq, LANES), lambda i, h, j: (i, j, h)),
        out_shape=jax.ShapeDtypeStruct((b, s, ATT_WIDTH), BF16),
        scratch_shapes=[pltpu.VMEM((2, tk, tq), F32)] * 2,
        compiler_params=pltpu.CompilerParams(
            dimension_semantics=("arbitrary", "arbitrary", "arbitrary"),
            vmem_limit_bytes=VMEM_LIMIT),
        name="fox_attention",
    )(qt_aug, k_aug, vt, bias)


def _mix_kernel(x0_ref, xa_ref, xb_ref, y0_ref, ya_ref, yb_ref, g_ref, wcg_ref, cw_ref, wa_ref, wb_ref,
                wo_ref, g2_ref, wr_ref, br_ref, xe_ref, cnt_ref, ra_ref, rb_ref, carry_ref,
                *, tm, tiles_per_seq):
    step = pl.program_id(0)

    def residual(x_ref, y_ref, tile, dst_ref):
        x = x_ref[...]
        h = _rms(x, g_ref[...]).astype(BF16)
        pc = _bdot(h, wcg_ref[...])
        cw = CONV_WIDTH
        cb = pc[:, 0:cw]
        prod = pc[:, cw:2 * cw] * pc[:, 2 * cw:3 * cw]
        ga = pc[:, 3 * cw:3 * cw + D_MODEL]
        gb = pc[:, 3 * cw + D_MODEL:3 * cw + 2 * D_MODEL]

        prev = jnp.where(tile % tiles_per_seq == 0, 0.0, carry_ref[...])
        crow = lax.broadcasted_iota(jnp.int32, (tm, cw), 0)
        m1 = jnp.where(crow == 0, prev[7:8, :], pltpu.roll(prod, 1, axis=0))
        m2 = jnp.where(crow == 0, prev[6:7, :],
                       jnp.where(crow == 1, prev[7:8, :], pltpu.roll(prod, 2, axis=0)))
        carry_ref[...] = prod[tm - 8:tm, :]
        w = cw_ref[...]
        y_conv = cb * (w[0:1, :] * m2 + w[1:2, :] * m1 + w[2:3, :] * prod)

        a = _bdot(y_ref[...], wa_ref[...])
        bb = _bdot(y_conv.astype(BF16), wb_ref[...])
        merged = jax.nn.sigmoid(ga) * a + jax.nn.sigmoid(gb) * bb
        dst_ref[...] = x + _bdot(merged.astype(BF16), wo_ref[...])

    @pl.when(step == 0)
    def _():
        carry_ref[...] = jnp.zeros_like(carry_ref)
        residual(x0_ref, y0_ref, 0, ra_ref)

    _route(ra_ref, g2_ref, wr_ref, br_ref, xe_ref, cnt_ref, 0, tm)
    residual(xa_ref, ya_ref, 2 * step + 1, rb_ref)
    _route(rb_ref, g2_ref, wr_ref, br_ref, xe_ref, cnt_ref, 1, tm)
    residual(xb_ref, yb_ref, 2 * step + 2, ra_ref)


def _route(x1_ref, g2_ref, wr_ref, br_ref, xe_ref, cnt_ref, half, tm):
    rows = slice(half * tm, (half + 1) * tm)
    x1 = x1_ref[...]
    xe_ref[rows, 0:D_MODEL] = x1
    h2 = _rms(x1, g2_ref[...]).astype(BF16)

    logits = _bdot(h2, wr_ref[...]) + br_ref[...]
    lane = lax.broadcasted_iota(jnp.int32, (tm, LANES), 1)
    lanef = lane.astype(F32)
    is_g = (lane >= N_EXPERTS) & (lane < N_EXPERTS + N_GROUPS)
    gl = jnp.where(is_g, logits, NEG_INF)
    gmax = jnp.max(gl, axis=1, keepdims=True)
    gsum = jnp.sum(jnp.exp(gl - gmax), axis=1, keepdims=True)
    g_val = 1.0 / gsum
    g_lane = jnp.min(jnp.where(gl == gmax, lanef, float(LANES)), axis=1, keepdims=True)
    e_lo = (g_lane - float(N_EXPERTS)) * float(EXPERTS_PER_GROUP)
    in_grp = (lanef >= e_lo) & (lanef < e_lo + float(EXPERTS_PER_GROUP))
    el = jnp.where(in_grp, logits, NEG_INF)
    t1 = jnp.max(el, axis=1, keepdims=True)
    i1 = jnp.min(jnp.where(el == t1, lanef, float(LANES)), axis=1, keepdims=True)
    el2 = jnp.where(lanef == i1, NEG_INF, el)
    t2 = jnp.max(el2, axis=1, keepdims=True)
    i2 = jnp.min(jnp.where(el2 == t2, lanef, float(LANES)), axis=1, keepdims=True)
    e2 = jnp.exp(t2 - t1)
    w1 = g_val / (1.0 + e2)
    w2 = g_val * e2 / (1.0 + e2)
    comb = jnp.where(lanef == i1, w1, 0.0) + jnp.where(lanef == i2, w2, 0.0)

    cnt_ref[half] = jnp.sum(jnp.where(is_g & (lanef == g_lane), 1.0, 0.0), axis=0, keepdims=True)
    xe_ref[rows, D_MODEL:XE_WIDTH] = comb + jnp.where(lane == GID_LANE, g_lane - float(N_EXPERTS), 0.0)


def _mix(x, y_att, g, wcg, conv_w, wa, wb, wo, g2, wr, br, *, tm):
    b, s, d = x.shape
    t = b * s
    n_tiles = t // tm
    x2 = x.reshape(t, d)
    y2 = y_att.reshape(t, ATT_WIDTH)
    once = lambda a: pl.BlockSpec(a.shape, lambda i: (0,) * a.ndim, pipeline_mode=pl.Buffered(1))

    def tiles(width):
        nxt = lambda i: (jnp.minimum(2 * i + 2, n_tiles - 1), 0)
        return [pl.BlockSpec((tm, width), lambda i: (0, 0), pipeline_mode=pl.Buffered(1)),
                pl.BlockSpec((tm, width), lambda i: (2 * i + 1, 0)),
                pl.BlockSpec((tm, width), nxt)]

    return pl.pallas_call(
        functools.partial(_mix_kernel, tm=tm, tiles_per_seq=s // tm),
        grid=(n_tiles // 2,),
        in_specs=tiles(d) + tiles(ATT_WIDTH) + [once(g), once(wcg), once(conv_w), once(wa), once(wb),
                                                once(wo), once(g2), once(wr), once(br)],
        out_specs=[pl.BlockSpec((2 * tm, XE_WIDTH), lambda i: (i, 0)),
                   pl.BlockSpec((2, 1, LANES), lambda i: (i, 0, 0))],
        out_shape=[
            jax.ShapeDtypeStruct((t, XE_WIDTH), F32),
            jax.ShapeDtypeStruct((n_tiles, 1, LANES), F32),
        ],
        scratch_shapes=[pltpu.VMEM((tm, d), F32), pltpu.VMEM((tm, d), F32),
                        pltpu.VMEM((8, CONV_WIDTH), F32)],
        compiler_params=pltpu.CompilerParams(
            dimension_semantics=("arbitrary",), vmem_limit_bytes=VMEM_LIMIT),
        name="mix",
    )(x2, x2, x2, y2, y2, y2, g, wcg, conv_w, wa, wb, wo, g2, wr, br)


def _moe_kernel(tcnt_ref, xe_ref, p_ref, g2_ref, wg_ref, wu_ref, wd_ref, g3_ref, wpg_ref, wple_ref,
                o_ref, hs_ref, rs_ref, ys_ref, *, tm):
    i = pl.program_id(0)
    n = [tcnt_ref[i * N_GROUPS + k] for k in range(N_GROUPS)]
    starts = [jnp.int32(0)]
    for k in range(N_GROUPS - 1):
        starts.append(starts[-1] + pl.cdiv(n[k], BF16_ROWS) * BF16_ROWS)

    x1 = xe_ref[:, 0:D_MODEL]
    route = xe_ref[:, D_MODEL:XE_WIDTH]
    h2 = _rms(x1, g2_ref[...]).astype(BF16)

    lane = lax.broadcasted_iota(jnp.int32, (tm, LANES), 1)
    lanef = lane.astype(F32)
    gid = jnp.sum(jnp.where(lane == GID_LANE, route, 0.0), axis=1, keepdims=True)
    onehot = jnp.where((lanef == gid) & (lane < N_GROUPS), 1.0, 0.0)
    earlier = (lax.broadcasted_iota(jnp.int32, (tm, tm), 0)
               > lax.broadcasted_iota(jnp.int32, (tm, tm), 1)).astype(BF16)
    rank = jnp.sum(onehot * _bdot(earlier, onehot.astype(BF16)), axis=1, keepdims=True)
    base = starts[N_GROUPS - 1].astype(F32)
    for k in range(N_GROUPS - 2, -1, -1):
        base = jnp.where(gid == float(k), starts[k].astype(F32), base)
    pos = base + rank

    unsort = (pos == lax.broadcasted_iota(jnp.int32, (tm, MOE_SORTED), 1).astype(F32)).astype(BF16)
    digit_hi = jnp.floor(pos * (1.0 / 32.0))
    digits = jnp.where(lane == 0, digit_hi, jnp.where(lane == 1, pos - 32.0 * digit_hi, 0.0))
    sel_lane = lax.broadcasted_iota(jnp.int32, (BF16_ROWS, LANES), 1)
    sel = jnp.where(sel_lane == 0, 32.0, jnp.where(sel_lane == 1, 1.0, 0.0)).astype(BF16)
    pos_row = _bdot_nt(sel, digits.astype(BF16))[0:1, :]
    sort = (lax.broadcasted_iota(jnp.int32, (MOE_SORTED, tm), 0).astype(F32) == pos_row).astype(BF16)

    hs_ref[0:MOE_SORTED] = _bdot(sort, h2).astype(BF16)
    r_hi, r_mid, r_lo = _bf16_pieces(route)
    packed = (r_hi.astype(F32) + pltpu.roll(r_mid.astype(F32), ROUTE_PITCH, axis=1)
              + pltpu.roll(r_lo.astype(F32), 2 * ROUTE_PITCH, axis=1)).astype(BF16)
    rsorted = _bdot(sort, packed)
    rs_ref[0:MOE_SORTED] = (rsorted + pltpu.roll(rsorted, LANES - ROUTE_PITCH, axis=1)
                            + pltpu.roll(rsorted, LANES - 2 * ROUTE_PITCH, axis=1))
    hs_ref[MOE_SORTED:MOE_ROWS] = jnp.zeros((MOE_CHUNK, D_MODEL), BF16)
    rs_ref[MOE_SORTED:MOE_ROWS] = jnp.zeros((MOE_CHUNK, LANES), F32)
    ys_ref[...] = jnp.zeros_like(ys_ref)

    clane = lax.broadcasted_iota(jnp.int32, (MOE_CHUNK, LANES), 1)
    for g in range(N_GROUPS):
        def chunk(c, _, g=g):
            r0 = pl.multiple_of(starts[g] + c * MOE_CHUNK, BF16_ROWS)
            hrows = hs_ref[pl.ds(r0, MOE_CHUNK), :]
            rt = rs_ref[pl.ds(r0, MOE_CHUNK), :]
            parts = []
            for e in range(EXPERTS_PER_GROUP):
                ex = g * EXPERTS_PER_GROUP + e
                a = _bdot(hrows, wg_ref[ex])
                u = _bdot(hrows, wu_ref[ex])
                ce = jnp.sum(jnp.where(clane == ex, rt, 0.0), axis=1, keepdims=True)
                parts.append(((a * jax.nn.sigmoid(a)) * u * ce).astype(BF16))
            ys_ref[pl.ds(r0, MOE_CHUNK), :] = _bdot(jnp.concatenate(parts, axis=1), wd_ref[g]).astype(BF16)
            return 0
        lax.fori_loop(0, pl.cdiv(n[g], MOE_CHUNK), chunk, 0)

    x2 = x1 + _bdot(unsort, ys_ref[0:MOE_SORTED])
    h3 = _rms(x2, g3_ref[...]).astype(BF16)
    gate = jax.nn.sigmoid(_bdot(h3, wpg_ref[...]))
    emb = _bdot(p_ref[...].astype(BF16), wple_ref[...])
    o_ref[...] = x2 + gate * emb


def _moe(tcnt, xe, p, g2, wg, wu, wd, g3, wpg, wple, *, tm):
    t = xe.shape[0]
    d = D_MODEL
    once = lambda a: pl.BlockSpec(a.shape, lambda i, c: (0,) * a.ndim, pipeline_mode=pl.Buffered(1))
    row = lambda width: pl.BlockSpec((tm, width), lambda i, c: (i, 0))
    return pl.pallas_call(
        functools.partial(_moe_kernel, tm=tm),
        grid_spec=pltpu.PrefetchScalarGridSpec(
            num_scalar_prefetch=1,
            grid=(t // tm,),
            in_specs=[row(XE_WIDTH), row(PLE_DIM), once(g2), once(wg), once(wu), once(wd), once(g3),
                      once(wpg), once(wple)],
            out_specs=row(d),
            scratch_shapes=[pltpu.VMEM((MOE_ROWS, d), BF16), pltpu.VMEM((MOE_ROWS, LANES), F32),
                            pltpu.VMEM((MOE_ROWS, d), BF16)],
        ),
        out_shape=jax.ShapeDtypeStruct((t, d), F32),
        compiler_params=pltpu.CompilerParams(
            dimension_semantics=("arbitrary",), vmem_limit_bytes=VMEM_LIMIT),
        name="moe",
    )(tcnt, xe, p, g2, wg, wu, wd, g3, wpg, wple)


def _c_select_matrices():
    pq = np.zeros((LANES, N_HEADS * LANES), np.float32)
    pk = np.zeros((LANES, N_HEADS * LANES), np.float32)
    for idx in range(C_PIECES):
        for h in range(N_HEADS):
            pq[idx * N_HEADS + h, h * LANES + QC_LANE + idx] = 1.0
            pk[idx * N_HEADS + h, h * LANES + KC_LANE + idx] = -1.0
    return jnp.asarray(pq, BF16), jnp.asarray(pk, BF16)


def kernel(x, p, attn_norm_g, w_in, b_f, q_norm_g, k_norm_g, conv_w, w_out_att, w_out_conv, w_o,
           ffn_norm_g, w_rg, b_rg, w_re, b_re, w_gate, w_up, w_down, ple_norm_g, w_pg, w_ple):
    b, s, d = x.shape
    t = b * s
    aw = ATT_WIDTH
    for i in range(w_in.shape[0]):
        wi = w_in[i]
        wf = wi[:, 3 * aw:3 * aw + N_HEADS]
        w_qkvf = jnp.concatenate(
            [wi[:, :3 * aw], wf, wf, wf, jnp.zeros((d, LANES - C_PIECES * N_HEADS), F32)],
            axis=1).astype(BF16)
        w_cg = wi[:, 3 * aw + N_HEADS:].astype(BF16)
        bf3 = jnp.concatenate([b_f[i]] * C_PIECES + [jnp.zeros((LANES - C_PIECES * N_HEADS,), F32)])[None, :]
        scale = HEAD_DIM ** -0.5 * LOG2E
        qg = jnp.tile(q_norm_g[i] * scale, 2)[None, :]
        kg = jnp.tile(k_norm_g[i], 2)[None, :]
        pq, pk = _c_select_matrices()
        qt_aug, k_aug, vt = _qkv_proj(x, attn_norm_g[i][None, :], w_qkvf, bf3, qg, kg, pq, pk, tm=TM_QKV)
        y_att = _attention(qt_aug, k_aug, vt, tq=TQ_ATT, tk=TK_ATT)

        w_r = jnp.concatenate(
            [w_re[i], w_rg[i], jnp.zeros((d, LANES - N_EXPERTS - N_GROUPS), F32)], axis=1).astype(BF16)
        b_r = jnp.concatenate(
            [b_re[i], b_rg[i], jnp.zeros((LANES - N_EXPERTS - N_GROUPS,), F32)])[None, :]
        g_ffn = ffn_norm_g[i][None, :]
        xe, counts = _mix(x, y_att, attn_norm_g[i][None, :], w_cg, conv_w[i],
                          w_out_att[i].astype(BF16), w_out_conv[i].astype(BF16),
                          w_o[i].astype(BF16), g_ffn, w_r, b_r, tm=TM_MIX)
        tcnt = counts[:, 0, N_EXPERTS:N_EXPERTS + N_GROUPS].astype(jnp.int32).reshape(-1)

        w_dn = w_down[i].reshape(N_GROUPS, GROUP_WIDTH, d).astype(BF16)
        x = _moe(tcnt, xe.reshape(t, XE_WIDTH), p[i].reshape(t, PLE_DIM), g_ffn,
                 w_gate[i].astype(BF16), w_up[i].astype(BF16), w_dn,
                 ple_norm_g[i][None, :], w_pg[i].astype(BF16), w_ple[i].astype(BF16),
                 tm=TM_MOE).reshape(b, s, d)
    return x
```

```python
import functools

import numpy as np
import jax
import jax.numpy as jnp
from jax import lax
from jax.experimental import pallas as pl
from jax.experimental.pallas import tpu as pltpu

D_MODEL = 1024
N_HEADS = 8
HEAD_DIM = 64
ATT_WIDTH = N_HEADS * HEAD_DIM
CONV_WIDTH = 512
CONV_K = 3
N_GROUPS = 4
EXPERTS_PER_GROUP = 4
N_EXPERTS = 16
D_EXPERT = 256
GROUP_WIDTH = EXPERTS_PER_GROUP * D_EXPERT
PLE_DIM = 256
EPS = 1e-6
NEG_INF = -1e30
LOG2E = 1.4426950408889634

LANES = 128
BF16_ROWS = 16
C_PIECES = 3
QC_LANE = HEAD_DIM
KC_LANE = HEAD_DIM + C_PIECES
V_ROWS = HEAD_DIM + BF16_ROWS
QKV_WIDTH = 3 * ATT_WIDTH + LANES
GID_LANE = N_EXPERTS
ROUTE_PITCH = 32
XE_WIDTH = D_MODEL + LANES
VMEM_LIMIT = 56 * 1024 * 1024

TM_QKV = 512
TQ_ATT = 1024
TK_ATT = 256
TM_MIX = 512
TM_MOE = TM_MIX
MOE_SORTED = -(-(TM_MOE + N_GROUPS * BF16_ROWS) // LANES) * LANES
MOE_CHUNK = 144
MOE_ROWS = MOE_SORTED + MOE_CHUNK

F32 = jnp.float32
BF16 = jnp.bfloat16


def _rms(xf, g):
    return xf * lax.rsqrt(jnp.mean(xf * xf, axis=-1, keepdims=True) + EPS) * g


def _log_sigmoid(z):
    return jnp.minimum(z, 0.0) - jnp.log1p(jnp.exp(-jnp.abs(z)))


def _bdot(a, b):
    return jnp.dot(a, b, preferred_element_type=F32)


def _bdot_nt(a, b):
    return lax.dot_general(a, b, (((1,), (1,)), ((), ())), preferred_element_type=F32)


def _bf16_pieces(x):
    hi = x.astype(BF16)
    r1 = x - hi.astype(F32)
    mid = r1.astype(BF16)
    lo = (r1 - mid.astype(F32)).astype(BF16)
    return hi, mid, lo


def _qkv_kernel(x0_ref, xa_ref, xb_ref, g_ref, w_ref, bf_ref, qg_ref, kg_ref, pq_ref, pk_ref,
                qt_ref, k_ref, vt_ref, pa_ref, pb_ref, carry_ref, *, tm, tiles_per_seq):
    step = pl.program_id(0)

    def project(x_ref, dst_ref):
        dst_ref[...] = _bdot(_rms(x_ref[...], g_ref[...]).astype(BF16), w_ref[...])

    @pl.when(step == 0)
    def _():
        carry_ref[...] = jnp.zeros_like(carry_ref)
        project(x0_ref, pa_ref)

    lane = lax.broadcasted_iota(jnp.int32, (tm, LANES), 1)
    row = lax.broadcasted_iota(jnp.int32, (tm, LANES), 0)
    q_ones = jnp.where((lane >= KC_LANE) & (lane < KC_LANE + C_PIECES), 1.0, 0.0)
    k_ones = jnp.where((lane >= QC_LANE) & (lane < QC_LANE + C_PIECES), 1.0, 0.0)
    low = lane < HEAD_DIM
    ext_row = lax.broadcasted_iota(jnp.int32, (V_ROWS - HEAD_DIM, tm), 0)
    v_ext = jnp.where(ext_row == 0, 1.0, 0.0)

    def finish(p_ref, tile, half):
        rows = slice(half * tm, (half + 1) * tm)
        for j in range(N_HEADS // 2):
            pair_t = p_ref[:, 2 * ATT_WIDTH + LANES * j: 2 * ATT_WIDTH + LANES * (j + 1)].T
            for hh in range(2):
                vt_ref[0, 2 * j + hh, :, rows] = jnp.concatenate(
                    [pair_t[HEAD_DIM * hh: HEAD_DIM * (hh + 1)], v_ext], axis=0).astype(BF16)

        f3 = p_ref[:, 3 * ATT_WIDTH:QKV_WIDTH] + bf_ref[...]
        c = jnp.where(lane < C_PIECES * N_HEADS, _log_sigmoid(f3), 0.0)
        sh = 1
        while sh < tm:
            c = c + jnp.where(row >= sh, pltpu.roll(c, sh, axis=0), 0.0)
            sh *= 2
        c = c + jnp.where(tile % tiles_per_seq == 0, 0.0, carry_ref[...])
        carry_ref[...] = c[tm - 1:tm, :]
        hi, mid, lo = _bf16_pieces(c * LOG2E)
        piece = jnp.where(lane < N_HEADS, hi, jnp.where(lane < 2 * N_HEADS, mid, lo))
        qc = _bdot(piece, pq_ref[...])
        kc = _bdot(piece, pk_ref[...])

        def heads(base, gain_ref, cmat, ones, store):
            gain = gain_ref[...]
            for j in range(N_HEADS // 2):
                pair = p_ref[:, base + LANES * j: base + LANES * (j + 1)]
                sq = pair * pair
                ss_lo = jnp.sum(jnp.where(low, sq, 0.0), axis=1, keepdims=True)
                ss_hi = jnp.sum(jnp.where(low, 0.0, sq), axis=1, keepdims=True)
                n_lo = pair * lax.rsqrt(ss_lo * (1.0 / HEAD_DIM) + EPS) * gain
                n_hi = pltpu.roll(pair, HEAD_DIM, axis=1) * lax.rsqrt(ss_hi * (1.0 / HEAD_DIM) + EPS) * gain
                for hh, nrm in ((2 * j, n_lo), (2 * j + 1, n_hi)):
                    store(hh, jnp.where(low, nrm, 0.0) + cmat[:, LANES * hh: LANES * (hh + 1)] + ones)

        def store_q(hh, aug):
            qt_ref[0, hh, :, rows] = aug.T.astype(BF16)

        def store_k(hh, aug):
            k_ref[0, hh, rows, :] = aug.astype(BF16)

        heads(0, qg_ref, qc, q_ones, store_q)
        heads(ATT_WIDTH, kg_ref, kc, k_ones, store_k)

    finish(pa_ref, 2 * step, 0)
    project(xa_ref, pb_ref)
    finish(pb_ref, 2 * step + 1, 1)
    project(xb_ref, pa_ref)


def _qkv_proj(x, g, w, bf3, qg, kg, pq, pk, *, tm):
    b, s, d = x.shape
    x2 = x.reshape(b * s, d)
    n_tiles = b * s // tm
    per_seq = s // (2 * tm)
    full = lambda shape: pl.BlockSpec(shape, lambda i: (0,) * len(shape))
    return pl.pallas_call(
        functools.partial(_qkv_kernel, tm=tm, tiles_per_seq=s // tm),
        grid=(n_tiles // 2,),
        in_specs=[
            pl.BlockSpec((tm, d), lambda i: (0, 0)),
            pl.BlockSpec((tm, d), lambda i: (2 * i + 1, 0)),
            pl.BlockSpec((tm, d), lambda i: (jnp.minimum(2 * i + 2, n_tiles - 1), 0)),
            full(g.shape), full(w.shape), full(bf3.shape), full(qg.shape), full(kg.shape),
            full(pq.shape), full(pk.shape),
        ],
        out_specs=[pl.BlockSpec((1, N_HEADS, LANES, 2 * tm), lambda i: (i // per_seq, 0, 0, i % per_seq)),
                   pl.BlockSpec((1, N_HEADS, 2 * tm, LANES), lambda i: (i // per_seq, 0, i % per_seq, 0)),
                   pl.BlockSpec((1, N_HEADS, V_ROWS, 2 * tm), lambda i: (i // per_seq, 0, 0, i % per_seq))],
        out_shape=[
            jax.ShapeDtypeStruct((b, N_HEADS, LANES, s), BF16),
            jax.ShapeDtypeStruct((b, N_HEADS, s, LANES), BF16),
            jax.ShapeDtypeStruct((b, N_HEADS, V_ROWS, s), BF16),
        ],
        scratch_shapes=[pltpu.VMEM((tm, QKV_WIDTH), F32), pltpu.VMEM((tm, QKV_WIDTH), F32),
                        pltpu.VMEM((1, LANES), F32)],
        compiler_params=pltpu.CompilerParams(
            dimension_semantics=("arbitrary",), vmem_limit_bytes=VMEM_LIMIT),
        name="qkv_proj",
    )(x2, x2, x2, g, w, bf3, qg, kg, pq, pk)


def _attn_kernel(qt_ref, qn_ref, k_ref, vt_ref, bias_ref, o_ref, sa_ref, sb_ref, *, tq, tk):
    qi = pl.program_id(2)
    n_sub = tq // tk
    bufs = (sa_ref, sb_ref)

    def produce(j, s_ref, lo=0, q_ref=qt_ref):
        start = pl.multiple_of(j * tk, tk)
        for hh in range(2):
            s_ref[hh, :, lo:tq] = _bdot(k_ref[0, hh, pl.ds(start, tk), :], q_ref[0, hh, :, lo:tq])

    def consume(j, s_ref, carry, lo=0, masked=False):
        start = pl.multiple_of(j * tk, tk)
        out = []
        for hh in range(2):
            m, acc = carry[hh]

            def read():
                if not masked:
                    return s_ref[hh, :, lo:tq]
                tri = s_ref[hh, :, lo:lo + tk] + bias_ref[...]
                return tri if lo + tk == tq else jnp.concatenate([tri, s_ref[hh, :, lo + tk:tq]], axis=1)

            m_new = jnp.maximum(m[:, lo:tq], jnp.max(read(), axis=0, keepdims=True))
            alpha = jnp.exp2(m[:, lo:tq] - m_new)
            p = jnp.exp2(read() - m_new).astype(BF16)
            acc_new = alpha * acc[:, lo:tq] + _bdot(vt_ref[0, hh, :, pl.ds(start, tk)], p)
            if lo:
                m_new = jnp.concatenate([m[:, 0:lo], m_new], axis=1)
                acc_new = jnp.concatenate([acc[:, 0:lo], acc_new], axis=1)
            out.append((m_new, acc_new))
        return tuple(out)

    def trip(r, carry):
        for t in range(n_sub):
            j = r * n_sub + t
            produce(j + 1, bufs[(t + 1) % 2])
            carry = consume(j, bufs[t % 2], carry)
        return carry

    @pl.when(qi == 0)
    def _():
        produce(0, sa_ref)

    init = tuple((jnp.full((1, tq), NEG_INF, F32), jnp.zeros((V_ROWS, tq), F32)) for _ in range(2))
    carry = lax.fori_loop(0, qi, trip, init)
    for t in range(n_sub):
        j = qi * n_sub + t
        if t + 1 < n_sub:
            produce(j + 1, bufs[(t + 1) % 2], lo=(t + 1) * tk)
        else:
            produce(0, bufs[(t + 1) % 2], q_ref=qn_ref)
        carry = consume(j, bufs[t % 2], carry, lo=t * tk, masked=True)
    y_t = jnp.concatenate([acc[0:HEAD_DIM] / acc[HEAD_DIM:HEAD_DIM + 1] for _, acc in carry], axis=0)
    o_ref[0] = y_t.T.astype(BF16)


def _attention(qt_aug, k_aug, vt, *, tq, tk):
    b, nh, s, _ = k_aug.shape
    assert tq % (2 * tk) == 0 and s % tq == 0
    future = np.arange(tk)[:, None] > np.arange(tk)[None, :]
    bias = jnp.asarray(np.where(future, np.float32(NEG_INF), np.float32(0.0)))
    grid = (b, nh // 2, s // tq)
    return pl.pallas_call(
        functools.partial(_attn_kernel, tq=tq, tk=tk),
        grid=grid,
        in_specs=[
            pl.BlockSpec((1, 2, LANES, tq), lambda i, h, j: (i, h, 0, j)),
            pl.BlockSpec((1, 2, LANES, tq), lambda i, h, j: (i, h, 0, jnp.minimum(j + 1, s // tq - 1))),
            pl.BlockSpec((1, 2, s, LANES), lambda i, h, j: (i, h, 0, 0)),
            pl.BlockSpec((1, 2, V_ROWS, s), lambda i, h, j: (i, h, 0, 0)),
            pl.BlockSpec((tk, tk), lambda i, h, j: (0, 0)),
        ],
        out_specs=pl.BlockSpec((1, tq, LANES), lambda i, h, j: (i, j, h)),
        out_shape=jax.ShapeDtypeStruct((b, s, ATT_WIDTH), BF16),
        scratch_shapes=[pltpu.VMEM((2, tk, tq), F32)] * 2,
        compiler_params=pltpu.CompilerParams(
            dimension_semantics=("arbitrary", "arbitrary", "arbitrary"),
            vmem_limit_bytes=VMEM_LIMIT),
        name="fox_attention",
    )(qt_aug, qt_aug, k_aug, vt, bias)


def _mix_kernel(x0_ref, xa_ref, xb_ref, y0_ref, ya_ref, yb_ref, g_ref, wcg_ref, cw_ref, wa_ref, wb_ref,
                wo_ref, g2_ref, wr_ref, br_ref, xe_ref, cnt_ref, ra_ref, rb_ref, carry_ref,
                *, tm, tiles_per_seq):
    step = pl.program_id(0)

    def residual(x_ref, y_ref, tile, dst_ref):
        x = x_ref[...]
        h = _rms(x, g_ref[...]).astype(BF16)
        pc = _bdot(h, wcg_ref[...])
        cw = CONV_WIDTH
        cb = pc[:, 0:cw]
        prod = pc[:, cw:2 * cw] * pc[:, 2 * cw:3 * cw]
        ga = pc[:, 3 * cw:3 * cw + D_MODEL]
        gb = pc[:, 3 * cw + D_MODEL:3 * cw + 2 * D_MODEL]

        prev = jnp.where(tile % tiles_per_seq == 0, 0.0, carry_ref[...])
        crow = lax.broadcasted_iota(jnp.int32, (tm, cw), 0)
        m1 = jnp.where(crow == 0, prev[7:8, :], pltpu.roll(prod, 1, axis=0))
        m2 = jnp.where(crow == 0, prev[6:7, :],
                       jnp.where(crow == 1, prev[7:8, :], pltpu.roll(prod, 2, axis=0)))
        carry_ref[...] = prod[tm - 8:tm, :]
        w = cw_ref[...]
        y_conv = cb * (w[0:1, :] * m2 + w[1:2, :] * m1 + w[2:3, :] * prod)

        a = _bdot(y_ref[...], wa_ref[...])
        bb = _bdot(y_conv.astype(BF16), wb_ref[...])
        merged = jax.nn.sigmoid(ga) * a + jax.nn.sigmoid(gb) * bb
        dst_ref[...] = x + _bdot(merged.astype(BF16), wo_ref[...])

    @pl.when(step == 0)
    def _():
        carry_ref[...] = jnp.zeros_like(carry_ref)
        residual(x0_ref, y0_ref, 0, ra_ref)

    _route(ra_ref, g2_ref, wr_ref, br_ref, xe_ref, cnt_ref, 0, tm)
    residual(xa_ref, ya_ref, 2 * step + 1, rb_ref)
    _route(rb_ref, g2_ref, wr_ref, br_ref, xe_ref, cnt_ref, 1, tm)
    residual(xb_ref, yb_ref, 2 * step + 2, ra_ref)


def _route(x1_ref, g2_ref, wr_ref, br_ref, xe_ref, cnt_ref, half, tm):
    rows = slice(half * tm, (half + 1) * tm)
    x1 = x1_ref[...]
    xe_ref[rows, 0:D_MODEL] = x1
    h2 = _rms(x1, g2_ref[...]).astype(BF16)

    logits = _bdot(h2, wr_ref[...]) + br_ref[...]
    lane = lax.broadcasted_iota(jnp.int32, (tm, LANES), 1)
    lanef = lane.astype(F32)
    is_g = (lane >= N_EXPERTS) & (lane < N_EXPERTS + N_GROUPS)
    gl = jnp.where(is_g, logits, NEG_INF)
    gmax = jnp.max(gl, axis=1, keepdims=True)
    gsum = jnp.sum(jnp.exp(gl - gmax), axis=1, keepdims=True)
    g_val = 1.0 / gsum
    g_lane = jnp.min(jnp.where(gl == gmax, lanef, float(LANES)), axis=1, keepdims=True)
    e_lo = (g_lane - float(N_EXPERTS)) * float(EXPERTS_PER_GROUP)
    in_grp = (lanef >= e_lo) & (lanef < e_lo + float(EXPERTS_PER_GROUP))
    el = jnp.where(in_grp, logits, NEG_INF)
    t1 = jnp.max(el, axis=1, keepdims=True)
    i1 = jnp.min(jnp.where(el == t1, lanef, float(LANES)), axis=1, keepdims=True)
    el2 = jnp.where(lanef == i1, NEG_INF, el)
    t2 = jnp.max(el2, axis=1, keepdims=True)
    i2 = jnp.min(jnp.where(el2 == t2, lanef, float(LANES)), axis=1, keepdims=True)
    e2 = jnp.exp(t2 - t1)
    w1 = g_val / (1.0 + e2)
    w2 = g_val * e2 / (1.0 + e2)
    comb = jnp.where(lanef == i1, w1, 0.0) + jnp.where(lanef == i2, w2, 0.0)

    cnt_ref[half] = jnp.sum(jnp.where(is_g & (lanef == g_lane), 1.0, 0.0), axis=0, keepdims=True)
    xe_ref[rows, D_MODEL:XE_WIDTH] = comb + jnp.where(lane == GID_LANE, g_lane - float(N_EXPERTS), 0.0)


def _mix(x, y_att, g, wcg, conv_w, wa, wb, wo, g2, wr, br, *, tm):
    b, s, d = x.shape
    t = b * s
    n_tiles = t // tm
    x2 = x.reshape(t, d)
    y2 = y_att.reshape(t, ATT_WIDTH)
    once = lambda a: pl.BlockSpec(a.shape, lambda i: (0,) * a.ndim, pipeline_mode=pl.Buffered(1))

    def tiles(width):
        nxt = lambda i: (jnp.minimum(2 * i + 2, n_tiles - 1), 0)
        return [pl.BlockSpec((tm, width), lambda i: (0, 0), pipeline_mode=pl.Buffered(1)),
                pl.BlockSpec((tm, width), lambda i: (2 * i + 1, 0)),
                pl.BlockSpec((tm, width), nxt)]

    return pl.pallas_call(
        functools.partial(_mix_kernel, tm=tm, tiles_per_seq=s // tm),
        grid=(n_tiles // 2,),
        in_specs=tiles(d) + tiles(ATT_WIDTH) + [once(g), once(wcg), once(conv_w), once(wa), once(wb),
                                                once(wo), once(g2), once(wr), once(br)],
        out_specs=[pl.BlockSpec((2 * tm, XE_WIDTH), lambda i: (i, 0)),
                   pl.BlockSpec((2, 1, LANES), lambda i: (i, 0, 0))],
        out_shape=[
            jax.ShapeDtypeStruct((t, XE_WIDTH), F32),
            jax.ShapeDtypeStruct((n_tiles, 1, LANES), F32),
        ],
        scratch_shapes=[pltpu.VMEM((tm, d), F32), pltpu.VMEM((tm, d), F32),
                        pltpu.VMEM((8, CONV_WIDTH), F32)],
        compiler_params=pltpu.CompilerParams(
            dimension_semantics=("arbitrary",), vmem_limit_bytes=VMEM_LIMIT),
        name="mix",
    )(x2, x2, x2, y2, y2, y2, g, wcg, conv_w, wa, wb, wo, g2, wr, br)


def _moe_kernel(tcnt_ref, xe_ref, p_ref, g2_ref, wg_ref, wu_ref, wd_ref, g3_ref, wpg_ref, wple_ref,
                earlier_ref, o_ref, hs_ref, rs_ref, ys_ref, *, tm):
    i = pl.program_id(0)
    n = [tcnt_ref[i * N_GROUPS + k] for k in range(N_GROUPS)]
    starts = [jnp.int32(0)]
    for k in range(N_GROUPS - 1):
        starts.append(starts[-1] + pl.cdiv(n[k], BF16_ROWS) * BF16_ROWS)

    x1 = xe_ref[:, 0:D_MODEL]
    route = xe_ref[:, D_MODEL:XE_WIDTH]
    h2 = _rms(x1, g2_ref[...]).astype(BF16)

    lane = lax.broadcasted_iota(jnp.int32, (tm, LANES), 1)
    lanef = lane.astype(F32)
    gid = jnp.sum(jnp.where(lane == GID_LANE, route, 0.0), axis=1, keepdims=True)
    onehot = jnp.where((lanef == gid) & (lane < N_GROUPS), 1.0, 0.0)
    rank = jnp.sum(onehot * _bdot(earlier_ref[...], onehot.astype(BF16)), axis=1, keepdims=True)
    base = starts[N_GROUPS - 1].astype(F32)
    for k in range(N_GROUPS - 2, -1, -1):
        base = jnp.where(gid == float(k), starts[k].astype(F32), base)
    pos = base + rank

    unsort = (pos == lax.broadcasted_iota(jnp.int32, (tm, MOE_SORTED), 1).astype(F32)).astype(BF16)
    digit_hi = jnp.floor(pos * (1.0 / 32.0))
    digits = jnp.where(lane == 0, digit_hi, jnp.where(lane == 1, pos - 32.0 * digit_hi, 0.0))
    sel_lane = lax.broadcasted_iota(jnp.int32, (BF16_ROWS, LANES), 1)
    sel = jnp.where(sel_lane == 0, 32.0, jnp.where(sel_lane == 1, 1.0, 0.0)).astype(BF16)
    pos_row = _bdot_nt(sel, digits.astype(BF16))[0:1, :]
    sort = (lax.broadcasted_iota(jnp.int32, (MOE_SORTED, tm), 0).astype(F32) == pos_row).astype(BF16)

    hs_ref[0:MOE_SORTED] = _bdot(sort, h2).astype(BF16)
    r_hi, r_mid, r_lo = _bf16_pieces(route)
    packed = (r_hi.astype(F32) + pltpu.roll(r_mid.astype(F32), ROUTE_PITCH, axis=1)
              + pltpu.roll(r_lo.astype(F32), 2 * ROUTE_PITCH, axis=1)).astype(BF16)
    rsorted = _bdot(sort, packed)
    rs_ref[0:MOE_SORTED] = (rsorted + pltpu.roll(rsorted, LANES - ROUTE_PITCH, axis=1)
                            + pltpu.roll(rsorted, LANES - 2 * ROUTE_PITCH, axis=1))
    hs_ref[MOE_SORTED:MOE_ROWS] = jnp.zeros((MOE_CHUNK, D_MODEL), BF16)
    rs_ref[MOE_SORTED:MOE_ROWS] = jnp.zeros((MOE_CHUNK, LANES), F32)
    ys_ref[...] = jnp.zeros_like(ys_ref)

    clane = lax.broadcasted_iota(jnp.int32, (MOE_CHUNK, LANES), 1)
    for g in range(N_GROUPS):
        def chunk(c, _, g=g):
            r0 = pl.multiple_of(starts[g] + c * MOE_CHUNK, BF16_ROWS)
            hrows = hs_ref[pl.ds(r0, MOE_CHUNK), :]
            rt = rs_ref[pl.ds(r0, MOE_CHUNK), :]
            parts = []
            for e in range(EXPERTS_PER_GROUP):
                ex = g * EXPERTS_PER_GROUP + e
                a = _bdot(hrows, wg_ref[ex])
                u = _bdot(hrows, wu_ref[ex])
                ce = jnp.sum(jnp.where(clane == ex, rt, 0.0), axis=1, keepdims=True)
                parts.append(((a * jax.nn.sigmoid(a)) * u * ce).astype(BF16))
            ys_ref[pl.ds(r0, MOE_CHUNK), :] = _bdot(jnp.concatenate(parts, axis=1), wd_ref[g]).astype(BF16)
            return 0
        lax.fori_loop(0, pl.cdiv(n[g], MOE_CHUNK), chunk, 0)

    x2 = x1 + _bdot(unsort, ys_ref[0:MOE_SORTED])
    h3 = _rms(x2, g3_ref[...]).astype(BF16)
    gate = jax.nn.sigmoid(_bdot(h3, wpg_ref[...]))
    emb = _bdot(p_ref[...].astype(BF16), wple_ref[...])
    o_ref[...] = x2 + gate * emb


def _moe(tcnt, xe, p, g2, wg, wu, wd, g3, wpg, wple, *, tm):
    t = xe.shape[0]
    d = D_MODEL
    once = lambda a: pl.BlockSpec(a.shape, lambda i, c: (0,) * a.ndim, pipeline_mode=pl.Buffered(1))
    row = lambda width: pl.BlockSpec((tm, width), lambda i, c: (i, 0))
    earlier = jnp.asarray(np.tril(np.ones((tm, tm), np.float32), -1), BF16)
    return pl.pallas_call(
        functools.partial(_moe_kernel, tm=tm),
        grid_spec=pltpu.PrefetchScalarGridSpec(
            num_scalar_prefetch=1,
            grid=(t // tm,),
            in_specs=[row(XE_WIDTH), row(PLE_DIM), once(g2), once(wg), once(wu), once(wd), once(g3),
                      once(wpg), once(wple), once(earlier)],
            out_specs=row(d),
            scratch_shapes=[pltpu.VMEM((MOE_ROWS, d), BF16), pltpu.VMEM((MOE_ROWS, LANES), F32),
                            pltpu.VMEM((MOE_ROWS, d), BF16)],
        ),
        out_shape=jax.ShapeDtypeStruct((t, d), F32),
        compiler_params=pltpu.CompilerParams(
            dimension_semantics=("arbitrary",), vmem_limit_bytes=VMEM_LIMIT),
        name="moe",
    )(tcnt, xe, p, g2, wg, wu, wd, g3, wpg, wple, earlier)


def _c_select_matrices():
    pq = np.zeros((LANES, N_HEADS * LANES), np.float32)
    pk = np.zeros((LANES, N_HEADS * LANES), np.float32)
    for idx in range(C_PIECES):
        for h in range(N_HEADS):
            pq[idx * N_HEADS + h, h * LANES + QC_LANE + idx] = 1.0
            pk[idx * N_HEADS + h, h * LANES + KC_LANE + idx] = -1.0
    return jnp.asarray(pq, BF16), jnp.asarray(pk, BF16)


def kernel(x, p, attn_norm_g, w_in, b_f, q_norm_g, k_norm_g, conv_w, w_out_att, w_out_conv, w_o,
           ffn_norm_g, w_rg, b_rg, w_re, b_re, w_gate, w_up, w_down, ple_norm_g, w_pg, w_ple):
    b, s, d = x.shape
    t = b * s
    aw = ATT_WIDTH
    for i in range(w_in.shape[0]):
        wi = w_in[i]
        wf = wi[:, 3 * aw:3 * aw + N_HEADS]
        w_qkvf = jnp.concatenate(
            [wi[:, :3 * aw], wf, wf, wf, jnp.zeros((d, LANES - C_PIECES * N_HEADS), F32)],
            axis=1).astype(BF16)
        w_cg = wi[:, 3 * aw + N_HEADS:].astype(BF16)
        bf3 = jnp.concatenate([b_f[i]] * C_PIECES + [jnp.zeros((LANES - C_PIECES * N_HEADS,), F32)])[None, :]
        scale = HEAD_DIM ** -0.5 * LOG2E
        qg = jnp.tile(q_norm_g[i] * scale, 2)[None, :]
        kg = jnp.tile(k_norm_g[i], 2)[None, :]
        pq, pk = _c_select_matrices()
        qt_aug, k_aug, vt = _qkv_proj(x, attn_norm_g[i][None, :], w_qkvf, bf3, qg, kg, pq, pk, tm=TM_QKV)
        y_att = _attention(qt_aug, k_aug, vt, tq=TQ_ATT, tk=TK_ATT)

        w_r = jnp.concatenate(
            [w_re[i], w_rg[i], jnp.zeros((d, LANES - N_EXPERTS - N_GROUPS), F32)], axis=1).astype(BF16)
        b_r = jnp.concatenate(
            [b_re[i], b_rg[i], jnp.zeros((LANES - N_EXPERTS - N_GROUPS,), F32)])[None, :]
        g_ffn = ffn_norm_g[i][None, :]
        xe, counts = _mix(x, y_att, attn_norm_g[i][None, :], w_cg, conv_w[i],
                          w_out_att[i].astype(BF16), w_out_conv[i].astype(BF16),
                          w_o[i].astype(BF16), g_ffn, w_r, b_r, tm=TM_MIX)
        tcnt = counts[:, 0, N_EXPERTS:N_EXPERTS + N_GROUPS].astype(jnp.int32).reshape(-1)

        w_dn = w_down[i].reshape(N_GROUPS, GROUP_WIDTH, d).astype(BF16)
        x = _moe(tcnt, xe.reshape(t, XE_WIDTH), p[i].reshape(t, PLE_DIM), g_ffn,
                 w_gate[i].astype(BF16), w_up[i].astype(BF16), w_dn,
                 ple_norm_g[i][None, :], w_pg[i].astype(BF16), w_ple[i].astype(BF16),
                 tm=TM_MOE).reshape(b, s, d)
    return x
```

```python
import functools

import numpy as np
import jax
import jax.numpy as jnp
from jax import lax
from jax.experimental import pallas as pl
from jax.experimental.pallas import tpu as pltpu

D_MODEL = 1024
N_HEADS = 8
HEAD_DIM = 64
ATT_WIDTH = N_HEADS * HEAD_DIM
CONV_WIDTH = 512
CONV_K = 3
N_GROUPS = 4
EXPERTS_PER_GROUP = 4
N_EXPERTS = 16
D_EXPERT = 256
GROUP_WIDTH = EXPERTS_PER_GROUP * D_EXPERT
PLE_DIM = 256
EPS = 1e-6
NEG_INF = -1e30
LOG2E = 1.4426950408889634

LANES = 128
BF16_ROWS = 16
C_PIECES = 3
QC_LANE = HEAD_DIM
KC_LANE = HEAD_DIM + C_PIECES
V_ROWS = HEAD_DIM + BF16_ROWS
QKV_WIDTH = 3 * ATT_WIDTH + LANES
GID_LANE = N_EXPERTS
ROUTE_PITCH = 32
XE_WIDTH = D_MODEL + LANES
VMEM_LIMIT = 56 * 1024 * 1024

TM_W_IN = 128
TM_QKV = 512
TQ_ATT = 1024
TK_ATT = 256
TM_MIX = 512
TM_MOE = TM_MIX
MOE_SORTED = -(-(TM_MOE + N_GROUPS * BF16_ROWS) // LANES) * LANES
MOE_CHUNK = 144
MOE_ROWS = MOE_SORTED + MOE_CHUNK

F32 = jnp.float32
BF16 = jnp.bfloat16


def _rms(xf, g):
    return xf * lax.rsqrt(jnp.mean(xf * xf, axis=-1, keepdims=True) + EPS) * g


def _log_sigmoid(z):
    return jnp.minimum(z, 0.0) - jnp.log1p(jnp.exp(-jnp.abs(z)))


def _bdot(a, b):
    return jnp.dot(a, b, preferred_element_type=F32)


def _bdot_nt(a, b):
    return lax.dot_general(a, b, (((1,), (1,)), ((), ())), preferred_element_type=F32)


def _bf16_pieces(x):
    hi = x.astype(BF16)
    r1 = x - hi.astype(F32)
    mid = r1.astype(BF16)
    lo = (r1 - mid.astype(F32)).astype(BF16)
    return hi, mid, lo


def _w_in_kernel(w_ref, qkvf_ref, cg_ref):
    rows = w_ref.shape[1]
    qkv = 3 * ATT_WIDTH
    qkvf_ref[:, 0:qkv] = w_ref[0, :, 0:qkv].astype(BF16)
    lane = lax.broadcasted_iota(jnp.int32, (rows, LANES), 1)
    f = jnp.where(lane < N_HEADS, w_ref[0, :, qkv:qkv + LANES], 0.0)
    f3 = f
    for k in range(1, C_PIECES):
        f3 = f3 + pltpu.roll(f, k * N_HEADS, axis=1)
    qkvf_ref[:, qkv:QKV_WIDTH] = f3.astype(BF16)
    cg_ref[...] = w_ref[0, :, qkv + N_HEADS:].astype(BF16)


def _prep_w_in(w_in_layer, *, rows):
    _, d, width = w_in_layer.shape
    cg_width = width - 3 * ATT_WIDTH - N_HEADS
    return pl.pallas_call(
        _w_in_kernel,
        grid=(d // rows,),
        in_specs=[pl.BlockSpec((1, rows, width), lambda i: (0, i, 0))],
        out_specs=[pl.BlockSpec((rows, QKV_WIDTH), lambda i: (i, 0)),
                   pl.BlockSpec((rows, cg_width), lambda i: (i, 0))],
        out_shape=[jax.ShapeDtypeStruct((d, QKV_WIDTH), BF16),
                   jax.ShapeDtypeStruct((d, cg_width), BF16)],
        compiler_params=pltpu.CompilerParams(
            dimension_semantics=("arbitrary",), vmem_limit_bytes=VMEM_LIMIT),
        name="prep_w_in",
    )(w_in_layer)


def _qkv_kernel(x0_ref, xa_ref, xb_ref, g_ref, w_ref, bf_ref, qg_ref, kg_ref, pq_ref, pk_ref,
                qt_ref, k_ref, vt_ref, pa_ref, pb_ref, carry_ref, *, tm, tiles_per_seq):
    step = pl.program_id(0)

    def project(x_ref, dst_ref):
        dst_ref[...] = _bdot(_rms(x_ref[...], g_ref[...]).astype(BF16), w_ref[...])

    @pl.when(step == 0)
    def _():
        carry_ref[...] = jnp.zeros_like(carry_ref)
        project(x0_ref, pa_ref)

    lane = lax.broadcasted_iota(jnp.int32, (tm, LANES), 1)
    row = lax.broadcasted_iota(jnp.int32, (tm, LANES), 0)
    q_ones = jnp.where((lane >= KC_LANE) & (lane < KC_LANE + C_PIECES), 1.0, 0.0)
    k_ones = jnp.where((lane >= QC_LANE) & (lane < QC_LANE + C_PIECES), 1.0, 0.0)
    low = lane < HEAD_DIM
    ext_row = lax.broadcasted_iota(jnp.int32, (V_ROWS - HEAD_DIM, tm), 0)
    v_ext = jnp.where(ext_row == 0, 1.0, 0.0)

    def finish(p_ref, tile, half):
        rows = slice(half * tm, (half + 1) * tm)
        for j in range(N_HEADS // 2):
            pair_t = p_ref[:, 2 * ATT_WIDTH + LANES * j: 2 * ATT_WIDTH + LANES * (j + 1)].T
            for hh in range(2):
                vt_ref[0, 2 * j + hh, :, rows] = jnp.concatenate(
                    [pair_t[HEAD_DIM * hh: HEAD_DIM * (hh + 1)], v_ext], axis=0).astype(BF16)

        f3 = p_ref[:, 3 * ATT_WIDTH:QKV_WIDTH] + bf_ref[...]
        c = jnp.where(lane < C_PIECES * N_HEADS, _log_sigmoid(f3), 0.0)
        sh = 1
        while sh < tm:
            c = c + jnp.where(row >= sh, pltpu.roll(c, sh, axis=0), 0.0)
            sh *= 2
        c = c + jnp.where(tile % tiles_per_seq == 0, 0.0, carry_ref[...])
        carry_ref[...] = c[tm - 1:tm, :]
        hi, mid, lo = _bf16_pieces(c * LOG2E)
        piece = jnp.where(lane < N_HEADS, hi, jnp.where(lane < 2 * N_HEADS, mid, lo))
        qc = _bdot(piece, pq_ref[...])
        kc = _bdot(piece, pk_ref[...])

        def heads(base, gain_ref, cmat, ones, store):
            gain = gain_ref[...]
            for j in range(N_HEADS // 2):
                pair = p_ref[:, base + LANES * j: base + LANES * (j + 1)]
                sq = pair * pair
                ss_lo = jnp.sum(jnp.where(low, sq, 0.0), axis=1, keepdims=True)
                ss_hi = jnp.sum(jnp.where(low, 0.0, sq), axis=1, keepdims=True)
                n_lo = pair * lax.rsqrt(ss_lo * (1.0 / HEAD_DIM) + EPS) * gain
                n_hi = pltpu.roll(pair, HEAD_DIM, axis=1) * lax.rsqrt(ss_hi * (1.0 / HEAD_DIM) + EPS) * gain
                for hh, nrm in ((2 * j, n_lo), (2 * j + 1, n_hi)):
                    store(hh, jnp.where(low, nrm, 0.0) + cmat[:, LANES * hh: LANES * (hh + 1)] + ones)

        def store_q(hh, aug):
            qt_ref[0, hh, :, rows] = aug.T.astype(BF16)

        def store_k(hh, aug):
            k_ref[0, hh, rows, :] = aug.astype(BF16)

        heads(0, qg_ref, qc, q_ones, store_q)
        heads(ATT_WIDTH, kg_ref, kc, k_ones, store_k)

    finish(pa_ref, 2 * step, 0)
    project(xa_ref, pb_ref)
    finish(pb_ref, 2 * step + 1, 1)
    project(xb_ref, pa_ref)


def _qkv_proj(x, g, w, bf3, qg, kg, pq, pk, *, tm):
    b, s, d = x.shape
    x2 = x.reshape(b * s, d)
    n_tiles = b * s // tm
    per_seq = s // (2 * tm)
    full = lambda shape: pl.BlockSpec(shape, lambda i: (0,) * len(shape))
    return pl.pallas_call(
        functools.partial(_qkv_kernel, tm=tm, tiles_per_seq=s // tm),
        grid=(n_tiles // 2,),
        in_specs=[
            pl.BlockSpec((tm, d), lambda i: (0, 0)),
            pl.BlockSpec((tm, d), lambda i: (2 * i + 1, 0)),
            pl.BlockSpec((tm, d), lambda i: (jnp.minimum(2 * i + 2, n_tiles - 1), 0)),
            full(g.shape), full(w.shape), full(bf3.shape), full(qg.shape), full(kg.shape),
            full(pq.shape), full(pk.shape),
        ],
        out_specs=[pl.BlockSpec((1, N_HEADS, LANES, 2 * tm), lambda i: (i // per_seq, 0, 0, i % per_seq)),
                   pl.BlockSpec((1, N_HEADS, 2 * tm, LANES), lambda i: (i // per_seq, 0, i % per_seq, 0)),
                   pl.BlockSpec((1, N_HEADS, V_ROWS, 2 * tm), lambda i: (i // per_seq, 0, 0, i % per_seq))],
        out_shape=[
            jax.ShapeDtypeStruct((b, N_HEADS, LANES, s), BF16),
            jax.ShapeDtypeStruct((b, N_HEADS, s, LANES), BF16),
            jax.ShapeDtypeStruct((b, N_HEADS, V_ROWS, s), BF16),
        ],
        scratch_shapes=[pltpu.VMEM((tm, QKV_WIDTH), F32), pltpu.VMEM((tm, QKV_WIDTH), F32),
                        pltpu.VMEM((1, LANES), F32)],
        compiler_params=pltpu.CompilerParams(
            dimension_semantics=("arbitrary",), vmem_limit_bytes=VMEM_LIMIT),
        name="qkv_proj",
    )(x2, x2, x2, g, w, bf3, qg, kg, pq, pk)


def _attn_kernel(qt_ref, qn_ref, k_ref, vt_ref, bias_ref, o_ref, sa_ref, sb_ref, *, tq, tk):
    qi = pl.program_id(2)
    n_sub = tq // tk
    bufs = (sa_ref, sb_ref)

    def produce(j, s_ref, lo=0, q_ref=qt_ref):
        start = pl.multiple_of(j * tk, tk)
        for hh in range(2):
            s_ref[hh, :, lo:tq] = _bdot(k_ref[0, hh, pl.ds(start, tk), :], q_ref[0, hh, :, lo:tq])

    def consume(j, s_ref, carry, lo=0, masked=False):
        start = pl.multiple_of(j * tk, tk)
        out = []
        for hh in range(2):
            m, acc = carry[hh]

            def read():
                if not masked:
                    return s_ref[hh, :, lo:tq]
                tri = s_ref[hh, :, lo:lo + tk] + bias_ref[...]
                return tri if lo + tk == tq else jnp.concatenate([tri, s_ref[hh, :, lo + tk:tq]], axis=1)

            m_new = jnp.maximum(m[:, lo:tq], jnp.max(read(), axis=0, keepdims=True))
            alpha = jnp.exp2(m[:, lo:tq] - m_new)
            p = jnp.exp2(read() - m_new).astype(BF16)
            acc_new = alpha * acc[:, lo:tq] + _bdot(vt_ref[0, hh, :, pl.ds(start, tk)], p)
            if lo:
                m_new = jnp.concatenate([m[:, 0:lo], m_new], axis=1)
                acc_new = jnp.concatenate([acc[:, 0:lo], acc_new], axis=1)
            out.append((m_new, acc_new))
        return tuple(out)

    def trip(r, carry):
        for t in range(n_sub):
            j = r * n_sub + t
            produce(j + 1, bufs[(t + 1) % 2])
            carry = consume(j, bufs[t % 2], carry)
        return carry

    @pl.when(qi == 0)
    def _():
        produce(0, sa_ref)

    init = tuple((jnp.full((1, tq), NEG_INF, F32), jnp.zeros((V_ROWS, tq), F32)) for _ in range(2))
    carry = lax.fori_loop(0, qi, trip, init)
    for t in range(n_sub):
        j = qi * n_sub + t
        if t + 1 < n_sub:
            produce(j + 1, bufs[(t + 1) % 2], lo=(t + 1) * tk)
        else:
            produce(0, bufs[(t + 1) % 2], q_ref=qn_ref)
        carry = consume(j, bufs[t % 2], carry, lo=t * tk, masked=True)
    y_t = jnp.concatenate([acc[0:HEAD_DIM] / acc[HEAD_DIM:HEAD_DIM + 1] for _, acc in carry], axis=0)
    o_ref[0] = y_t.T.astype(BF16)


def _attention(qt_aug, k_aug, vt, *, tq, tk):
    b, nh, s, _ = k_aug.shape
    assert tq % (2 * tk) == 0 and s % tq == 0
    future = np.arange(tk)[:, None] > np.arange(tk)[None, :]
    bias = jnp.asarray(np.where(future, np.float32(NEG_INF), np.float32(0.0)))
    grid = (b, nh // 2, s // tq)
    return pl.pallas_call(
        functools.partial(_attn_kernel, tq=tq, tk=tk),
        grid=grid,
        in_specs=[
            pl.BlockSpec((1, 2, LANES, tq), lambda i, h, j: (i, h, 0, j)),
            pl.BlockSpec((1, 2, LANES, tq), lambda i, h, j: (i, h, 0, jnp.minimum(j + 1, s // tq - 1))),
            pl.BlockSpec((1, 2, s, LANES), lambda i, h, j: (i, h, 0, 0)),
            pl.BlockSpec((1, 2, V_ROWS, s), lambda i, h, j: (i, h, 0, 0)),
            pl.BlockSpec((tk, tk), lambda i, h, j: (0, 0)),
        ],
        out_specs=pl.BlockSpec((1, tq, LANES), lambda i, h, j: (i, j, h)),
        out_shape=jax.ShapeDtypeStruct((b, s, ATT_WIDTH), BF16),
        scratch_shapes=[pltpu.VMEM((2, tk, tq), F32)] * 2,
        compiler_params=pltpu.CompilerParams(
            dimension_semantics=("arbitrary", "arbitrary", "arbitrary"),
            vmem_limit_bytes=VMEM_LIMIT),
        name="fox_attention",
    )(qt_aug, qt_aug, k_aug, vt, bias)


def _mix_kernel(x0_ref, xa_ref, xb_ref, y0_ref, ya_ref, yb_ref, g_ref, wcg_ref, cw_ref, wa_ref, wb_ref,
                wo_ref, g2_ref, wr_ref, br_ref, xe_ref, cnt_ref, ra_ref, rb_ref, carry_ref,
                *, tm, tiles_per_seq):
    step = pl.program_id(0)

    def residual(x_ref, y_ref, tile, dst_ref):
        x = x_ref[...]
        h = _rms(x, g_ref[...]).astype(BF16)
        pc = _bdot(h, wcg_ref[...])
        cw = CONV_WIDTH
        cb = pc[:, 0:cw]
        prod = pc[:, cw:2 * cw] * pc[:, 2 * cw:3 * cw]
        ga = pc[:, 3 * cw:3 * cw + D_MODEL]
        gb = pc[:, 3 * cw + D_MODEL:3 * cw + 2 * D_MODEL]

        prev = jnp.where(tile % tiles_per_seq == 0, 0.0, carry_ref[...])
        crow = lax.broadcasted_iota(jnp.int32, (tm, cw), 0)
        m1 = jnp.where(crow == 0, prev[7:8, :], pltpu.roll(prod, 1, axis=0))
        m2 = jnp.where(crow == 0, prev[6:7, :],
                       jnp.where(crow == 1, prev[7:8, :], pltpu.roll(prod, 2, axis=0)))
        carry_ref[...] = prod[tm - 8:tm, :]
        w = cw_ref[...]
        y_conv = cb * (w[0:1, :] * m2 + w[1:2, :] * m1 + w[2:3, :] * prod)

        a = _bdot(y_ref[...], wa_ref[...])
        bb = _bdot(y_conv.astype(BF16), wb_ref[...])
        merged = jax.nn.sigmoid(ga) * a + jax.nn.sigmoid(gb) * bb
        dst_ref[...] = x + _bdot(merged.astype(BF16), wo_ref[...])

    @pl.when(step == 0)
    def _():
        carry_ref[...] = jnp.zeros_like(carry_ref)
        residual(x0_ref, y0_ref, 0, ra_ref)

    _route(ra_ref, g2_ref, wr_ref, br_ref, xe_ref, cnt_ref, 0, tm)
    residual(xa_ref, ya_ref, 2 * step + 1, rb_ref)
    _route(rb_ref, g2_ref, wr_ref, br_ref, xe_ref, cnt_ref, 1, tm)
    residual(xb_ref, yb_ref, 2 * step + 2, ra_ref)


def _route(x1_ref, g2_ref, wr_ref, br_ref, xe_ref, cnt_ref, half, tm):
    rows = slice(half * tm, (half + 1) * tm)
    x1 = x1_ref[...]
    xe_ref[rows, 0:D_MODEL] = x1
    h2 = _rms(x1, g2_ref[...]).astype(BF16)

    logits = _bdot(h2, wr_ref[...]) + br_ref[...]
    lane = lax.broadcasted_iota(jnp.int32, (tm, LANES), 1)
    lanef = lane.astype(F32)
    is_g = (lane >= N_EXPERTS) & (lane < N_EXPERTS + N_GROUPS)
    gl = jnp.where(is_g, logits, NEG_INF)
    gmax = jnp.max(gl, axis=1, keepdims=True)
    gsum = jnp.sum(jnp.exp(gl - gmax), axis=1, keepdims=True)
    g_val = 1.0 / gsum
    g_lane = jnp.min(jnp.where(gl == gmax, lanef, float(LANES)), axis=1, keepdims=True)
    e_lo = (g_lane - float(N_EXPERTS)) * float(EXPERTS_PER_GROUP)
    in_grp = (lanef >= e_lo) & (lanef < e_lo + float(EXPERTS_PER_GROUP))
    el = jnp.where(in_grp, logits, NEG_INF)
    t1 = jnp.max(el, axis=1, keepdims=True)
    i1 = jnp.min(jnp.where(el == t1, lanef, float(LANES)), axis=1, keepdims=True)
    el2 = jnp.where(lanef == i1, NEG_INF, el)
    t2 = jnp.max(el2, axis=1, keepdims=True)
    i2 = jnp.min(jnp.where(el2 == t2, lanef, float(LANES)), axis=1, keepdims=True)
    e2 = jnp.exp(t2 - t1)
    w1 = g_val / (1.0 + e2)
    w2 = g_val * e2 / (1.0 + e2)
    comb = jnp.where(lanef == i1, w1, 0.0) + jnp.where(lanef == i2, w2, 0.0)

    cnt_ref[half] = jnp.sum(jnp.where(is_g & (lanef == g_lane), 1.0, 0.0), axis=0, keepdims=True)
    xe_ref[rows, D_MODEL:XE_WIDTH] = comb + jnp.where(lane == GID_LANE, g_lane - float(N_EXPERTS), 0.0)


def _mix(x, y_att, g, wcg, conv_w, wa, wb, wo, g2, wr, br, *, tm):
    b, s, d = x.shape
    t = b * s
    n_tiles = t // tm
    x2 = x.reshape(t, d)
    y2 = y_att.reshape(t, ATT_WIDTH)
    once = lambda a: pl.BlockSpec(a.shape, lambda i: (0,) * a.ndim, pipeline_mode=pl.Buffered(1))

    def tiles(width):
        nxt = lambda i: (jnp.minimum(2 * i + 2, n_tiles - 1), 0)
        return [pl.BlockSpec((tm, width), lambda i: (0, 0), pipeline_mode=pl.Buffered(1)),
                pl.BlockSpec((tm, width), lambda i: (2 * i + 1, 0)),
                pl.BlockSpec((tm, width), nxt)]

    return pl.pallas_call(
        functools.partial(_mix_kernel, tm=tm, tiles_per_seq=s // tm),
        grid=(n_tiles // 2,),
        in_specs=tiles(d) + tiles(ATT_WIDTH) + [once(g), once(wcg), once(conv_w), once(wa), once(wb),
                                                once(wo), once(g2), once(wr), once(br)],
        out_specs=[pl.BlockSpec((2 * tm, XE_WIDTH), lambda i: (i, 0)),
                   pl.BlockSpec((2, 1, LANES), lambda i: (i, 0, 0))],
        out_shape=[
            jax.ShapeDtypeStruct((t, XE_WIDTH), F32),
            jax.ShapeDtypeStruct((n_tiles, 1, LANES), F32),
        ],
        scratch_shapes=[pltpu.VMEM((tm, d), F32), pltpu.VMEM((tm, d), F32),
                        pltpu.VMEM((8, CONV_WIDTH), F32)],
        compiler_params=pltpu.CompilerParams(
            dimension_semantics=("arbitrary",), vmem_limit_bytes=VMEM_LIMIT),
        name="mix",
    )(x2, x2, x2, y2, y2, y2, g, wcg, conv_w, wa, wb, wo, g2, wr, br)


def _moe_kernel(tcnt_ref, xe_ref, p_ref, g2_ref, wg_ref, wu_ref, wd_ref, g3_ref, wpg_ref, wple_ref,
                earlier_ref, o_ref, hs_ref, rs_ref, ys_ref, *, tm):
    i = pl.program_id(0)
    n = [tcnt_ref[i * N_GROUPS + k] for k in range(N_GROUPS)]
    starts = [jnp.int32(0)]
    for k in range(N_GROUPS - 1):
        starts.append(starts[-1] + pl.cdiv(n[k], BF16_ROWS) * BF16_ROWS)

    x1 = xe_ref[:, 0:D_MODEL]
    route = xe_ref[:, D_MODEL:XE_WIDTH]
    h2 = _rms(x1, g2_ref[...]).astype(BF16)

    lane = lax.broadcasted_iota(jnp.int32, (tm, LANES), 1)
    lanef = lane.astype(F32)
    gid = jnp.sum(jnp.where(lane == GID_LANE, route, 0.0), axis=1, keepdims=True)
    onehot = jnp.where((lanef == gid) & (lane < N_GROUPS), 1.0, 0.0)
    rank = jnp.sum(onehot * _bdot(earlier_ref[...], onehot.astype(BF16)), axis=1, keepdims=True)
    base = starts[N_GROUPS - 1].astype(F32)
    for k in range(N_GROUPS - 2, -1, -1):
        base = jnp.where(gid == float(k), starts[k].astype(F32), base)
    pos = base + rank

    unsort = (pos == lax.broadcasted_iota(jnp.int32, (tm, MOE_SORTED), 1).astype(F32)).astype(BF16)
    digit_hi = jnp.floor(pos * (1.0 / 32.0))
    digits = jnp.where(lane == 0, digit_hi, jnp.where(lane == 1, pos - 32.0 * digit_hi, 0.0))
    sel_lane = lax.broadcasted_iota(jnp.int32, (BF16_ROWS, LANES), 1)
    sel = jnp.where(sel_lane == 0, 32.0, jnp.where(sel_lane == 1, 1.0, 0.0)).astype(BF16)
    pos_row = _bdot_nt(sel, digits.astype(BF16))[0:1, :]
    sort = (lax.broadcasted_iota(jnp.int32, (MOE_SORTED, tm), 0).astype(F32) == pos_row).astype(BF16)

    hs_ref[0:MOE_SORTED] = _bdot(sort, h2).astype(BF16)
    r_hi, r_mid, r_lo = _bf16_pieces(route)
    packed = (r_hi.astype(F32) + pltpu.roll(r_mid.astype(F32), ROUTE_PITCH, axis=1)
              + pltpu.roll(r_lo.astype(F32), 2 * ROUTE_PITCH, axis=1)).astype(BF16)
    rsorted = _bdot(sort, packed)
    rs_ref[0:MOE_SORTED] = (rsorted + pltpu.roll(rsorted, LANES - ROUTE_PITCH, axis=1)
                            + pltpu.roll(rsorted, LANES - 2 * ROUTE_PITCH, axis=1))
    hs_ref[MOE_SORTED:MOE_ROWS] = jnp.zeros((MOE_CHUNK, D_MODEL), BF16)
    rs_ref[MOE_SORTED:MOE_ROWS] = jnp.zeros((MOE_CHUNK, LANES), F32)
    ys_ref[...] = jnp.zeros_like(ys_ref)

    clane = lax.broadcasted_iota(jnp.int32, (MOE_CHUNK, LANES), 1)
    for g in range(N_GROUPS):
        def chunk(c, _, g=g):
            r0 = pl.multiple_of(starts[g] + c * MOE_CHUNK, BF16_ROWS)
            hrows = hs_ref[pl.ds(r0, MOE_CHUNK), :]
            rt = rs_ref[pl.ds(r0, MOE_CHUNK), :]
            parts = []
            for e in range(EXPERTS_PER_GROUP):
                ex = g * EXPERTS_PER_GROUP + e
                a = _bdot(hrows, wg_ref[ex])
                u = _bdot(hrows, wu_ref[ex])
                ce = jnp.sum(jnp.where(clane == ex, rt, 0.0), axis=1, keepdims=True)
                parts.append(((a * jax.nn.sigmoid(a)) * u * ce).astype(BF16))
            ys_ref[pl.ds(r0, MOE_CHUNK), :] = _bdot(jnp.concatenate(parts, axis=1), wd_ref[g]).astype(BF16)
            return 0
        lax.fori_loop(0, pl.cdiv(n[g], MOE_CHUNK), chunk, 0)

    x2 = x1 + _bdot(unsort, ys_ref[0:MOE_SORTED])
    h3 = _rms(x2, g3_ref[...]).astype(BF16)
    gate = jax.nn.sigmoid(_bdot(h3, wpg_ref[...]))
    emb = _bdot(p_ref[...].astype(BF16), wple_ref[...])
    o_ref[...] = x2 + gate * emb


def _moe(tcnt, xe, p, g2, wg, wu, wd, g3, wpg, wple, *, tm):
    t = xe.shape[0]
    d = D_MODEL
    once = lambda a: pl.BlockSpec(a.shape, lambda i, c: (0,) * a.ndim, pipeline_mode=pl.Buffered(1))
    row = lambda width: pl.BlockSpec((tm, width), lambda i, c: (i, 0))
    earlier = jnp.asarray(np.tril(np.ones((tm, tm), np.float32), -1), BF16)
    return pl.pallas_call(
        functools.partial(_moe_kernel, tm=tm),
        grid_spec=pltpu.PrefetchScalarGridSpec(
            num_scalar_prefetch=1,
            grid=(t // tm,),
            in_specs=[row(XE_WIDTH), row(PLE_DIM), once(g2), once(wg), once(wu), once(wd), once(g3),
                      once(wpg), once(wple), once(earlier)],
            out_specs=row(d),
            scratch_shapes=[pltpu.VMEM((MOE_ROWS, d), BF16), pltpu.VMEM((MOE_ROWS, LANES), F32),
                            pltpu.VMEM((MOE_ROWS, d), BF16)],
        ),
        out_shape=jax.ShapeDtypeStruct((t, d), F32),
        compiler_params=pltpu.CompilerParams(
            dimension_semantics=("arbitrary",), vmem_limit_bytes=VMEM_LIMIT),
        name="moe",
    )(tcnt, xe, p, g2, wg, wu, wd, g3, wpg, wple, earlier)


def _c_select_matrices():
    pq = np.zeros((LANES, N_HEADS * LANES), np.float32)
    pk = np.zeros((LANES, N_HEADS * LANES), np.float32)
    for idx in range(C_PIECES):
        for h in range(N_HEADS):
            pq[idx * N_HEADS + h, h * LANES + QC_LANE + idx] = 1.0
            pk[idx * N_HEADS + h, h * LANES + KC_LANE + idx] = -1.0
    return jnp.asarray(pq, BF16), jnp.asarray(pk, BF16)


def kernel(x, p, attn_norm_g, w_in, b_f, q_norm_g, k_norm_g, conv_w, w_out_att, w_out_conv, w_o,
           ffn_norm_g, w_rg, b_rg, w_re, b_re, w_gate, w_up, w_down, ple_norm_g, w_pg, w_ple):
    b, s, d = x.shape
    t = b * s
    for i in range(w_in.shape[0]):
        w_qkvf, w_cg = _prep_w_in(w_in[i:i + 1], rows=TM_W_IN)
        bf3 =jnp.concatenate([b_f[i]] * C_PIECES + [jnp.zeros((LANES - C_PIECES * N_HEADS,), F32)])[None, :]
        scale = HEAD_DIM ** -0.5 * LOG2E
        qg = jnp.tile(q_norm_g[i] * scale, 2)[None, :]
        kg = jnp.tile(k_norm_g[i], 2)[None, :]
        pq, pk = _c_select_matrices()
        qt_aug, k_aug, vt = _qkv_proj(x, attn_norm_g[i][None, :], w_qkvf, bf3, qg, kg, pq, pk, tm=TM_QKV)
        y_att = _attention(qt_aug, k_aug, vt, tq=TQ_ATT, tk=TK_ATT)

        w_r = jnp.concatenate(
            [w_re[i], w_rg[i], jnp.zeros((d, LANES - N_EXPERTS - N_GROUPS), F32)], axis=1).astype(BF16)
        b_r = jnp.concatenate(
            [b_re[i], b_rg[i], jnp.zeros((LANES - N_EXPERTS - N_GROUPS,), F32)])[None, :]
        g_ffn = ffn_norm_g[i][None, :]
        xe, counts = _mix(x, y_att, attn_norm_g[i][None, :], w_cg, conv_w[i],
                          w_out_att[i].astype(BF16), w_out_conv[i].astype(BF16),
                          w_o[i].astype(BF16), g_ffn, w_r, b_r, tm=TM_MIX)
        tcnt = counts[:, 0, N_EXPERTS:N_EXPERTS + N_GROUPS].astype(jnp.int32).reshape(-1)

        w_dn = w_down[i].reshape(N_GROUPS, GROUP_WIDTH, d).astype(BF16)
        x = _moe(tcnt, xe.reshape(t, XE_WIDTH), p[i].reshape(t, PLE_DIM), g_ffn,
                 w_gate[i].astype(BF16), w_up[i].astype(BF16), w_dn,
                 ple_norm_g[i][None, :], w_pg[i].astype(BF16), w_ple[i].astype(BF16),
                 tm=TM_MOE).reshape(b, s, d)
    return x
```

```python
import functools

import numpy as np
import jax
import jax.numpy as jnp
from jax import lax
from jax.experimental import pallas as pl
from jax.experimental.pallas import tpu as pltpu

D_MODEL = 1024
N_HEADS = 8
HEAD_DIM = 64
ATT_WIDTH = N_HEADS * HEAD_DIM
CONV_WIDTH = 512
CONV_K = 3
N_GROUPS = 4
EXPERTS_PER_GROUP = 4
N_EXPERTS = 16
D_EXPERT = 256
GROUP_WIDTH = EXPERTS_PER_GROUP * D_EXPERT
PLE_DIM = 256
EPS = 1e-6
NEG_INF = -1e30
LOG2E = 1.4426950408889634

LANES = 128
BF16_ROWS = 16
C_PIECES = 3
QC_LANE = HEAD_DIM
KC_LANE = HEAD_DIM + C_PIECES
V_ROWS = HEAD_DIM + BF16_ROWS
QKV_WIDTH = 3 * ATT_WIDTH + LANES
GID_LANE = N_EXPERTS
ROUTE_PITCH = 32
XE_WIDTH = D_MODEL + LANES
VMEM_LIMIT = 56 * 1024 * 1024

TM_QKV = 512
TQ_ATT = 1024
TK_ATT = 256
ATT_HEADS = 2
TM_MIX = 512
TM_MOE = TM_MIX
MOE_SORTED = -(-(TM_MOE + N_GROUPS * BF16_ROWS) // LANES) * LANES
MOE_CHUNK = 144
MOE_ROWS = MOE_SORTED + MOE_CHUNK

F32 = jnp.float32
BF16 = jnp.bfloat16


def _rms(xf, g):
    return xf * lax.rsqrt(jnp.mean(xf * xf, axis=-1, keepdims=True) + EPS) * g


def _log_sigmoid(z):
    return jnp.minimum(z, 0.0) - jnp.log1p(jnp.exp(-jnp.abs(z)))


def _bdot(a, b):
    return jnp.dot(a, b, preferred_element_type=F32)


def _bdot_nt(a, b):
    return lax.dot_general(a, b, (((1,), (1,)), ((), ())), preferred_element_type=F32)


def _bf16_pieces(x):
    hi = x.astype(BF16)
    r1 = x - hi.astype(F32)
    mid = r1.astype(BF16)
    lo = (r1 - mid.astype(F32)).astype(BF16)
    return hi, mid, lo


def _qkv_kernel(x0_ref, xa_ref, xb_ref, g_ref, w_ref, bf_ref, qg_ref, kg_ref, pq_ref, pk_ref,
                qt_ref, k_ref, vt_ref, pa_ref, pb_ref, carry_ref, *, tm, tiles_per_seq):
    step = pl.program_id(0)

    def project(x_ref, dst_ref):
        dst_ref[...] = _bdot(_rms(x_ref[...], g_ref[...]).astype(BF16), w_ref[...])

    @pl.when(step == 0)
    def _():
        carry_ref[...] = jnp.zeros_like(carry_ref)
        project(x0_ref, pa_ref)

    lane = lax.broadcasted_iota(jnp.int32, (tm, LANES), 1)
    row = lax.broadcasted_iota(jnp.int32, (tm, LANES), 0)
    q_ones = jnp.where((lane >= KC_LANE) & (lane < KC_LANE + C_PIECES), 1.0, 0.0)
    k_ones = jnp.where((lane >= QC_LANE) & (lane < QC_LANE + C_PIECES), 1.0, 0.0)
    low = lane < HEAD_DIM
    ext_row = lax.broadcasted_iota(jnp.int32, (V_ROWS - HEAD_DIM, tm), 0)
    v_ext = jnp.where(ext_row == 0, 1.0, 0.0)

    def finish(p_ref, tile, half):
        rows = slice(half * tm, (half + 1) * tm)
        for j in range(N_HEADS // 2):
            pair_t = p_ref[:, 2 * ATT_WIDTH + LANES * j: 2 * ATT_WIDTH + LANES * (j + 1)].T
            for hh in range(2):
                vt_ref[0, 2 * j + hh, :, rows] = jnp.concatenate(
                    [pair_t[HEAD_DIM * hh: HEAD_DIM * (hh + 1)], v_ext], axis=0).astype(BF16)

        f3 = p_ref[:, 3 * ATT_WIDTH:QKV_WIDTH] + bf_ref[...]
        c = jnp.where(lane < C_PIECES * N_HEADS, _log_sigmoid(f3), 0.0)
        sh = 1
        while sh < tm:
            c = c + jnp.where(row >= sh, pltpu.roll(c, sh, axis=0), 0.0)
            sh *= 2
        c = c + jnp.where(tile % tiles_per_seq == 0, 0.0, carry_ref[...])
        carry_ref[...] = c[tm - 1:tm, :]
        hi, mid, lo = _bf16_pieces(c * LOG2E)
        piece = jnp.where(lane < N_HEADS, hi, jnp.where(lane < 2 * N_HEADS, mid, lo))
        qc = _bdot(piece, pq_ref[...])
        kc = _bdot(piece, pk_ref[...])

        def heads(base, gain_ref, cmat, ones, store):
            gain = gain_ref[...]
            for j in range(N_HEADS // 2):
                pair = p_ref[:, base + LANES * j: base + LANES * (j + 1)]
                sq = pair * pair
                ss_lo = jnp.sum(jnp.where(low, sq, 0.0), axis=1, keepdims=True)
                ss_hi = jnp.sum(jnp.where(low, 0.0, sq), axis=1, keepdims=True)
                n_lo = pair * lax.rsqrt(ss_lo * (1.0 / HEAD_DIM) + EPS) * gain
                n_hi = pltpu.roll(pair, HEAD_DIM, axis=1) * lax.rsqrt(ss_hi * (1.0 / HEAD_DIM) + EPS) * gain
                for hh, nrm in ((2 * j, n_lo), (2 * j + 1, n_hi)):
                    store(hh, jnp.where(low, nrm, 0.0) + cmat[:, LANES * hh: LANES * (hh + 1)] + ones)

        def store_q(hh, aug):
            qt_ref[0, hh, :, rows] = aug.T.astype(BF16)

        def store_k(hh, aug):
            k_ref[0, hh, rows, :] = aug.astype(BF16)

        heads(0, qg_ref, qc, q_ones, store_q)
        heads(ATT_WIDTH, kg_ref, kc, k_ones, store_k)

    finish(pa_ref, 2 * step, 0)
    project(xa_ref, pb_ref)
    finish(pb_ref, 2 * step + 1, 1)
    project(xb_ref, pa_ref)


def _qkv_proj(x, g, w, bf3, qg, kg, pq, pk, *, tm):
    b, s, d = x.shape
    x2 = x.reshape(b * s, d)
    n_tiles = b * s // tm
    per_seq = s // (2 * tm)
    full = lambda shape: pl.BlockSpec(shape, lambda i: (0,) * len(shape))
    return pl.pallas_call(
        functools.partial(_qkv_kernel, tm=tm, tiles_per_seq=s // tm),
        grid=(n_tiles // 2,),
        in_specs=[
            pl.BlockSpec((tm, d), lambda i: (0, 0)),
            pl.BlockSpec((tm, d), lambda i: (2 * i + 1, 0)),
            pl.BlockSpec((tm, d), lambda i: (jnp.minimum(2 * i + 2, n_tiles - 1), 0)),
            full(g.shape), full(w.shape), full(bf3.shape), full(qg.shape), full(kg.shape),
            full(pq.shape), full(pk.shape),
        ],
        out_specs=[pl.BlockSpec((1, N_HEADS, LANES, 2 * tm), lambda i: (i // per_seq, 0, 0, i % per_seq)),
                   pl.BlockSpec((1, N_HEADS, 2 * tm, LANES), lambda i: (i // per_seq, 0, i % per_seq, 0)),
                   pl.BlockSpec((1, N_HEADS, V_ROWS, 2 * tm), lambda i: (i // per_seq, 0, 0, i % per_seq))],
        out_shape=[
            jax.ShapeDtypeStruct((b, N_HEADS, LANES, s), BF16),
            jax.ShapeDtypeStruct((b, N_HEADS, s, LANES), BF16),
            jax.ShapeDtypeStruct((b, N_HEADS, V_ROWS, s), BF16),
        ],
        scratch_shapes=[pltpu.VMEM((tm, QKV_WIDTH), F32), pltpu.VMEM((tm, QKV_WIDTH), F32),
                        pltpu.VMEM((1, LANES), F32)],
        compiler_params=pltpu.CompilerParams(
            dimension_semantics=("arbitrary",), vmem_limit_bytes=VMEM_LIMIT),
        name="qkv_proj",
    )(x2, x2, x2, g, w, bf3, qg, kg, pq, pk)


def _attn_kernel(qt_ref, qn_ref, k_ref, vt_ref, bias_ref, o_ref, sa_ref, sb_ref, *, tq, tk):
    qi = pl.program_id(2)
    n_sub = tq // tk
    bufs = (sa_ref, sb_ref)

    def produce(j, s_ref, lo=0, q_ref=qt_ref):
        start = pl.multiple_of(j * tk, tk)
        for hh in range(ATT_HEADS):
            s_ref[hh, :, lo:tq] = _bdot(k_ref[0, hh, pl.ds(start, tk), :], q_ref[0, hh, :, lo:tq])

    def consume(j, s_ref, carry, lo=0, masked=False):
        start = pl.multiple_of(j * tk, tk)
        out = []
        for hh in range(ATT_HEADS):
            m, acc = carry[hh]

            def read():
                if not masked:
                    return s_ref[hh, :, lo:tq]
                tri = s_ref[hh, :, lo:lo + tk] + bias_ref[...]
                return tri if lo + tk == tq else jnp.concatenate([tri, s_ref[hh, :, lo + tk:tq]], axis=1)

            m_new = jnp.maximum(m[:, lo:tq], jnp.max(read(), axis=0, keepdims=True))
            alpha = jnp.exp2(m[:, lo:tq] - m_new)
            p = jnp.exp2(read() - m_new).astype(BF16)
            acc_new = alpha * acc[:, lo:tq] + _bdot(vt_ref[0, hh, :, pl.ds(start, tk)], p)
            if lo:
                m_new = jnp.concatenate([m[:, 0:lo], m_new], axis=1)
                acc_new = jnp.concatenate([acc[:, 0:lo], acc_new], axis=1)
            out.append((m_new, acc_new))
        return tuple(out)

    def trip(r, carry):
        for t in range(n_sub):
            j = r * n_sub + t
            produce(j + 1, bufs[(t + 1) % 2])
            carry = consume(j, bufs[t % 2], carry)
        return carry

    @pl.when(qi == 0)
    def _():
        produce(0, sa_ref)

    init = tuple((jnp.full((1, tq), NEG_INF, F32), jnp.zeros((V_ROWS, tq), F32)) for _ in range(ATT_HEADS))
    carry = lax.fori_loop(0, qi, trip, init)
    for t in range(n_sub):
        j = qi * n_sub + t
        if t + 1 < n_sub:
            produce(j + 1, bufs[(t + 1) % 2], lo=(t + 1) * tk)
        else:
            produce(0, bufs[(t + 1) % 2], q_ref=qn_ref)
        carry = consume(j, bufs[t % 2], carry, lo=t * tk, masked=True)
    y_t = jnp.concatenate([acc[0:HEAD_DIM] / acc[HEAD_DIM:HEAD_DIM + 1] for _, acc in carry], axis=0)
    o_ref[0] = y_t.T.astype(BF16)


def _attention(qt_aug, k_aug, vt, *, tq, tk):
    b, nh, s, _ = k_aug.shape
    assert tq % (2 * tk) == 0 and s % tq == 0
    future = np.arange(tk)[:, None] > np.arange(tk)[None, :]
    bias = jnp.asarray(np.where(future, np.float32(NEG_INF), np.float32(0.0)))
    nh_step = ATT_HEADS
    grid = (b, nh // nh_step, s // tq)
    return pl.pallas_call(
        functools.partial(_attn_kernel, tq=tq, tk=tk),
        grid=grid,
        in_specs=[
            pl.BlockSpec((1, nh_step, LANES, tq), lambda i, h, j: (i, h, 0, j)),
            pl.BlockSpec((1, nh_step, LANES, tq), lambda i, h, j: (i, h, 0, jnp.minimum(j + 1, s // tq - 1))),
            pl.BlockSpec((1, nh_step, s, LANES), lambda i, h, j: (i, h, 0, 0)),
            pl.BlockSpec((1, nh_step, V_ROWS, s), lambda i, h, j: (i, h, 0, 0)),
            pl.BlockSpec((tk, tk), lambda i, h, j: (0, 0)),
        ],
        out_specs=pl.BlockSpec((1, tq, nh_step * HEAD_DIM), lambda i, h, j: (i, j, h)),
        out_shape=jax.ShapeDtypeStruct((b, s, ATT_WIDTH), BF16),
        scratch_shapes=[pltpu.VMEM((nh_step, tk, tq), F32)] * 2,
        compiler_params=pltpu.CompilerParams(
            dimension_semantics=("arbitrary", "arbitrary", "arbitrary"),
            vmem_limit_bytes=VMEM_LIMIT),
        name="fox_attention",
    )(qt_aug, qt_aug, k_aug, vt, bias)


def _mix_kernel(x0_ref, xa_ref, xb_ref, y0_ref, ya_ref, yb_ref, g_ref, wcg_ref, cw_ref, wa_ref, wb_ref,
                wo_ref, g2_ref, wr_ref, br_ref, xe_ref, cnt_ref, ra_ref, rb_ref, carry_ref,
                *, tm, tiles_per_seq):
    step = pl.program_id(0)

    def residual(x_ref, y_ref, tile, dst_ref):
        x = x_ref[...]
        h = _rms(x, g_ref[...]).astype(BF16)
        pc = _bdot(h, wcg_ref[...])
        cw = CONV_WIDTH
        cb = pc[:, 0:cw]
        prod = pc[:, cw:2 * cw] * pc[:, 2 * cw:3 * cw]
        ga = pc[:, 3 * cw:3 * cw + D_MODEL]
        gb = pc[:, 3 * cw + D_MODEL:3 * cw + 2 * D_MODEL]

        prev = jnp.where(tile % tiles_per_seq == 0, 0.0, carry_ref[...])
        crow = lax.broadcasted_iota(jnp.int32, (tm, cw), 0)
        m1 = jnp.where(crow == 0, prev[7:8, :], pltpu.roll(prod, 1, axis=0))
        m2 = jnp.where(crow == 0, prev[6:7, :],
                       jnp.where(crow == 1, prev[7:8, :], pltpu.roll(prod, 2, axis=0)))
        carry_ref[...] = prod[tm - 8:tm, :]
        w = cw_ref[...]
        y_conv = cb * (w[0:1, :] * m2 + w[1:2, :] * m1 + w[2:3, :] * prod)

        a = _bdot(y_ref[...], wa_ref[...])
        bb = _bdot(y_conv.astype(BF16), wb_ref[...])
        merged = jax.nn.sigmoid(ga) * a + jax.nn.sigmoid(gb) * bb
        dst_ref[...] = x + _bdot(merged.astype(BF16), wo_ref[...])

    @pl.when(step == 0)
    def _():
        carry_ref[...] = jnp.zeros_like(carry_ref)
        residual(x0_ref, y0_ref, 0, ra_ref)

    _route(ra_ref, g2_ref, wr_ref, br_ref, xe_ref, cnt_ref, 0, tm)
    residual(xa_ref, ya_ref, 2 * step + 1, rb_ref)
    _route(rb_ref, g2_ref, wr_ref, br_ref, xe_ref, cnt_ref, 1, tm)
    residual(xb_ref, yb_ref, 2 * step + 2, ra_ref)


def _route(x1_ref, g2_ref, wr_ref, br_ref, xe_ref, cnt_ref, half, tm):
    rows = slice(half * tm, (half + 1) * tm)
    x1 = x1_ref[...]
    xe_ref[rows, 0:D_MODEL] = x1
    h2 = _rms(x1, g2_ref[...]).astype(BF16)

    logits = _bdot(h2, wr_ref[...]) + br_ref[...]
    lane = lax.broadcasted_iota(jnp.int32, (tm, LANES), 1)
    lanef = lane.astype(F32)
    is_g = (lane >= N_EXPERTS) & (lane < N_EXPERTS + N_GROUPS)
    gl = jnp.where(is_g, logits, NEG_INF)
    gmax = jnp.max(gl, axis=1, keepdims=True)
    gsum = jnp.sum(jnp.exp(gl - gmax), axis=1, keepdims=True)
    g_val = 1.0 / gsum
    g_lane = jnp.min(jnp.where(gl == gmax, lanef, float(LANES)), axis=1, keepdims=True)
    e_lo = (g_lane - float(N_EXPERTS)) * float(EXPERTS_PER_GROUP)
    in_grp = (lanef >= e_lo) & (lanef < e_lo + float(EXPERTS_PER_GROUP))
    el = jnp.where(in_grp, logits, NEG_INF)
    t1 = jnp.max(el, axis=1, keepdims=True)
    i1 = jnp.min(jnp.where(el == t1, lanef, float(LANES)), axis=1, keepdims=True)
    el2 = jnp.where(lanef == i1, NEG_INF, el)
    t2 = jnp.max(el2, axis=1, keepdims=True)
    i2 = jnp.min(jnp.where(el2 == t2, lanef, float(LANES)), axis=1, keepdims=True)
    e2 = jnp.exp(t2 - t1)
    w1 = g_val / (1.0 + e2)
    w2 = g_val * e2 / (1.0 + e2)
    comb = jnp.where(lanef == i1, w1, 0.0) + jnp.where(lanef == i2, w2, 0.0)

    cnt_ref[half] = jnp.sum(jnp.where(is_g & (lanef == g_lane), 1.0, 0.0), axis=0, keepdims=True)
    xe_ref[rows, D_MODEL:XE_WIDTH] = comb + jnp.where(lane == GID_LANE, g_lane - float(N_EXPERTS), 0.0)


def _mix(x, y_att, g, wcg, conv_w, wa, wb, wo, g2, wr, br, *, tm):
    b, s, d = x.shape
    t = b * s
    n_tiles = t // tm
    x2 = x.reshape(t, d)
    y2 = y_att.reshape(t, ATT_WIDTH)
    once = lambda a: pl.BlockSpec(a.shape, lambda i: (0,) * a.ndim, pipeline_mode=pl.Buffered(1))

    def tiles(width):
        nxt = lambda i: (jnp.minimum(2 * i + 2, n_tiles - 1), 0)
        return [pl.BlockSpec((tm, width), lambda i: (0, 0), pipeline_mode=pl.Buffered(1)),
                pl.BlockSpec((tm, width), lambda i: (2 * i + 1, 0)),
                pl.BlockSpec((tm, width), nxt)]

    return pl.pallas_call(
        functools.partial(_mix_kernel, tm=tm, tiles_per_seq=s // tm),
        grid=(n_tiles // 2,),
        in_specs=tiles(d) + tiles(ATT_WIDTH) + [once(g), once(wcg), once(conv_w), once(wa), once(wb),
                                                once(wo), once(g2), once(wr), once(br)],
        out_specs=[pl.BlockSpec((2 * tm, XE_WIDTH), lambda i: (i, 0)),
                   pl.BlockSpec((2, 1, LANES), lambda i: (i, 0, 0))],
        out_shape=[
            jax.ShapeDtypeStruct((t, XE_WIDTH), F32),
            jax.ShapeDtypeStruct((n_tiles, 1, LANES), F32),
        ],
        scratch_shapes=[pltpu.VMEM((tm, d), F32), pltpu.VMEM((tm, d), F32),
                        pltpu.VMEM((8, CONV_WIDTH), F32)],
        compiler_params=pltpu.CompilerParams(
            dimension_semantics=("arbitrary",), vmem_limit_bytes=VMEM_LIMIT),
        name="mix",
    )(x2, x2, x2, y2, y2, y2, g, wcg, conv_w, wa, wb, wo, g2, wr, br)


def _moe_kernel(tcnt_ref, xe_ref, p_ref, g2_ref, wg_ref, wu_ref, wd_ref, g3_ref, wpg_ref, wple_ref,
                earlier_ref, o_ref, hs_ref, rs_ref, ys_ref, *, tm):
    i = pl.program_id(0)
    n = [tcnt_ref[i * N_GROUPS + k] for k in range(N_GROUPS)]
    starts = [jnp.int32(0)]
    for k in range(N_GROUPS - 1):
        starts.append(starts[-1] + pl.cdiv(n[k], BF16_ROWS) * BF16_ROWS)

    x1 = xe_ref[:, 0:D_MODEL]
    route = xe_ref[:, D_MODEL:XE_WIDTH]
    h2 = _rms(x1, g2_ref[...]).astype(BF16)

    lane = lax.broadcasted_iota(jnp.int32, (tm, LANES), 1)
    lanef = lane.astype(F32)
    gid = jnp.sum(jnp.where(lane == GID_LANE, route, 0.0), axis=1, keepdims=True)
    onehot = jnp.where((lanef == gid) & (lane < N_GROUPS), 1.0, 0.0)
    rank = jnp.sum(onehot * _bdot(earlier_ref[...], onehot.astype(BF16)), axis=1, keepdims=True)
    base = starts[N_GROUPS - 1].astype(F32)
    for k in range(N_GROUPS - 2, -1, -1):
        base = jnp.where(gid == float(k), starts[k].astype(F32), base)
    pos = base + rank

    unsort = (pos == lax.broadcasted_iota(jnp.int32, (tm, MOE_SORTED), 1).astype(F32)).astype(BF16)
    digit_hi = jnp.floor(pos * (1.0 / 32.0))
    digits = jnp.where(lane == 0, digit_hi, jnp.where(lane == 1, pos - 32.0 * digit_hi, 0.0))
    sel_lane = lax.broadcasted_iota(jnp.int32, (BF16_ROWS, LANES), 1)
    sel = jnp.where(sel_lane == 0, 32.0, jnp.where(sel_lane == 1, 1.0, 0.0)).astype(BF16)
    pos_row = _bdot_nt(sel, digits.astype(BF16))[0:1, :]
    sort = (lax.broadcasted_iota(jnp.int32, (MOE_SORTED, tm), 0).astype(F32) == pos_row).astype(BF16)

    hs_ref[0:MOE_SORTED] = _bdot(sort, h2).astype(BF16)
    r_hi, r_mid, r_lo = _bf16_pieces(route)
    packed = (r_hi.astype(F32) + pltpu.roll(r_mid.astype(F32), ROUTE_PITCH, axis=1)
              + pltpu.roll(r_lo.astype(F32), 2 * ROUTE_PITCH, axis=1)).astype(BF16)
    rsorted = _bdot(sort, packed)
    rs_ref[0:MOE_SORTED] = (rsorted + pltpu.roll(rsorted, LANES - ROUTE_PITCH, axis=1)
                            + pltpu.roll(rsorted, LANES - 2 * ROUTE_PITCH, axis=1))
    hs_ref[MOE_SORTED:MOE_ROWS] = jnp.zeros((MOE_CHUNK, D_MODEL), BF16)
    rs_ref[MOE_SORTED:MOE_ROWS] = jnp.zeros((MOE_CHUNK, LANES), F32)
    ys_ref[...] = jnp.zeros_like(ys_ref)

    clane = lax.broadcasted_iota(jnp.int32, (MOE_CHUNK, LANES), 1)
    for g in range(N_GROUPS):
        def chunk(c, _, g=g):
            r0 = pl.multiple_of(starts[g] + c * MOE_CHUNK, BF16_ROWS)
            hrows = hs_ref[pl.ds(r0, MOE_CHUNK), :]
            rt = rs_ref[pl.ds(r0, MOE_CHUNK), :]
            parts = []
            for e in range(EXPERTS_PER_GROUP):
                ex = g * EXPERTS_PER_GROUP + e
                a = _bdot(hrows, wg_ref[ex])
                u = _bdot(hrows, wu_ref[ex])
                ce = jnp.sum(jnp.where(clane == ex, rt, 0.0), axis=1, keepdims=True)
                parts.append(((a * jax.nn.sigmoid(a)) * u * ce).astype(BF16))
            ys_ref[pl.ds(r0, MOE_CHUNK), :] = _bdot(jnp.concatenate(parts, axis=1), wd_ref[g]).astype(BF16)
            return 0
        lax.fori_loop(0, pl.cdiv(n[g], MOE_CHUNK), chunk, 0)

    x2 = x1 + _bdot(unsort, ys_ref[0:MOE_SORTED])
    h3 = _rms(x2, g3_ref[...]).astype(BF16)
    gate = jax.nn.sigmoid(_bdot(h3, wpg_ref[...]))
    emb = _bdot(p_ref[...].astype(BF16), wple_ref[...])
    o_ref[...] = x2 + gate * emb


def _moe(tcnt, xe, p, g2, wg, wu, wd, g3, wpg, wple, *, tm):
    t = xe.shape[0]
    d = D_MODEL
    once = lambda a: pl.BlockSpec(a.shape, lambda i, c: (0,) * a.ndim, pipeline_mode=pl.Buffered(1))
    row = lambda width: pl.BlockSpec((tm, width), lambda i, c: (i, 0))
    earlier = jnp.asarray(np.tril(np.ones((tm, tm), np.float32), -1), BF16)
    return pl.pallas_call(
        functools.partial(_moe_kernel, tm=tm),
        grid_spec=pltpu.PrefetchScalarGridSpec(
            num_scalar_prefetch=1,
            grid=(t // tm,),
            in_specs=[row(XE_WIDTH), row(PLE_DIM), once(g2), once(wg), once(wu), once(wd), once(g3),
                      once(wpg), once(wple), once(earlier)],
            out_specs=row(d),
            scratch_shapes=[pltpu.VMEM((MOE_ROWS, d), BF16), pltpu.VMEM((MOE_ROWS, LANES), F32),
                            pltpu.VMEM((MOE_ROWS, d), BF16)],
        ),
        out_shape=jax.ShapeDtypeStruct((t, d), F32),
        compiler_params=pltpu.CompilerParams(
            dimension_semantics=("arbitrary",), vmem_limit_bytes=VMEM_LIMIT),
        name="moe",
    )(tcnt, xe, p, g2, wg, wu, wd, g3, wpg, wple, earlier)


def _c_select_matrices():
    pq = np.zeros((LANES, N_HEADS * LANES), np.float32)
    pk = np.zeros((LANES, N_HEADS * LANES), np.float32)
    for idx in range(C_PIECES):
        for h in range(N_HEADS):
            pq[idx * N_HEADS + h, h * LANES + QC_LANE + idx] = 1.0
            pk[idx * N_HEADS + h, h * LANES + KC_LANE + idx] = -1.0
    return jnp.asarray(pq, BF16), jnp.asarray(pk, BF16)


def kernel(x, p, attn_norm_g, w_in, b_f, q_norm_g, k_norm_g, conv_w, w_out_att, w_out_conv, w_o,
           ffn_norm_g, w_rg, b_rg, w_re, b_re, w_gate, w_up, w_down, ple_norm_g, w_pg, w_ple):
    b, s, d = x.shape
    t = b * s
    aw = ATT_WIDTH
    for i in range(w_in.shape[0]):
        wi = w_in[i]
        wf = wi[:, 3 * aw:3 * aw + N_HEADS]
        w_qkvf = jnp.concatenate(
            [wi[:, :3 * aw], wf, wf, wf, jnp.zeros((d, LANES - C_PIECES * N_HEADS), F32)],
            axis=1).astype(BF16)
        w_cg = wi[:, 3 * aw + N_HEADS:].astype(BF16)
        bf3 = jnp.concatenate([b_f[i]] * C_PIECES + [jnp.zeros((LANES - C_PIECES * N_HEADS,), F32)])[None, :]
        scale = HEAD_DIM ** -0.5 * LOG2E
        qg = jnp.tile(q_norm_g[i] * scale, 2)[None, :]
        kg = jnp.tile(k_norm_g[i], 2)[None, :]
        pq, pk = _c_select_matrices()
        qt_aug, k_aug, vt = _qkv_proj(x, attn_norm_g[i][None, :], w_qkvf, bf3, qg, kg, pq, pk, tm=TM_QKV)
        y_att = _attention(qt_aug, k_aug, vt, tq=TQ_ATT, tk=TK_ATT)

        w_r = jnp.concatenate(
            [w_re[i], w_rg[i], jnp.zeros((d, LANES - N_EXPERTS - N_GROUPS), F32)], axis=1).astype(BF16)
        b_r = jnp.concatenate(
            [b_re[i], b_rg[i], jnp.zeros((LANES - N_EXPERTS - N_GROUPS,), F32)])[None, :]
        g_ffn = ffn_norm_g[i][None, :]
        xe, counts = _mix(x, y_att, attn_norm_g[i][None, :], w_cg, conv_w[i],
                          w_out_att[i].astype(BF16), w_out_conv[i].astype(BF16),
                          w_o[i].astype(BF16), g_ffn, w_r, b_r, tm=TM_MIX)
        tcnt = counts[:, 0, N_EXPERTS:N_EXPERTS + N_GROUPS].astype(jnp.int32).reshape(-1)

        w_dn = w_down[i].reshape(N_GROUPS, GROUP_WIDTH, d).astype(BF16)
        x = _moe(tcnt, xe.reshape(t, XE_WIDTH), p[i].reshape(t, PLE_DIM), g_ffn,
                 w_gate[i].astype(BF16), w_up[i].astype(BF16), w_dn,
                 ple_norm_g[i][None, :], w_pg[i].astype(BF16), w_ple[i].astype(BF16),
                 tm=TM_MOE).reshape(b, s, d)
    return x
```

```python
import functools

import numpy as np
import jax
import jax.numpy as jnp
from jax import lax
from jax.experimental import pallas as pl
from jax.experimental.pallas import tpu as pltpu

D_MODEL = 1024
N_HEADS = 8
HEAD_DIM = 64
ATT_WIDTH = N_HEADS * HEAD_DIM
CONV_WIDTH = 512
CONV_K = 3
N_GROUPS = 4
EXPERTS_PER_GROUP = 4
N_EXPERTS = 16
D_EXPERT = 256
GROUP_WIDTH = EXPERTS_PER_GROUP * D_EXPERT
PLE_DIM = 256
EPS = 1e-6
NEG_INF = -1e30
LOG2E = 1.4426950408889634

LANES = 128
BF16_ROWS = 16
C_PIECES = 3
QC_LANE = HEAD_DIM
KC_LANE = HEAD_DIM + C_PIECES
V_ROWS = HEAD_DIM + BF16_ROWS
QKV_WIDTH = 3 * ATT_WIDTH + LANES
GID_LANE = N_EXPERTS
ROUTE_PITCH = 32
XE_WIDTH = D_MODEL + LANES
VMEM_LIMIT = 56 * 1024 * 1024

TM_QKV = 512
TQ_ATT = 1024
TK_ATT = 256
ATT_HEADS = 2
TM_MIX = 512
TM_MOE = TM_MIX
MOE_SORTED = -(-(TM_MOE + N_GROUPS * BF16_ROWS) // LANES) * LANES
MOE_CHUNK = 144
MOE_ROWS = MOE_SORTED + MOE_CHUNK

F32 = jnp.float32
BF16 = jnp.bfloat16


def _rms(xf, g):
    return xf * lax.rsqrt(jnp.mean(xf * xf, axis=-1, keepdims=True) + EPS) * g


def _log_sigmoid(z):
    return jnp.minimum(z, 0.0) - jnp.log1p(jnp.exp(-jnp.abs(z)))


def _bdot(a, b):
    return jnp.dot(a, b, preferred_element_type=F32)


def _bdot_nt(a, b):
    return lax.dot_general(a, b, (((1,), (1,)), ((), ())), preferred_element_type=F32)


def _bf16_pieces(x):
    hi = x.astype(BF16)
    r1 = x - hi.astype(F32)
    mid = r1.astype(BF16)
    lo = (r1 - mid.astype(F32)).astype(BF16)
    return hi, mid, lo


def _qkv_kernel(x0_ref, xa_ref, xb_ref, g_ref, w_ref, bf_ref, qg_ref, kg_ref, pk_ref,
                qt_ref, k_ref, vt_ref, pa_ref, pb_ref, carry_ref, *, tm, tiles_per_seq):
    step = pl.program_id(0)

    def project(x_ref, dst_ref):
        dst_ref[...] = _bdot(_rms(x_ref[...], g_ref[...]).astype(BF16), w_ref[...])

    @pl.when(step == 0)
    def _():
        carry_ref[...] = jnp.zeros_like(carry_ref)
        project(x0_ref, pa_ref)

    lane = lax.broadcasted_iota(jnp.int32, (tm, LANES), 1)
    row = lax.broadcasted_iota(jnp.int32, (tm, LANES), 0)
    k_ones = jnp.where((lane >= QC_LANE) & (lane < QC_LANE + C_PIECES), 1.0, 0.0)
    low = lane < HEAD_DIM
    ext_row = lax.broadcasted_iota(jnp.int32, (V_ROWS - HEAD_DIM, tm), 0)
    v_ext = jnp.where(ext_row == 0, 1.0, 0.0)

    def finish(p_ref, tile, half):
        rows = slice(half * tm, (half + 1) * tm)
        for j in range(N_HEADS // 2):
            pair_t = p_ref[:, 2 * ATT_WIDTH + LANES * j: 2 * ATT_WIDTH + LANES * (j + 1)].T
            for hh in range(2):
                vt_ref[0, 2 * j + hh, :, rows] = jnp.concatenate(
                    [pair_t[HEAD_DIM * hh: HEAD_DIM * (hh + 1)], v_ext], axis=0).astype(BF16)

        f3 = p_ref[:, 3 * ATT_WIDTH:QKV_WIDTH] + bf_ref[...]
        c = jnp.where(lane < C_PIECES * N_HEADS, _log_sigmoid(f3), 0.0)
        sh = 1
        while sh < tm:
            c = c + jnp.where(row >= sh, pltpu.roll(c, sh, axis=0), 0.0)
            sh *= 2
        c = c + jnp.where(tile % tiles_per_seq == 0, 0.0, carry_ref[...])
        carry_ref[...] = c[tm - 1:tm, :]
        hi, mid, lo = _bf16_pieces(c * LOG2E)
        piece = jnp.where(lane < N_HEADS, hi, jnp.where(lane < 2 * N_HEADS, mid, lo))

        def rms_scales(pair):
            sq = pair * pair
            ss_lo = jnp.sum(jnp.where(low, sq, 0.0), axis=1, keepdims=True)
            ss_hi = jnp.sum(jnp.where(low, 0.0, sq), axis=1, keepdims=True)
            return (lax.rsqrt(ss_lo * (1.0 / HEAD_DIM) + EPS), lax.rsqrt(ss_hi * (1.0 / HEAD_DIM) + EPS))

        piece_t = piece.astype(F32).T
        erow = lax.broadcasted_iota(jnp.int32, (8, tm), 0)
        pad_t = jnp.zeros((LANES - HEAD_DIM - 8, tm), F32)
        qgain = qg_ref[...]
        for j in range(N_HEADS // 2):
            pair = p_ref[:, LANES * j: LANES * (j + 1)]
            r_lo, r_hi = rms_scales(pair)
            qn_t = (pair * jnp.where(low, r_lo, r_hi) * qgain).T
            for hh in range(2):
                h = 2 * j + hh
                c_t = jnp.where(erow >= C_PIECES, jnp.where(erow < 2 * C_PIECES, 1.0, 0.0), 0.0)
                for idx in range(C_PIECES):
                    c_t = jnp.where(erow == idx, piece_t[idx * N_HEADS + h: idx * N_HEADS + h + 1], c_t)
                qt_ref[0, h, :, rows] = jnp.concatenate(
                    [qn_t[HEAD_DIM * hh: HEAD_DIM * (hh + 1)], c_t, pad_t], axis=0).astype(BF16)

        kc = _bdot(piece, pk_ref[...])
        kgain = kg_ref[...]
        for j in range(N_HEADS // 2):
            pair = p_ref[:, ATT_WIDTH + LANES * j: ATT_WIDTH + LANES * (j + 1)]
            r_lo, r_hi = rms_scales(pair)
            n_lo = pair * r_lo * kgain
            n_hi = pltpu.roll(pair, HEAD_DIM, axis=1) * r_hi * kgain
            for h, nrm in ((2 * j, n_lo), (2 * j + 1, n_hi)):
                aug = jnp.where(low, nrm, 0.0) + kc[:, LANES * h: LANES * (h + 1)] + k_ones
                k_ref[0, h, rows, :] = aug.astype(BF16)

    finish(pa_ref, 2 * step, 0)
    project(xa_ref, pb_ref)
    finish(pb_ref, 2 * step + 1, 1)
    project(xb_ref, pa_ref)


def _qkv_proj(x, g, w, bf3, qg, kg, pk, *, tm):
    b, s, d = x.shape
    x2 = x.reshape(b * s, d)
    n_tiles = b * s // tm
    per_seq = s // (2 * tm)
    full = lambda shape: pl.BlockSpec(shape, lambda i: (0,) * len(shape))
    return pl.pallas_call(
        functools.partial(_qkv_kernel, tm=tm, tiles_per_seq=s // tm),
        grid=(n_tiles // 2,),
        in_specs=[
            pl.BlockSpec((tm, d), lambda i: (0, 0)),
            pl.BlockSpec((tm, d), lambda i: (2 * i + 1, 0)),
            pl.BlockSpec((tm, d), lambda i: (jnp.minimum(2 * i + 2, n_tiles - 1), 0)),
            full(g.shape), full(w.shape), full(bf3.shape), full(qg.shape), full(kg.shape),
            full(pk.shape),
        ],
        out_specs=[pl.BlockSpec((1, N_HEADS, LANES, 2 * tm), lambda i: (i // per_seq, 0, 0, i % per_seq)),
                   pl.BlockSpec((1, N_HEADS, 2 * tm, LANES), lambda i: (i // per_seq, 0, i % per_seq, 0)),
                   pl.BlockSpec((1, N_HEADS, V_ROWS, 2 * tm), lambda i: (i // per_seq, 0, 0, i % per_seq))],
        out_shape=[
            jax.ShapeDtypeStruct((b, N_HEADS, LANES, s), BF16),
            jax.ShapeDtypeStruct((b, N_HEADS, s, LANES), BF16),
            jax.ShapeDtypeStruct((b, N_HEADS, V_ROWS, s), BF16),
        ],
        scratch_shapes=[pltpu.VMEM((tm, QKV_WIDTH), F32), pltpu.VMEM((tm, QKV_WIDTH), F32),
                        pltpu.VMEM((1, LANES), F32)],
        compiler_params=pltpu.CompilerParams(
            dimension_semantics=("arbitrary",), vmem_limit_bytes=VMEM_LIMIT),
        name="qkv_proj",
    )(x2, x2, x2, g, w, bf3, qg, kg, pk)


def _attn_kernel(qt_ref, qn_ref, k_ref, vt_ref, bias_ref, o_ref, sa_ref, sb_ref, *, tq, tk):
    qi = pl.program_id(2)
    n_sub = tq // tk
    bufs = (sa_ref, sb_ref)

    def produce(j, s_ref, lo=0, q_ref=qt_ref):
        start = pl.multiple_of(j * tk, tk)
        for hh in range(ATT_HEADS):
            s_ref[hh, :, lo:tq] = _bdot(k_ref[0, hh, pl.ds(start, tk), :], q_ref[0, hh, :, lo:tq])

    def consume(j, s_ref, carry, lo=0, masked=False):
        start = pl.multiple_of(j * tk, tk)
        out = []
        for hh in range(ATT_HEADS):
            m, acc = carry[hh]

            def read():
                if not masked:
                    return s_ref[hh, :, lo:tq]
                tri = s_ref[hh, :, lo:lo + tk] + bias_ref[...]
                return tri if lo + tk == tq else jnp.concatenate([tri, s_ref[hh, :, lo + tk:tq]], axis=1)

            m_new = jnp.maximum(m[:, lo:tq], jnp.max(read(), axis=0, keepdims=True))
            alpha = jnp.exp2(m[:, lo:tq] - m_new)
            p = jnp.exp2(read() - m_new).astype(BF16)
            acc_new = alpha * acc[:, lo:tq] + _bdot(vt_ref[0, hh, :, pl.ds(start, tk)], p)
            if lo:
                m_new = jnp.concatenate([m[:, 0:lo], m_new], axis=1)
                acc_new = jnp.concatenate([acc[:, 0:lo], acc_new], axis=1)
            out.append((m_new, acc_new))
        return tuple(out)

    def trip(r, carry):
        for t in range(n_sub):
            j = r * n_sub + t
            produce(j + 1, bufs[(t + 1) % 2])
            carry = consume(j, bufs[t % 2], carry)
        return carry

    @pl.when(qi == 0)
    def _():
        produce(0, sa_ref)

    init = tuple((jnp.full((1, tq), NEG_INF, F32), jnp.zeros((V_ROWS, tq), F32)) for _ in range(ATT_HEADS))
    carry = lax.fori_loop(0, qi, trip, init)
    for t in range(n_sub):
        j = qi * n_sub + t
        if t + 1 < n_sub:
            produce(j + 1, bufs[(t + 1) % 2], lo=(t + 1) * tk)
        else:
            produce(0, bufs[(t + 1) % 2], q_ref=qn_ref)
        carry = consume(j, bufs[t % 2], carry, lo=t * tk, masked=True)
    y_t = jnp.concatenate([acc[0:HEAD_DIM] / acc[HEAD_DIM:HEAD_DIM + 1] for _, acc in carry], axis=0)
    o_ref[0] = y_t.T.astype(BF16)


def _attention(qt_aug, k_aug, vt, *, tq, tk):
    b, nh, s, _ = k_aug.shape
    assert tq % (2 * tk) == 0 and s % tq == 0
    future = np.arange(tk)[:, None] > np.arange(tk)[None, :]
    bias = jnp.asarray(np.where(future, np.float32(NEG_INF), np.float32(0.0)))
    nh_step = ATT_HEADS
    grid = (b, nh // nh_step, s // tq)
    return pl.pallas_call(
        functools.partial(_attn_kernel, tq=tq, tk=tk),
        grid=grid,
        in_specs=[
            pl.BlockSpec((1, nh_step, LANES, tq), lambda i, h, j: (i, h, 0, j)),
            pl.BlockSpec((1, nh_step, LANES, tq), lambda i, h, j: (i, h, 0, jnp.minimum(j + 1, s // tq - 1))),
            pl.BlockSpec((1, nh_step, s, LANES), lambda i, h, j: (i, h, 0, 0)),
            pl.BlockSpec((1, nh_step, V_ROWS, s), lambda i, h, j: (i, h, 0, 0)),
            pl.BlockSpec((tk, tk), lambda i, h, j: (0, 0)),
        ],
        out_specs=pl.BlockSpec((1, tq, nh_step * HEAD_DIM), lambda i, h, j: (i, j, h)),
        out_shape=jax.ShapeDtypeStruct((b, s, ATT_WIDTH), BF16),
        scratch_shapes=[pltpu.VMEM((nh_step, tk, tq), F32)] * 2,
        compiler_params=pltpu.CompilerParams(
            dimension_semantics=("arbitrary", "arbitrary", "arbitrary"),
            vmem_limit_bytes=VMEM_LIMIT),
        name="fox_attention",
    )(qt_aug, qt_aug, k_aug, vt, bias)


def _mix_kernel(x0_ref, xa_ref, xb_ref, y0_ref, ya_ref, yb_ref, g_ref, wcg_ref, cw_ref, wa_ref, wb_ref,
                wo_ref, g2_ref, wr_ref, br_ref, xe_ref, cnt_ref, ra_ref, rb_ref, carry_ref,
                *, tm, tiles_per_seq):
    step = pl.program_id(0)

    def residual(x_ref, y_ref, tile, dst_ref):
        x = x_ref[...]
        h = _rms(x, g_ref[...]).astype(BF16)
        pc = _bdot(h, wcg_ref[...])
        cw = CONV_WIDTH
        cb = pc[:, 0:cw]
        prod = pc[:, cw:2 * cw] * pc[:, 2 * cw:3 * cw]
        ga = pc[:, 3 * cw:3 * cw + D_MODEL]
        gb = pc[:, 3 * cw + D_MODEL:3 * cw + 2 * D_MODEL]

        prev = jnp.where(tile % tiles_per_seq == 0, 0.0, carry_ref[...])
        crow = lax.broadcasted_iota(jnp.int32, (tm, cw), 0)
        m1 = jnp.where(crow == 0, prev[7:8, :], pltpu.roll(prod, 1, axis=0))
        m2 = jnp.where(crow == 0, prev[6:7, :],
                       jnp.where(crow == 1, prev[7:8, :], pltpu.roll(prod, 2, axis=0)))
        carry_ref[...] = prod[tm - 8:tm, :]
        w = cw_ref[...]
        y_conv = cb * (w[0:1, :] * m2 + w[1:2, :] * m1 + w[2:3, :] * prod)

        a = _bdot(y_ref[...], wa_ref[...])
        bb = _bdot(y_conv.astype(BF16), wb_ref[...])
        merged = jax.nn.sigmoid(ga) * a + jax.nn.sigmoid(gb) * bb
        dst_ref[...] = x + _bdot(merged.astype(BF16), wo_ref[...])

    @pl.when(step == 0)
    def _():
        carry_ref[...] = jnp.zeros_like(carry_ref)
        residual(x0_ref, y0_ref, 0, ra_ref)

    _route(ra_ref, g2_ref, wr_ref, br_ref, xe_ref, cnt_ref, 0, tm)
    residual(xa_ref, ya_ref, 2 * step + 1, rb_ref)
    _route(rb_ref, g2_ref, wr_ref, br_ref, xe_ref, cnt_ref, 1, tm)
    residual(xb_ref, yb_ref, 2 * step + 2, ra_ref)


def _route(x1_ref, g2_ref, wr_ref, br_ref, xe_ref, cnt_ref, half, tm):
    rows = slice(half * tm, (half + 1) * tm)
    x1 = x1_ref[...]
    xe_ref[rows, 0:D_MODEL] = x1
    h2 = _rms(x1, g2_ref[...]).astype(BF16)

    logits = _bdot(h2, wr_ref[...]) + br_ref[...]
    lane = lax.broadcasted_iota(jnp.int32, (tm, LANES), 1)
    lanef = lane.astype(F32)
    is_g = (lane >= N_EXPERTS) & (lane < N_EXPERTS + N_GROUPS)
    gl = jnp.where(is_g, logits, NEG_INF)
    gmax = jnp.max(gl, axis=1, keepdims=True)
    gsum = jnp.sum(jnp.exp(gl - gmax), axis=1, keepdims=True)
    g_val = 1.0 / gsum
    g_lane = jnp.min(jnp.where(gl == gmax, lanef, float(LANES)), axis=1, keepdims=True)
    e_lo = (g_lane - float(N_EXPERTS)) * float(EXPERTS_PER_GROUP)
    in_grp = (lanef >= e_lo) & (lanef < e_lo + float(EXPERTS_PER_GROUP))
    el = jnp.where(in_grp, logits, NEG_INF)
    t1 = jnp.max(el, axis=1, keepdims=True)
    i1 = jnp.min(jnp.where(el == t1, lanef, float(LANES)), axis=1, keepdims=True)
    el2 = jnp.where(lanef == i1, NEG_INF, el)
    t2 = jnp.max(el2, axis=1, keepdims=True)
    i2 = jnp.min(jnp.where(el2 == t2, lanef, float(LANES)), axis=1, keepdims=True)
    e2 = jnp.exp(t2 - t1)
    w1 = g_val / (1.0 + e2)
    w2 = g_val * e2 / (1.0 + e2)
    comb = jnp.where(lanef == i1, w1, 0.0) + jnp.where(lanef == i2, w2, 0.0)

    cnt_ref[half] = jnp.sum(jnp.where(is_g & (lanef == g_lane), 1.0, 0.0), axis=0, keepdims=True)
    xe_ref[rows, D_MODEL:XE_WIDTH] = comb + jnp.where(lane == GID_LANE, g_lane - float(N_EXPERTS), 0.0)


def _mix(x, y_att, g, wcg, conv_w, wa, wb, wo, g2, wr, br, *, tm):
    b, s, d = x.shape
    t = b * s
    n_tiles = t // tm
    x2 = x.reshape(t, d)
    y2 = y_att.reshape(t, ATT_WIDTH)
    once = lambda a: pl.BlockSpec(a.shape, lambda i: (0,) * a.ndim, pipeline_mode=pl.Buffered(1))

    def tiles(width):
        nxt = lambda i: (jnp.minimum(2 * i + 2, n_tiles - 1), 0)
        return [pl.BlockSpec((tm, width), lambda i: (0, 0), pipeline_mode=pl.Buffered(1)),
                pl.BlockSpec((tm, width), lambda i: (2 * i + 1, 0)),
                pl.BlockSpec((tm, width), nxt)]

    return pl.pallas_call(
        functools.partial(_mix_kernel, tm=tm, tiles_per_seq=s // tm),
        grid=(n_tiles // 2,),
        in_specs=tiles(d) + tiles(ATT_WIDTH) + [once(g), once(wcg), once(conv_w), once(wa), once(wb),
                                                once(wo), once(g2), once(wr), once(br)],
        out_specs=[pl.BlockSpec((2 * tm, XE_WIDTH), lambda i: (i, 0)),
                   pl.BlockSpec((2, 1, LANES), lambda i: (i, 0, 0))],
        out_shape=[
            jax.ShapeDtypeStruct((t, XE_WIDTH), F32),
            jax.ShapeDtypeStruct((n_tiles, 1, LANES), F32),
        ],
        scratch_shapes=[pltpu.VMEM((tm, d), F32), pltpu.VMEM((tm, d), F32),
                        pltpu.VMEM((8, CONV_WIDTH), F32)],
        compiler_params=pltpu.CompilerParams(
            dimension_semantics=("arbitrary",), vmem_limit_bytes=VMEM_LIMIT),
        name="mix",
    )(x2, x2, x2, y2, y2, y2, g, wcg, conv_w, wa, wb, wo, g2, wr, br)


def _moe_kernel(tcnt_ref, xe_ref, p_ref, g2_ref, wg_ref, wu_ref, wd_ref, g3_ref, wpg_ref, wple_ref,
                earlier_ref, o_ref, hs_ref, rs_ref, ys_ref, *, tm):
    i = pl.program_id(0)
    n = [tcnt_ref[i * N_GROUPS + k] for k in range(N_GROUPS)]
    starts = [jnp.int32(0)]
    for k in range(N_GROUPS - 1):
        starts.append(starts[-1] + pl.cdiv(n[k], BF16_ROWS) * BF16_ROWS)

    x1 = xe_ref[:, 0:D_MODEL]
    route = xe_ref[:, D_MODEL:XE_WIDTH]
    h2 = _rms(x1, g2_ref[...]).astype(BF16)

    lane = lax.broadcasted_iota(jnp.int32, (tm, LANES), 1)
    lanef = lane.astype(F32)
    gid = jnp.sum(jnp.where(lane == GID_LANE, route, 0.0), axis=1, keepdims=True)
    onehot = jnp.where((lanef == gid) & (lane < N_GROUPS), 1.0, 0.0)
    rank = jnp.sum(onehot * _bdot(earlier_ref[...], onehot.astype(BF16)), axis=1, keepdims=True)
    base = starts[N_GROUPS - 1].astype(F32)
    for k in range(N_GROUPS - 2, -1, -1):
        base = jnp.where(gid == float(k), starts[k].astype(F32), base)
    pos = base + rank

    unsort = (pos == lax.broadcasted_iota(jnp.int32, (tm, MOE_SORTED), 1).astype(F32)).astype(BF16)
    digit_hi = jnp.floor(pos * (1.0 / 32.0))
    digits = jnp.where(lane == 0, digit_hi, jnp.where(lane == 1, pos - 32.0 * digit_hi, 0.0))
    sel_lane = lax.broadcasted_iota(jnp.int32, (BF16_ROWS, LANES), 1)
    sel = jnp.where(sel_lane == 0, 32.0, jnp.where(sel_lane == 1, 1.0, 0.0)).astype(BF16)
    pos_row = _bdot_nt(sel, digits.astype(BF16))[0:1, :]
    sort = (lax.broadcasted_iota(jnp.int32, (MOE_SORTED, tm), 0).astype(F32) == pos_row).astype(BF16)

    hs_ref[0:MOE_SORTED] = _bdot(sort, h2).astype(BF16)
    r_hi, r_mid, r_lo = _bf16_pieces(route)
    packed = (r_hi.astype(F32) + pltpu.roll(r_mid.astype(F32), ROUTE_PITCH, axis=1)
              + pltpu.roll(r_lo.astype(F32), 2 * ROUTE_PITCH, axis=1)).astype(BF16)
    rsorted = _bdot(sort, packed)
    rs_ref[0:MOE_SORTED] = (rsorted + pltpu.roll(rsorted, LANES - ROUTE_PITCH, axis=1)
                            + pltpu.roll(rsorted, LANES - 2 * ROUTE_PITCH, axis=1))
    hs_ref[MOE_SORTED:MOE_ROWS] = jnp.zeros((MOE_CHUNK, D_MODEL), BF16)
    rs_ref[MOE_SORTED:MOE_ROWS] = jnp.zeros((MOE_CHUNK, LANES), F32)
    ys_ref[...] = jnp.zeros_like(ys_ref)

    clane = lax.broadcasted_iota(jnp.int32, (MOE_CHUNK, LANES), 1)
    for g in range(N_GROUPS):
        def chunk(c, _, g=g):
            r0 = pl.multiple_of(starts[g] + c * MOE_CHUNK, BF16_ROWS)
            hrows = hs_ref[pl.ds(r0, MOE_CHUNK), :]
            rt = rs_ref[pl.ds(r0, MOE_CHUNK), :]
            parts = []
            for e in range(EXPERTS_PER_GROUP):
                ex = g * EXPERTS_PER_GROUP + e
                a = _bdot(hrows, wg_ref[ex])
                u = _bdot(hrows, wu_ref[ex])
                ce = jnp.sum(jnp.where(clane == ex, rt, 0.0), axis=1, keepdims=True)
                parts.append(((a * jax.nn.sigmoid(a)) * u * ce).astype(BF16))
            ys_ref[pl.ds(r0, MOE_CHUNK), :] = _bdot(jnp.concatenate(parts, axis=1), wd_ref[g]).astype(BF16)
            return 0
        lax.fori_loop(0, pl.cdiv(n[g], MOE_CHUNK), chunk, 0)

    x2 = x1 + _bdot(unsort, ys_ref[0:MOE_SORTED])
    h3 = _rms(x2, g3_ref[...]).astype(BF16)
    gate = jax.nn.sigmoid(_bdot(h3, wpg_ref[...]))
    emb = _bdot(p_ref[...].astype(BF16), wple_ref[...])
    o_ref[...] = x2 + gate * emb


def _moe(tcnt, xe, p, g2, wg, wu, wd, g3, wpg, wple, *, tm):
    t = xe.shape[0]
    d = D_MODEL
    once = lambda a: pl.BlockSpec(a.shape, lambda i, c: (0,) * a.ndim, pipeline_mode=pl.Buffered(1))
    row = lambda width: pl.BlockSpec((tm, width), lambda i, c: (i, 0))
    earlier = jnp.asarray(np.tril(np.ones((tm, tm), np.float32), -1), BF16)
    return pl.pallas_call(
        functools.partial(_moe_kernel, tm=tm),
        grid_spec=pltpu.PrefetchScalarGridSpec(
            num_scalar_prefetch=1,
            grid=(t // tm,),
            in_specs=[row(XE_WIDTH), row(PLE_DIM), once(g2), once(wg), once(wu), once(wd), once(g3),
                      once(wpg), once(wple), once(earlier)],
            out_specs=row(d),
            scratch_shapes=[pltpu.VMEM((MOE_ROWS, d), BF16), pltpu.VMEM((MOE_ROWS, LANES), F32),
                            pltpu.VMEM((MOE_ROWS, d), BF16)],
        ),
        out_shape=jax.ShapeDtypeStruct((t, d), F32),
        compiler_params=pltpu.CompilerParams(
            dimension_semantics=("arbitrary",), vmem_limit_bytes=VMEM_LIMIT),
        name="moe",
    )(tcnt, xe, p, g2, wg, wu, wd, g3, wpg, wple, earlier)


def _k_select_matrix():
    pk = np.zeros((LANES, N_HEADS * LANES), np.float32)
    for idx in range(C_PIECES):
        for h in range(N_HEADS):
            pk[idx * N_HEADS + h, h * LANES + KC_LANE + idx] = -1.0
    return jnp.asarray(pk, BF16)


def kernel(x, p, attn_norm_g, w_in, b_f, q_norm_g, k_norm_g, conv_w, w_out_att, w_out_conv, w_o,
           ffn_norm_g, w_rg, b_rg, w_re, b_re, w_gate, w_up, w_down, ple_norm_g, w_pg, w_ple):
    b, s, d = x.shape
    t = b * s
    aw = ATT_WIDTH
    for i in range(w_in.shape[0]):
        wi = w_in[i]
        wf = wi[:, 3 * aw:3 * aw + N_HEADS]
        w_qkvf = jnp.concatenate(
            [wi[:, :3 * aw], wf, wf, wf, jnp.zeros((d, LANES - C_PIECES * N_HEADS), F32)],
            axis=1).astype(BF16)
        w_cg = wi[:, 3 * aw + N_HEADS:].astype(BF16)
        bf3 = jnp.concatenate([b_f[i]] * C_PIECES + [jnp.zeros((LANES - C_PIECES * N_HEADS,), F32)])[None, :]
        scale = HEAD_DIM ** -0.5 * LOG2E
        qg = jnp.tile(q_norm_g[i] * scale, 2)[None, :]
        kg = jnp.tile(k_norm_g[i], 2)[None, :]
        qt_aug, k_aug, vt = _qkv_proj(x, attn_norm_g[i][None, :], w_qkvf, bf3, qg, kg, _k_select_matrix(),
                                      tm=TM_QKV)
        y_att = _attention(qt_aug, k_aug, vt, tq=TQ_ATT, tk=TK_ATT)

        w_r = jnp.concatenate(
            [w_re[i], w_rg[i], jnp.zeros((d, LANES - N_EXPERTS - N_GROUPS), F32)], axis=1).astype(BF16)
        b_r = jnp.concatenate(
            [b_re[i], b_rg[i], jnp.zeros((LANES - N_EXPERTS - N_GROUPS,), F32)])[None, :]
        g_ffn = ffn_norm_g[i][None, :]
        xe, counts = _mix(x, y_att, attn_norm_g[i][None, :], w_cg, conv_w[i],
                          w_out_att[i].astype(BF16), w_out_conv[i].astype(BF16),
                          w_o[i].astype(BF16), g_ffn, w_r, b_r, tm=TM_MIX)
        tcnt = counts[:, 0, N_EXPERTS:N_EXPERTS + N_GROUPS].astype(jnp.int32).reshape(-1)

        w_dn = w_down[i].reshape(N_GROUPS, GROUP_WIDTH, d).astype(BF16)
        x = _moe(tcnt, xe.reshape(t, XE_WIDTH), p[i].reshape(t, PLE_DIM), g_ffn,
                 w_gate[i].astype(BF16), w_up[i].astype(BF16), w_dn,
                 ple_norm_g[i][None, :], w_pg[i].astype(BF16), w_ple[i].astype(BF16),
                 tm=TM_MOE).reshape(b, s, d)
    return x
```

```python
import functools

import numpy as np
import jax
import jax.numpy as jnp
from jax import lax
from jax.experimental import pallas as pl
from jax.experimental.pallas import tpu as pltpu

D_MODEL = 1024
N_HEADS = 8
HEAD_DIM = 64
ATT_WIDTH = N_HEADS * HEAD_DIM
CONV_WIDTH = 512
CONV_K = 3
N_GROUPS = 4
EXPERTS_PER_GROUP = 4
N_EXPERTS = 16
D_EXPERT = 256
GROUP_WIDTH = EXPERTS_PER_GROUP * D_EXPERT
PLE_DIM = 256
EPS = 1e-6
NEG_INF = -1e30
LOG2E = 1.4426950408889634

LANES = 128
BF16_ROWS = 16
C_PIECES = 3
QC_LANE = HEAD_DIM
KC_LANE = HEAD_DIM + C_PIECES
V_ROWS = HEAD_DIM + BF16_ROWS
QKV_WIDTH = 3 * ATT_WIDTH + LANES
GID_LANE = N_EXPERTS
ROUTE_PITCH = 32
XE_WIDTH = D_MODEL + LANES
VMEM_LIMIT = 56 * 1024 * 1024

TM_QKV = 512
TQ_ATT = 1024
TK_ATT = 256
ATT_HEADS = 2
TM_MIX = 512
TM_MOE = TM_MIX
MOE_SORTED = -(-(TM_MOE + N_GROUPS * BF16_ROWS) // LANES) * LANES
MOE_CHUNK = 144
MOE_ROWS = MOE_SORTED + MOE_CHUNK

F32 = jnp.float32
BF16 = jnp.bfloat16


def _rms(xf, g):
    return xf * lax.rsqrt(jnp.mean(xf * xf, axis=-1, keepdims=True) + EPS) * g


def _log_sigmoid(z):
    return jnp.minimum(z, 0.0) - jnp.log1p(jnp.exp(-jnp.abs(z)))


def _bdot(a, b):
    return jnp.dot(a, b, preferred_element_type=F32)


def _bdot_nt(a, b):
    return lax.dot_general(a, b, (((1,), (1,)), ((), ())), preferred_element_type=F32)


def _bf16_pieces(x):
    hi = x.astype(BF16)
    r1 = x - hi.astype(F32)
    mid = r1.astype(BF16)
    lo = (r1 - mid.astype(F32)).astype(BF16)
    return hi, mid, lo


def _qkv_kernel(x0_ref, xa_ref, xb_ref, g_ref, w_ref, bf_ref, qg_ref, kg_ref, pk_ref,
                qt_ref, k_ref, vt_ref, pa_ref, pb_ref, carry_ref, *, tm, tiles_per_seq):
    step = pl.program_id(0)

    def project(x_ref, dst_ref):
        dst_ref[...] = _bdot(_rms(x_ref[...], g_ref[...]).astype(BF16), w_ref[...])

    @pl.when(step == 0)
    def _():
        carry_ref[...] = jnp.zeros_like(carry_ref)
        project(x0_ref, pa_ref)

    lane = lax.broadcasted_iota(jnp.int32, (tm, LANES), 1)
    row = lax.broadcasted_iota(jnp.int32, (tm, LANES), 0)
    k_ones = jnp.where((lane >= QC_LANE) & (lane < QC_LANE + C_PIECES), 1.0, 0.0)
    low = lane < HEAD_DIM
    ext_row = lax.broadcasted_iota(jnp.int32, (V_ROWS - HEAD_DIM, tm), 0)
    v_ext = jnp.where(ext_row == 0, 1.0, 0.0)

    def finish(p_ref, tile, half):
        rows = slice(half * tm, (half + 1) * tm)
        for j in range(N_HEADS // 2):
            pair_t = p_ref[:, 2 * ATT_WIDTH + LANES * j: 2 * ATT_WIDTH + LANES * (j + 1)].T
            for hh in range(2):
                vt_ref[0, 2 * j + hh, :, rows] = jnp.concatenate(
                    [pair_t[HEAD_DIM * hh: HEAD_DIM * (hh + 1)], v_ext], axis=0).astype(BF16)

        f3 = p_ref[:, 3 * ATT_WIDTH:QKV_WIDTH] + bf_ref[...]
        c = jnp.where(lane < C_PIECES * N_HEADS, _log_sigmoid(f3), 0.0)
        sh = 1
        while sh < tm:
            c = c + jnp.where(row >= sh, pltpu.roll(c, sh, axis=0), 0.0)
            sh *= 2
        c = c + jnp.where(tile % tiles_per_seq == 0, 0.0, carry_ref[...])
        carry_ref[...] = c[tm - 1:tm, :]
        hi, mid, lo = _bf16_pieces(c * LOG2E)
        piece = jnp.where(lane < N_HEADS, hi, jnp.where(lane < 2 * N_HEADS, mid, lo))

        def rms_scales(pair):
            sq = pair * pair
            ss_lo = jnp.sum(jnp.where(low, sq, 0.0), axis=1, keepdims=True)
            ss_hi = jnp.sum(jnp.where(low, 0.0, sq), axis=1, keepdims=True)
            return (lax.rsqrt(ss_lo * (1.0 / HEAD_DIM) + EPS), lax.rsqrt(ss_hi * (1.0 / HEAD_DIM) + EPS))

        piece_t = piece.astype(F32).T
        erow = lax.broadcasted_iota(jnp.int32, (8, tm), 0)
        pad_t = jnp.zeros((LANES - HEAD_DIM - 8, tm), F32)
        qgain = qg_ref[...]
        for j in range(N_HEADS // 2):
            pair = p_ref[:, LANES * j: LANES * (j + 1)]
            r_lo, r_hi = rms_scales(pair)
            qn_t = (pair * jnp.where(low, r_lo, r_hi) * qgain).T
            for hh in range(2):
                h = 2 * j + hh
                c_t = jnp.where(erow >= C_PIECES, jnp.where(erow < 2 * C_PIECES, 1.0, 0.0), 0.0)
                for idx in range(C_PIECES):
                    c_t = jnp.where(erow == idx, piece_t[idx * N_HEADS + h: idx * N_HEADS + h + 1], c_t)
                qt_ref[0, h, :, rows] = jnp.concatenate(
                    [qn_t[HEAD_DIM * hh: HEAD_DIM * (hh + 1)], c_t, pad_t], axis=0).astype(BF16)

        kc = _bdot(piece, pk_ref[...])
        kgain = kg_ref[...]
        for j in range(N_HEADS // 2):
            pair = p_ref[:, ATT_WIDTH + LANES * j: ATT_WIDTH + LANES * (j + 1)]
            r_lo, r_hi = rms_scales(pair)
            n_lo = pair * r_lo * kgain
            n_hi = pltpu.roll(pair, HEAD_DIM, axis=1) * r_hi * kgain
            for h, nrm in ((2 * j, n_lo), (2 * j + 1, n_hi)):
                aug = jnp.where(low, nrm, 0.0) + kc[:, LANES * h: LANES * (h + 1)] + k_ones
                k_ref[0, h, rows, :] = aug.astype(BF16)

    finish(pa_ref, 2 * step, 0)
    project(xa_ref, pb_ref)
    finish(pb_ref, 2 * step + 1, 1)
    project(xb_ref, pa_ref)


def _qkv_proj(x, g, w, bf3, qg, kg, pk, *, tm):
    b, s, d = x.shape
    x2 = x.reshape(b * s, d)
    n_tiles = b * s // tm
    per_seq = s // (2 * tm)
    full = lambda shape: pl.BlockSpec(shape, lambda i: (0,) * len(shape))
    return pl.pallas_call(
        functools.partial(_qkv_kernel, tm=tm, tiles_per_seq=s // tm),
        grid=(n_tiles // 2,),
        in_specs=[
            pl.BlockSpec((tm, d), lambda i: (0, 0)),
            pl.BlockSpec((tm, d), lambda i: (2 * i + 1, 0)),
            pl.BlockSpec((tm, d), lambda i: (jnp.minimum(2 * i + 2, n_tiles - 1), 0)),
            full(g.shape), full(w.shape), full(bf3.shape), full(qg.shape), full(kg.shape),
            full(pk.shape),
        ],
        out_specs=[pl.BlockSpec((1, N_HEADS, LANES, 2 * tm), lambda i: (i // per_seq, 0, 0, i % per_seq)),
                   pl.BlockSpec((1, N_HEADS, 2 * tm, LANES), lambda i: (i // per_seq, 0, i % per_seq, 0)),
                   pl.BlockSpec((1, N_HEADS, V_ROWS, 2 * tm), lambda i: (i // per_seq, 0, 0, i % per_seq))],
        out_shape=[
            jax.ShapeDtypeStruct((b, N_HEADS, LANES, s), BF16),
            jax.ShapeDtypeStruct((b, N_HEADS, s, LANES), BF16),
            jax.ShapeDtypeStruct((b, N_HEADS, V_ROWS, s), BF16),
        ],
        scratch_shapes=[pltpu.VMEM((tm, QKV_WIDTH), F32), pltpu.VMEM((tm, QKV_WIDTH), F32),
                        pltpu.VMEM((1, LANES), F32)],
        compiler_params=pltpu.CompilerParams(
            dimension_semantics=("arbitrary",), vmem_limit_bytes=VMEM_LIMIT),
        name="qkv_proj",
    )(x2, x2, x2, g, w, bf3, qg, kg, pk)


def _attn_kernel(qt_ref, qn_ref, k_ref, vt_ref, bias_ref, o_ref, sa_ref, sb_ref, *, tq, tk, n_q):
    qi = pl.program_id(2)
    n_sub = tq // tk
    bufs = (sa_ref, sb_ref)

    def produce(j, s_ref, lo=0, q_ref=qt_ref):
        start = j * tk
        for hh in range(ATT_HEADS):
            s_ref[hh, :, lo:tq] = _bdot(k_ref[0, hh, pl.ds(start, tk), :], q_ref[0, hh, :, lo:tq])

    def consume(j, s_ref, carry, lo=0, masked=False):
        start = j * tk
        out = []
        for hh in range(ATT_HEADS):
            m, acc = carry[hh]

            def read():
                if not masked:
                    return s_ref[hh, :, lo:tq]
                tri = s_ref[hh, :, lo:lo + tk] + bias_ref[...]
                return tri if lo + tk == tq else jnp.concatenate([tri, s_ref[hh, :, lo + tk:tq]], axis=1)

            m_new = jnp.maximum(m[:, lo:tq], jnp.max(read(), axis=0, keepdims=True))
            alpha = jnp.exp2(m[:, lo:tq] - m_new)
            p = jnp.exp2(read() - m_new).astype(BF16)
            acc_new = alpha * acc[:, lo:tq] + _bdot(vt_ref[0, hh, :, pl.ds(start, tk)], p)
            if lo:
                m_new = jnp.concatenate([m[:, 0:lo], m_new], axis=1)
                acc_new = jnp.concatenate([acc[:, 0:lo], acc_new], axis=1)
            out.append((m_new, acc_new))
        return tuple(out)

    def trip(r, carry):
        for t in range(n_sub):
            j = r * n_sub + t
            produce(j + 1, bufs[(t + 1) % 2])
            carry = consume(j, bufs[t % 2], carry)
        return carry

    def query_tile(q):
        if q == 0:
            produce(0, sa_ref)
        carry = tuple((jnp.full((1, tq), NEG_INF, F32), jnp.zeros((V_ROWS, tq), F32))
                      for _ in range(ATT_HEADS))
        for r in range(q):
            carry = trip(r, carry)
        for t in range(n_sub):
            j = q * n_sub + t
            if t + 1 < n_sub:
                produce(j + 1, bufs[(t + 1) % 2], lo=(t + 1) * tk)
            elif q + 1 < n_q:
                produce(0, bufs[(t + 1) % 2], q_ref=qn_ref)
            carry = consume(j, bufs[t % 2], carry, lo=t * tk, masked=True)
        y_t = jnp.concatenate([acc[0:HEAD_DIM] / acc[HEAD_DIM:HEAD_DIM + 1] for _, acc in carry], axis=0)
        o_ref[0] = y_t.T.astype(BF16)

    for q in range(n_q):
        pl.when(qi == q)(functools.partial(query_tile, q))


def _attention(qt_aug, k_aug, vt, *, tq, tk):
    b, nh, s, _ = k_aug.shape
    assert tq % (2 * tk) == 0 and s % tq == 0
    future = np.arange(tk)[:, None] > np.arange(tk)[None, :]
    bias = jnp.asarray(np.where(future, np.float32(NEG_INF), np.float32(0.0)))
    nh_step = ATT_HEADS
    grid = (b, nh // nh_step, s // tq)
    return pl.pallas_call(
        functools.partial(_attn_kernel, tq=tq, tk=tk, n_q=s // tq),
        grid=grid,
        in_specs=[
            pl.BlockSpec((1, nh_step, LANES, tq), lambda i, h, j: (i, h, 0, j)),
            pl.BlockSpec((1, nh_step, LANES, tq), lambda i, h, j: (i, h, 0, jnp.minimum(j + 1, s // tq - 1))),
            pl.BlockSpec((1, nh_step, s, LANES), lambda i, h, j: (i, h, 0, 0)),
            pl.BlockSpec((1, nh_step, V_ROWS, s), lambda i, h, j: (i, h, 0, 0)),
            pl.BlockSpec((tk, tk), lambda i, h, j: (0, 0)),
        ],
        out_specs=pl.BlockSpec((1, tq, nh_step * HEAD_DIM), lambda i, h, j: (i, j, h)),
        out_shape=jax.ShapeDtypeStruct((b, s, ATT_WIDTH), BF16),
        scratch_shapes=[pltpu.VMEM((nh_step, tk, tq), F32)] * 2,
        compiler_params=pltpu.CompilerParams(
            dimension_semantics=("arbitrary", "arbitrary", "arbitrary"),
            vmem_limit_bytes=VMEM_LIMIT),
        name="fox_attention",
    )(qt_aug, qt_aug, k_aug, vt, bias)


def _mix_kernel(x0_ref, xa_ref, xb_ref, y0_ref, ya_ref, yb_ref, g_ref, wcg_ref, cw_ref, wa_ref, wb_ref,
                wo_ref, g2_ref, wr_ref, br_ref, xe_ref, cnt_ref, ra_ref, rb_ref, carry_ref,
                *, tm, tiles_per_seq):
    step = pl.program_id(0)

    def residual(x_ref, y_ref, tile, dst_ref):
        x = x_ref[...]
        h = _rms(x, g_ref[...]).astype(BF16)
        pc = _bdot(h, wcg_ref[...])
        cw = CONV_WIDTH
        cb = pc[:, 0:cw]
        prod = pc[:, cw:2 * cw] * pc[:, 2 * cw:3 * cw]
        ga = pc[:, 3 * cw:3 * cw + D_MODEL]
        gb = pc[:, 3 * cw + D_MODEL:3 * cw + 2 * D_MODEL]

        prev = jnp.where(tile % tiles_per_seq == 0, 0.0, carry_ref[...])
        crow = lax.broadcasted_iota(jnp.int32, (tm, cw), 0)
        m1 = jnp.where(crow == 0, prev[7:8, :], pltpu.roll(prod, 1, axis=0))
        m2 = jnp.where(crow == 0, prev[6:7, :],
                       jnp.where(crow == 1, prev[7:8, :], pltpu.roll(prod, 2, axis=0)))
        carry_ref[...] = prod[tm - 8:tm, :]
        w = cw_ref[...]
        y_conv = cb * (w[0:1, :] * m2 + w[1:2, :] * m1 + w[2:3, :] * prod)

        a = _bdot(y_ref[...], wa_ref[...])
        bb = _bdot(y_conv.astype(BF16), wb_ref[...])
        merged = jax.nn.sigmoid(ga) * a + jax.nn.sigmoid(gb) * bb
        dst_ref[...] = x + _bdot(merged.astype(BF16), wo_ref[...])

    @pl.when(step == 0)
    def _():
        carry_ref[...] = jnp.zeros_like(carry_ref)
        residual(x0_ref, y0_ref, 0, ra_ref)

    _route(ra_ref, g2_ref, wr_ref, br_ref, xe_ref, cnt_ref, 0, tm)
    residual(xa_ref, ya_ref, 2 * step + 1, rb_ref)
    _route(rb_ref, g2_ref, wr_ref, br_ref, xe_ref, cnt_ref, 1, tm)
    residual(xb_ref, yb_ref, 2 * step + 2, ra_ref)


def _route(x1_ref, g2_ref, wr_ref, br_ref, xe_ref, cnt_ref, half, tm):
    rows = slice(half * tm, (half + 1) * tm)
    x1 = x1_ref[...]
    xe_ref[rows, 0:D_MODEL] = x1
    h2 = _rms(x1, g2_ref[...]).astype(BF16)

    logits = _bdot(h2, wr_ref[...]) + br_ref[...]
    lane = lax.broadcasted_iota(jnp.int32, (tm, LANES), 1)
    lanef = lane.astype(F32)
    is_g = (lane >= N_EXPERTS) & (lane < N_EXPERTS + N_GROUPS)
    gl = jnp.where(is_g, logits, NEG_INF)
    gmax = jnp.max(gl, axis=1, keepdims=True)
    gsum = jnp.sum(jnp.exp(gl - gmax), axis=1, keepdims=True)
    g_val = 1.0 / gsum
    g_lane = jnp.min(jnp.where(gl == gmax, lanef, float(LANES)), axis=1, keepdims=True)
    e_lo = (g_lane - float(N_EXPERTS)) * float(EXPERTS_PER_GROUP)
    in_grp = (lanef >= e_lo) & (lanef < e_lo + float(EXPERTS_PER_GROUP))
    el = jnp.where(in_grp, logits, NEG_INF)
    t1 = jnp.max(el, axis=1, keepdims=True)
    i1 = jnp.min(jnp.where(el == t1, lanef, float(LANES)), axis=1, keepdims=True)
    el2 = jnp.where(lanef == i1, NEG_INF, el)
    t2 = jnp.max(el2, axis=1, keepdims=True)
    i2 = jnp.min(jnp.where(el2 == t2, lanef, float(LANES)), axis=1, keepdims=True)
    e2 = jnp.exp(t2 - t1)
    w1 = g_val / (1.0 + e2)
    w2 = g_val * e2 / (1.0 + e2)
    comb = jnp.where(lanef == i1, w1, 0.0) + jnp.where(lanef == i2, w2, 0.0)

    cnt_ref[half] = jnp.sum(jnp.where(is_g & (lanef == g_lane), 1.0, 0.0), axis=0, keepdims=True)
    xe_ref[rows, D_MODEL:XE_WIDTH] = comb + jnp.where(lane == GID_LANE, g_lane - float(N_EXPERTS), 0.0)


def _mix(x, y_att, g, wcg, conv_w, wa, wb, wo, g2, wr, br, *, tm):
    b, s, d = x.shape
    t = b * s
    n_tiles = t // tm
    x2 = x.reshape(t, d)
    y2 = y_att.reshape(t, ATT_WIDTH)
    once = lambda a: pl.BlockSpec(a.shape, lambda i: (0,) * a.ndim, pipeline_mode=pl.Buffered(1))

    def tiles(width):
        nxt = lambda i: (jnp.minimum(2 * i + 2, n_tiles - 1), 0)
        return [pl.BlockSpec((tm, width), lambda i: (0, 0), pipeline_mode=pl.Buffered(1)),
                pl.BlockSpec((tm, width), lambda i: (2 * i + 1, 0)),
                pl.BlockSpec((tm, width), nxt)]

    return pl.pallas_call(
        functools.partial(_mix_kernel, tm=tm, tiles_per_seq=s // tm),
        grid=(n_tiles // 2,),
        in_specs=tiles(d) + tiles(ATT_WIDTH) + [once(g), once(wcg), once(conv_w), once(wa), once(wb),
                                                once(wo), once(g2), once(wr), once(br)],
        out_specs=[pl.BlockSpec((2 * tm, XE_WIDTH), lambda i: (i, 0)),
                   pl.BlockSpec((2, 1, LANES), lambda i: (i, 0, 0))],
        out_shape=[
            jax.ShapeDtypeStruct((t, XE_WIDTH), F32),
            jax.ShapeDtypeStruct((n_tiles, 1, LANES), F32),
        ],
        scratch_shapes=[pltpu.VMEM((tm, d), F32), pltpu.VMEM((tm, d), F32),
                        pltpu.VMEM((8, CONV_WIDTH), F32)],
        compiler_params=pltpu.CompilerParams(
            dimension_semantics=("arbitrary",), vmem_limit_bytes=VMEM_LIMIT),
        name="mix",
    )(x2, x2, x2, y2, y2, y2, g, wcg, conv_w, wa, wb, wo, g2, wr, br)


def _moe_kernel(tcnt_ref, xe_ref, p_ref, g2_ref, wg_ref, wu_ref, wd_ref, g3_ref, wpg_ref, wple_ref,
                earlier_ref, o_ref, hs_ref, rs_ref, ys_ref, *, tm):
    i = pl.program_id(0)
    n = [tcnt_ref[i * N_GROUPS + k] for k in range(N_GROUPS)]
    starts = [jnp.int32(0)]
    for k in range(N_GROUPS - 1):
        starts.append(starts[-1] + pl.cdiv(n[k], BF16_ROWS) * BF16_ROWS)

    x1 = xe_ref[:, 0:D_MODEL]
    route = xe_ref[:, D_MODEL:XE_WIDTH]
    h2 = _rms(x1, g2_ref[...]).astype(BF16)

    lane = lax.broadcasted_iota(jnp.int32, (tm, LANES), 1)
    lanef = lane.astype(F32)
    gid = jnp.sum(jnp.where(lane == GID_LANE, route, 0.0), axis=1, keepdims=True)
    onehot = jnp.where((lanef == gid) & (lane < N_GROUPS), 1.0, 0.0)
    rank = jnp.sum(onehot * _bdot(earlier_ref[...], onehot.astype(BF16)), axis=1, keepdims=True)
    base = starts[N_GROUPS - 1].astype(F32)
    for k in range(N_GROUPS - 2, -1, -1):
        base = jnp.where(gid == float(k), starts[k].astype(F32), base)
    pos = base + rank

    unsort = (pos == lax.broadcasted_iota(jnp.int32, (tm, MOE_SORTED), 1).astype(F32)).astype(BF16)
    digit_hi = jnp.floor(pos * (1.0 / 32.0))
    digits = jnp.where(lane == 0, digit_hi, jnp.where(lane == 1, pos - 32.0 * digit_hi, 0.0))
    sel_lane = lax.broadcasted_iota(jnp.int32, (BF16_ROWS, LANES), 1)
    sel = jnp.where(sel_lane == 0, 32.0, jnp.where(sel_lane == 1, 1.0, 0.0)).astype(BF16)
    pos_row = _bdot_nt(sel, digits.astype(BF16))[0:1, :]
    sort = (lax.broadcasted_iota(jnp.int32, (MOE_SORTED, tm), 0).astype(F32) == pos_row).astype(BF16)

    hs_ref[0:MOE_SORTED] = _bdot(sort, h2).astype(BF16)
    r_hi, r_mid, r_lo = _bf16_pieces(route)
    packed = (r_hi.astype(F32) + pltpu.roll(r_mid.astype(F32), ROUTE_PITCH, axis=1)
              + pltpu.roll(r_lo.astype(F32), 2 * ROUTE_PITCH, axis=1)).astype(BF16)
    rsorted = _bdot(sort, packed)
    rs_ref[0:MOE_SORTED] = (rsorted + pltpu.roll(rsorted, LANES - ROUTE_PITCH, axis=1)
                            + pltpu.roll(rsorted, LANES - 2 * ROUTE_PITCH, axis=1))
    hs_ref[MOE_SORTED:MOE_ROWS] = jnp.zeros((MOE_CHUNK, D_MODEL), BF16)
    rs_ref[MOE_SORTED:MOE_ROWS] = jnp.zeros((MOE_CHUNK, LANES), F32)
    ys_ref[...] = jnp.zeros_like(ys_ref)

    clane = lax.broadcasted_iota(jnp.int32, (MOE_CHUNK, LANES), 1)
    for g in range(N_GROUPS):
        def chunk(c, _, g=g):
            r0 = pl.multiple_of(starts[g] + c * MOE_CHUNK, BF16_ROWS)
            hrows = hs_ref[pl.ds(r0, MOE_CHUNK), :]
            rt = rs_ref[pl.ds(r0, MOE_CHUNK), :]
            parts = []
            for e in range(EXPERTS_PER_GROUP):
                ex = g * EXPERTS_PER_GROUP + e
                a = _bdot(hrows, wg_ref[ex])
                u = _bdot(hrows, wu_ref[ex])
                ce = jnp.sum(jnp.where(clane == ex, rt, 0.0), axis=1, keepdims=True)
                parts.append(((a * jax.nn.sigmoid(a)) * u * ce).astype(BF16))
            ys_ref[pl.ds(r0, MOE_CHUNK), :] = _bdot(jnp.concatenate(parts, axis=1), wd_ref[g]).astype(BF16)
            return 0
        lax.fori_loop(0, pl.cdiv(n[g], MOE_CHUNK), chunk, 0)

    x2 = x1 + _bdot(unsort, ys_ref[0:MOE_SORTED])
    h3 = _rms(x2, g3_ref[...]).astype(BF16)
    gate = jax.nn.sigmoid(_bdot(h3, wpg_ref[...]))
    emb = _bdot(p_ref[...].astype(BF16), wple_ref[...])
    o_ref[...] = x2 + gate * emb


def _moe(tcnt, xe, p, g2, wg, wu, wd, g3, wpg, wple, *, tm):
    t = xe.shape[0]
    d = D_MODEL
    once = lambda a: pl.BlockSpec(a.shape, lambda i, c: (0,) * a.ndim, pipeline_mode=pl.Buffered(1))
    row = lambda width: pl.BlockSpec((tm, width), lambda i, c: (i, 0))
    earlier = jnp.asarray(np.tril(np.ones((tm, tm), np.float32), -1), BF16)
    return pl.pallas_call(
        functools.partial(_moe_kernel, tm=tm),
        grid_spec=pltpu.PrefetchScalarGridSpec(
            num_scalar_prefetch=1,
            grid=(t // tm,),
            in_specs=[row(XE_WIDTH), row(PLE_DIM), once(g2), once(wg), once(wu), once(wd), once(g3),
                      once(wpg), once(wple), once(earlier)],
            out_specs=row(d),
            scratch_shapes=[pltpu.VMEM((MOE_ROWS, d), BF16), pltpu.VMEM((MOE_ROWS, LANES), F32),
                            pltpu.VMEM((MOE_ROWS, d), BF16)],
        ),
        out_shape=jax.ShapeDtypeStruct((t, d), F32),
        compiler_params=pltpu.CompilerParams(
            dimension_semantics=("arbitrary",), vmem_limit_bytes=VMEM_LIMIT),
        name="moe",
    )(tcnt, xe, p, g2, wg, wu, wd, g3, wpg, wple, earlier)


def _k_select_matrix():
    pk = np.zeros((LANES, N_HEADS * LANES), np.float32)
    for idx in range(C_PIECES):
        for h in range(N_HEADS):
            pk[idx * N_HEADS + h, h * LANES + KC_LANE + idx] = -1.0
    return jnp.asarray(pk, BF16)


def kernel(x, p, attn_norm_g, w_in, b_f, q_norm_g, k_norm_g, conv_w, w_out_att, w_out_conv, w_o,
           ffn_norm_g, w_rg, b_rg, w_re, b_re, w_gate, w_up, w_down, ple_norm_g, w_pg, w_ple):
    b, s, d = x.shape
    t = b * s
    aw = ATT_WIDTH
    for i in range(w_in.shape[0]):
        wi = w_in[i]
        wf = wi[:, 3 * aw:3 * aw + N_HEADS]
        w_qkvf = jnp.concatenate(
            [wi[:, :3 * aw], wf, wf, wf, jnp.zeros((d, LANES - C_PIECES * N_HEADS), F32)],
            axis=1).astype(BF16)
        w_cg = wi[:, 3 * aw + N_HEADS:].astype(BF16)
        bf3 = jnp.concatenate([b_f[i]] * C_PIECES + [jnp.zeros((LANES - C_PIECES * N_HEADS,), F32)])[None, :]
        scale = HEAD_DIM ** -0.5 * LOG2E
        qg = jnp.tile(q_norm_g[i] * scale, 2)[None, :]
        kg = jnp.tile(k_norm_g[i], 2)[None, :]
        qt_aug, k_aug, vt = _qkv_proj(x, attn_norm_g[i][None, :], w_qkvf, bf3, qg, kg, _k_select_matrix(),
                                      tm=TM_QKV)
        y_att = _attention(qt_aug, k_aug, vt, tq=TQ_ATT, tk=TK_ATT)

        w_r = jnp.concatenate(
            [w_re[i], w_rg[i], jnp.zeros((d, LANES - N_EXPERTS - N_GROUPS), F32)], axis=1).astype(BF16)
        b_r = jnp.concatenate(
            [b_re[i], b_rg[i], jnp.zeros((LANES - N_EXPERTS - N_GROUPS,), F32)])[None, :]
        g_ffn = ffn_norm_g[i][None, :]
        xe, counts = _mix(x, y_att, attn_norm_g[i][None, :], w_cg, conv_w[i],
                          w_out_att[i].astype(BF16), w_out_conv[i].astype(BF16),
                          w_o[i].astype(BF16), g_ffn, w_r, b_r, tm=TM_MIX)
        tcnt = counts[:, 0, N_EXPERTS:N_EXPERTS + N_GROUPS].astype(jnp.int32).reshape(-1)

        w_dn = w_down[i].reshape(N_GROUPS, GROUP_WIDTH, d).astype(BF16)
        x = _moe(tcnt, xe.reshape(t, XE_WIDTH), p[i].reshape(t, PLE_DIM), g_ffn,
                 w_gate[i].astype(BF16), w_up[i].astype(BF16), w_dn,
                 ple_norm_g[i][None, :], w_pg[i].astype(BF16), w_ple[i].astype(BF16),
                 tm=TM_MOE).reshape(b, s, d)
    return x
```

```python
import functools

import numpy as np
import jax
import jax.numpy as jnp
from jax import lax
from jax.experimental import pallas as pl
from jax.experimental.pallas import tpu as pltpu

D_MODEL = 1024
N_HEADS = 8
HEAD_DIM = 64
ATT_WIDTH = N_HEADS * HEAD_DIM
CONV_WIDTH = 512
CONV_K = 3
N_GROUPS = 4
EXPERTS_PER_GROUP = 4
N_EXPERTS = 16
D_EXPERT = 256
GROUP_WIDTH = EXPERTS_PER_GROUP * D_EXPERT
PLE_DIM = 256
EPS = 1e-6
NEG_INF = -1e30
LOG2E = 1.4426950408889634

LANES = 128
BF16_ROWS = 16
C_PIECES = 3
QC_LANE = HEAD_DIM
KC_LANE = HEAD_DIM + C_PIECES
V_ROWS = HEAD_DIM + BF16_ROWS
QKV_WIDTH = 3 * ATT_WIDTH + LANES
GID_LANE = N_EXPERTS
ROUTE_PITCH = 32
XE_WIDTH = D_MODEL + LANES
VMEM_LIMIT = 56 * 1024 * 1024

TM_QKV = 512
TQ_ATT = 1024
TK_ATT = 256
ATT_HEADS = 2
TM_MIX = 512
TM_MOE = TM_MIX
MOE_SORTED = -(-(TM_MOE + N_GROUPS * BF16_ROWS) // LANES) * LANES
MOE_CHUNK = 144
MOE_ROWS = MOE_SORTED + MOE_CHUNK

F32 = jnp.float32
BF16 = jnp.bfloat16


def _rms(xf, g):
    return xf * lax.rsqrt(jnp.mean(xf * xf, axis=-1, keepdims=True) + EPS) * g


def _log_sigmoid(z):
    return jnp.minimum(z, 0.0) - jnp.log1p(jnp.exp(-jnp.abs(z)))


def _bdot(a, b):
    return jnp.dot(a, b, preferred_element_type=F32)


def _bdot_nt(a, b):
    return lax.dot_general(a, b, (((1,), (1,)), ((), ())), preferred_element_type=F32)


def _bf16_pieces(x):
    hi = x.astype(BF16)
    r1 = x - hi.astype(F32)
    mid = r1.astype(BF16)
    lo = (r1 - mid.astype(F32)).astype(BF16)
    return hi, mid, lo


def _qkv_kernel(x0_ref, xa_ref, xb_ref, g_ref, w_ref, bf_ref, qg_ref, kg_ref, pk_ref,
                qt_ref, k_ref, vt_ref, pa_ref, pb_ref, carry_ref, *, tm, tiles_per_seq):
    step = pl.program_id(0)

    def project(x_ref, dst_ref):
        dst_ref[...] = _bdot(_rms(x_ref[...], g_ref[...]).astype(BF16), w_ref[...])

    @pl.when(step == 0)
    def _():
        carry_ref[...] = jnp.zeros_like(carry_ref)
        project(x0_ref, pa_ref)

    lane = lax.broadcasted_iota(jnp.int32, (tm, LANES), 1)
    row = lax.broadcasted_iota(jnp.int32, (tm, LANES), 0)
    k_ones = jnp.where((lane >= QC_LANE) & (lane < QC_LANE + C_PIECES), 1.0, 0.0)
    low = lane < HEAD_DIM
    ext_row = lax.broadcasted_iota(jnp.int32, (V_ROWS - HEAD_DIM, tm), 0)
    v_ext = jnp.where(ext_row == 0, 1.0, 0.0)

    def finish(p_ref, tile, half):
        rows = slice(half * tm, (half + 1) * tm)
        for j in range(N_HEADS // 2):
            pair_t = p_ref[:, 2 * ATT_WIDTH + LANES * j: 2 * ATT_WIDTH + LANES * (j + 1)].T
            for hh in range(2):
                vt_ref[0, 2 * j + hh, :, rows] = jnp.concatenate(
                    [pair_t[HEAD_DIM * hh: HEAD_DIM * (hh + 1)], v_ext], axis=0).astype(BF16)

        f3 = p_ref[:, 3 * ATT_WIDTH:QKV_WIDTH] + bf_ref[...]
        c = jnp.where(lane < C_PIECES * N_HEADS, _log_sigmoid(f3), 0.0)
        sh = 1
        while sh < tm:
            c = c + jnp.where(row >= sh, pltpu.roll(c, sh, axis=0), 0.0)
            sh *= 2
        c = c + jnp.where(tile % tiles_per_seq == 0, 0.0, carry_ref[...])
        carry_ref[...] = c[tm - 1:tm, :]
        hi, mid, lo = _bf16_pieces(c * LOG2E)
        piece = jnp.where(lane < N_HEADS, hi, jnp.where(lane < 2 * N_HEADS, mid, lo))

        def rms_scales(pair):
            sq = pair * pair
            ss_lo = jnp.sum(jnp.where(low, sq, 0.0), axis=1, keepdims=True)
            ss_hi = jnp.sum(jnp.where(low, 0.0, sq), axis=1, keepdims=True)
            return (lax.rsqrt(ss_lo * (1.0 / HEAD_DIM) + EPS), lax.rsqrt(ss_hi * (1.0 / HEAD_DIM) + EPS))

        piece_t = piece.astype(F32).T
        erow = lax.broadcasted_iota(jnp.int32, (8, tm), 0)
        pad_t = jnp.zeros((LANES - HEAD_DIM - 8, tm), F32)
        qgain = qg_ref[...]
        for j in range(N_HEADS // 2):
            pair = p_ref[:, LANES * j: LANES * (j + 1)]
            r_lo, r_hi = rms_scales(pair)
            qn_t = (pair * jnp.where(low, r_lo, r_hi) * qgain).T
            for hh in range(2):
                h = 2 * j + hh
                c_t = jnp.where(erow >= C_PIECES, jnp.where(erow < 2 * C_PIECES, 1.0, 0.0), 0.0)
                for idx in range(C_PIECES):
                    c_t = jnp.where(erow == idx, piece_t[idx * N_HEADS + h: idx * N_HEADS + h + 1], c_t)
                qt_ref[0, h, :, rows] = jnp.concatenate(
                    [qn_t[HEAD_DIM * hh: HEAD_DIM * (hh + 1)], c_t, pad_t], axis=0).astype(BF16)

        kc = _bdot(piece, pk_ref[...])
        kgain = kg_ref[...]
        for j in range(N_HEADS // 2):
            pair = p_ref[:, ATT_WIDTH + LANES * j: ATT_WIDTH + LANES * (j + 1)]
            r_lo, r_hi = rms_scales(pair)
            n_lo = pair * r_lo * kgain
            n_hi = pltpu.roll(pair, HEAD_DIM, axis=1) * r_hi * kgain
            for h, nrm in ((2 * j, n_lo), (2 * j + 1, n_hi)):
                aug = jnp.where(low, nrm, 0.0) + kc[:, LANES * h: LANES * (h + 1)] + k_ones
                k_ref[0, h, rows, :] = aug.astype(BF16)

    finish(pa_ref, 2 * step, 0)
    project(xa_ref, pb_ref)
    finish(pb_ref, 2 * step + 1, 1)
    project(xb_ref, pa_ref)


def _qkv_proj(x, g, w, bf3, qg, kg, pk, *, tm):
    b, s, d = x.shape
    x2 = x.reshape(b * s, d)
    n_tiles = b * s // tm
    per_seq = s // (2 * tm)
    full = lambda shape: pl.BlockSpec(shape, lambda i: (0,) * len(shape))
    return pl.pallas_call(
        functools.partial(_qkv_kernel, tm=tm, tiles_per_seq=s // tm),
        grid=(n_tiles // 2,),
        in_specs=[
            pl.BlockSpec((tm, d), lambda i: (0, 0)),
            pl.BlockSpec((tm, d), lambda i: (2 * i + 1, 0)),
            pl.BlockSpec((tm, d), lambda i: (jnp.minimum(2 * i + 2, n_tiles - 1), 0)),
            full(g.shape), full(w.shape), full(bf3.shape), full(qg.shape), full(kg.shape),
            full(pk.shape),
        ],
        out_specs=[pl.BlockSpec((1, N_HEADS, LANES, 2 * tm), lambda i: (i // per_seq, 0, 0, i % per_seq)),
                   pl.BlockSpec((1, N_HEADS, 2 * tm, LANES), lambda i: (i // per_seq, 0, i % per_seq, 0)),
                   pl.BlockSpec((1, N_HEADS, V_ROWS, 2 * tm), lambda i: (i // per_seq, 0, 0, i % per_seq))],
        out_shape=[
            jax.ShapeDtypeStruct((b, N_HEADS, LANES, s), BF16),
            jax.ShapeDtypeStruct((b, N_HEADS, s, LANES), BF16),
            jax.ShapeDtypeStruct((b, N_HEADS, V_ROWS, s), BF16),
        ],
        scratch_shapes=[pltpu.VMEM((tm, QKV_WIDTH), F32), pltpu.VMEM((tm, QKV_WIDTH), F32),
                        pltpu.VMEM((1, LANES), F32)],
        compiler_params=pltpu.CompilerParams(
            dimension_semantics=("arbitrary",), vmem_limit_bytes=VMEM_LIMIT),
        name="qkv_proj",
    )(x2, x2, x2, g, w, bf3, qg, kg, pk)


def _attn_kernel(qt_ref, qn_ref, k_ref, vt_ref, bias_ref, o_ref, sa_ref, sb_ref, *, tq, tk, n_q):
    qi = pl.program_id(2)
    n_sub = tq // tk
    bufs = (sa_ref, sb_ref)

    def produce(j, s_ref, lo=0, q_ref=qt_ref):
        start = j * tk
        for hh in range(ATT_HEADS):
            s_ref[hh, :, lo:tq] = _bdot(k_ref[0, hh, pl.ds(start, tk), :], q_ref[0, hh, :, lo:tq])

    def consume(j, s_ref, carry, lo=0, masked=False):
        start = j * tk
        out = []
        for hh in range(ATT_HEADS):
            m, acc = carry[hh]

            def read():
                if not masked:
                    return s_ref[hh, :, lo:tq]
                tri = s_ref[hh, :, lo:lo + tk] + bias_ref[...]
                return tri if lo + tk == tq else jnp.concatenate([tri, s_ref[hh, :, lo + tk:tq]], axis=1)

            m_new = jnp.maximum(m[:, lo:tq], jnp.max(read(), axis=0, keepdims=True))
            alpha = jnp.exp2(m[:, lo:tq] - m_new)
            p = jnp.exp2(read() - m_new).astype(BF16)
            acc_new = alpha * acc[:, lo:tq] + _bdot(vt_ref[0, hh, :, pl.ds(start, tk)], p)
            if lo:
                m_new = jnp.concatenate([m[:, 0:lo], m_new], axis=1)
                acc_new = jnp.concatenate([acc[:, 0:lo], acc_new], axis=1)
            out.append((m_new, acc_new))
        return tuple(out)

    def trip(r, carry):
        for t in range(n_sub):
            j = r * n_sub + t
            produce(j + 1, bufs[(t + 1) % 2])
            carry = consume(j, bufs[t % 2], carry)
        return carry

    def query_tile(q):
        if q == 0:
            produce(0, sa_ref)
        carry = tuple((jnp.full((1, tq), NEG_INF, F32), jnp.zeros((V_ROWS, tq), F32))
                      for _ in range(ATT_HEADS))
        for r in range(q):
            carry = trip(r, carry)
        for t in range(n_sub):
            j = q * n_sub + t
            if t + 1 < n_sub:
                produce(j + 1, bufs[(t + 1) % 2], lo=(t + 1) * tk)
            elif q + 1 < n_q:
                produce(0, bufs[(t + 1) % 2], q_ref=qn_ref)
            carry = consume(j, bufs[t % 2], carry, lo=t * tk, masked=True)
        y_t = jnp.concatenate([acc[0:HEAD_DIM] / acc[HEAD_DIM:HEAD_DIM + 1] for _, acc in carry], axis=0)
        o_ref[0] = y_t.T.astype(BF16)

    for q in range(n_q):
        pl.when(qi == q)(functools.partial(query_tile, q))


def _attention(qt_aug, k_aug, vt, *, tq, tk):
    b, nh, s, _ = k_aug.shape
    assert tq % (2 * tk) == 0 and s % tq == 0
    future = np.arange(tk)[:, None] > np.arange(tk)[None, :]
    bias = jnp.asarray(np.where(future, np.float32(NEG_INF), np.float32(0.0)))
    nh_step = ATT_HEADS
    grid = (b, nh // nh_step, s // tq)
    return pl.pallas_call(
        functools.partial(_attn_kernel, tq=tq, tk=tk, n_q=s // tq),
        grid=grid,
        in_specs=[
            pl.BlockSpec((1, nh_step, LANES, tq), lambda i, h, j: (i, h, 0, j)),
            pl.BlockSpec((1, nh_step, LANES, tq), lambda i, h, j: (i, h, 0, jnp.minimum(j + 1, s // tq - 1))),
            pl.BlockSpec((1, nh_step, s, LANES), lambda i, h, j: (i, h, 0, 0)),
            pl.BlockSpec((1, nh_step, V_ROWS, s), lambda i, h, j: (i, h, 0, 0)),
            pl.BlockSpec((tk, tk), lambda i, h, j: (0, 0)),
        ],
        out_specs=pl.BlockSpec((1, tq, nh_step * HEAD_DIM), lambda i, h, j: (i, j, h)),
        out_shape=jax.ShapeDtypeStruct((b, s, ATT_WIDTH), BF16),
        scratch_shapes=[pltpu.VMEM((nh_step, tk, tq), F32)] * 2,
        compiler_params=pltpu.CompilerParams(
            dimension_semantics=("arbitrary", "arbitrary", "arbitrary"),
            vmem_limit_bytes=VMEM_LIMIT),
        name="fox_attention",
    )(qt_aug, qt_aug, k_aug, vt, bias)


def _mix_kernel(x0_ref, xa_ref, xb_ref, y0_ref, ya_ref, yb_ref, g_ref, wcg_ref, cw_ref, wa_ref, wb_ref,
                wo_ref, g2_ref, wr_ref, br_ref, xe_ref, cnt_ref, ra_ref, rb_ref, carry_ref,
                *, tm, tiles_per_seq):
    step = pl.program_id(0)

    def residual(x_ref, y_ref, tile, dst_ref):
        x = x_ref[...]
        h = _rms(x, g_ref[...]).astype(BF16)
        pc = _bdot(h, wcg_ref[...])
        cw = CONV_WIDTH
        cb = pc[:, 0:cw]
        prod = pc[:, cw:2 * cw] * pc[:, 2 * cw:3 * cw]
        ga = pc[:, 3 * cw:3 * cw + D_MODEL]
        gb = pc[:, 3 * cw + D_MODEL:3 * cw + 2 * D_MODEL]

        prev = jnp.where(tile % tiles_per_seq == 0, 0.0, carry_ref[...])
        crow = lax.broadcasted_iota(jnp.int32, (tm, cw), 0)
        m1 = jnp.where(crow == 0, prev[7:8, :], pltpu.roll(prod, 1, axis=0))
        m2 = jnp.where(crow == 0, prev[6:7, :],
                       jnp.where(crow == 1, prev[7:8, :], pltpu.roll(prod, 2, axis=0)))
        carry_ref[...] = prod[tm - 8:tm, :]
        w = cw_ref[...]
        y_conv = cb * (w[0:1, :] * m2 + w[1:2, :] * m1 + w[2:3, :] * prod)

        a = _bdot(y_ref[...], wa_ref[...])
        bb = _bdot(y_conv.astype(BF16), wb_ref[...])
        merged = jax.nn.sigmoid(ga) * a + jax.nn.sigmoid(gb) * bb
        dst_ref[...] = x + _bdot(merged.astype(BF16), wo_ref[...])

    @pl.when(step == 0)
    def _():
        carry_ref[...] = jnp.zeros_like(carry_ref)
        residual(x0_ref, y0_ref, 0, ra_ref)

    _route(ra_ref, g2_ref, wr_ref, br_ref, xe_ref, cnt_ref, 0, tm)
    residual(xa_ref, ya_ref, 2 * step + 1, rb_ref)
    _route(rb_ref, g2_ref, wr_ref, br_ref, xe_ref, cnt_ref, 1, tm)
    residual(xb_ref, yb_ref, 2 * step + 2, ra_ref)


def _route(x1_ref, g2_ref, wr_ref, br_ref, xe_ref, cnt_ref, half, tm):
    rows = slice(half * tm, (half + 1) * tm)
    x1 = x1_ref[...]
    xe_ref[rows, 0:D_MODEL] = x1
    h2 = _rms(x1, g2_ref[...]).astype(BF16)

    logits = _bdot(h2, wr_ref[...]) + br_ref[...]
    lane = lax.broadcasted_iota(jnp.int32, (tm, LANES), 1)
    lanef = lane.astype(F32)
    is_g = (lane >= N_EXPERTS) & (lane < N_EXPERTS + N_GROUPS)
    gl = jnp.where(is_g, logits, NEG_INF)
    gmax = jnp.max(gl, axis=1, keepdims=True)
    gsum = jnp.sum(jnp.exp(gl - gmax), axis=1, keepdims=True)
    g_val = 1.0 / gsum
    g_lane = jnp.min(jnp.where(gl == gmax, lanef, float(LANES)), axis=1, keepdims=True)
    e_lo = (g_lane - float(N_EXPERTS)) * float(EXPERTS_PER_GROUP)
    in_grp = (lanef >= e_lo) & (lanef < e_lo + float(EXPERTS_PER_GROUP))
    el = jnp.where(in_grp, logits, NEG_INF)
    t1 = jnp.max(el, axis=1, keepdims=True)
    i1 = jnp.min(jnp.where(el == t1, lanef, float(LANES)), axis=1, keepdims=True)
    el2 = jnp.where(lanef == i1, NEG_INF, el)
    t2 = jnp.max(el2, axis=1, keepdims=True)
    i2 = jnp.min(jnp.where(el2 == t2, lanef, float(LANES)), axis=1, keepdims=True)
    e2 = jnp.exp(t2 - t1)
    w1 = g_val / (1.0 + e2)
    w2 = g_val * e2 / (1.0 + e2)
    comb = jnp.where(lanef == i1, w1, 0.0) + jnp.where(lanef == i2, w2, 0.0)

    cnt_ref[half] = jnp.sum(jnp.where(is_g & (lanef == g_lane), 1.0, 0.0), axis=0, keepdims=True)
    xe_ref[rows, D_MODEL:XE_WIDTH] = comb + jnp.where(lane == GID_LANE, g_lane - float(N_EXPERTS), 0.0)


def _mix(x, y_att, g, wcg, conv_w, wa, wb, wo, g2, wr, br, *, tm):
    b, s, d = x.shape
    t = b * s
    n_tiles = t // tm
    x2 = x.reshape(t, d)
    y2 = y_att.reshape(t, ATT_WIDTH)
    once = lambda a: pl.BlockSpec(a.shape, lambda i: (0,) * a.ndim, pipeline_mode=pl.Buffered(1))

    def tiles(width):
        nxt = lambda i: (jnp.minimum(2 * i + 2, n_tiles - 1), 0)
        return [pl.BlockSpec((tm, width), lambda i: (0, 0), pipeline_mode=pl.Buffered(1)),
                pl.BlockSpec((tm, width), lambda i: (2 * i + 1, 0)),
                pl.BlockSpec((tm, width), nxt)]

    return pl.pallas_call(
        functools.partial(_mix_kernel, tm=tm, tiles_per_seq=s // tm),
        grid=(n_tiles // 2,),
        in_specs=tiles(d) + tiles(ATT_WIDTH) + [once(g), once(wcg), once(conv_w), once(wa), once(wb),
                                                once(wo), once(g2), once(wr), once(br)],
        out_specs=[pl.BlockSpec((2 * tm, XE_WIDTH), lambda i: (i, 0)),
                   pl.BlockSpec((2, 1, LANES), lambda i: (i, 0, 0))],
        out_shape=[
            jax.ShapeDtypeStruct((t, XE_WIDTH), F32),
            jax.ShapeDtypeStruct((n_tiles, 1, LANES), F32),
        ],
        scratch_shapes=[pltpu.VMEM((tm, d), F32), pltpu.VMEM((tm, d), F32),
                        pltpu.VMEM((8, CONV_WIDTH), F32)],
        compiler_params=pltpu.CompilerParams(
            dimension_semantics=("arbitrary",), vmem_limit_bytes=VMEM_LIMIT),
        name="mix",
    )(x2, x2, x2, y2, y2, y2, g, wcg, conv_w, wa, wb, wo, g2, wr, br)


def _moe_kernel(tcnt_ref, xe_ref, p_ref, g2_ref, wg_ref, wu_ref, wd_ref, g3_ref, wpg_ref, wple_ref,
                earlier_ref, o_ref, hs_ref, rs_ref, ys_ref, *, tm):
    i = pl.program_id(0)
    n = [tcnt_ref[i * N_GROUPS + k] for k in range(N_GROUPS)]
    starts = [jnp.int32(0)]
    for k in range(N_GROUPS - 1):
        starts.append(starts[-1] + pl.cdiv(n[k], BF16_ROWS) * BF16_ROWS)

    x1 = xe_ref[:, 0:D_MODEL]
    route = xe_ref[:, D_MODEL:XE_WIDTH]
    h2 = _rms(x1, g2_ref[...]).astype(BF16)

    lane = lax.broadcasted_iota(jnp.int32, (tm, LANES), 1)
    lanef = lane.astype(F32)
    gid = jnp.sum(jnp.where(lane == GID_LANE, route, 0.0), axis=1, keepdims=True)
    onehot = jnp.where((lanef == gid) & (lane < N_GROUPS), 1.0, 0.0)
    rank = jnp.sum(onehot * _bdot(earlier_ref[...], onehot.astype(BF16)), axis=1, keepdims=True)
    base = starts[N_GROUPS - 1].astype(F32)
    for k in range(N_GROUPS - 2, -1, -1):
        base = jnp.where(gid == float(k), starts[k].astype(F32), base)
    pos = base + rank

    unsort = (pos == lax.broadcasted_iota(jnp.int32, (tm, MOE_SORTED), 1).astype(F32)).astype(BF16)
    digit_hi = jnp.floor(pos * (1.0 / 32.0))
    digits = jnp.where(lane == 0, digit_hi, jnp.where(lane == 1, pos - 32.0 * digit_hi, 0.0))
    sel_lane = lax.broadcasted_iota(jnp.int32, (BF16_ROWS, LANES), 1)
    sel = jnp.where(sel_lane == 0, 32.0, jnp.where(sel_lane == 1, 1.0, 0.0)).astype(BF16)
    pos_row = _bdot_nt(sel, digits.astype(BF16))[0:1, :]
    sort = (lax.broadcasted_iota(jnp.int32, (MOE_SORTED, tm), 0).astype(F32) == pos_row).astype(BF16)

    hs_ref[0:MOE_SORTED] = _bdot(sort, h2).astype(BF16)
    r_hi, r_mid, r_lo = _bf16_pieces(route)
    packed = (r_hi.astype(F32) + pltpu.roll(r_mid.astype(F32), ROUTE_PITCH, axis=1)
              + pltpu.roll(r_lo.astype(F32), 2 * ROUTE_PITCH, axis=1)).astype(BF16)
    rsorted = _bdot(sort, packed)
    rs_ref[0:MOE_SORTED] = (rsorted + pltpu.roll(rsorted, LANES - ROUTE_PITCH, axis=1)
                            + pltpu.roll(rsorted, LANES - 2 * ROUTE_PITCH, axis=1))
    hs_ref[MOE_SORTED:MOE_ROWS] = jnp.zeros((MOE_CHUNK, D_MODEL), BF16)
    rs_ref[MOE_SORTED:MOE_ROWS] = jnp.zeros((MOE_CHUNK, LANES), F32)
    ys_ref[...] = jnp.zeros_like(ys_ref)

    clane = lax.broadcasted_iota(jnp.int32, (MOE_CHUNK, LANES), 1)

    def chunk(g, c):
        r0 = pl.multiple_of(starts[g] + c * MOE_CHUNK, BF16_ROWS)
        hrows = hs_ref[pl.ds(r0, MOE_CHUNK), :]
        rt = rs_ref[pl.ds(r0, MOE_CHUNK), :]
        parts = []
        for e in range(EXPERTS_PER_GROUP):
            ex = g * EXPERTS_PER_GROUP + e
            a = _bdot(hrows, wg_ref[ex])
            u = _bdot(hrows, wu_ref[ex])
            ce = jnp.sum(jnp.where(clane == ex, rt, 0.0), axis=1, keepdims=True)
            parts.append(((a * jax.nn.sigmoid(a)) * u * ce).astype(BF16))
        out = _bdot(jnp.concatenate(parts, axis=1), wd_ref[g])
        ys_ref[pl.ds(r0, MOE_CHUNK), :] = (ys_ref[pl.ds(r0, MOE_CHUNK), :].astype(F32) + out).astype(BF16)

    for g in range(N_GROUPS):
        def more(c, _, g=g):
            chunk(g, c)
            return 0
        lax.fori_loop(1, pl.cdiv(n[g], MOE_CHUNK), more, 0)
    for g in range(N_GROUPS):
        chunk(g, 0)

    x2 = x1 + _bdot(unsort, ys_ref[0:MOE_SORTED])
    h3 = _rms(x2, g3_ref[...]).astype(BF16)
    gate = jax.nn.sigmoid(_bdot(h3, wpg_ref[...]))
    emb = _bdot(p_ref[...].astype(BF16), wple_ref[...])
    o_ref[...] = x2 + gate * emb


def _moe(tcnt, xe, p, g2, wg, wu, wd, g3, wpg, wple, *, tm):
    t = xe.shape[0]
    d = D_MODEL
    once = lambda a: pl.BlockSpec(a.shape, lambda i, c: (0,) * a.ndim, pipeline_mode=pl.Buffered(1))
    row = lambda width: pl.BlockSpec((tm, width), lambda i, c: (i, 0))
    earlier = jnp.asarray(np.tril(np.ones((tm, tm), np.float32), -1), BF16)
    return pl.pallas_call(
        functools.partial(_moe_kernel, tm=tm),
        grid_spec=pltpu.PrefetchScalarGridSpec(
            num_scalar_prefetch=1,
            grid=(t // tm,),
            in_specs=[row(XE_WIDTH), row(PLE_DIM), once(g2), once(wg), once(wu), once(wd), once(g3),
                      once(wpg), once(wple), once(earlier)],
            out_specs=row(d),
            scratch_shapes=[pltpu.VMEM((MOE_ROWS, d), BF16), pltpu.VMEM((MOE_ROWS, LANES), F32),
                            pltpu.VMEM((MOE_ROWS, d), BF16)],
        ),
        out_shape=jax.ShapeDtypeStruct((t, d), F32),
        compiler_params=pltpu.CompilerParams(
            dimension_semantics=("arbitrary",), vmem_limit_bytes=VMEM_LIMIT),
        name="moe",
    )(tcnt, xe, p, g2, wg, wu, wd, g3, wpg, wple, earlier)


def _k_select_matrix():
    pk = np.zeros((LANES, N_HEADS * LANES), np.float32)
    for idx in range(C_PIECES):
        for h in range(N_HEADS):
            pk[idx * N_HEADS + h, h * LANES + KC_LANE + idx] = -1.0
    return jnp.asarray(pk, BF16)


def kernel(x, p, attn_norm_g, w_in, b_f, q_norm_g, k_norm_g, conv_w, w_out_att, w_out_conv, w_o,
           ffn_norm_g, w_rg, b_rg, w_re, b_re, w_gate, w_up, w_down, ple_norm_g, w_pg, w_ple):
    b, s, d = x.shape
    t = b * s
    aw = ATT_WIDTH
    for i in range(w_in.shape[0]):
        wi = w_in[i]
        wf = wi[:, 3 * aw:3 * aw + N_HEADS]
        w_qkvf = jnp.concatenate(
            [wi[:, :3 * aw], wf, wf, wf, jnp.zeros((d, LANES - C_PIECES * N_HEADS), F32)],
            axis=1).astype(BF16)
        w_cg = wi[:, 3 * aw + N_HEADS:].astype(BF16)
        bf3 = jnp.concatenate([b_f[i]] * C_PIECES + [jnp.zeros((LANES - C_PIECES * N_HEADS,), F32)])[None, :]
        scale = HEAD_DIM ** -0.5 * LOG2E
        qg = jnp.tile(q_norm_g[i] * scale, 2)[None, :]
        kg = jnp.tile(k_norm_g[i], 2)[None, :]
        qt_aug, k_aug, vt = _qkv_proj(x, attn_norm_g[i][None, :], w_qkvf, bf3, qg, kg, _k_select_matrix(),
                                      tm=TM_QKV)
        y_att = _attention(qt_aug, k_aug, vt, tq=TQ_ATT, tk=TK_ATT)

        w_r = jnp.concatenate(
            [w_re[i], w_rg[i], jnp.zeros((d, LANES - N_EXPERTS - N_GROUPS), F32)], axis=1).astype(BF16)
        b_r = jnp.concatenate(
            [b_re[i], b_rg[i], jnp.zeros((LANES - N_EXPERTS - N_GROUPS,), F32)])[None, :]
        g_ffn = ffn_norm_g[i][None, :]
        xe, counts = _mix(x, y_att, attn_norm_g[i][None, :], w_cg, conv_w[i],
                          w_out_att[i].astype(BF16), w_out_conv[i].astype(BF16),
                          w_o[i].astype(BF16), g_ffn, w_r, b_r, tm=TM_MIX)
        tcnt = counts[:, 0, N_EXPERTS:N_EXPERTS + N_GROUPS].astype(jnp.int32).reshape(-1)

        w_dn = w_down[i].reshape(N_GROUPS, GROUP_WIDTH, d).astype(BF16)
        x = _moe(tcnt, xe.reshape(t, XE_WIDTH), p[i].reshape(t, PLE_DIM), g_ffn,
                 w_gate[i].astype(BF16), w_up[i].astype(BF16), w_dn,
                 ple_norm_g[i][None, :], w_pg[i].astype(BF16), w_ple[i].astype(BF16),
                 tm=TM_MOE).reshape(b, s, d)
    return x
```

```python
import functools

import numpy as np
import jax
import jax.numpy as jnp
from jax import lax
from jax.experimental import pallas as pl
from jax.experimental.pallas import tpu as pltpu

D_MODEL = 1024
N_HEADS = 8
HEAD_DIM = 64
ATT_WIDTH = N_HEADS * HEAD_DIM
CONV_WIDTH = 512
CONV_K = 3
N_GROUPS = 4
EXPERTS_PER_GROUP = 4
N_EXPERTS = 16
D_EXPERT = 256
GROUP_WIDTH = EXPERTS_PER_GROUP * D_EXPERT
PLE_DIM = 256
EPS = 1e-6
NEG_INF = -1e30
LOG2E = 1.4426950408889634

LANES = 128
BF16_ROWS = 16
C_PIECES = 3
QC_LANE = HEAD_DIM
KC_LANE = HEAD_DIM + C_PIECES
V_ROWS = HEAD_DIM + BF16_ROWS
QKV_WIDTH = 3 * ATT_WIDTH + LANES
GID_LANE = N_EXPERTS
ROUTE_PITCH = 32
XE_WIDTH = D_MODEL + LANES
VMEM_LIMIT = 56 * 1024 * 1024

TM_QKV = 512
TQ_ATT = 1024
TK_ATT = 256
ATT_HEADS = 2
TM_MIX = 512
TM_MOE = TM_MIX
MOE_SORTED = -(-(TM_MOE + N_GROUPS * BF16_ROWS) // LANES) * LANES
MOE_CHUNK = 144
MOE_ROWS = MOE_SORTED + MOE_CHUNK

F32 = jnp.float32
BF16 = jnp.bfloat16


def _rms(xf, g):
    return xf * lax.rsqrt(jnp.mean(xf * xf, axis=-1, keepdims=True) + EPS) * g


def _log_sigmoid(z):
    return jnp.minimum(z, 0.0) - jnp.log1p(jnp.exp(-jnp.abs(z)))


def _bdot(a, b):
    return jnp.dot(a, b, preferred_element_type=F32)


def _bdot_nt(a, b):
    return lax.dot_general(a, b, (((1,), (1,)), ((), ())), preferred_element_type=F32)


def _bf16_pieces(x):
    hi = x.astype(BF16)
    r1 = x - hi.astype(F32)
    mid = r1.astype(BF16)
    lo = (r1 - mid.astype(F32)).astype(BF16)
    return hi, mid, lo


def _qkv_kernel(x0_ref, xa_ref, xb_ref, g_ref, w_ref, bf_ref, qg_ref, kg_ref, pk_ref,
                qt_ref, k_ref, vt_ref, pa_ref, pb_ref, carry_ref, *, tm, tiles_per_seq):
    step = pl.program_id(0)

    def project(x_ref, dst_ref):
        dst_ref[...] = _bdot(_rms(x_ref[...], g_ref[...]).astype(BF16), w_ref[...])

    @pl.when(step == 0)
    def _():
        carry_ref[...] = jnp.zeros_like(carry_ref)
        project(x0_ref, pa_ref)

    lane = lax.broadcasted_iota(jnp.int32, (tm, LANES), 1)
    row = lax.broadcasted_iota(jnp.int32, (tm, LANES), 0)
    k_ones = jnp.where((lane >= QC_LANE) & (lane < QC_LANE + C_PIECES), 1.0, 0.0)
    low = lane < HEAD_DIM
    ext_row = lax.broadcasted_iota(jnp.int32, (V_ROWS - HEAD_DIM, tm), 0)
    v_ext = jnp.where(ext_row == 0, 1.0, 0.0)

    def finish(p_ref, tile, half):
        rows = slice(half * tm, (half + 1) * tm)
        for j in range(N_HEADS // 2):
            pair_t = p_ref[:, 2 * ATT_WIDTH + LANES * j: 2 * ATT_WIDTH + LANES * (j + 1)].T
            for hh in range(2):
                vt_ref[0, 2 * j + hh, :, rows] = jnp.concatenate(
                    [pair_t[HEAD_DIM * hh: HEAD_DIM * (hh + 1)], v_ext], axis=0).astype(BF16)

        f3 = p_ref[:, 3 * ATT_WIDTH:QKV_WIDTH] + bf_ref[...]
        c = jnp.where(lane < C_PIECES * N_HEADS, _log_sigmoid(f3), 0.0)
        sh = 1
        while sh < tm:
            c = c + jnp.where(row >= sh, pltpu.roll(c, sh, axis=0), 0.0)
            sh *= 2
        c = c + jnp.where(tile % tiles_per_seq == 0, 0.0, carry_ref[...])
        carry_ref[...] = c[tm - 1:tm, :]
        hi, mid, lo = _bf16_pieces(c * LOG2E)
        piece = jnp.where(lane < N_HEADS, hi, jnp.where(lane < 2 * N_HEADS, mid, lo))

        def rms_scales(pair):
            sq = pair * pair
            ss_lo = jnp.sum(jnp.where(low, sq, 0.0), axis=1, keepdims=True)
            ss_hi = jnp.sum(jnp.where(low, 0.0, sq), axis=1, keepdims=True)
            return (lax.rsqrt(ss_lo * (1.0 / HEAD_DIM) + EPS), lax.rsqrt(ss_hi * (1.0 / HEAD_DIM) + EPS))

        piece_t = piece.astype(F32).T
        erow = lax.broadcasted_iota(jnp.int32, (8, tm), 0)
        pad_t = jnp.zeros((LANES - HEAD_DIM - 8, tm), F32)
        qgain = qg_ref[...]
        for j in range(N_HEADS // 2):
            pair = p_ref[:, LANES * j: LANES * (j + 1)]
            r_lo, r_hi = rms_scales(pair)
            qn_t = (pair * jnp.where(low, r_lo, r_hi) * qgain).T
            for hh in range(2):
                h = 2 * j + hh
                c_t = jnp.where(erow >= C_PIECES, jnp.where(erow < 2 * C_PIECES, 1.0, 0.0), 0.0)
                for idx in range(C_PIECES):
                    c_t = jnp.where(erow == idx, piece_t[idx * N_HEADS + h: idx * N_HEADS + h + 1], c_t)
                qt_ref[0, h, :, rows] = jnp.concatenate(
                    [qn_t[HEAD_DIM * hh: HEAD_DIM * (hh + 1)], c_t, pad_t], axis=0).astype(BF16)

        kc = _bdot(piece, pk_ref[...])
        kgain = kg_ref[...]
        for j in range(N_HEADS // 2):
            pair = p_ref[:, ATT_WIDTH + LANES * j: ATT_WIDTH + LANES * (j + 1)]
            r_lo, r_hi = rms_scales(pair)
            n_lo = pair * r_lo * kgain
            n_hi = pltpu.roll(pair, HEAD_DIM, axis=1) * r_hi * kgain
            for h, nrm in ((2 * j, n_lo), (2 * j + 1, n_hi)):
                aug = jnp.where(low, nrm, 0.0) + kc[:, LANES * h: LANES * (h + 1)] + k_ones
                k_ref[0, h, rows, :] = aug.astype(BF16)

    finish(pa_ref, 2 * step, 0)
    project(xa_ref, pb_ref)
    finish(pb_ref, 2 * step + 1, 1)
    project(xb_ref, pa_ref)


def _qkv_proj(x, g, w, bf3, qg, kg, pk, *, tm):
    b, s, d = x.shape
    x2 = x.reshape(b * s, d)
    n_tiles = b * s // tm
    per_seq = s // (2 * tm)
    full = lambda shape: pl.BlockSpec(shape, lambda i: (0,) * len(shape))
    return pl.pallas_call(
        functools.partial(_qkv_kernel, tm=tm, tiles_per_seq=s // tm),
        grid=(n_tiles // 2,),
        in_specs=[
            pl.BlockSpec((tm, d), lambda i: (0, 0)),
            pl.BlockSpec((tm, d), lambda i: (2 * i + 1, 0)),
            pl.BlockSpec((tm, d), lambda i: (jnp.minimum(2 * i + 2, n_tiles - 1), 0)),
            full(g.shape), full(w.shape), full(bf3.shape), full(qg.shape), full(kg.shape),
            full(pk.shape),
        ],
        out_specs=[pl.BlockSpec((1, N_HEADS, LANES, 2 * tm), lambda i: (i // per_seq, 0, 0, i % per_seq)),
                   pl.BlockSpec((1, N_HEADS, 2 * tm, LANES), lambda i: (i // per_seq, 0, i % per_seq, 0)),
                   pl.BlockSpec((1, N_HEADS, V_ROWS, 2 * tm), lambda i: (i // per_seq, 0, 0, i % per_seq))],
        out_shape=[
            jax.ShapeDtypeStruct((b, N_HEADS, LANES, s), BF16),
            jax.ShapeDtypeStruct((b, N_HEADS, s, LANES), BF16),
            jax.ShapeDtypeStruct((b, N_HEADS, V_ROWS, s), BF16),
        ],
        scratch_shapes=[pltpu.VMEM((tm, QKV_WIDTH), F32), pltpu.VMEM((tm, QKV_WIDTH), F32),
                        pltpu.VMEM((1, LANES), F32)],
        compiler_params=pltpu.CompilerParams(
            dimension_semantics=("arbitrary",), vmem_limit_bytes=VMEM_LIMIT),
        name="qkv_proj",
    )(x2, x2, x2, g, w, bf3, qg, kg, pk)


def _attn_kernel(qt_ref, qn_ref, k_ref, vt_ref, bias_ref, o_ref, sa_ref, sb_ref, *, tq, tk, n_q):
    qi = pl.program_id(2)
    n_sub = tq // tk
    bufs = (sa_ref, sb_ref)

    def produce(j, s_ref, lo=0, q_ref=qt_ref):
        start = j * tk
        for hh in range(ATT_HEADS):
            s_ref[hh, :, lo:tq] = _bdot(k_ref[0, hh, pl.ds(start, tk), :], q_ref[0, hh, :, lo:tq])

    def consume(j, s_ref, carry, lo=0, masked=False):
        start = j * tk
        out = []
        for hh in range(ATT_HEADS):
            m, acc = carry[hh]

            def read():
                if not masked:
                    return s_ref[hh, :, lo:tq]
                tri = s_ref[hh, :, lo:lo + tk] + bias_ref[...]
                return tri if lo + tk == tq else jnp.concatenate([tri, s_ref[hh, :, lo + tk:tq]], axis=1)

            m_new = jnp.maximum(m[:, lo:tq], jnp.max(read(), axis=0, keepdims=True))
            alpha = jnp.exp2(m[:, lo:tq] - m_new)
            p = jnp.exp2(read() - m_new).astype(BF16)
            acc_new = alpha * acc[:, lo:tq] + _bdot(vt_ref[0, hh, :, pl.ds(start, tk)], p)
            if lo:
                m_new = jnp.concatenate([m[:, 0:lo], m_new], axis=1)
                acc_new = jnp.concatenate([acc[:, 0:lo], acc_new], axis=1)
            out.append((m_new, acc_new))
        return tuple(out)

    def trip(r, carry):
        for t in range(n_sub):
            j = r * n_sub + t
            produce(j + 1, bufs[(t + 1) % 2])
            carry = consume(j, bufs[t % 2], carry)
        return carry

    def query_tile(q):
        if q == 0:
            produce(0, sa_ref)
        carry = tuple((jnp.full((1, tq), NEG_INF, F32), jnp.zeros((V_ROWS, tq), F32))
                      for _ in range(ATT_HEADS))
        for r in range(q):
            carry = trip(r, carry)
        for t in range(n_sub):
            j = q * n_sub + t
            if t + 1 < n_sub:
                produce(j + 1, bufs[(t + 1) % 2], lo=(t + 1) * tk)
            elif q + 1 < n_q:
                produce(0, bufs[(t + 1) % 2], q_ref=qn_ref)
            carry = consume(j, bufs[t % 2], carry, lo=t * tk, masked=True)
        y_t = jnp.concatenate([acc[0:HEAD_DIM] / acc[HEAD_DIM:HEAD_DIM + 1] for _, acc in carry], axis=0)
        o_ref[0] = y_t.T.astype(BF16)

    for q in range(n_q):
        pl.when(qi == q)(functools.partial(query_tile, q))


def _attention(qt_aug, k_aug, vt, *, tq, tk):
    b, nh, s, _ = k_aug.shape
    assert tq % (2 * tk) == 0 and s % tq == 0
    future = np.arange(tk)[:, None] > np.arange(tk)[None, :]
    bias = jnp.asarray(np.where(future, np.float32(NEG_INF), np.float32(0.0)))
    nh_step = ATT_HEADS
    grid = (b, nh // nh_step, s // tq)
    return pl.pallas_call(
        functools.partial(_attn_kernel, tq=tq, tk=tk, n_q=s // tq),
        grid=grid,
        in_specs=[
            pl.BlockSpec((1, nh_step, LANES, tq), lambda i, h, j: (i, h, 0, j)),
            pl.BlockSpec((1, nh_step, LANES, tq), lambda i, h, j: (i, h, 0, jnp.minimum(j + 1, s // tq - 1))),
            pl.BlockSpec((1, nh_step, s, LANES), lambda i, h, j: (i, h, 0, 0)),
            pl.BlockSpec((1, nh_step, V_ROWS, s), lambda i, h, j: (i, h, 0, 0)),
            pl.BlockSpec((tk, tk), lambda i, h, j: (0, 0)),
        ],
        out_specs=pl.BlockSpec((1, tq, nh_step * HEAD_DIM), lambda i, h, j: (i, j, h)),
        out_shape=jax.ShapeDtypeStruct((b, s, ATT_WIDTH), BF16),
        scratch_shapes=[pltpu.VMEM((nh_step, tk, tq), F32)] * 2,
        compiler_params=pltpu.CompilerParams(
            dimension_semantics=("arbitrary", "arbitrary", "arbitrary"),
            vmem_limit_bytes=VMEM_LIMIT),
        name="fox_attention",
    )(qt_aug, qt_aug, k_aug, vt, bias)


def _mix_kernel(x0_ref, xa_ref, xb_ref, y0_ref, ya_ref, yb_ref, g_ref, wcg_ref, cw_ref, wa_ref, wb_ref,
                wo_ref, g2_ref, wr_ref, br_ref, xe_ref, cnt_ref, ra_ref, rb_ref, carry_ref,
                *, tm, tiles_per_seq):
    step = pl.program_id(0)

    def residual(x_ref, y_ref, tile, dst_ref):
        x = x_ref[...]
        h = _rms(x, g_ref[...]).astype(BF16)
        pc = _bdot(h, wcg_ref[...])
        cw = CONV_WIDTH
        cb = pc[:, 0:cw]
        prod = pc[:, cw:2 * cw] * pc[:, 2 * cw:3 * cw]
        ga = pc[:, 3 * cw:3 * cw + D_MODEL]
        gb = pc[:, 3 * cw + D_MODEL:3 * cw + 2 * D_MODEL]

        prev = jnp.where(tile % tiles_per_seq == 0, 0.0, carry_ref[...])
        crow = lax.broadcasted_iota(jnp.int32, (tm, cw), 0)
        m1 = jnp.where(crow == 0, prev[7:8, :], pltpu.roll(prod, 1, axis=0))
        m2 = jnp.where(crow == 0, prev[6:7, :],
                       jnp.where(crow == 1, prev[7:8, :], pltpu.roll(prod, 2, axis=0)))
        carry_ref[...] = prod[tm - 8:tm, :]
        w = cw_ref[...]
        y_conv = cb * (w[0:1, :] * m2 + w[1:2, :] * m1 + w[2:3, :] * prod)

        a = _bdot(y_ref[...], wa_ref[...])
        bb = _bdot(y_conv.astype(BF16), wb_ref[...])
        merged = jax.nn.sigmoid(ga) * a + jax.nn.sigmoid(gb) * bb
        dst_ref[...] = x + _bdot(merged.astype(BF16), wo_ref[...])

    @pl.when(step == 0)
    def _():
        carry_ref[...] = jnp.zeros_like(carry_ref)
        residual(x0_ref, y0_ref, 0, ra_ref)

    _route(ra_ref, g2_ref, wr_ref, br_ref, xe_ref, cnt_ref, 0, tm)
    residual(xa_ref, ya_ref, 2 * step + 1, rb_ref)
    _route(rb_ref, g2_ref, wr_ref, br_ref, xe_ref, cnt_ref, 1, tm)
    residual(xb_ref, yb_ref, 2 * step + 2, ra_ref)


def _route(x1_ref, g2_ref, wr_ref, br_ref, xe_ref, cnt_ref, half, tm):
    rows = slice(half * tm, (half + 1) * tm)
    x1 = x1_ref[...]
    xe_ref[rows, 0:D_MODEL] = x1
    h2 = _rms(x1, g2_ref[...]).astype(BF16)

    logits = _bdot(h2, wr_ref[...]) + br_ref[...]
    lane = lax.broadcasted_iota(jnp.int32, (tm, LANES), 1)
    lanef = lane.astype(F32)
    is_g = (lane >= N_EXPERTS) & (lane < N_EXPERTS + N_GROUPS)
    gl = jnp.where(is_g, logits, NEG_INF)
    gmax = jnp.max(gl, axis=1, keepdims=True)
    gsum = jnp.sum(jnp.exp(gl - gmax), axis=1, keepdims=True)
    g_val = 1.0 / gsum
    g_lane = jnp.min(jnp.where(gl == gmax, lanef, float(LANES)), axis=1, keepdims=True)
    e_lo = (g_lane - float(N_EXPERTS)) * float(EXPERTS_PER_GROUP)
    in_grp = (lanef >= e_lo) & (lanef < e_lo + float(EXPERTS_PER_GROUP))
    el = jnp.where(in_grp, logits, NEG_INF)
    t1 = jnp.max(el, axis=1, keepdims=True)
    i1 = jnp.min(jnp.where(el == t1, lanef, float(LANES)), axis=1, keepdims=True)
    el2 = jnp.where(lanef == i1, NEG_INF, el)
    t2 = jnp.max(el2, axis=1, keepdims=True)
    i2 = jnp.min(jnp.where(el2 == t2, lanef, float(LANES)), axis=1, keepdims=True)
    e2 = jnp.exp(t2 - t1)
    w1 = g_val / (1.0 + e2)
    w2 = g_val * e2 / (1.0 + e2)
    comb = jnp.where(lanef == i1, w1, 0.0) + jnp.where(lanef == i2, w2, 0.0)

    cnt_ref[half] = jnp.sum(jnp.where(is_g & (lanef == g_lane), 1.0, 0.0), axis=0, keepdims=True)
    xe_ref[rows, D_MODEL:XE_WIDTH] = comb + jnp.where(lane == GID_LANE, g_lane - float(N_EXPERTS), 0.0)


def _mix(x, y_att, g, wcg, conv_w, wa, wb, wo, g2, wr, br, *, tm):
    b, s, d = x.shape
    t = b * s
    n_tiles = t // tm
    x2 = x.reshape(t, d)
    y2 = y_att.reshape(t, ATT_WIDTH)
    once = lambda a: pl.BlockSpec(a.shape, lambda i: (0,) * a.ndim, pipeline_mode=pl.Buffered(1))

    def tiles(width):
        nxt = lambda i: (jnp.minimum(2 * i + 2, n_tiles - 1), 0)
        return [pl.BlockSpec((tm, width), lambda i: (0, 0), pipeline_mode=pl.Buffered(1)),
                pl.BlockSpec((tm, width), lambda i: (2 * i + 1, 0)),
                pl.BlockSpec((tm, width), nxt)]

    return pl.pallas_call(
        functools.partial(_mix_kernel, tm=tm, tiles_per_seq=s // tm),
        grid=(n_tiles // 2,),
        in_specs=tiles(d) + tiles(ATT_WIDTH) + [once(g), once(wcg), once(conv_w), once(wa), once(wb),
                                                once(wo), once(g2), once(wr), once(br)],
        out_specs=[pl.BlockSpec((2 * tm, XE_WIDTH), lambda i: (i, 0)),
                   pl.BlockSpec((2, 1, LANES), lambda i: (i, 0, 0))],
        out_shape=[
            jax.ShapeDtypeStruct((t, XE_WIDTH), F32),
            jax.ShapeDtypeStruct((n_tiles, 1, LANES), F32),
        ],
        scratch_shapes=[pltpu.VMEM((tm, d), F32), pltpu.VMEM((tm, d), F32),
                        pltpu.VMEM((8, CONV_WIDTH), F32)],
        compiler_params=pltpu.CompilerParams(
            dimension_semantics=("arbitrary",), vmem_limit_bytes=VMEM_LIMIT),
        name="mix",
    )(x2, x2, x2, y2, y2, y2, g, wcg, conv_w, wa, wb, wo, g2, wr, br)


def _moe_kernel(tcnt_ref, xe_ref, p_ref, g2_ref, wg_ref, wu_ref, wd_ref, g3_ref, wpg_ref, wple_ref,
                earlier_ref, o_ref, hs_ref, rs_ref, ya_ref, yb_ref, *, tm):
    step = pl.program_id(0)
    lane = lax.broadcasted_iota(jnp.int32, (tm, LANES), 1)
    lanef = lane.astype(F32)
    clane = lax.broadcasted_iota(jnp.int32, (MOE_CHUNK, LANES), 1)

    def run_layout(tile):
        n = [tcnt_ref[tile * N_GROUPS + k] for k in range(N_GROUPS)]
        starts = [jnp.int32(0)]
        for k in range(N_GROUPS - 1):
            starts.append(starts[-1] + pl.cdiv(n[k], BF16_ROWS) * BF16_ROWS)
        return n, starts

    def sort_rows(rows, starts, ys_ref):
        x1 = xe_ref[rows, 0:D_MODEL]
        route = xe_ref[rows, D_MODEL:XE_WIDTH]
        h2 = _rms(x1, g2_ref[...]).astype(BF16)

        gid = jnp.sum(jnp.where(lane == GID_LANE, route, 0.0), axis=1, keepdims=True)
        onehot = jnp.where((lanef == gid) & (lane < N_GROUPS), 1.0, 0.0)
        rank = jnp.sum(onehot * _bdot(earlier_ref[...], onehot.astype(BF16)), axis=1, keepdims=True)
        base = starts[N_GROUPS - 1].astype(F32)
        for k in range(N_GROUPS - 2, -1, -1):
            base = jnp.where(gid == float(k), starts[k].astype(F32), base)
        pos = base + rank

        digit_hi = jnp.floor(pos * (1.0 / 32.0))
        digits = jnp.where(lane == 0, digit_hi, jnp.where(lane == 1, pos - 32.0 * digit_hi, 0.0))
        sel_lane = lax.broadcasted_iota(jnp.int32, (BF16_ROWS, LANES), 1)
        sel = jnp.where(sel_lane == 0, 32.0, jnp.where(sel_lane == 1, 1.0, 0.0)).astype(BF16)
        pos_row = _bdot_nt(sel, digits.astype(BF16))[0:1, :]
        sort = (lax.broadcasted_iota(jnp.int32, (MOE_SORTED, tm), 0).astype(F32) == pos_row).astype(BF16)

        hs_ref[0:MOE_SORTED] = _bdot(sort, h2).astype(BF16)
        r_hi, r_mid, r_lo = _bf16_pieces(route)
        packed = (r_hi.astype(F32) + pltpu.roll(r_mid.astype(F32), ROUTE_PITCH, axis=1)
                  + pltpu.roll(r_lo.astype(F32), 2 * ROUTE_PITCH, axis=1)).astype(BF16)
        rsorted = _bdot(sort, packed)
        rs_ref[0:MOE_SORTED] = (rsorted + pltpu.roll(rsorted, LANES - ROUTE_PITCH, axis=1)
                                + pltpu.roll(rsorted, LANES - 2 * ROUTE_PITCH, axis=1))
        hs_ref[MOE_SORTED:MOE_ROWS] = jnp.zeros((MOE_CHUNK, D_MODEL), BF16)
        rs_ref[MOE_SORTED:MOE_ROWS] = jnp.zeros((MOE_CHUNK, LANES), F32)
        ys_ref[...] = jnp.zeros_like(ys_ref)
        return pos

    def chunk(g, c, starts, ys_ref):
        r0 = pl.multiple_of(starts[g] + c * MOE_CHUNK, BF16_ROWS)
        hrows = hs_ref[pl.ds(r0, MOE_CHUNK), :]
        rt = rs_ref[pl.ds(r0, MOE_CHUNK), :]
        parts = []
        for e in range(EXPERTS_PER_GROUP):
            ex = g * EXPERTS_PER_GROUP + e
            a = _bdot(hrows, wg_ref[ex])
            u = _bdot(hrows, wu_ref[ex])
            ce = jnp.sum(jnp.where(clane == ex, rt, 0.0), axis=1, keepdims=True)
            parts.append(((a * jax.nn.sigmoid(a)) * u * ce).astype(BF16))
        out = _bdot(jnp.concatenate(parts, axis=1), wd_ref[g])
        ys_ref[pl.ds(r0, MOE_CHUNK), :] = (ys_ref[pl.ds(r0, MOE_CHUNK), :].astype(F32) + out).astype(BF16)

    def experts(n, starts, ys_ref):
        for g in range(N_GROUPS):
            def more(c, _, g=g):
                chunk(g, c, starts, ys_ref)
                return 0
            lax.fori_loop(1, pl.cdiv(n[g], MOE_CHUNK), more, 0)
        for g in range(N_GROUPS):
            chunk(g, 0, starts, ys_ref)

    def finish(rows, pos, ys_ref):
        unsort = (pos == lax.broadcasted_iota(jnp.int32, (tm, MOE_SORTED), 1).astype(F32)).astype(BF16)
        x2 = xe_ref[rows, 0:D_MODEL] + _bdot(unsort, ys_ref[0:MOE_SORTED])
        h3 = _rms(x2, g3_ref[...]).astype(BF16)
        gate = jax.nn.sigmoid(_bdot(h3, wpg_ref[...]))
        emb = _bdot(p_ref[rows, :].astype(BF16), wple_ref[...])
        o_ref[rows, :] = x2 + gate * emb

    rows_a, rows_b = slice(0, tm), slice(tm, 2 * tm)
    n_a, starts_a = run_layout(2 * step)
    n_b, starts_b = run_layout(2 * step + 1)
    pos_a = sort_rows(rows_a, starts_a, ya_ref)
    experts(n_a, starts_a, ya_ref)
    pos_b = sort_rows(rows_b, starts_b, yb_ref)
    finish(rows_a, pos_a, ya_ref)
    experts(n_b, starts_b, yb_ref)
    finish(rows_b, pos_b, yb_ref)


def _moe(tcnt, xe, p, g2, wg, wu, wd, g3, wpg, wple, *, tm):
    t = xe.shape[0]
    d = D_MODEL
    once = lambda a: pl.BlockSpec(a.shape, lambda i, c: (0,) * a.ndim, pipeline_mode=pl.Buffered(1))
    row = lambda width: pl.BlockSpec((2 * tm, width), lambda i, c: (i, 0))
    earlier = jnp.asarray(np.tril(np.ones((tm, tm), np.float32), -1), BF16)
    return pl.pallas_call(
        functools.partial(_moe_kernel, tm=tm),
        grid_spec=pltpu.PrefetchScalarGridSpec(
            num_scalar_prefetch=1,
            grid=(t // (2 * tm),),
            in_specs=[row(XE_WIDTH), row(PLE_DIM), once(g2), once(wg), once(wu), once(wd), once(g3),
                      once(wpg), once(wple), once(earlier)],
            out_specs=row(d),
            scratch_shapes=[pltpu.VMEM((MOE_ROWS, d), BF16), pltpu.VMEM((MOE_ROWS, LANES), F32),
                            pltpu.VMEM((MOE_ROWS, d), BF16), pltpu.VMEM((MOE_ROWS, d), BF16)],
        ),
        out_shape=jax.ShapeDtypeStruct((t, d), F32),
        compiler_params=pltpu.CompilerParams(
            dimension_semantics=("arbitrary",), vmem_limit_bytes=VMEM_LIMIT),
        name="moe",
    )(tcnt, xe, p, g2, wg, wu, wd, g3, wpg, wple, earlier)


def _k_select_matrix():
    pk = np.zeros((LANES, N_HEADS * LANES), np.float32)
    for idx in range(C_PIECES):
        for h in range(N_HEADS):
            pk[idx * N_HEADS + h, h * LANES + KC_LANE + idx] = -1.0
    return jnp.asarray(pk, BF16)


def kernel(x, p, attn_norm_g, w_in, b_f, q_norm_g, k_norm_g, conv_w, w_out_att, w_out_conv, w_o,
           ffn_norm_g, w_rg, b_rg, w_re, b_re, w_gate, w_up, w_down, ple_norm_g, w_pg, w_ple):
    b, s, d = x.shape
    t = b * s
    aw = ATT_WIDTH
    for i in range(w_in.shape[0]):
        wi = w_in[i]
        wf = wi[:, 3 * aw:3 * aw + N_HEADS]
        w_qkvf = jnp.concatenate(
            [wi[:, :3 * aw], wf, wf, wf, jnp.zeros((d, LANES - C_PIECES * N_HEADS), F32)],
            axis=1).astype(BF16)
        w_cg = wi[:, 3 * aw + N_HEADS:].astype(BF16)
        bf3 = jnp.concatenate([b_f[i]] * C_PIECES + [jnp.zeros((LANES - C_PIECES * N_HEADS,), F32)])[None, :]
        scale = HEAD_DIM ** -0.5 * LOG2E
        qg = jnp.tile(q_norm_g[i] * scale, 2)[None, :]
        kg = jnp.tile(k_norm_g[i], 2)[None, :]
        qt_aug, k_aug, vt = _qkv_proj(x, attn_norm_g[i][None, :], w_qkvf, bf3, qg, kg, _k_select_matrix(),
                                      tm=TM_QKV)
        y_att = _attention(qt_aug, k_aug, vt, tq=TQ_ATT, tk=TK_ATT)

        w_r = jnp.concatenate(
            [w_re[i], w_rg[i], jnp.zeros((d, LANES - N_EXPERTS - N_GROUPS), F32)], axis=1).astype(BF16)
        b_r = jnp.concatenate(
            [b_re[i], b_rg[i], jnp.zeros((LANES - N_EXPERTS - N_GROUPS,), F32)])[None, :]
        g_ffn = ffn_norm_g[i][None, :]
        xe, counts = _mix(x, y_att, attn_norm_g[i][None, :], w_cg, conv_w[i],
                          w_out_att[i].astype(BF16), w_out_conv[i].astype(BF16),
                          w_o[i].astype(BF16), g_ffn, w_r, b_r, tm=TM_MIX)
        tcnt = counts[:, 0, N_EXPERTS:N_EXPERTS + N_GROUPS].astype(jnp.int32).reshape(-1)

        w_dn = w_down[i].reshape(N_GROUPS, GROUP_WIDTH, d).astype(BF16)
        x = _moe(tcnt, xe.reshape(t, XE_WIDTH), p[i].reshape(t, PLE_DIM), g_ffn,
                 w_gate[i].astype(BF16), w_up[i].astype(BF16), w_dn,
                 ple_norm_g[i][None, :], w_pg[i].astype(BF16), w_ple[i].astype(BF16),
                 tm=TM_MOE).reshape(b, s, d)
    return x
```

```python
import functools

import numpy as np
import jax
import jax.numpy as jnp
from jax import lax
from jax.experimental import pallas as pl
from jax.experimental.pallas import tpu as pltpu

D_MODEL = 1024
N_HEADS = 8
HEAD_DIM = 64
ATT_WIDTH = N_HEADS * HEAD_DIM
CONV_WIDTH = 512
CONV_K = 3
N_GROUPS = 4
EXPERTS_PER_GROUP = 4
N_EXPERTS = 16
D_EXPERT = 256
GROUP_WIDTH = EXPERTS_PER_GROUP * D_EXPERT
PLE_DIM = 256
EPS = 1e-6
NEG_INF = -1e30
LOG2E = 1.4426950408889634

LANES = 128
BF16_ROWS = 16
C_PIECES = 3
QC_LANE = HEAD_DIM
KC_LANE = HEAD_DIM + C_PIECES
V_ROWS = HEAD_DIM + BF16_ROWS
QKV_WIDTH = 3 * ATT_WIDTH + LANES
GID_LANE = N_EXPERTS
ROUTE_PITCH = 32
XE_WIDTH = D_MODEL + LANES
VMEM_LIMIT = 56 * 1024 * 1024

TM_QKV = 512
TQ_ATT = 1024
TK_ATT = 256
ATT_HEADS = 2
TM_MIX = 512
TM_MOE = TM_MIX
MOE_SORTED = -(-(TM_MOE + N_GROUPS * BF16_ROWS) // LANES) * LANES
MOE_CHUNK = 144
MOE_ROWS = MOE_SORTED + MOE_CHUNK

F32 = jnp.float32
BF16 = jnp.bfloat16


def _rms(xf, g):
    return xf * lax.rsqrt(jnp.mean(xf * xf, axis=-1, keepdims=True) + EPS) * g


def _log_sigmoid(z):
    return jnp.minimum(z, 0.0) - jnp.log1p(jnp.exp(-jnp.abs(z)))


def _bdot(a, b):
    return jnp.dot(a, b, preferred_element_type=F32)


def _bdot_nt(a, b):
    return lax.dot_general(a, b, (((1,), (1,)), ((), ())), preferred_element_type=F32)


def _bf16_pieces(x):
    hi = x.astype(BF16)
    r1 = x - hi.astype(F32)
    mid = r1.astype(BF16)
    lo = (r1 - mid.astype(F32)).astype(BF16)
    return hi, mid, lo


def _qkv_kernel(x0_ref, xa_ref, xb_ref, g_ref, w_ref, bf_ref, qg_ref, kg_ref, pk_ref,
                qt_ref, k_ref, vt_ref, pa_ref, pb_ref, carry_ref, *, tm, tiles_per_seq):
    step = pl.program_id(0)

    def project(x_ref, dst_ref):
        dst_ref[...] = _bdot(_rms(x_ref[...], g_ref[...]).astype(BF16), w_ref[...])

    @pl.when(step == 0)
    def _():
        carry_ref[...] = jnp.zeros_like(carry_ref)
        project(x0_ref, pa_ref)

    lane = lax.broadcasted_iota(jnp.int32, (tm, LANES), 1)
    row = lax.broadcasted_iota(jnp.int32, (tm, LANES), 0)
    k_ones = jnp.where((lane >= QC_LANE) & (lane < QC_LANE + C_PIECES), 1.0, 0.0)
    low = lane < HEAD_DIM
    ext_row = lax.broadcasted_iota(jnp.int32, (V_ROWS - HEAD_DIM, tm), 0)
    v_ext = jnp.where(ext_row == 0, 1.0, 0.0)

    def finish(p_ref, tile, half):
        rows = slice(half * tm, (half + 1) * tm)
        for j in range(N_HEADS // 2):
            pair_t = p_ref[:, 2 * ATT_WIDTH + LANES * j: 2 * ATT_WIDTH + LANES * (j + 1)].T
            for hh in range(2):
                vt_ref[0, 2 * j + hh, :, rows] = jnp.concatenate(
                    [pair_t[HEAD_DIM * hh: HEAD_DIM * (hh + 1)], v_ext], axis=0).astype(BF16)

        f3 = p_ref[:, 3 * ATT_WIDTH:QKV_WIDTH] + bf_ref[...]
        c = jnp.where(lane < C_PIECES * N_HEADS, _log_sigmoid(f3), 0.0)
        sh = 1
        while sh < tm:
            c = c + jnp.where(row >= sh, pltpu.roll(c, sh, axis=0), 0.0)
            sh *= 2
        c = c + jnp.where(tile % tiles_per_seq == 0, 0.0, carry_ref[...])
        carry_ref[...] = c[tm - 1:tm, :]
        hi, mid, lo = _bf16_pieces(c * LOG2E)
        piece = jnp.where(lane < N_HEADS, hi, jnp.where(lane < 2 * N_HEADS, mid, lo))

        def rms_scales(pair):
            sq = pair * pair
            ss_lo = jnp.sum(jnp.where(low, sq, 0.0), axis=1, keepdims=True)
            ss_hi = jnp.sum(jnp.where(low, 0.0, sq), axis=1, keepdims=True)
            return (lax.rsqrt(ss_lo * (1.0 / HEAD_DIM) + EPS), lax.rsqrt(ss_hi * (1.0 / HEAD_DIM) + EPS))

        piece_t = piece.astype(F32).T
        erow = lax.broadcasted_iota(jnp.int32, (8, tm), 0)
        pad_t = jnp.zeros((LANES - HEAD_DIM - 8, tm), F32)
        qgain = qg_ref[...]
        for j in range(N_HEADS // 2):
            pair = p_ref[:, LANES * j: LANES * (j + 1)]
            r_lo, r_hi = rms_scales(pair)
            qn_t = (pair * jnp.where(low, r_lo, r_hi) * qgain).T
            for hh in range(2):
                h = 2 * j + hh
                c_t = jnp.where(erow >= C_PIECES, jnp.where(erow < 2 * C_PIECES, 1.0, 0.0), 0.0)
                for idx in range(C_PIECES):
                    c_t = jnp.where(erow == idx, piece_t[idx * N_HEADS + h: idx * N_HEADS + h + 1], c_t)
                qt_ref[0, h, :, rows] = jnp.concatenate(
                    [qn_t[HEAD_DIM * hh: HEAD_DIM * (hh + 1)], c_t, pad_t], axis=0).astype(BF16)

        kc = _bdot(piece, pk_ref[...])
        kgain = kg_ref[...]
        for j in range(N_HEADS // 2):
            pair = p_ref[:, ATT_WIDTH + LANES * j: ATT_WIDTH + LANES * (j + 1)]
            r_lo, r_hi = rms_scales(pair)
            n_lo = pair * r_lo * kgain
            n_hi = pltpu.roll(pair, HEAD_DIM, axis=1) * r_hi * kgain
            for h, nrm in ((2 * j, n_lo), (2 * j + 1, n_hi)):
                aug = jnp.where(low, nrm, 0.0) + kc[:, LANES * h: LANES * (h + 1)] + k_ones
                k_ref[0, h, rows, :] = aug.astype(BF16)

    finish(pa_ref, 2 * step, 0)
    project(xa_ref, pb_ref)
    finish(pb_ref, 2 * step + 1, 1)
    project(xb_ref, pa_ref)


def _qkv_proj(x, g, w, bf3, qg, kg, pk, *, tm):
    b, s, d = x.shape
    x2 = x.reshape(b * s, d)
    n_tiles = b * s // tm
    per_seq = s // (2 * tm)
    full = lambda shape: pl.BlockSpec(shape, lambda i: (0,) * len(shape))
    return pl.pallas_call(
        functools.partial(_qkv_kernel, tm=tm, tiles_per_seq=s // tm),
        grid=(n_tiles // 2,),
        in_specs=[
            pl.BlockSpec((tm, d), lambda i: (0, 0)),
            pl.BlockSpec((tm, d), lambda i: (2 * i + 1, 0)),
            pl.BlockSpec((tm, d), lambda i: (jnp.minimum(2 * i + 2, n_tiles - 1), 0)),
            full(g.shape), full(w.shape), full(bf3.shape), full(qg.shape), full(kg.shape),
            full(pk.shape),
        ],
        out_specs=[pl.BlockSpec((1, N_HEADS, LANES, 2 * tm), lambda i: (i // per_seq, 0, 0, i % per_seq)),
                   pl.BlockSpec((1, N_HEADS, 2 * tm, LANES), lambda i: (i // per_seq, 0, i % per_seq, 0)),
                   pl.BlockSpec((1, N_HEADS, V_ROWS, 2 * tm), lambda i: (i // per_seq, 0, 0, i % per_seq))],
        out_shape=[
            jax.ShapeDtypeStruct((b, N_HEADS, LANES, s), BF16),
            jax.ShapeDtypeStruct((b, N_HEADS, s, LANES), BF16),
            jax.ShapeDtypeStruct((b, N_HEADS, V_ROWS, s), BF16),
        ],
        scratch_shapes=[pltpu.VMEM((tm, QKV_WIDTH), F32), pltpu.VMEM((tm, QKV_WIDTH), F32),
                        pltpu.VMEM((1, LANES), F32)],
        compiler_params=pltpu.CompilerParams(
            dimension_semantics=("arbitrary",), vmem_limit_bytes=VMEM_LIMIT),
        name="qkv_proj",
    )(x2, x2, x2, g, w, bf3, qg, kg, pk)


def _attn_kernel(qt_ref, k_ref, vt_ref, bias_ref, o_ref, sa_ref, sb_ref, *, tq, tk, n_q):
    n_sub = tq // tk
    bufs = (sa_ref, sb_ref)

    def produce(q, j, s_ref, lo=0):
        start = j * tk
        for hh in range(ATT_HEADS):
            s_ref[hh, :, lo:tq] = _bdot(k_ref[0, hh, pl.ds(start, tk), :],
                                        qt_ref[0, hh, :, q * tq + lo:(q + 1) * tq])

    def consume(j, s_ref, carry, lo=0, masked=False):
        start = j * tk
        out = []
        for hh in range(ATT_HEADS):
            m, acc = carry[hh]

            def read():
                if not masked:
                    return s_ref[hh, :, lo:tq]
                tri = s_ref[hh, :, lo:lo + tk] + bias_ref[...]
                return tri if lo + tk == tq else jnp.concatenate([tri, s_ref[hh, :, lo + tk:tq]], axis=1)

            m_new = jnp.maximum(m[:, lo:tq], jnp.max(read(), axis=0, keepdims=True))
            alpha = jnp.exp2(m[:, lo:tq] - m_new)
            p = jnp.exp2(read() - m_new).astype(BF16)
            acc_new = alpha * acc[:, lo:tq] + _bdot(vt_ref[0, hh, :, pl.ds(start, tk)], p)
            if lo:
                m_new = jnp.concatenate([m[:, 0:lo], m_new], axis=1)
                acc_new = jnp.concatenate([acc[:, 0:lo], acc_new], axis=1)
            out.append((m_new, acc_new))
        return tuple(out)

    def trip(q, r, carry):
        for t in range(n_sub):
            j = r * n_sub + t
            produce(q, j + 1, bufs[(t + 1) % 2])
            carry = consume(j, bufs[t % 2], carry)
        return carry

    def query_tile(q):
        carry = tuple((jnp.full((1, tq), NEG_INF, F32), jnp.zeros((V_ROWS, tq), F32))
                      for _ in range(ATT_HEADS))
        for r in range(q):
            carry = trip(q, r, carry)
        for t in range(n_sub):
            j = q * n_sub + t
            if t + 1 < n_sub:
                produce(q, j + 1, bufs[(t + 1) % 2], lo=(t + 1) * tk)
            elif q + 1 < n_q:
                produce(q + 1, 0, bufs[(t + 1) % 2])
            carry = consume(j, bufs[t % 2], carry, lo=t * tk, masked=True)
        y_t = jnp.concatenate([acc[0:HEAD_DIM] / acc[HEAD_DIM:HEAD_DIM + 1] for _, acc in carry], axis=0)
        o_ref[0, q * tq:(q + 1) * tq, :] = y_t.T.astype(BF16)

    produce(0, 0, sa_ref)
    for q in range(n_q):
        query_tile(q)


def _attention(qt_aug, k_aug, vt, *, tq, tk):
    b, nh, s, _ = k_aug.shape
    assert tq % (2 * tk) == 0 and s % tq == 0
    future = np.arange(tk)[:, None] > np.arange(tk)[None, :]
    bias = jnp.asarray(np.where(future, np.float32(NEG_INF), np.float32(0.0)))
    nh_step = ATT_HEADS
    grid = (b, nh // nh_step)
    return pl.pallas_call(
        functools.partial(_attn_kernel, tq=tq, tk=tk, n_q=s // tq),
        grid=grid,
        in_specs=[
            pl.BlockSpec((1, nh_step, LANES, s), lambda i, h: (i, h, 0, 0)),
            pl.BlockSpec((1, nh_step, s, LANES), lambda i, h: (i, h, 0, 0)),
            pl.BlockSpec((1, nh_step, V_ROWS, s), lambda i, h: (i, h, 0, 0)),
            pl.BlockSpec((tk, tk), lambda i, h: (0, 0)),
        ],
        out_specs=pl.BlockSpec((1, s, nh_step * HEAD_DIM), lambda i, h: (i, 0, h)),
        out_shape=jax.ShapeDtypeStruct((b, s, ATT_WIDTH), BF16),
        scratch_shapes=[pltpu.VMEM((nh_step, tk, tq), F32)] * 2,
        compiler_params=pltpu.CompilerParams(
            dimension_semantics=("arbitrary", "arbitrary"), vmem_limit_bytes=VMEM_LIMIT),
        name="fox_attention",
    )(qt_aug, k_aug, vt, bias)


def _mix_kernel(x0_ref, xa_ref, xb_ref, y0_ref, ya_ref, yb_ref, g_ref, wcg_ref, cw_ref, wa_ref, wb_ref,
                wo_ref, g2_ref, wr_ref, br_ref, xe_ref, cnt_ref, ra_ref, rb_ref, carry_ref,
                *, tm, tiles_per_seq):
    step = pl.program_id(0)

    def residual(x_ref, y_ref, tile, dst_ref):
        x = x_ref[...]
        h = _rms(x, g_ref[...]).astype(BF16)
        pc = _bdot(h, wcg_ref[...])
        cw = CONV_WIDTH
        cb = pc[:, 0:cw]
        prod = pc[:, cw:2 * cw] * pc[:, 2 * cw:3 * cw]
        ga = pc[:, 3 * cw:3 * cw + D_MODEL]
        gb = pc[:, 3 * cw + D_MODEL:3 * cw + 2 * D_MODEL]

        prev = jnp.where(tile % tiles_per_seq == 0, 0.0, carry_ref[...])
        crow = lax.broadcasted_iota(jnp.int32, (tm, cw), 0)
        m1 = jnp.where(crow == 0, prev[7:8, :], pltpu.roll(prod, 1, axis=0))
        m2 = jnp.where(crow == 0, prev[6:7, :],
                       jnp.where(crow == 1, prev[7:8, :], pltpu.roll(prod, 2, axis=0)))
        carry_ref[...] = prod[tm - 8:tm, :]
        w = cw_ref[...]
        y_conv = cb * (w[0:1, :] * m2 + w[1:2, :] * m1 + w[2:3, :] * prod)

        a = _bdot(y_ref[...], wa_ref[...])
        bb = _bdot(y_conv.astype(BF16), wb_ref[...])
        merged = jax.nn.sigmoid(ga) * a + jax.nn.sigmoid(gb) * bb
        dst_ref[...] = x + _bdot(merged.astype(BF16), wo_ref[...])

    @pl.when(step == 0)
    def _():
        carry_ref[...] = jnp.zeros_like(carry_ref)
        residual(x0_ref, y0_ref, 0, ra_ref)

    _route(ra_ref, g2_ref, wr_ref, br_ref, xe_ref, cnt_ref, 0, tm)
    residual(xa_ref, ya_ref, 2 * step + 1, rb_ref)
    _route(rb_ref, g2_ref, wr_ref, br_ref, xe_ref, cnt_ref, 1, tm)
    residual(xb_ref, yb_ref, 2 * step + 2, ra_ref)


def _route(x1_ref, g2_ref, wr_ref, br_ref, xe_ref, cnt_ref, half, tm):
    rows = slice(half * tm, (half + 1) * tm)
    x1 = x1_ref[...]
    xe_ref[rows, 0:D_MODEL] = x1
    h2 = _rms(x1, g2_ref[...]).astype(BF16)

    logits = _bdot(h2, wr_ref[...]) + br_ref[...]
    lane = lax.broadcasted_iota(jnp.int32, (tm, LANES), 1)
    lanef = lane.astype(F32)
    is_g = (lane >= N_EXPERTS) & (lane < N_EXPERTS + N_GROUPS)
    gl = jnp.where(is_g, logits, NEG_INF)
    gmax = jnp.max(gl, axis=1, keepdims=True)
    gsum = jnp.sum(jnp.exp(gl - gmax), axis=1, keepdims=True)
    g_val = 1.0 / gsum
    g_lane = jnp.min(jnp.where(gl == gmax, lanef, float(LANES)), axis=1, keepdims=True)
    e_lo = (g_lane - float(N_EXPERTS)) * float(EXPERTS_PER_GROUP)
    in_grp = (lanef >= e_lo) & (lanef < e_lo + float(EXPERTS_PER_GROUP))
    el = jnp.where(in_grp, logits, NEG_INF)
    t1 = jnp.max(el, axis=1, keepdims=True)
    i1 = jnp.min(jnp.where(el == t1, lanef, float(LANES)), axis=1, keepdims=True)
    el2 = jnp.where(lanef == i1, NEG_INF, el)
    t2 = jnp.max(el2, axis=1, keepdims=True)
    i2 = jnp.min(jnp.where(el2 == t2, lanef, float(LANES)), axis=1, keepdims=True)
    e2 = jnp.exp(t2 - t1)
    w1 = g_val / (1.0 + e2)
    w2 = g_val * e2 / (1.0 + e2)
    comb = jnp.where(lanef == i1, w1, 0.0) + jnp.where(lanef == i2, w2, 0.0)

    cnt_ref[half] = jnp.sum(jnp.where(is_g & (lanef == g_lane), 1.0, 0.0), axis=0, keepdims=True)
    xe_ref[rows, D_MODEL:XE_WIDTH] = comb + jnp.where(lane == GID_LANE, g_lane - float(N_EXPERTS), 0.0)


def _mix(x, y_att, g, wcg, conv_w, wa, wb, wo, g2, wr, br, *, tm):
    b, s, d = x.shape
    t = b * s
    n_tiles = t // tm
    x2 = x.reshape(t, d)
    y2 = y_att.reshape(t, ATT_WIDTH)
    once = lambda a: pl.BlockSpec(a.shape, lambda i: (0,) * a.ndim, pipeline_mode=pl.Buffered(1))

    def tiles(width):
        nxt = lambda i: (jnp.minimum(2 * i + 2, n_tiles - 1), 0)
        return [pl.BlockSpec((tm, width), lambda i: (0, 0), pipeline_mode=pl.Buffered(1)),
                pl.BlockSpec((tm, width), lambda i: (2 * i + 1, 0)),
                pl.BlockSpec((tm, width), nxt)]

    return pl.pallas_call(
        functools.partial(_mix_kernel, tm=tm, tiles_per_seq=s // tm),
        grid=(n_tiles // 2,),
        in_specs=tiles(d) + tiles(ATT_WIDTH) + [once(g), once(wcg), once(conv_w), once(wa), once(wb),
                                                once(wo), once(g2), once(wr), once(br)],
        out_specs=[pl.BlockSpec((2 * tm, XE_WIDTH), lambda i: (i, 0)),
                   pl.BlockSpec((2, 1, LANES), lambda i: (i, 0, 0))],
        out_shape=[
            jax.ShapeDtypeStruct((t, XE_WIDTH), F32),
            jax.ShapeDtypeStruct((n_tiles, 1, LANES), F32),
        ],
        scratch_shapes=[pltpu.VMEM((tm, d), F32), pltpu.VMEM((tm, d), F32),
                        pltpu.VMEM((8, CONV_WIDTH), F32)],
        compiler_params=pltpu.CompilerParams(
            dimension_semantics=("arbitrary",), vmem_limit_bytes=VMEM_LIMIT),
        name="mix",
    )(x2, x2, x2, y2, y2, y2, g, wcg, conv_w, wa, wb, wo, g2, wr, br)


def _moe_kernel(tcnt_ref, xe_ref, p_ref, g2_ref, wg_ref, wu_ref, wd_ref, g3_ref, wpg_ref, wple_ref,
                earlier_ref, o_ref, hs_ref, rs_ref, ys_ref, *, tm):
    i = pl.program_id(0)
    n = [tcnt_ref[i * N_GROUPS + k] for k in range(N_GROUPS)]
    starts = [jnp.int32(0)]
    for k in range(N_GROUPS - 1):
        starts.append(starts[-1] + pl.cdiv(n[k], BF16_ROWS) * BF16_ROWS)

    x1 = xe_ref[:, 0:D_MODEL]
    route = xe_ref[:, D_MODEL:XE_WIDTH]
    h2 = _rms(x1, g2_ref[...]).astype(BF16)

    lane = lax.broadcasted_iota(jnp.int32, (tm, LANES), 1)
    lanef = lane.astype(F32)
    gid = jnp.sum(jnp.where(lane == GID_LANE, route, 0.0), axis=1, keepdims=True)
    onehot = jnp.where((lanef == gid) & (lane < N_GROUPS), 1.0, 0.0)
    rank = jnp.sum(onehot * _bdot(earlier_ref[...], onehot.astype(BF16)), axis=1, keepdims=True)
    base = starts[N_GROUPS - 1].astype(F32)
    for k in range(N_GROUPS - 2, -1, -1):
        base = jnp.where(gid == float(k), starts[k].astype(F32), base)
    pos = base + rank

    unsort = (pos == lax.broadcasted_iota(jnp.int32, (tm, MOE_SORTED), 1).astype(F32)).astype(BF16)
    digit_hi = jnp.floor(pos * (1.0 / 32.0))
    digits = jnp.where(lane == 0, digit_hi, jnp.where(lane == 1, pos - 32.0 * digit_hi, 0.0))
    sel_lane = lax.broadcasted_iota(jnp.int32, (BF16_ROWS, LANES), 1)
    sel = jnp.where(sel_lane == 0, 32.0, jnp.where(sel_lane == 1, 1.0, 0.0)).astype(BF16)
    pos_row = _bdot_nt(sel, digits.astype(BF16))[0:1, :]
    sort = (lax.broadcasted_iota(jnp.int32, (MOE_SORTED, tm), 0).astype(F32) == pos_row).astype(BF16)

    hs_ref[0:MOE_SORTED] = _bdot(sort, h2).astype(BF16)
    r_hi, r_mid, r_lo = _bf16_pieces(route)
    packed = (r_hi.astype(F32) + pltpu.roll(r_mid.astype(F32), ROUTE_PITCH, axis=1)
              + pltpu.roll(r_lo.astype(F32), 2 * ROUTE_PITCH, axis=1)).astype(BF16)
    rsorted = _bdot(sort, packed)
    rs_ref[0:MOE_SORTED] = (rsorted + pltpu.roll(rsorted, LANES - ROUTE_PITCH, axis=1)
                            + pltpu.roll(rsorted, LANES - 2 * ROUTE_PITCH, axis=1))
    hs_ref[MOE_SORTED:MOE_ROWS] = jnp.zeros((MOE_CHUNK, D_MODEL), BF16)
    rs_ref[MOE_SORTED:MOE_ROWS] = jnp.zeros((MOE_CHUNK, LANES), F32)
    ys_ref[...] = jnp.zeros_like(ys_ref)

    clane = lax.broadcasted_iota(jnp.int32, (MOE_CHUNK, LANES), 1)

    def chunk(g, c):
        r0 = pl.multiple_of(starts[g] + c * MOE_CHUNK, BF16_ROWS)
        hrows = hs_ref[pl.ds(r0, MOE_CHUNK), :]
        rt = rs_ref[pl.ds(r0, MOE_CHUNK), :]
        parts = []
        for e in range(EXPERTS_PER_GROUP):
            ex = g * EXPERTS_PER_GROUP + e
            a = _bdot(hrows, wg_ref[ex])
            u = _bdot(hrows, wu_ref[ex])
            ce = jnp.sum(jnp.where(clane == ex, rt, 0.0), axis=1, keepdims=True)
            parts.append(((a * jax.nn.sigmoid(a)) * u * ce).astype(BF16))
        out = _bdot(jnp.concatenate(parts, axis=1), wd_ref[g])
        ys_ref[pl.ds(r0, MOE_CHUNK), :] = (ys_ref[pl.ds(r0, MOE_CHUNK), :].astype(F32) + out).astype(BF16)

    for g in range(N_GROUPS):
        def more(c, _, g=g):
            chunk(g, c)
            return 0
        lax.fori_loop(1, pl.cdiv(n[g], MOE_CHUNK), more, 0)
    for g in range(N_GROUPS):
        chunk(g, 0)

    x2 = x1 + _bdot(unsort, ys_ref[0:MOE_SORTED])
    h3 = _rms(x2, g3_ref[...]).astype(BF16)
    gate = jax.nn.sigmoid(_bdot(h3, wpg_ref[...]))
    emb = _bdot(p_ref[...].astype(BF16), wple_ref[...])
    o_ref[...] = x2 + gate * emb


def _moe(tcnt, xe, p, g2, wg, wu, wd, g3, wpg, wple, *, tm):
    t = xe.shape[0]
    d = D_MODEL
    once = lambda a: pl.BlockSpec(a.shape, lambda i, c: (0,) * a.ndim, pipeline_mode=pl.Buffered(1))
    row = lambda width: pl.BlockSpec((tm, width), lambda i, c: (i, 0))
    earlier = jnp.asarray(np.tril(np.ones((tm, tm), np.float32), -1), BF16)
    return pl.pallas_call(
        functools.partial(_moe_kernel, tm=tm),
        grid_spec=pltpu.PrefetchScalarGridSpec(
            num_scalar_prefetch=1,
            grid=(t // tm,),
            in_specs=[row(XE_WIDTH), row(PLE_DIM), once(g2), once(wg), once(wu), once(wd), once(g3),
                      once(wpg), once(wple), once(earlier)],
            out_specs=row(d),
            scratch_shapes=[pltpu.VMEM((MOE_ROWS, d), BF16), pltpu.VMEM((MOE_ROWS, LANES), F32),
                            pltpu.VMEM((MOE_ROWS, d), BF16)],
        ),
        out_shape=jax.ShapeDtypeStruct((t, d), F32),
        compiler_params=pltpu.CompilerParams(
            dimension_semantics=("arbitrary",), vmem_limit_bytes=VMEM_LIMIT),
        name="moe",
    )(tcnt, xe, p, g2, wg, wu, wd, g3, wpg, wple, earlier)


def _k_select_matrix():
    pk = np.zeros((LANES, N_HEADS * LANES), np.float32)
    for idx in range(C_PIECES):
        for h in range(N_HEADS):
            pk[idx * N_HEADS + h, h * LANES + KC_LANE + idx] = -1.0
    return jnp.asarray(pk, BF16)


def kernel(x, p, attn_norm_g, w_in, b_f, q_norm_g, k_norm_g, conv_w, w_out_att, w_out_conv, w_o,
           ffn_norm_g, w_rg, b_rg, w_re, b_re, w_gate, w_up, w_down, ple_norm_g, w_pg, w_ple):
    b, s, d = x.shape
    t = b * s
    aw = ATT_WIDTH
    for i in range(w_in.shape[0]):
        wi = w_in[i]
        wf = wi[:, 3 * aw:3 * aw + N_HEADS]
        w_qkvf = jnp.concatenate(
            [wi[:, :3 * aw], wf, wf, wf, jnp.zeros((d, LANES - C_PIECES * N_HEADS), F32)],
            axis=1).astype(BF16)
        w_cg = wi[:, 3 * aw + N_HEADS:].astype(BF16)
        bf3 = jnp.concatenate([b_f[i]] * C_PIECES + [jnp.zeros((LANES - C_PIECES * N_HEADS,), F32)])[None, :]
        scale = HEAD_DIM ** -0.5 * LOG2E
        qg = jnp.tile(q_norm_g[i] * scale, 2)[None, :]
        kg = jnp.tile(k_norm_g[i], 2)[None, :]
        qt_aug, k_aug, vt = _qkv_proj(x, attn_norm_g[i][None, :], w_qkvf, bf3, qg, kg, _k_select_matrix(),
                                      tm=TM_QKV)
        y_att = _attention(qt_aug, k_aug, vt, tq=TQ_ATT, tk=TK_ATT)

        w_r = jnp.concatenate(
            [w_re[i], w_rg[i], jnp.zeros((d, LANES - N_EXPERTS - N_GROUPS), F32)], axis=1).astype(BF16)
        b_r = jnp.concatenate(
            [b_re[i], b_rg[i], jnp.zeros((LANES - N_EXPERTS - N_GROUPS,), F32)])[None, :]
        g_ffn = ffn_norm_g[i][None, :]
        xe, counts = _mix(x, y_att, attn_norm_g[i][None, :], w_cg, conv_w[i],
                          w_out_att[i].astype(BF16), w_out_conv[i].astype(BF16),
                          w_o[i].astype(BF16), g_ffn, w_r, b_r, tm=TM_MIX)
        tcnt = counts[:, 0, N_EXPERTS:N_EXPERTS + N_GROUPS].astype(jnp.int32).reshape(-1)

        w_dn = w_down[i].reshape(N_GROUPS, GROUP_WIDTH, d).astype(BF16)
        x = _moe(tcnt, xe.reshape(t, XE_WIDTH), p[i].reshape(t, PLE_DIM), g_ffn,
                 w_gate[i].astype(BF16), w_up[i].astype(BF16), w_dn,
                 ple_norm_g[i][None, :], w_pg[i].astype(BF16), w_ple[i].astype(BF16),
                 tm=TM_MOE).reshape(b, s, d)
    return x
```

```python
import functools

import numpy as np
import jax
import jax.numpy as jnp
from jax import lax
from jax.experimental import pallas as pl
from jax.experimental.pallas import tpu as pltpu

D_MODEL = 1024
N_HEADS = 8
HEAD_DIM = 64
ATT_WIDTH = N_HEADS * HEAD_DIM
CONV_WIDTH = 512
CONV_K = 3
N_GROUPS = 4
EXPERTS_PER_GROUP = 4
N_EXPERTS = 16
D_EXPERT = 256
GROUP_WIDTH = EXPERTS_PER_GROUP * D_EXPERT
PLE_DIM = 256
EPS = 1e-6
NEG_INF = -1e30
LOG2E = 1.4426950408889634

LANES = 128
BF16_ROWS = 16
C_PIECES = 3
QC_LANE = HEAD_DIM
KC_LANE = HEAD_DIM + C_PIECES
V_ROWS = HEAD_DIM + BF16_ROWS
QKV_WIDTH = 3 * ATT_WIDTH + LANES
GID_LANE = N_EXPERTS
ROUTE_PITCH = 32
XE_WIDTH = D_MODEL + LANES
VMEM_LIMIT = 56 * 1024 * 1024

TM_QKV = 512
TQ_ATT = 512
TK_ATT = 256
ATT_HEADS = 2
TM_MIX = 512
TM_MOE = TM_MIX
MOE_SORTED = -(-(TM_MOE + N_GROUPS * BF16_ROWS) // LANES) * LANES
MOE_CHUNK = 144
MOE_ROWS = MOE_SORTED + MOE_CHUNK

F32 = jnp.float32
BF16 = jnp.bfloat16


def _rms(xf, g):
    return xf * lax.rsqrt(jnp.mean(xf * xf, axis=-1, keepdims=True) + EPS) * g


def _log_sigmoid(z):
    return jnp.minimum(z, 0.0) - jnp.log1p(jnp.exp(-jnp.abs(z)))


def _bdot(a, b):
    return jnp.dot(a, b, preferred_element_type=F32)


def _bdot_nt(a, b):
    return lax.dot_general(a, b, (((1,), (1,)), ((), ())), preferred_element_type=F32)


def _bf16_pieces(x):
    hi = x.astype(BF16)
    r1 = x - hi.astype(F32)
    mid = r1.astype(BF16)
    lo = (r1 - mid.astype(F32)).astype(BF16)
    return hi, mid, lo


def _qkv_kernel(x0_ref, xa_ref, xb_ref, g_ref, w_ref, bf_ref, qg_ref, kg_ref, pk_ref,
                qt_ref, k_ref, vt_ref, pa_ref, pb_ref, carry_ref, *, tm, tiles_per_seq):
    step = pl.program_id(0)

    def project(x_ref, dst_ref):
        dst_ref[...] = _bdot(_rms(x_ref[...], g_ref[...]).astype(BF16), w_ref[...])

    @pl.when(step == 0)
    def _():
        carry_ref[...] = jnp.zeros_like(carry_ref)
        project(x0_ref, pa_ref)

    lane = lax.broadcasted_iota(jnp.int32, (tm, LANES), 1)
    row = lax.broadcasted_iota(jnp.int32, (tm, LANES), 0)
    k_ones = jnp.where((lane >= QC_LANE) & (lane < QC_LANE + C_PIECES), 1.0, 0.0)
    low = lane < HEAD_DIM
    ext_row = lax.broadcasted_iota(jnp.int32, (V_ROWS - HEAD_DIM, tm), 0)
    v_ext = jnp.where(ext_row == 0, 1.0, 0.0)

    def finish(p_ref, tile, half):
        rows = slice(half * tm, (half + 1) * tm)
        for j in range(N_HEADS // 2):
            pair_t = p_ref[:, 2 * ATT_WIDTH + LANES * j: 2 * ATT_WIDTH + LANES * (j + 1)].T
            for hh in range(2):
                vt_ref[0, 2 * j + hh, :, rows] = jnp.concatenate(
                    [pair_t[HEAD_DIM * hh: HEAD_DIM * (hh + 1)], v_ext], axis=0).astype(BF16)

        f3 = p_ref[:, 3 * ATT_WIDTH:QKV_WIDTH] + bf_ref[...]
        c = jnp.where(lane < C_PIECES * N_HEADS, _log_sigmoid(f3), 0.0)
        sh = 1
        while sh < tm:
            c = c + jnp.where(row >= sh, pltpu.roll(c, sh, axis=0), 0.0)
            sh *= 2
        c = c + jnp.where(tile % tiles_per_seq == 0, 0.0, carry_ref[...])
        carry_ref[...] = c[tm - 1:tm, :]
        hi, mid, lo = _bf16_pieces(c * LOG2E)
        piece = jnp.where(lane < N_HEADS, hi, jnp.where(lane < 2 * N_HEADS, mid, lo))

        def rms_scales(pair):
            sq = pair * pair
            ss_lo = jnp.sum(jnp.where(low, sq, 0.0), axis=1, keepdims=True)
            ss_hi = jnp.sum(jnp.where(low, 0.0, sq), axis=1, keepdims=True)
            return (lax.rsqrt(ss_lo * (1.0 / HEAD_DIM) + EPS), lax.rsqrt(ss_hi * (1.0 / HEAD_DIM) + EPS))

        piece_t = piece.astype(F32).T
        erow = lax.broadcasted_iota(jnp.int32, (8, tm), 0)
        pad_t = jnp.zeros((LANES - HEAD_DIM - 8, tm), F32)
        qgain = qg_ref[...]
        for j in range(N_HEADS // 2):
            pair = p_ref[:, LANES * j: LANES * (j + 1)]
            r_lo, r_hi = rms_scales(pair)
            qn_t = (pair * jnp.where(low, r_lo, r_hi) * qgain).T
            for hh in range(2):
                h = 2 * j + hh
                c_t = jnp.where(erow >= C_PIECES, jnp.where(erow < 2 * C_PIECES, 1.0, 0.0), 0.0)
                for idx in range(C_PIECES):
                    c_t = jnp.where(erow == idx, piece_t[idx * N_HEADS + h: idx * N_HEADS + h + 1], c_t)
                qt_ref[0, h, :, rows] = jnp.concatenate(
                    [qn_t[HEAD_DIM * hh: HEAD_DIM * (hh + 1)], c_t, pad_t], axis=0).astype(BF16)

        kc = _bdot(piece, pk_ref[...])
        kgain = kg_ref[...]
        for j in range(N_HEADS // 2):
            pair = p_ref[:, ATT_WIDTH + LANES * j: ATT_WIDTH + LANES * (j + 1)]
            r_lo, r_hi = rms_scales(pair)
            n_lo = pair * r_lo * kgain
            n_hi = pltpu.roll(pair, HEAD_DIM, axis=1) * r_hi * kgain
            for h, nrm in ((2 * j, n_lo), (2 * j + 1, n_hi)):
                aug = jnp.where(low, nrm, 0.0) + kc[:, LANES * h: LANES * (h + 1)] + k_ones
                k_ref[0, h, rows, :] = aug.astype(BF16)

    finish(pa_ref, 2 * step, 0)
    project(xa_ref, pb_ref)
    finish(pb_ref, 2 * step + 1, 1)
    project(xb_ref, pa_ref)


def _qkv_proj(x, g, w, bf3, qg, kg, pk, *, tm):
    b, s, d = x.shape
    x2 = x.reshape(b * s, d)
    n_tiles = b * s // tm
    per_seq = s // (2 * tm)
    full = lambda shape: pl.BlockSpec(shape, lambda i: (0,) * len(shape))
    return pl.pallas_call(
        functools.partial(_qkv_kernel, tm=tm, tiles_per_seq=s // tm),
        grid=(n_tiles // 2,),
        in_specs=[
            pl.BlockSpec((tm, d), lambda i: (0, 0)),
            pl.BlockSpec((tm, d), lambda i: (2 * i + 1, 0)),
            pl.BlockSpec((tm, d), lambda i: (jnp.minimum(2 * i + 2, n_tiles - 1), 0)),
            full(g.shape), full(w.shape), full(bf3.shape), full(qg.shape), full(kg.shape),
            full(pk.shape),
        ],
        out_specs=[pl.BlockSpec((1, N_HEADS, LANES, 2 * tm), lambda i: (i // per_seq, 0, 0, i % per_seq)),
                   pl.BlockSpec((1, N_HEADS, 2 * tm, LANES), lambda i: (i // per_seq, 0, i % per_seq, 0)),
                   pl.BlockSpec((1, N_HEADS, V_ROWS, 2 * tm), lambda i: (i // per_seq, 0, 0, i % per_seq))],
        out_shape=[
            jax.ShapeDtypeStruct((b, N_HEADS, LANES, s), BF16),
            jax.ShapeDtypeStruct((b, N_HEADS, s, LANES), BF16),
            jax.ShapeDtypeStruct((b, N_HEADS, V_ROWS, s), BF16),
        ],
        scratch_shapes=[pltpu.VMEM((tm, QKV_WIDTH), F32), pltpu.VMEM((tm, QKV_WIDTH), F32),
                        pltpu.VMEM((1, LANES), F32)],
        compiler_params=pltpu.CompilerParams(
            dimension_semantics=("arbitrary",), vmem_limit_bytes=VMEM_LIMIT),
        name="qkv_proj",
    )(x2, x2, x2, g, w, bf3, qg, kg, pk)


def _attn_kernel(qt_ref, qn_ref, k_ref, vt_ref, bias_ref, o_ref, sa_ref, sb_ref, *, tq, tk, n_q):
    qi = pl.program_id(2)
    n_sub = tq // tk
    bufs = (sa_ref, sb_ref)

    def produce(j, s_ref, lo=0, q_ref=qt_ref):
        start = j * tk
        for hh in range(ATT_HEADS):
            s_ref[hh, :, lo:tq] = _bdot(k_ref[0, hh, pl.ds(start, tk), :], q_ref[0, hh, :, lo:tq])

    def consume(j, s_ref, carry, lo=0, masked=False):
        start = j * tk
        out = []
        for hh in range(ATT_HEADS):
            m, acc = carry[hh]

            def read():
                if not masked:
                    return s_ref[hh, :, lo:tq]
                tri = s_ref[hh, :, lo:lo + tk] + bias_ref[...]
                return tri if lo + tk == tq else jnp.concatenate([tri, s_ref[hh, :, lo + tk:tq]], axis=1)

            m_new = jnp.maximum(m[:, lo:tq], jnp.max(read(), axis=0, keepdims=True))
            alpha = jnp.exp2(m[:, lo:tq] - m_new)
            p = jnp.exp2(read() - m_new).astype(BF16)
            acc_new = alpha * acc[:, lo:tq] + _bdot(vt_ref[0, hh, :, pl.ds(start, tk)], p)
            if lo:
                m_new = jnp.concatenate([m[:, 0:lo], m_new], axis=1)
                acc_new = jnp.concatenate([acc[:, 0:lo], acc_new], axis=1)
            out.append((m_new, acc_new))
        return tuple(out)

    def trip(r, carry):
        for t in range(n_sub):
            j = r * n_sub + t
            produce(j + 1, bufs[(t + 1) % 2])
            carry = consume(j, bufs[t % 2], carry)
        return carry

    def query_tile(q):
        if q == 0:
            produce(0, sa_ref)
        carry = tuple((jnp.full((1, tq), NEG_INF, F32), jnp.zeros((V_ROWS, tq), F32))
                      for _ in range(ATT_HEADS))
        for r in range(q):
            carry = trip(r, carry)
        for t in range(n_sub):
            j = q * n_sub + t
            if t + 1 < n_sub:
                produce(j + 1, bufs[(t + 1) % 2], lo=(t + 1) * tk)
            elif q + 1 < n_q:
                produce(0, bufs[(t + 1) % 2], q_ref=qn_ref)
            carry = consume(j, bufs[t % 2], carry, lo=t * tk, masked=True)
        y_t = jnp.concatenate([acc[0:HEAD_DIM] / acc[HEAD_DIM:HEAD_DIM + 1] for _, acc in carry], axis=0)
        o_ref[0] = y_t.T.astype(BF16)

    for q in range(n_q):
        pl.when(qi == q)(functools.partial(query_tile, q))


def _attention(qt_aug, k_aug, vt, *, tq, tk):
    b, nh, s, _ = k_aug.shape
    assert tq % (2 * tk) == 0 and s % tq == 0
    future = np.arange(tk)[:, None] > np.arange(tk)[None, :]
    bias = jnp.asarray(np.where(future, np.float32(NEG_INF), np.float32(0.0)))
    nh_step = ATT_HEADS
    grid = (b, nh // nh_step, s // tq)
    return pl.pallas_call(
        functools.partial(_attn_kernel, tq=tq, tk=tk, n_q=s // tq),
        grid=grid,
        in_specs=[
            pl.BlockSpec((1, nh_step, LANES, tq), lambda i, h, j: (i, h, 0, j)),
            pl.BlockSpec((1, nh_step, LANES, tq), lambda i, h, j: (i, h, 0, jnp.minimum(j + 1, s // tq - 1))),
            pl.BlockSpec((1, nh_step, s, LANES), lambda i, h, j: (i, h, 0, 0)),
            pl.BlockSpec((1, nh_step, V_ROWS, s), lambda i, h, j: (i, h, 0, 0)),
            pl.BlockSpec((tk, tk), lambda i, h, j: (0, 0)),
        ],
        out_specs=pl.BlockSpec((1, tq, nh_step * HEAD_DIM), lambda i, h, j: (i, j, h)),
        out_shape=jax.ShapeDtypeStruct((b, s, ATT_WIDTH), BF16),
        scratch_shapes=[pltpu.VMEM((nh_step, tk, tq), F32)] * 2,
        compiler_params=pltpu.CompilerParams(
            dimension_semantics=("arbitrary", "arbitrary", "arbitrary"),
            vmem_limit_bytes=VMEM_LIMIT),
        name="fox_attention",
    )(qt_aug, qt_aug, k_aug, vt, bias)


def _mix_kernel(x0_ref, xa_ref, xb_ref, y0_ref, ya_ref, yb_ref, g_ref, wcg_ref, cw_ref, wa_ref, wb_ref,
                wo_ref, g2_ref, wr_ref, br_ref, xe_ref, cnt_ref, ra_ref, rb_ref, carry_ref,
                *, tm, tiles_per_seq):
    step = pl.program_id(0)

    def residual(x_ref, y_ref, tile, dst_ref):
        x = x_ref[...]
        h = _rms(x, g_ref[...]).astype(BF16)
        pc = _bdot(h, wcg_ref[...])
        cw = CONV_WIDTH
        cb = pc[:, 0:cw]
        prod = pc[:, cw:2 * cw] * pc[:, 2 * cw:3 * cw]
        ga = pc[:, 3 * cw:3 * cw + D_MODEL]
        gb = pc[:, 3 * cw + D_MODEL:3 * cw + 2 * D_MODEL]

        prev = jnp.where(tile % tiles_per_seq == 0, 0.0, carry_ref[...])
        crow = lax.broadcasted_iota(jnp.int32, (tm, cw), 0)
        m1 = jnp.where(crow == 0, prev[7:8, :], pltpu.roll(prod, 1, axis=0))
        m2 = jnp.where(crow == 0, prev[6:7, :],
                       jnp.where(crow == 1, prev[7:8, :], pltpu.roll(prod, 2, axis=0)))
        carry_ref[...] = prod[tm - 8:tm, :]
        w = cw_ref[...]
        y_conv = cb * (w[0:1, :] * m2 + w[1:2, :] * m1 + w[2:3, :] * prod)

        a = _bdot(y_ref[...], wa_ref[...])
        bb = _bdot(y_conv.astype(BF16), wb_ref[...])
        merged = jax.nn.sigmoid(ga) * a + jax.nn.sigmoid(gb) * bb
        dst_ref[...] = x + _bdot(merged.astype(BF16), wo_ref[...])

    @pl.when(step == 0)
    def _():
        carry_ref[...] = jnp.zeros_like(carry_ref)
        residual(x0_ref, y0_ref, 0, ra_ref)

    _route(ra_ref, g2_ref, wr_ref, br_ref, xe_ref, cnt_ref, 0, tm)
    residual(xa_ref, ya_ref, 2 * step + 1, rb_ref)
    _route(rb_ref, g2_ref, wr_ref, br_ref, xe_ref, cnt_ref, 1, tm)
    residual(xb_ref, yb_ref, 2 * step + 2, ra_ref)


def _route(x1_ref, g2_ref, wr_ref, br_ref, xe_ref, cnt_ref, half, tm):
    rows = slice(half * tm, (half + 1) * tm)
    x1 = x1_ref[...]
    xe_ref[rows, 0:D_MODEL] = x1
    h2 = _rms(x1, g2_ref[...]).astype(BF16)

    logits = _bdot(h2, wr_ref[...]) + br_ref[...]
    lane = lax.broadcasted_iota(jnp.int32, (tm, LANES), 1)
    lanef = lane.astype(F32)
    is_g = (lane >= N_EXPERTS) & (lane < N_EXPERTS + N_GROUPS)
    gl = jnp.where(is_g, logits, NEG_INF)
    gmax = jnp.max(gl, axis=1, keepdims=True)
    gsum = jnp.sum(jnp.exp(gl - gmax), axis=1, keepdims=True)
    g_val = 1.0 / gsum
    g_lane = jnp.min(jnp.where(gl == gmax, lanef, float(LANES)), axis=1, keepdims=True)
    e_lo = (g_lane - float(N_EXPERTS)) * float(EXPERTS_PER_GROUP)
    in_grp = (lanef >= e_lo) & (lanef < e_lo + float(EXPERTS_PER_GROUP))
    el = jnp.where(in_grp, logits, NEG_INF)
    t1 = jnp.max(el, axis=1, keepdims=True)
    i1 = jnp.min(jnp.where(el == t1, lanef, float(LANES)), axis=1, keepdims=True)
    el2 = jnp.where(lanef == i1, NEG_INF, el)
    t2 = jnp.max(el2, axis=1, keepdims=True)
    i2 = jnp.min(jnp.where(el2 == t2, lanef, float(LANES)), axis=1, keepdims=True)
    e2 = jnp.exp(t2 - t1)
    w1 = g_val / (1.0 + e2)
    w2 = g_val * e2 / (1.0 + e2)
    comb = jnp.where(lanef == i1, w1, 0.0) + jnp.where(lanef == i2, w2, 0.0)

    cnt_ref[half] = jnp.sum(jnp.where(is_g & (lanef == g_lane), 1.0, 0.0), axis=0, keepdims=True)
    xe_ref[rows, D_MODEL:XE_WIDTH] = comb + jnp.where(lane == GID_LANE, g_lane - float(N_EXPERTS), 0.0)


def _mix(x, y_att, g, wcg, conv_w, wa, wb, wo, g2, wr, br, *, tm):
    b, s, d = x.shape
    t = b * s
    n_tiles = t // tm
    x2 = x.reshape(t, d)
    y2 = y_att.reshape(t, ATT_WIDTH)
    once = lambda a: pl.BlockSpec(a.shape, lambda i: (0,) * a.ndim, pipeline_mode=pl.Buffered(1))

    def tiles(width):
        nxt = lambda i: (jnp.minimum(2 * i + 2, n_tiles - 1), 0)
        return [pl.BlockSpec((tm, width), lambda i: (0, 0), pipeline_mode=pl.Buffered(1)),
                pl.BlockSpec((tm, width), lambda i: (2 * i + 1, 0)),
                pl.BlockSpec((tm, width), nxt)]

    return pl.pallas_call(
        functools.partial(_mix_kernel, tm=tm, tiles_per_seq=s // tm),
        grid=(n_tiles // 2,),
        in_specs=tiles(d) + tiles(ATT_WIDTH) + [once(g), once(wcg), once(conv_w), once(wa), once(wb),
                                                once(wo), once(g2), once(wr), once(br)],
        out_specs=[pl.BlockSpec((2 * tm, XE_WIDTH), lambda i: (i, 0)),
                   pl.BlockSpec((2, 1, LANES), lambda i: (i, 0, 0))],
        out_shape=[
            jax.ShapeDtypeStruct((t, XE_WIDTH), F32),
            jax.ShapeDtypeStruct((n_tiles, 1, LANES), F32),
        ],
        scratch_shapes=[pltpu.VMEM((tm, d), F32), pltpu.VMEM((tm, d), F32),
                        pltpu.VMEM((8, CONV_WIDTH), F32)],
        compiler_params=pltpu.CompilerParams(
            dimension_semantics=("arbitrary",), vmem_limit_bytes=VMEM_LIMIT),
        name="mix",
    )(x2, x2, x2, y2, y2, y2, g, wcg, conv_w, wa, wb, wo, g2, wr, br)


def _moe_kernel(tcnt_ref, xe_ref, p_ref, g2_ref, wg_ref, wu_ref, wd_ref, g3_ref, wpg_ref, wple_ref,
                earlier_ref, o_ref, hs_ref, rs_ref, ys_ref, *, tm):
    i = pl.program_id(0)
    n = [tcnt_ref[i * N_GROUPS + k] for k in range(N_GROUPS)]
    starts = [jnp.int32(0)]
    for k in range(N_GROUPS - 1):
        starts.append(starts[-1] + pl.cdiv(n[k], BF16_ROWS) * BF16_ROWS)

    x1 = xe_ref[:, 0:D_MODEL]
    route = xe_ref[:, D_MODEL:XE_WIDTH]
    h2 = _rms(x1, g2_ref[...]).astype(BF16)

    lane = lax.broadcasted_iota(jnp.int32, (tm, LANES), 1)
    lanef = lane.astype(F32)
    gid = jnp.sum(jnp.where(lane == GID_LANE, route, 0.0), axis=1, keepdims=True)
    onehot = jnp.where((lanef == gid) & (lane < N_GROUPS), 1.0, 0.0)
    rank = jnp.sum(onehot * _bdot(earlier_ref[...], onehot.astype(BF16)), axis=1, keepdims=True)
    base = starts[N_GROUPS - 1].astype(F32)
    for k in range(N_GROUPS - 2, -1, -1):
        base = jnp.where(gid == float(k), starts[k].astype(F32), base)
    pos = base + rank

    unsort = (pos == lax.broadcasted_iota(jnp.int32, (tm, MOE_SORTED), 1).astype(F32)).astype(BF16)
    digit_hi = jnp.floor(pos * (1.0 / 32.0))
    digits = jnp.where(lane == 0, digit_hi, jnp.where(lane == 1, pos - 32.0 * digit_hi, 0.0))
    sel_lane = lax.broadcasted_iota(jnp.int32, (BF16_ROWS, LANES), 1)
    sel = jnp.where(sel_lane == 0, 32.0, jnp.where(sel_lane == 1, 1.0, 0.0)).astype(BF16)
    pos_row = _bdot_nt(sel, digits.astype(BF16))[0:1, :]
    sort = (lax.broadcasted_iota(jnp.int32, (MOE_SORTED, tm), 0).astype(F32) == pos_row).astype(BF16)

    hs_ref[0:MOE_SORTED] = _bdot(sort, h2).astype(BF16)
    r_hi, r_mid, r_lo = _bf16_pieces(route)
    packed = (r_hi.astype(F32) + pltpu.roll(r_mid.astype(F32), ROUTE_PITCH, axis=1)
              + pltpu.roll(r_lo.astype(F32), 2 * ROUTE_PITCH, axis=1)).astype(BF16)
    rsorted = _bdot(sort, packed)
    rs_ref[0:MOE_SORTED] = (rsorted + pltpu.roll(rsorted, LANES - ROUTE_PITCH, axis=1)
                            + pltpu.roll(rsorted, LANES - 2 * ROUTE_PITCH, axis=1))
    hs_ref[MOE_SORTED:MOE_ROWS] = jnp.zeros((MOE_CHUNK, D_MODEL), BF16)
    rs_ref[MOE_SORTED:MOE_ROWS] = jnp.zeros((MOE_CHUNK, LANES), F32)
    ys_ref[...] = jnp.zeros_like(ys_ref)

    clane = lax.broadcasted_iota(jnp.int32, (MOE_CHUNK, LANES), 1)

    def chunk(g, c):
        r0 = pl.multiple_of(starts[g] + c * MOE_CHUNK, BF16_ROWS)
        hrows = hs_ref[pl.ds(r0, MOE_CHUNK), :]
        rt = rs_ref[pl.ds(r0, MOE_CHUNK), :]
        parts = []
        for e in range(EXPERTS_PER_GROUP):
            ex = g * EXPERTS_PER_GROUP + e
            a = _bdot(hrows, wg_ref[ex])
            u = _bdot(hrows, wu_ref[ex])
            ce = jnp.sum(jnp.where(clane == ex, rt, 0.0), axis=1, keepdims=True)
            parts.append(((a * jax.nn.sigmoid(a)) * u * ce).astype(BF16))
        out = _bdot(jnp.concatenate(parts, axis=1), wd_ref[g])
        ys_ref[pl.ds(r0, MOE_CHUNK), :] = (ys_ref[pl.ds(r0, MOE_CHUNK), :].astype(F32) + out).astype(BF16)

    for g in range(N_GROUPS):
        def more(c, _, g=g):
            chunk(g, c)
            return 0
        lax.fori_loop(1, pl.cdiv(n[g], MOE_CHUNK), more, 0)
    for g in range(N_GROUPS):
        chunk(g, 0)

    x2 = x1 + _bdot(unsort, ys_ref[0:MOE_SORTED])
    h3 = _rms(x2, g3_ref[...]).astype(BF16)
    gate = jax.nn.sigmoid(_bdot(h3, wpg_ref[...]))
    emb = _bdot(p_ref[...].astype(BF16), wple_ref[...])
    o_ref[...] = x2 + gate * emb


def _moe(tcnt, xe, p, g2, wg, wu, wd, g3, wpg, wple, *, tm):
    t = xe.shape[0]
    d = D_MODEL
    once = lambda a: pl.BlockSpec(a.shape, lambda i, c: (0,) * a.ndim, pipeline_mode=pl.Buffered(1))
    row = lambda width: pl.BlockSpec((tm, width), lambda i, c: (i, 0))
    earlier = jnp.asarray(np.tril(np.ones((tm, tm), np.float32), -1), BF16)
    return pl.pallas_call(
        functools.partial(_moe_kernel, tm=tm),
        grid_spec=pltpu.PrefetchScalarGridSpec(
            num_scalar_prefetch=1,
            grid=(t // tm,),
            in_specs=[row(XE_WIDTH), row(PLE_DIM), once(g2), once(wg), once(wu), once(wd), once(g3),
                      once(wpg), once(wple), once(earlier)],
            out_specs=row(d),
            scratch_shapes=[pltpu.VMEM((MOE_ROWS, d), BF16), pltpu.VMEM((MOE_ROWS, LANES), F32),
                            pltpu.VMEM((MOE_ROWS, d), BF16)],
        ),
        out_shape=jax.ShapeDtypeStruct((t, d), F32),
        compiler_params=pltpu.CompilerParams(
            dimension_semantics=("arbitrary",), vmem_limit_bytes=VMEM_LIMIT),
        name="moe",
    )(tcnt, xe, p, g2, wg, wu, wd, g3, wpg, wple, earlier)


def _k_select_matrix():
    pk = np.zeros((LANES, N_HEADS * LANES), np.float32)
    for idx in range(C_PIECES):
        for h in range(N_HEADS):
            pk[idx * N_HEADS + h, h * LANES + KC_LANE + idx] = -1.0
    return jnp.asarray(pk, BF16)


def kernel(x, p, attn_norm_g, w_in, b_f, q_norm_g, k_norm_g, conv_w, w_out_att, w_out_conv, w_o,
           ffn_norm_g, w_rg, b_rg, w_re, b_re, w_gate, w_up, w_down, ple_norm_g, w_pg, w_ple):
    b, s, d = x.shape
    t = b * s
    aw = ATT_WIDTH
    for i in range(w_in.shape[0]):
        wi = w_in[i]
        wf = wi[:, 3 * aw:3 * aw + N_HEADS]
        w_qkvf = jnp.concatenate(
            [wi[:, :3 * aw], wf, wf, wf, jnp.zeros((d, LANES - C_PIECES * N_HEADS), F32)],
            axis=1).astype(BF16)
        w_cg = wi[:, 3 * aw + N_HEADS:].astype(BF16)
        bf3 = jnp.concatenate([b_f[i]] * C_PIECES + [jnp.zeros((LANES - C_PIECES * N_HEADS,), F32)])[None, :]
        scale = HEAD_DIM ** -0.5 * LOG2E
        qg = jnp.tile(q_norm_g[i] * scale, 2)[None, :]
        kg = jnp.tile(k_norm_g[i], 2)[None, :]
        qt_aug, k_aug, vt = _qkv_proj(x, attn_norm_g[i][None, :], w_qkvf, bf3, qg, kg, _k_select_matrix(),
                                      tm=TM_QKV)
        y_att = _attention(qt_aug, k_aug, vt, tq=TQ_ATT, tk=TK_ATT)

        w_r = jnp.concatenate(
            [w_re[i], w_rg[i], jnp.zeros((d, LANES - N_EXPERTS - N_GROUPS), F32)], axis=1).astype(BF16)
        b_r = jnp.concatenate(
            [b_re[i], b_rg[i], jnp.zeros((LANES - N_EXPERTS - N_GROUPS,), F32)])[None, :]
        g_ffn = ffn_norm_g[i][None, :]
        xe, counts = _mix(x, y_att, attn_norm_g[i][None, :], w_cg, conv_w[i],
                          w_out_att[i].astype(BF16), w_out_conv[i].astype(BF16),
                          w_o[i].astype(BF16), g_ffn, w_r, b_r, tm=TM_MIX)
        tcnt = counts[:, 0, N_EXPERTS:N_EXPERTS + N_GROUPS].astype(jnp.int32).reshape(-1)

        w_dn = w_down[i].reshape(N_GROUPS, GROUP_WIDTH, d).astype(BF16)
        x = _moe(tcnt, xe.reshape(t, XE_WIDTH), p[i].reshape(t, PLE_DIM), g_ffn,
                 w_gate[i].astype(BF16), w_up[i].astype(BF16), w_dn,
                 ple_norm_g[i][None, :], w_pg[i].astype(BF16), w_ple[i].astype(BF16),
                 tm=TM_MOE).reshape(b, s, d)
    return x
```

```python
import functools

import numpy as np
import jax
import jax.numpy as jnp
from jax import lax
from jax.experimental import pallas as pl
from jax.experimental.pallas import tpu as pltpu

D_MODEL = 1024
N_HEADS = 8
HEAD_DIM = 64
ATT_WIDTH = N_HEADS * HEAD_DIM
CONV_WIDTH = 512
CONV_K = 3
N_GROUPS = 4
EXPERTS_PER_GROUP = 4
N_EXPERTS = 16
D_EXPERT = 256
GROUP_WIDTH = EXPERTS_PER_GROUP * D_EXPERT
PLE_DIM = 256
EPS = 1e-6
NEG_INF = -1e30
LOG2E = 1.4426950408889634

LANES = 128
BF16_ROWS = 16
C_PIECES = 3
QC_LANE = HEAD_DIM
KC_LANE = HEAD_DIM + C_PIECES
V_ROWS = HEAD_DIM + BF16_ROWS
QKV_WIDTH = 3 * ATT_WIDTH + LANES
GID_LANE = N_EXPERTS
ROUTE_PITCH = 32
XE_WIDTH = D_MODEL + LANES
VMEM_LIMIT = 56 * 1024 * 1024

TM_QKV = 512
TQ_ATT = 512
TK_ATT = 256
ATT_HEADS = 2
TM_MIX = 512
TM_MOE = TM_MIX
MOE_SORTED = -(-(TM_MOE + N_GROUPS * BF16_ROWS) // LANES) * LANES
MOE_CHUNK = 144
MOE_ROWS = MOE_SORTED + MOE_CHUNK

F32 = jnp.float32
BF16 = jnp.bfloat16


def _rms(xf, g):
    return xf * lax.rsqrt(jnp.mean(xf * xf, axis=-1, keepdims=True) + EPS) * g


def _log_sigmoid(z):
    return jnp.minimum(z, 0.0) - jnp.log1p(jnp.exp(-jnp.abs(z)))


def _bdot(a, b):
    return jnp.dot(a, b, preferred_element_type=F32)


def _bdot_nt(a, b):
    return lax.dot_general(a, b, (((1,), (1,)), ((), ())), preferred_element_type=F32)


def _bf16_pieces(x):
    hi = x.astype(BF16)
    r1 = x - hi.astype(F32)
    mid = r1.astype(BF16)
    lo = (r1 - mid.astype(F32)).astype(BF16)
    return hi, mid, lo


def _qkv_kernel(x0_ref, xa_ref, xb_ref, g_ref, w_ref, bf_ref, qg_ref, kg_ref, pk_ref,
                qt_ref, k_ref, vt_ref, pa_ref, pb_ref, carry_ref, *, tm, tiles_per_seq):
    step = pl.program_id(0)

    def project(x_ref, dst_ref):
        dst_ref[...] = _bdot(_rms(x_ref[...], g_ref[...]).astype(BF16), w_ref[...])

    @pl.when(step == 0)
    def _():
        carry_ref[...] = jnp.zeros_like(carry_ref)
        project(x0_ref, pa_ref)

    lane = lax.broadcasted_iota(jnp.int32, (tm, LANES), 1)
    row = lax.broadcasted_iota(jnp.int32, (tm, LANES), 0)
    k_ones = jnp.where((lane >= QC_LANE) & (lane < QC_LANE + C_PIECES), 1.0, 0.0)
    low = lane < HEAD_DIM
    ext_row = lax.broadcasted_iota(jnp.int32, (V_ROWS - HEAD_DIM, tm), 0)
    v_ext = jnp.where(ext_row == 0, 1.0, 0.0)

    def finish(p_ref, tile, half):
        rows = slice(half * tm, (half + 1) * tm)
        for j in range(N_HEADS // 2):
            pair_t = p_ref[:, 2 * ATT_WIDTH + LANES * j: 2 * ATT_WIDTH + LANES * (j + 1)].T
            for hh in range(2):
                vt_ref[0, 2 * j + hh, :, rows] = jnp.concatenate(
                    [pair_t[HEAD_DIM * hh: HEAD_DIM * (hh + 1)], v_ext], axis=0).astype(BF16)

        f3 = p_ref[:, 3 * ATT_WIDTH:QKV_WIDTH] + bf_ref[...]
        c = jnp.where(lane < C_PIECES * N_HEADS, _log_sigmoid(f3), 0.0)
        sh = 1
        while sh < tm:
            c = c + jnp.where(row >= sh, pltpu.roll(c, sh, axis=0), 0.0)
            sh *= 2
        c = c + jnp.where(tile % tiles_per_seq == 0, 0.0, carry_ref[...])
        carry_ref[...] = c[tm - 1:tm, :]
        hi, mid, lo = _bf16_pieces(c * LOG2E)
        piece = jnp.where(lane < N_HEADS, hi, jnp.where(lane < 2 * N_HEADS, mid, lo))

        def rms_scales(pair):
            sq = pair * pair
            ss_lo = jnp.sum(jnp.where(low, sq, 0.0), axis=1, keepdims=True)
            ss_hi = jnp.sum(jnp.where(low, 0.0, sq), axis=1, keepdims=True)
            return (lax.rsqrt(ss_lo * (1.0 / HEAD_DIM) + EPS), lax.rsqrt(ss_hi * (1.0 / HEAD_DIM) + EPS))

        piece_t = piece.astype(F32).T
        erow = lax.broadcasted_iota(jnp.int32, (8, tm), 0)
        pad_t = jnp.zeros((LANES - HEAD_DIM - 8, tm), F32)
        qgain = qg_ref[...]
        for j in range(N_HEADS // 2):
            pair = p_ref[:, LANES * j: LANES * (j + 1)]
            r_lo, r_hi = rms_scales(pair)
            qn_t = (pair * jnp.where(low, r_lo, r_hi) * qgain).T
            for hh in range(2):
                h = 2 * j + hh
                c_t = jnp.where(erow >= C_PIECES, jnp.where(erow < 2 * C_PIECES, 1.0, 0.0), 0.0)
                for idx in range(C_PIECES):
                    c_t = jnp.where(erow == idx, piece_t[idx * N_HEADS + h: idx * N_HEADS + h + 1], c_t)
                qt_ref[0, h, :, rows] = jnp.concatenate(
                    [qn_t[HEAD_DIM * hh: HEAD_DIM * (hh + 1)], c_t, pad_t], axis=0).astype(BF16)

        kc = _bdot(piece, pk_ref[...])
        kgain = kg_ref[...]
        for j in range(N_HEADS // 2):
            pair = p_ref[:, ATT_WIDTH + LANES * j: ATT_WIDTH + LANES * (j + 1)]
            r_lo, r_hi = rms_scales(pair)
            n_lo = pair * r_lo * kgain
            n_hi = pltpu.roll(pair, HEAD_DIM, axis=1) * r_hi * kgain
            for h, nrm in ((2 * j, n_lo), (2 * j + 1, n_hi)):
                aug = jnp.where(low, nrm, 0.0) + kc[:, LANES * h: LANES * (h + 1)] + k_ones
                k_ref[0, h, rows, :] = aug.astype(BF16)

    finish(pa_ref, 2 * step, 0)
    project(xa_ref, pb_ref)
    finish(pb_ref, 2 * step + 1, 1)
    project(xb_ref, pa_ref)


def _qkv_proj(x, g, w, bf3, qg, kg, pk, *, tm):
    b, s, d = x.shape
    x2 = x.reshape(b * s, d)
    n_tiles = b * s // tm
    per_seq = s // (2 * tm)
    full = lambda shape: pl.BlockSpec(shape, lambda i: (0,) * len(shape))
    return pl.pallas_call(
        functools.partial(_qkv_kernel, tm=tm, tiles_per_seq=s // tm),
        grid=(n_tiles // 2,),
        in_specs=[
            pl.BlockSpec((tm, d), lambda i: (0, 0)),
            pl.BlockSpec((tm, d), lambda i: (2 * i + 1, 0)),
            pl.BlockSpec((tm, d), lambda i: (jnp.minimum(2 * i + 2, n_tiles - 1), 0)),
            full(g.shape), full(w.shape), full(bf3.shape), full(qg.shape), full(kg.shape),
            full(pk.shape),
        ],
        out_specs=[pl.BlockSpec((1, N_HEADS, LANES, 2 * tm), lambda i: (i // per_seq, 0, 0, i % per_seq)),
                   pl.BlockSpec((1, N_HEADS, 2 * tm, LANES), lambda i: (i // per_seq, 0, i % per_seq, 0)),
                   pl.BlockSpec((1, N_HEADS, V_ROWS, 2 * tm), lambda i: (i // per_seq, 0, 0, i % per_seq))],
        out_shape=[
            jax.ShapeDtypeStruct((b, N_HEADS, LANES, s), BF16),
            jax.ShapeDtypeStruct((b, N_HEADS, s, LANES), BF16),
            jax.ShapeDtypeStruct((b, N_HEADS, V_ROWS, s), BF16),
        ],
        scratch_shapes=[pltpu.VMEM((tm, QKV_WIDTH), F32), pltpu.VMEM((tm, QKV_WIDTH), F32),
                        pltpu.VMEM((1, LANES), F32)],
        compiler_params=pltpu.CompilerParams(
            dimension_semantics=("arbitrary",), vmem_limit_bytes=VMEM_LIMIT),
        name="qkv_proj",
    )(x2, x2, x2, g, w, bf3, qg, kg, pk)


def _attn_kernel(qt_ref, qn_ref, k_ref, vt_ref, bias_ref, o_ref, sa_ref, sb_ref, *, tq, tk, n_q):
    qi = pl.program_id(2)
    n_sub = tq // tk
    bufs = (sa_ref, sb_ref)

    def produce(j, s_ref, lo=0, q_ref=qt_ref):
        start = j * tk
        for hh in range(ATT_HEADS):
            s_ref[hh, :, lo:tq] = _bdot(k_ref[0, hh, pl.ds(start, tk), :], q_ref[0, hh, :, lo:tq])

    def consume(j, s_ref, carry, lo=0, masked=False):
        start = j * tk
        out = []
        for hh in range(ATT_HEADS):
            m, acc = carry[hh]

            def read():
                if not masked:
                    return s_ref[hh, :, lo:tq]
                tri = s_ref[hh, :, lo:lo + tk] + bias_ref[...]
                return tri if lo + tk == tq else jnp.concatenate([tri, s_ref[hh, :, lo + tk:tq]], axis=1)

            m_new = jnp.maximum(m[:, lo:tq], jnp.max(read(), axis=0, keepdims=True))
            alpha = jnp.exp2(m[:, lo:tq] - m_new)
            p = jnp.exp2(read() - m_new).astype(BF16)
            acc_new = alpha * acc[:, lo:tq] + _bdot(vt_ref[0, hh, :, pl.ds(start, tk)], p)
            if lo:
                m_new = jnp.concatenate([m[:, 0:lo], m_new], axis=1)
                acc_new = jnp.concatenate([acc[:, 0:lo], acc_new], axis=1)
            out.append((m_new, acc_new))
        return tuple(out)

    def trip(r, carry):
        for t in range(n_sub):
            j = r * n_sub + t
            produce(j + 1, bufs[(t + 1) % 2])
            carry = consume(j, bufs[t % 2], carry)
        return carry

    def query_tile(q):
        if q == 0:
            produce(0, sa_ref)
        carry = tuple((jnp.full((1, tq), NEG_INF, F32), jnp.zeros((V_ROWS, tq), F32))
                      for _ in range(ATT_HEADS))
        for r in range(q):
            carry = trip(r, carry)
        for t in range(n_sub):
            j = q * n_sub + t
            if t + 1 < n_sub:
                produce(j + 1, bufs[(t + 1) % 2], lo=(t + 1) * tk)
            elif q + 1 < n_q:
                produce(0, bufs[(t + 1) % 2], q_ref=qn_ref)
            carry = consume(j, bufs[t % 2], carry, lo=t * tk, masked=True)
        y_t = jnp.concatenate([acc[0:HEAD_DIM] / acc[HEAD_DIM:HEAD_DIM + 1] for _, acc in carry], axis=0)
        o_ref[0] = y_t.T.astype(BF16)

    for q in range(n_q):
        pl.when(qi == q)(functools.partial(query_tile, q))


def _attention(qt_aug, k_aug, vt, *, tq, tk):
    b, nh, s, _ = k_aug.shape
    assert tq % (2 * tk) == 0 and s % tq == 0
    future = np.arange(tk)[:, None] > np.arange(tk)[None, :]
    bias = jnp.asarray(np.where(future, np.float32(NEG_INF), np.float32(0.0)))
    nh_step = ATT_HEADS
    grid = (b, nh // nh_step, s // tq)
    return pl.pallas_call(
        functools.partial(_attn_kernel, tq=tq, tk=tk, n_q=s // tq),
        grid=grid,
        in_specs=[
            pl.BlockSpec((1, nh_step, LANES, tq), lambda i, h, j: (i, h, 0, j)),
            pl.BlockSpec((1, nh_step, LANES, tq), lambda i, h, j: (i, h, 0, jnp.minimum(j + 1, s // tq - 1))),
            pl.BlockSpec((1, nh_step, s, LANES), lambda i, h, j: (i, h, 0, 0)),
            pl.BlockSpec((1, nh_step, V_ROWS, s), lambda i, h, j: (i, h, 0, 0)),
            pl.BlockSpec((tk, tk), lambda i, h, j: (0, 0)),
        ],
        out_specs=pl.BlockSpec((1, tq, nh_step * HEAD_DIM), lambda i, h, j: (i, j, h)),
        out_shape=jax.ShapeDtypeStruct((b, s, ATT_WIDTH), BF16),
        scratch_shapes=[pltpu.VMEM((nh_step, tk, tq), F32)] * 2,
        compiler_params=pltpu.CompilerParams(
            dimension_semantics=("arbitrary", "arbitrary", "arbitrary"),
            vmem_limit_bytes=VMEM_LIMIT),
        name="fox_attention",
    )(qt_aug, qt_aug, k_aug, vt, bias)


def _mix_kernel(x0_ref, xa_ref, xb_ref, y0_ref, ya_ref, yb_ref, g_ref, wcg_ref, cw_ref, wa_ref, wb_ref,
                wo_ref, g2_ref, wr_ref, br_ref, xe_ref, cnt_ref, ra_ref, rb_ref, carry_ref,
                *, tm, tiles_per_seq):
    step = pl.program_id(0)

    def residual(x_ref, y_ref, tile, dst_ref):
        x = x_ref[...]
        h = _rms(x, g_ref[...]).astype(BF16)
        pc = _bdot(h, wcg_ref[...])
        cw = CONV_WIDTH
        cb = pc[:, 0:cw]
        prod = pc[:, cw:2 * cw] * pc[:, 2 * cw:3 * cw]
        ga = pc[:, 3 * cw:3 * cw + D_MODEL]
        gb = pc[:, 3 * cw + D_MODEL:3 * cw + 2 * D_MODEL]

        prev = jnp.where(tile % tiles_per_seq == 0, 0.0, carry_ref[...])
        crow = lax.broadcasted_iota(jnp.int32, (tm, cw), 0)
        m1 = jnp.where(crow == 0, prev[7:8, :], pltpu.roll(prod, 1, axis=0))
        m2 = jnp.where(crow == 0, prev[6:7, :],
                       jnp.where(crow == 1, prev[7:8, :], pltpu.roll(prod, 2, axis=0)))
        carry_ref[...] = prod[tm - 8:tm, :]
        w = cw_ref[...]
        y_conv = cb * (w[0:1, :] * m2 + w[1:2, :] * m1 + w[2:3, :] * prod)

        a = _bdot(y_ref[...], wa_ref[...])
        bb = _bdot(y_conv.astype(BF16), wb_ref[...])
        merged = jax.nn.sigmoid(ga) * a + jax.nn.sigmoid(gb) * bb
        dst_ref[...] = x + _bdot(merged.astype(BF16), wo_ref[...])

    @pl.when(step == 0)
    def _():
        carry_ref[...] = jnp.zeros_like(carry_ref)
        residual(x0_ref, y0_ref, 0, ra_ref)

    _route(ra_ref, g2_ref, wr_ref, br_ref, xe_ref, cnt_ref, 0, tm)
    residual(xa_ref, ya_ref, 2 * step + 1, rb_ref)
    _route(rb_ref, g2_ref, wr_ref, br_ref, xe_ref, cnt_ref, 1, tm)
    residual(xb_ref, yb_ref, 2 * step + 2, ra_ref)


def _route(x1_ref, g2_ref, wr_ref, br_ref, xe_ref, cnt_ref, half, tm):
    rows = slice(half * tm, (half + 1) * tm)
    x1 = x1_ref[...]
    xe_ref[rows, 0:D_MODEL] = x1
    h2 = _rms(x1, g2_ref[...]).astype(BF16)

    logits = _bdot(h2, wr_ref[...]) + br_ref[...]
    lane = lax.broadcasted_iota(jnp.int32, (tm, LANES), 1)
    lanef = lane.astype(F32)
    is_g = (lane >= N_EXPERTS) & (lane < N_EXPERTS + N_GROUPS)
    gl = jnp.where(is_g, logits, NEG_INF)
    gmax = jnp.max(gl, axis=1, keepdims=True)
    gsum = jnp.sum(jnp.exp(gl - gmax), axis=1, keepdims=True)
    g_val = 1.0 / gsum
    g_lane = jnp.min(jnp.where(gl == gmax, lanef, float(LANES)), axis=1, keepdims=True)
    e_lo = (g_lane - float(N_EXPERTS)) * float(EXPERTS_PER_GROUP)
    in_grp = (lanef >= e_lo) & (lanef < e_lo + float(EXPERTS_PER_GROUP))
    el = jnp.where(in_grp, logits, NEG_INF)
    t1 = jnp.max(el, axis=1, keepdims=True)
    i1 = jnp.min(jnp.where(el == t1, lanef, float(LANES)), axis=1, keepdims=True)
    el2 = jnp.where(lanef == i1, NEG_INF, el)
    t2 = jnp.max(el2, axis=1, keepdims=True)
    i2 = jnp.min(jnp.where(el2 == t2, lanef, float(LANES)), axis=1, keepdims=True)
    e2 = jnp.exp(t2 - t1)
    w1 = g_val / (1.0 + e2)
    w2 = g_val * e2 / (1.0 + e2)
    comb = jnp.where(lanef == i1, w1, 0.0) + jnp.where(lanef == i2, w2, 0.0)

    cnt_ref[half] = jnp.sum(jnp.where(is_g & (lanef == g_lane), 1.0, 0.0), axis=0, keepdims=True)
    xe_ref[rows, D_MODEL:XE_WIDTH] = comb + jnp.where(lane == GID_LANE, g_lane - float(N_EXPERTS), 0.0)


def _mix(x, y_att, g, wcg, conv_w, wa, wb, wo, g2, wr, br, *, tm):
    b, s, d = x.shape
    t = b * s
    n_tiles = t // tm
    x2 = x.reshape(t, d)
    y2 = y_att.reshape(t, ATT_WIDTH)
    once = lambda a: pl.BlockSpec(a.shape, lambda i: (0,) * a.ndim, pipeline_mode=pl.Buffered(1))

    def tiles(width):
        nxt = lambda i: (jnp.minimum(2 * i + 2, n_tiles - 1), 0)
        return [pl.BlockSpec((tm, width), lambda i: (0, 0), pipeline_mode=pl.Buffered(1)),
                pl.BlockSpec((tm, width), lambda i: (2 * i + 1, 0)),
                pl.BlockSpec((tm, width), nxt)]

    return pl.pallas_call(
        functools.partial(_mix_kernel, tm=tm, tiles_per_seq=s // tm),
        grid=(n_tiles // 2,),
        in_specs=tiles(d) + tiles(ATT_WIDTH) + [once(g), once(wcg), once(conv_w), once(wa), once(wb),
                                                once(wo), once(g2), once(wr), once(br)],
        out_specs=[pl.BlockSpec((2 * tm, XE_WIDTH), lambda i: (i, 0)),
                   pl.BlockSpec((2, 1, LANES), lambda i: (i, 0, 0))],
        out_shape=[
            jax.ShapeDtypeStruct((t, XE_WIDTH), F32),
            jax.ShapeDtypeStruct((n_tiles, 1, LANES), F32),
        ],
        scratch_shapes=[pltpu.VMEM((tm, d), F32), pltpu.VMEM((tm, d), F32),
                        pltpu.VMEM((8, CONV_WIDTH), F32)],
        compiler_params=pltpu.CompilerParams(
            dimension_semantics=("arbitrary",), vmem_limit_bytes=VMEM_LIMIT),
        name="mix",
    )(x2, x2, x2, y2, y2, y2, g, wcg, conv_w, wa, wb, wo, g2, wr, br)


def _moe_kernel(tcnt_ref, xe_ref, p_ref, g2_ref, wg_ref, wu_ref, wd_ref, g3_ref, wpg_ref, wple_ref,
                earlier_ref, o_ref, hs_ref, rs_ref, ys_ref, *, tm):
    i = pl.program_id(0)
    n = [tcnt_ref[i * N_GROUPS + k] for k in range(N_GROUPS)]
    starts = [jnp.int32(0)]
    for k in range(N_GROUPS - 1):
        starts.append(starts[-1] + pl.cdiv(n[k], BF16_ROWS) * BF16_ROWS)

    x1 = xe_ref[:, 0:D_MODEL]
    route = xe_ref[:, D_MODEL:XE_WIDTH]
    h2 = _rms(x1, g2_ref[...]).astype(BF16)

    lane = lax.broadcasted_iota(jnp.int32, (tm, LANES), 1)
    lanef = lane.astype(F32)
    gid = jnp.sum(jnp.where(lane == GID_LANE, route, 0.0), axis=1, keepdims=True)
    onehot = jnp.where((lanef == gid) & (lane < N_GROUPS), 1.0, 0.0)
    rank = jnp.sum(onehot * _bdot(earlier_ref[...], onehot.astype(BF16)), axis=1, keepdims=True)
    base = starts[N_GROUPS - 1].astype(F32)
    for k in range(N_GROUPS - 2, -1, -1):
        base = jnp.where(gid == float(k), starts[k].astype(F32), base)
    pos = base + rank

    unsort = (pos == lax.broadcasted_iota(jnp.int32, (tm, MOE_SORTED), 1).astype(F32)).astype(BF16)
    digit_hi = jnp.floor(pos * (1.0 / 32.0))
    digits = jnp.where(lane == 0, digit_hi, jnp.where(lane == 1, pos - 32.0 * digit_hi, 0.0))
    sel_lane = lax.broadcasted_iota(jnp.int32, (BF16_ROWS, LANES), 1)
    sel = jnp.where(sel_lane == 0, 32.0, jnp.where(sel_lane == 1, 1.0, 0.0)).astype(BF16)
    pos_row = _bdot_nt(sel, digits.astype(BF16))[0:1, :]
    sort = (lax.broadcasted_iota(jnp.int32, (MOE_SORTED, tm), 0).astype(F32) == pos_row).astype(BF16)

    hs_ref[0:MOE_SORTED] = _bdot(sort, h2).astype(BF16)
    r_hi, r_mid, r_lo = _bf16_pieces(route)
    packed = (r_hi.astype(F32) + pltpu.roll(r_mid.astype(F32), ROUTE_PITCH, axis=1)
              + pltpu.roll(r_lo.astype(F32), 2 * ROUTE_PITCH, axis=1)).astype(BF16)
    rsorted = _bdot(sort, packed)
    rs_ref[0:MOE_SORTED] = (rsorted + pltpu.roll(rsorted, LANES - ROUTE_PITCH, axis=1)
                            + pltpu.roll(rsorted, LANES - 2 * ROUTE_PITCH, axis=1))
    hs_ref[MOE_SORTED:MOE_ROWS] = jnp.zeros((MOE_CHUNK, D_MODEL), BF16)
    rs_ref[MOE_SORTED:MOE_ROWS] = jnp.zeros((MOE_CHUNK, LANES), F32)
    ys_ref[...] = jnp.zeros_like(ys_ref)

    clane = lax.broadcasted_iota(jnp.int32, (MOE_CHUNK, LANES), 1)

    def chunk(g, c):
        r0 = pl.multiple_of(starts[g] + c * MOE_CHUNK, BF16_ROWS)
        hrows = hs_ref[pl.ds(r0, MOE_CHUNK), :]
        rt = rs_ref[pl.ds(r0, MOE_CHUNK), :]
        parts = []
        for e in range(EXPERTS_PER_GROUP):
            ex = g * EXPERTS_PER_GROUP + e
            a = _bdot(hrows, wg_ref[ex])
            u = _bdot(hrows, wu_ref[ex])
            ce = jnp.sum(jnp.where(clane == ex, rt, 0.0), axis=1, keepdims=True)
            parts.append(((a * jax.nn.sigmoid(a)) * u * ce).astype(BF16))
        out = _bdot(jnp.concatenate(parts, axis=1), wd_ref[g])
        ys_ref[pl.ds(r0, MOE_CHUNK), :] = (ys_ref[pl.ds(r0, MOE_CHUNK), :].astype(F32) + out).astype(BF16)

    for g in range(N_GROUPS):
        def more(c, _, g=g):
            chunk(g, c)
            return 0
        lax.fori_loop(1, pl.cdiv(n[g], MOE_CHUNK), more, 0)
    for g in range(N_GROUPS):
        chunk(g, 0)

    x2 = x1 + _bdot(unsort, ys_ref[0:MOE_SORTED])
    h3 = _rms(x2, g3_ref[...]).astype(BF16)
    gate = jax.nn.sigmoid(_bdot(h3, wpg_ref[...]))
    emb = _bdot(p_ref[...].astype(BF16), wple_ref[...])
    o_ref[...] = x2 + gate * emb


def _moe(tcnt, xe, p, g2, wg, wu, wd, g3, wpg, wple, *, tm):
    t = xe.shape[0]
    d = D_MODEL
    once = lambda a: pl.BlockSpec(a.shape, lambda i, c: (0,) * a.ndim, pipeline_mode=pl.Buffered(1))
    row = lambda width: pl.BlockSpec((tm, width), lambda i, c: (i, 0))
    earlier = jnp.asarray(np.tril(np.ones((tm, tm), np.float32), -1), BF16)
    return pl.pallas_call(
        functools.partial(_moe_kernel, tm=tm),
        grid_spec=pltpu.PrefetchScalarGridSpec(
            num_scalar_prefetch=1,
            grid=(t // tm,),
            in_specs=[row(XE_WIDTH), row(PLE_DIM), once(g2), once(wg), once(wu), once(wd), once(g3),
                      once(wpg), once(wple), once(earlier)],
            out_specs=row(d),
            scratch_shapes=[pltpu.VMEM((MOE_ROWS, d), BF16), pltpu.VMEM((MOE_ROWS, LANES), F32),
                            pltpu.VMEM((MOE_ROWS, d), BF16)],
        ),
        out_shape=jax.ShapeDtypeStruct((t, d), F32),
        compiler_params=pltpu.CompilerParams(
            dimension_semantics=("arbitrary",), vmem_limit_bytes=VMEM_LIMIT),
        name="moe",
    )(tcnt, xe, p, g2, wg, wu, wd, g3, wpg, wple, earlier)


def _k_select_matrix():
    pk = np.zeros((LANES, N_HEADS * LANES), np.float32)
    for idx in range(C_PIECES):
        for h in range(N_HEADS):
            pk[idx * N_HEADS + h, h * LANES + KC_LANE + idx] = -1.0
    return jnp.asarray(pk, BF16)


def kernel(x, p, attn_norm_g, w_in, b_f, q_norm_g, k_norm_g, conv_w, w_out_att, w_out_conv, w_o,
           ffn_norm_g, w_rg, b_rg, w_re, b_re, w_gate, w_up, w_down, ple_norm_g, w_pg, w_ple):
    b, s, d = x.shape
    assert d == D_MODEL and conv_w.shape[1:] == (CONV_K, CONV_WIDTH) and s % (2 * TM_MIX) == 0
    t = b * s
    aw = ATT_WIDTH
    for i in range(w_in.shape[0]):
        wi = w_in[i]
        wf = wi[:, 3 * aw:3 * aw + N_HEADS]
        w_qkvf = jnp.concatenate(
            [wi[:, :3 * aw], wf, wf, wf, jnp.zeros((d, LANES - C_PIECES * N_HEADS), F32)],
            axis=1).astype(BF16)
        w_cg = wi[:, 3 * aw + N_HEADS:].astype(BF16)
        bf3 = jnp.concatenate([b_f[i]] * C_PIECES + [jnp.zeros((LANES - C_PIECES * N_HEADS,), F32)])[None, :]
        scale = HEAD_DIM ** -0.5 * LOG2E
        qg = jnp.tile(q_norm_g[i] * scale, 2)[None, :]
        kg = jnp.tile(k_norm_g[i], 2)[None, :]
        qt_aug, k_aug, vt = _qkv_proj(x, attn_norm_g[i][None, :], w_qkvf, bf3, qg, kg, _k_select_matrix(),
                                      tm=TM_QKV)
        y_att = _attention(qt_aug, k_aug, vt, tq=TQ_ATT, tk=TK_ATT)

        w_r = jnp.concatenate(
            [w_re[i], w_rg[i], jnp.zeros((d, LANES - N_EXPERTS - N_GROUPS), F32)], axis=1).astype(BF16)
        b_r = jnp.concatenate(
            [b_re[i], b_rg[i], jnp.zeros((LANES - N_EXPERTS - N_GROUPS,), F32)])[None, :]
        g_ffn = ffn_norm_g[i][None, :]
        xe, counts = _mix(x, y_att, attn_norm_g[i][None, :], w_cg, conv_w[i],
                          w_out_att[i].astype(BF16), w_out_conv[i].astype(BF16),
                          w_o[i].astype(BF16), g_ffn, w_r, b_r, tm=TM_MIX)
        tcnt = counts[:, 0, N_EXPERTS:N_EXPERTS + N_GROUPS].astype(jnp.int32).reshape(-1)

        w_dn = w_down[i].reshape(N_GROUPS, GROUP_WIDTH, d).astype(BF16)
        x = _moe(tcnt, xe.reshape(t, XE_WIDTH), p[i].reshape(t, PLE_DIM), g_ffn,
                 w_gate[i].astype(BF16), w_up[i].astype(BF16), w_dn,
                 ple_norm_g[i][None, :], w_pg[i].astype(BF16), w_ple[i].astype(BF16),
                 tm=TM_MOE).reshape(b, s, d)
    return x
```

```python
import functools

import numpy as np
import jax
import jax.numpy as jnp
from jax import lax
from jax.experimental import pallas as pl
from jax.experimental.pallas import tpu as pltpu

D_MODEL = 1024
N_HEADS = 8
HEAD_DIM = 64
ATT_WIDTH = N_HEADS * HEAD_DIM
CONV_WIDTH = 512
CONV_K = 3
N_GROUPS = 4
EXPERTS_PER_GROUP = 4
N_EXPERTS = 16
D_EXPERT = 256
GROUP_WIDTH = EXPERTS_PER_GROUP * D_EXPERT
PLE_DIM = 256
EPS = 1e-6
NEG_INF = -1e30
LOG2E = 1.4426950408889634

LANES = 128
BF16_ROWS = 16
C_PIECES = 3
QC_LANE = HEAD_DIM
KC_LANE = HEAD_DIM + C_PIECES
V_ROWS = HEAD_DIM + BF16_ROWS
QKV_WIDTH = 3 * ATT_WIDTH + LANES
GID_LANE = N_EXPERTS
ROUTE_PITCH = 32
XE_WIDTH = D_MODEL + LANES
VMEM_LIMIT = 56 * 1024 * 1024

TM_QKV = 512
TQ_ATT = 512
TK_ATT = 256
ATT_HEADS = 2
TM_MIX = 512
TM_MOE = TM_MIX
MOE_SORTED = -(-(TM_MOE + N_GROUPS * BF16_ROWS) // LANES) * LANES
MOE_CHUNK = 144
MOE_ROWS = MOE_SORTED + MOE_CHUNK

F32 = jnp.float32
BF16 = jnp.bfloat16


def _rms(xf, g):
    return xf * lax.rsqrt(jnp.mean(xf * xf, axis=-1, keepdims=True) + EPS) * g


def _log_sigmoid(z):
    return jnp.minimum(z, 0.0) - jnp.log1p(jnp.exp(-jnp.abs(z)))


def _bdot(a, b):
    return jnp.dot(a, b, preferred_element_type=F32)


def _bdot_nt(a, b):
    return lax.dot_general(a, b, (((1,), (1,)), ((), ())), preferred_element_type=F32)


def _bf16_pieces(x):
    hi = x.astype(BF16)
    r1 = x - hi.astype(F32)
    mid = r1.astype(BF16)
    lo = (r1 - mid.astype(F32)).astype(BF16)
    return hi, mid, lo


def _qkv_kernel(x0_ref, xa_ref, xb_ref, g_ref, w_ref, bf_ref, qg_ref, kg_ref, pk_ref,
                qt_ref, k_ref, vt_ref, pa_ref, pb_ref, carry_ref, *, tm, tiles_per_seq):
    step = pl.program_id(0)

    def project(x_ref, dst_ref):
        dst_ref[...] = _bdot_nt(_rms(x_ref[...], g_ref[...]).astype(BF16), w_ref[...])

    @pl.when(step == 0)
    def _():
        carry_ref[...] = jnp.zeros_like(carry_ref)
        project(x0_ref, pa_ref)

    lane = lax.broadcasted_iota(jnp.int32, (tm, LANES), 1)
    row = lax.broadcasted_iota(jnp.int32, (tm, LANES), 0)
    k_ones = jnp.where((lane >= QC_LANE) & (lane < QC_LANE + C_PIECES), 1.0, 0.0)
    low = lane < HEAD_DIM
    ext_row = lax.broadcasted_iota(jnp.int32, (V_ROWS - HEAD_DIM, tm), 0)
    v_ext = jnp.where(ext_row == 0, 1.0, 0.0)

    def finish(p_ref, tile, half):
        rows = slice(half * tm, (half + 1) * tm)
        for j in range(N_HEADS // 2):
            pair_t = p_ref[:, 2 * ATT_WIDTH + LANES * j: 2 * ATT_WIDTH + LANES * (j + 1)].T
            for hh in range(2):
                vt_ref[0, 2 * j + hh, :, rows] = jnp.concatenate(
                    [pair_t[HEAD_DIM * hh: HEAD_DIM * (hh + 1)], v_ext], axis=0).astype(BF16)

        f3 = p_ref[:, 3 * ATT_WIDTH:QKV_WIDTH] + bf_ref[...]
        c = jnp.where(lane < C_PIECES * N_HEADS, _log_sigmoid(f3), 0.0)
        sh = 1
        while sh < tm:
            c = c + jnp.where(row >= sh, pltpu.roll(c, sh, axis=0), 0.0)
            sh *= 2
        c = c + jnp.where(tile % tiles_per_seq == 0, 0.0, carry_ref[...])
        carry_ref[...] = c[tm - 1:tm, :]
        hi, mid, lo = _bf16_pieces(c * LOG2E)
        piece = jnp.where(lane < N_HEADS, hi, jnp.where(lane < 2 * N_HEADS, mid, lo))

        def rms_scales(pair):
            sq = pair * pair
            ss_lo = jnp.sum(jnp.where(low, sq, 0.0), axis=1, keepdims=True)
            ss_hi = jnp.sum(jnp.where(low, 0.0, sq), axis=1, keepdims=True)
            return (lax.rsqrt(ss_lo * (1.0 / HEAD_DIM) + EPS), lax.rsqrt(ss_hi * (1.0 / HEAD_DIM) + EPS))

        piece_t = piece.astype(F32).T
        erow = lax.broadcasted_iota(jnp.int32, (8, tm), 0)
        pad_t = jnp.zeros((LANES - HEAD_DIM - 8, tm), F32)
        qgain = qg_ref[...]
        for j in range(N_HEADS // 2):
            pair = p_ref[:, LANES * j: LANES * (j + 1)]
            r_lo, r_hi = rms_scales(pair)
            qn_t = (pair * jnp.where(low, r_lo, r_hi) * qgain).T
            for hh in range(2):
                h = 2 * j + hh
                c_t = jnp.where(erow >= C_PIECES, jnp.where(erow < 2 * C_PIECES, 1.0, 0.0), 0.0)
                for idx in range(C_PIECES):
                    c_t = jnp.where(erow == idx, piece_t[idx * N_HEADS + h: idx * N_HEADS + h + 1], c_t)
                qt_ref[0, h, :, rows] = jnp.concatenate(
                    [qn_t[HEAD_DIM * hh: HEAD_DIM * (hh + 1)], c_t, pad_t], axis=0).astype(BF16)

        kc = _bdot(piece, pk_ref[...])
        kgain = kg_ref[...]
        for j in range(N_HEADS // 2):
            pair = p_ref[:, ATT_WIDTH + LANES * j: ATT_WIDTH + LANES * (j + 1)]
            r_lo, r_hi = rms_scales(pair)
            n_lo = pair * r_lo * kgain
            n_hi = pltpu.roll(pair, HEAD_DIM, axis=1) * r_hi * kgain
            for h, nrm in ((2 * j, n_lo), (2 * j + 1, n_hi)):
                aug = jnp.where(low, nrm, 0.0) + kc[:, LANES * h: LANES * (h + 1)] + k_ones
                k_ref[0, h, rows, :] = aug.astype(BF16)

    finish(pa_ref, 2 * step, 0)
    project(xa_ref, pb_ref)
    finish(pb_ref, 2 * step + 1, 1)
    project(xb_ref, pa_ref)


def _qkv_proj(x, g, w, bf3, qg, kg, pk, *, tm):
    b, s, d = x.shape
    x2 = x.reshape(b * s, d)
    n_tiles = b * s // tm
    per_seq = s // (2 * tm)
    full = lambda shape: pl.BlockSpec(shape, lambda i: (0,) * len(shape))
    return pl.pallas_call(
        functools.partial(_qkv_kernel, tm=tm, tiles_per_seq=s // tm),
        grid=(n_tiles // 2,),
        in_specs=[
            pl.BlockSpec((tm, d), lambda i: (0, 0)),
            pl.BlockSpec((tm, d), lambda i: (2 * i + 1, 0)),
            pl.BlockSpec((tm, d), lambda i: (jnp.minimum(2 * i + 2, n_tiles - 1), 0)),
            full(g.shape), full(w.shape), full(bf3.shape), full(qg.shape), full(kg.shape),
            full(pk.shape),
        ],
        out_specs=[pl.BlockSpec((1, N_HEADS, LANES, 2 * tm), lambda i: (i // per_seq, 0, 0, i % per_seq)),
                   pl.BlockSpec((1, N_HEADS, 2 * tm, LANES), lambda i: (i // per_seq, 0, i % per_seq, 0)),
                   pl.BlockSpec((1, N_HEADS, V_ROWS, 2 * tm), lambda i: (i // per_seq, 0, 0, i % per_seq))],
        out_shape=[
            jax.ShapeDtypeStruct((b, N_HEADS, LANES, s), BF16),
            jax.ShapeDtypeStruct((b, N_HEADS, s, LANES), BF16),
            jax.ShapeDtypeStruct((b, N_HEADS, V_ROWS, s), BF16),
        ],
        scratch_shapes=[pltpu.VMEM((tm, QKV_WIDTH), F32), pltpu.VMEM((tm, QKV_WIDTH), F32),
                        pltpu.VMEM((1, LANES), F32)],
        compiler_params=pltpu.CompilerParams(
            dimension_semantics=("arbitrary",), vmem_limit_bytes=VMEM_LIMIT),
        name="qkv_proj",
    )(x2, x2, x2, g, w, bf3, qg, kg, pk)


def _attn_kernel(qt_ref, qn_ref, k_ref, vt_ref, bias_ref, o_ref, sa_ref, sb_ref, *, tq, tk, n_q):
    qi = pl.program_id(2)
    n_sub = tq // tk
    bufs = (sa_ref, sb_ref)

    def produce(j, s_ref, lo=0, q_ref=qt_ref):
        start = j * tk
        for hh in range(ATT_HEADS):
            s_ref[hh, :, lo:tq] = _bdot(k_ref[0, hh, pl.ds(start, tk), :], q_ref[0, hh, :, lo:tq])

    def consume(j, s_ref, carry, lo=0, masked=False):
        start = j * tk
        out = []
        for hh in range(ATT_HEADS):
            m, acc = carry[hh]

            def read():
                if not masked:
                    return s_ref[hh, :, lo:tq]
                tri = s_ref[hh, :, lo:lo + tk] + bias_ref[...]
                return tri if lo + tk == tq else jnp.concatenate([tri, s_ref[hh, :, lo + tk:tq]], axis=1)

            m_new = jnp.maximum(m[:, lo:tq], jnp.max(read(), axis=0, keepdims=True))
            alpha = jnp.exp2(m[:, lo:tq] - m_new)
            p = jnp.exp2(read() - m_new).astype(BF16)
            acc_new = alpha * acc[:, lo:tq] + _bdot(vt_ref[0, hh, :, pl.ds(start, tk)], p)
            if lo:
                m_new = jnp.concatenate([m[:, 0:lo], m_new], axis=1)
                acc_new = jnp.concatenate([acc[:, 0:lo], acc_new], axis=1)
            out.append((m_new, acc_new))
        return tuple(out)

    def trip(r, carry):
        for t in range(n_sub):
            j = r * n_sub + t
            produce(j + 1, bufs[(t + 1) % 2])
            carry = consume(j, bufs[t % 2], carry)
        return carry

    def query_tile(q):
        if q == 0:
            produce(0, sa_ref)
        carry = tuple((jnp.full((1, tq), NEG_INF, F32), jnp.zeros((V_ROWS, tq), F32))
                      for _ in range(ATT_HEADS))
        for r in range(q):
            carry = trip(r, carry)
        for t in range(n_sub):
            j = q * n_sub + t
            if t + 1 < n_sub:
                produce(j + 1, bufs[(t + 1) % 2], lo=(t + 1) * tk)
            elif q + 1 < n_q:
                produce(0, bufs[(t + 1) % 2], q_ref=qn_ref)
            carry = consume(j, bufs[t % 2], carry, lo=t * tk, masked=True)
        y_t = jnp.concatenate([acc[0:HEAD_DIM] / acc[HEAD_DIM:HEAD_DIM + 1] for _, acc in carry], axis=0)
        o_ref[0] = y_t.T.astype(BF16)

    for q in range(n_q):
        pl.when(qi == q)(functools.partial(query_tile, q))


def _attention(qt_aug, k_aug, vt, *, tq, tk):
    b, nh, s, _ = k_aug.shape
    assert tq % (2 * tk) == 0 and s % tq == 0
    future = np.arange(tk)[:, None] > np.arange(tk)[None, :]
    bias = jnp.asarray(np.where(future, np.float32(NEG_INF), np.float32(0.0)))
    nh_step = ATT_HEADS
    grid = (b, nh // nh_step, s // tq)
    return pl.pallas_call(
        functools.partial(_attn_kernel, tq=tq, tk=tk, n_q=s // tq),
        grid=grid,
        in_specs=[
            pl.BlockSpec((1, nh_step, LANES, tq), lambda i, h, j: (i, h, 0, j)),
            pl.BlockSpec((1, nh_step, LANES, tq), lambda i, h, j: (i, h, 0, jnp.minimum(j + 1, s // tq - 1))),
            pl.BlockSpec((1, nh_step, s, LANES), lambda i, h, j: (i, h, 0, 0)),
            pl.BlockSpec((1, nh_step, V_ROWS, s), lambda i, h, j: (i, h, 0, 0)),
            pl.BlockSpec((tk, tk), lambda i, h, j: (0, 0)),
        ],
        out_specs=pl.BlockSpec((1, tq, nh_step * HEAD_DIM), lambda i, h, j: (i, j, h)),
        out_shape=jax.ShapeDtypeStruct((b, s, ATT_WIDTH), BF16),
        scratch_shapes=[pltpu.VMEM((nh_step, tk, tq), F32)] * 2,
        compiler_params=pltpu.CompilerParams(
            dimension_semantics=("arbitrary", "arbitrary", "arbitrary"),
            vmem_limit_bytes=VMEM_LIMIT),
        name="fox_attention",
    )(qt_aug, qt_aug, k_aug, vt, bias)


def _mix_kernel(x0_ref, xa_ref, xb_ref, y0_ref, ya_ref, yb_ref, g_ref, wcg_ref, cw_ref, wa_ref, wb_ref,
                wo_ref, g2_ref, wr_ref, br_ref, xe_ref, cnt_ref, ra_ref, rb_ref, carry_ref,
                *, tm, tiles_per_seq):
    step = pl.program_id(0)

    def residual(x_ref, y_ref, tile, dst_ref):
        x = x_ref[...]
        h = _rms(x, g_ref[...]).astype(BF16)
        pc = _bdot_nt(h, wcg_ref[...])
        cw = CONV_WIDTH
        cb = pc[:, 0:cw]
        prod = pc[:, cw:2 * cw] * pc[:, 2 * cw:3 * cw]
        ga = pc[:, 3 * cw:3 * cw + D_MODEL]
        gb = pc[:, 3 * cw + D_MODEL:3 * cw + 2 * D_MODEL]

        prev = jnp.where(tile % tiles_per_seq == 0, 0.0, carry_ref[...])
        crow = lax.broadcasted_iota(jnp.int32, (tm, cw), 0)
        m1 = jnp.where(crow == 0, prev[7:8, :], pltpu.roll(prod, 1, axis=0))
        m2 = jnp.where(crow == 0, prev[6:7, :],
                       jnp.where(crow == 1, prev[7:8, :], pltpu.roll(prod, 2, axis=0)))
        carry_ref[...] = prod[tm - 8:tm, :]
        w = cw_ref[...]
        y_conv = cb * (w[0:1, :] * m2 + w[1:2, :] * m1 + w[2:3, :] * prod)

        a = _bdot(y_ref[...], wa_ref[...])
        bb = _bdot(y_conv.astype(BF16), wb_ref[...])
        merged = jax.nn.sigmoid(ga) * a + jax.nn.sigmoid(gb) * bb
        dst_ref[...] = x + _bdot(merged.astype(BF16), wo_ref[...])

    @pl.when(step == 0)
    def _():
        carry_ref[...] = jnp.zeros_like(carry_ref)
        residual(x0_ref, y0_ref, 0, ra_ref)

    _route(ra_ref, g2_ref, wr_ref, br_ref, xe_ref, cnt_ref, 0, tm)
    residual(xa_ref, ya_ref, 2 * step + 1, rb_ref)
    _route(rb_ref, g2_ref, wr_ref, br_ref, xe_ref, cnt_ref, 1, tm)
    residual(xb_ref, yb_ref, 2 * step + 2, ra_ref)


def _route(x1_ref, g2_ref, wr_ref, br_ref, xe_ref, cnt_ref, half, tm):
    rows = slice(half * tm, (half + 1) * tm)
    x1 = x1_ref[...]
    xe_ref[rows, 0:D_MODEL] = x1
    h2 = _rms(x1, g2_ref[...]).astype(BF16)

    logits = _bdot(h2, wr_ref[...]) + br_ref[...]
    lane = lax.broadcasted_iota(jnp.int32, (tm, LANES), 1)
    lanef = lane.astype(F32)
    is_g = (lane >= N_EXPERTS) & (lane < N_EXPERTS + N_GROUPS)
    gl = jnp.where(is_g, logits, NEG_INF)
    gmax = jnp.max(gl, axis=1, keepdims=True)
    gsum = jnp.sum(jnp.exp(gl - gmax), axis=1, keepdims=True)
    g_val = 1.0 / gsum
    g_lane = jnp.min(jnp.where(gl == gmax, lanef, float(LANES)), axis=1, keepdims=True)
    e_lo = (g_lane - float(N_EXPERTS)) * float(EXPERTS_PER_GROUP)
    in_grp = (lanef >= e_lo) & (lanef < e_lo + float(EXPERTS_PER_GROUP))
    el = jnp.where(in_grp, logits, NEG_INF)
    t1 = jnp.max(el, axis=1, keepdims=True)
    i1 = jnp.min(jnp.where(el == t1, lanef, float(LANES)), axis=1, keepdims=True)
    el2 = jnp.where(lanef == i1, NEG_INF, el)
    t2 = jnp.max(el2, axis=1, keepdims=True)
    i2 = jnp.min(jnp.where(el2 == t2, lanef, float(LANES)), axis=1, keepdims=True)
    e2 = jnp.exp(t2 - t1)
    w1 = g_val / (1.0 + e2)
    w2 = g_val * e2 / (1.0 + e2)
    comb = jnp.where(lanef == i1, w1, 0.0) + jnp.where(lanef == i2, w2, 0.0)

    cnt_ref[half] = jnp.sum(jnp.where(is_g & (lanef == g_lane), 1.0, 0.0), axis=0, keepdims=True)
    xe_ref[rows, D_MODEL:XE_WIDTH] = comb + jnp.where(lane == GID_LANE, g_lane - float(N_EXPERTS), 0.0)


def _mix(x, y_att, g, wcg, conv_w, wa, wb, wo, g2, wr, br, *, tm):
    b, s, d = x.shape
    t = b * s
    n_tiles = t // tm
    x2 = x.reshape(t, d)
    y2 = y_att.reshape(t, ATT_WIDTH)
    once = lambda a: pl.BlockSpec(a.shape, lambda i: (0,) * a.ndim, pipeline_mode=pl.Buffered(1))

    def tiles(width):
        nxt = lambda i: (jnp.minimum(2 * i + 2, n_tiles - 1), 0)
        return [pl.BlockSpec((tm, width), lambda i: (0, 0), pipeline_mode=pl.Buffered(1)),
                pl.BlockSpec((tm, width), lambda i: (2 * i + 1, 0)),
                pl.BlockSpec((tm, width), nxt)]

    return pl.pallas_call(
        functools.partial(_mix_kernel, tm=tm, tiles_per_seq=s // tm),
        grid=(n_tiles // 2,),
        in_specs=tiles(d) + tiles(ATT_WIDTH) + [once(g), once(wcg), once(conv_w), once(wa), once(wb),
                                                once(wo), once(g2), once(wr), once(br)],
        out_specs=[pl.BlockSpec((2 * tm, XE_WIDTH), lambda i: (i, 0)),
                   pl.BlockSpec((2, 1, LANES), lambda i: (i, 0, 0))],
        out_shape=[
            jax.ShapeDtypeStruct((t, XE_WIDTH), F32),
            jax.ShapeDtypeStruct((n_tiles, 1, LANES), F32),
        ],
        scratch_shapes=[pltpu.VMEM((tm, d), F32), pltpu.VMEM((tm, d), F32),
                        pltpu.VMEM((8, CONV_WIDTH), F32)],
        compiler_params=pltpu.CompilerParams(
            dimension_semantics=("arbitrary",), vmem_limit_bytes=VMEM_LIMIT),
        name="mix",
    )(x2, x2, x2, y2, y2, y2, g, wcg, conv_w, wa, wb, wo, g2, wr, br)


def _moe_kernel(tcnt_ref, xe_ref, p_ref, g2_ref, wg_ref, wu_ref, wd_ref, g3_ref, wpg_ref, wple_ref,
                earlier_ref, o_ref, hs_ref, rs_ref, ys_ref, *, tm):
    i = pl.program_id(0)
    n = [tcnt_ref[i * N_GROUPS + k] for k in range(N_GROUPS)]
    starts = [jnp.int32(0)]
    for k in range(N_GROUPS - 1):
        starts.append(starts[-1] + pl.cdiv(n[k], BF16_ROWS) * BF16_ROWS)

    x1 = xe_ref[:, 0:D_MODEL]
    route = xe_ref[:, D_MODEL:XE_WIDTH]
    h2 = _rms(x1, g2_ref[...]).astype(BF16)

    lane = lax.broadcasted_iota(jnp.int32, (tm, LANES), 1)
    lanef = lane.astype(F32)
    gid = jnp.sum(jnp.where(lane == GID_LANE, route, 0.0), axis=1, keepdims=True)
    onehot = jnp.where((lanef == gid) & (lane < N_GROUPS), 1.0, 0.0)
    rank = jnp.sum(onehot * _bdot(earlier_ref[...], onehot.astype(BF16)), axis=1, keepdims=True)
    base = starts[N_GROUPS - 1].astype(F32)
    for k in range(N_GROUPS - 2, -1, -1):
        base = jnp.where(gid == float(k), starts[k].astype(F32), base)
    pos = base + rank

    unsort = (pos == lax.broadcasted_iota(jnp.int32, (tm, MOE_SORTED), 1).astype(F32)).astype(BF16)
    digit_hi = jnp.floor(pos * (1.0 / 32.0))
    digits = jnp.where(lane == 0, digit_hi, jnp.where(lane == 1, pos - 32.0 * digit_hi, 0.0))
    sel_lane = lax.broadcasted_iota(jnp.int32, (BF16_ROWS, LANES), 1)
    sel = jnp.where(sel_lane == 0, 32.0, jnp.where(sel_lane == 1, 1.0, 0.0)).astype(BF16)
    pos_row = _bdot_nt(sel, digits.astype(BF16))[0:1, :]
    sort = (lax.broadcasted_iota(jnp.int32, (MOE_SORTED, tm), 0).astype(F32) == pos_row).astype(BF16)

    hs_ref[0:MOE_SORTED] = _bdot(sort, h2).astype(BF16)
    r_hi, r_mid, r_lo = _bf16_pieces(route)
    packed = (r_hi.astype(F32) + pltpu.roll(r_mid.astype(F32), ROUTE_PITCH, axis=1)
              + pltpu.roll(r_lo.astype(F32), 2 * ROUTE_PITCH, axis=1)).astype(BF16)
    rsorted = _bdot(sort, packed)
    rs_ref[0:MOE_SORTED] = (rsorted + pltpu.roll(rsorted, LANES - ROUTE_PITCH, axis=1)
                            + pltpu.roll(rsorted, LANES - 2 * ROUTE_PITCH, axis=1))
    hs_ref[MOE_SORTED:MOE_ROWS] = jnp.zeros((MOE_CHUNK, D_MODEL), BF16)
    rs_ref[MOE_SORTED:MOE_ROWS] = jnp.zeros((MOE_CHUNK, LANES), F32)
    ys_ref[...] = jnp.zeros_like(ys_ref)

    clane = lax.broadcasted_iota(jnp.int32, (MOE_CHUNK, LANES), 1)

    def chunk(g, c):
        r0 = pl.multiple_of(starts[g] + c * MOE_CHUNK, BF16_ROWS)
        hrows = hs_ref[pl.ds(r0, MOE_CHUNK), :]
        rt = rs_ref[pl.ds(r0, MOE_CHUNK), :]
        parts = []
        for e in range(EXPERTS_PER_GROUP):
            ex = g * EXPERTS_PER_GROUP + e
            a = _bdot(hrows, wg_ref[ex])
            u = _bdot(hrows, wu_ref[ex])
            ce = jnp.sum(jnp.where(clane == ex, rt, 0.0), axis=1, keepdims=True)
            parts.append(((a * jax.nn.sigmoid(a)) * u * ce).astype(BF16))
        out = _bdot(jnp.concatenate(parts, axis=1), wd_ref[g])
        ys_ref[pl.ds(r0, MOE_CHUNK), :] = (ys_ref[pl.ds(r0, MOE_CHUNK), :].astype(F32) + out).astype(BF16)

    for g in range(N_GROUPS):
        def more(c, _, g=g):
            chunk(g, c)
            return 0
        lax.fori_loop(1, pl.cdiv(n[g], MOE_CHUNK), more, 0)
    for g in range(N_GROUPS):
        chunk(g, 0)

    x2 = x1 + _bdot(unsort, ys_ref[0:MOE_SORTED])
    h3 = _rms(x2, g3_ref[...]).astype(BF16)
    gate = jax.nn.sigmoid(_bdot(h3, wpg_ref[...]))
    emb = _bdot(p_ref[...].astype(BF16), wple_ref[...])
    o_ref[...] = x2 + gate * emb


def _moe(tcnt, xe, p, g2, wg, wu, wd, g3, wpg, wple, *, tm):
    t = xe.shape[0]
    d = D_MODEL
    once = lambda a: pl.BlockSpec(a.shape, lambda i, c: (0,) * a.ndim, pipeline_mode=pl.Buffered(1))
    row = lambda width: pl.BlockSpec((tm, width), lambda i, c: (i, 0))
    earlier = jnp.asarray(np.tril(np.ones((tm, tm), np.float32), -1), BF16)
    return pl.pallas_call(
        functools.partial(_moe_kernel, tm=tm),
        grid_spec=pltpu.PrefetchScalarGridSpec(
            num_scalar_prefetch=1,
            grid=(t // tm,),
            in_specs=[row(XE_WIDTH), row(PLE_DIM), once(g2), once(wg), once(wu), once(wd), once(g3),
                      once(wpg), once(wple), once(earlier)],
            out_specs=row(d),
            scratch_shapes=[pltpu.VMEM((MOE_ROWS, d), BF16), pltpu.VMEM((MOE_ROWS, LANES), F32),
                            pltpu.VMEM((MOE_ROWS, d), BF16)],
        ),
        out_shape=jax.ShapeDtypeStruct((t, d), F32),
        compiler_params=pltpu.CompilerParams(
            dimension_semantics=("arbitrary",), vmem_limit_bytes=VMEM_LIMIT),
        name="moe",
    )(tcnt, xe, p, g2, wg, wu, wd, g3, wpg, wple, earlier)


def _k_select_matrix():
    pk = np.zeros((LANES, N_HEADS * LANES), np.float32)
    for idx in range(C_PIECES):
        for h in range(N_HEADS):
            pk[idx * N_HEADS + h, h * LANES + KC_LANE + idx] = -1.0
    return jnp.asarray(pk, BF16)


def kernel(x, p, attn_norm_g, w_in, b_f, q_norm_g, k_norm_g, conv_w, w_out_att, w_out_conv, w_o,
           ffn_norm_g, w_rg, b_rg, w_re, b_re, w_gate, w_up, w_down, ple_norm_g, w_pg, w_ple):
    b, s, d = x.shape
    assert d == D_MODEL and conv_w.shape[1:] == (CONV_K, CONV_WIDTH) and s % (2 * TM_MIX) == 0
    t = b * s
    aw = ATT_WIDTH
    for i in range(w_in.shape[0]):
        wt = w_in[i].T
        wf = wt[3 * aw:3 * aw + N_HEADS]
        w_qkvf = jnp.concatenate(
            [wt[:3 * aw], wf, wf, wf, jnp.zeros((LANES - C_PIECES * N_HEADS, d), F32)],
            axis=0).astype(BF16)
        w_cg = wt[3 * aw + N_HEADS:].astype(BF16)
        bf3 = jnp.concatenate([b_f[i]] * C_PIECES + [jnp.zeros((LANES - C_PIECES * N_HEADS,), F32)])[None, :]
        scale = HEAD_DIM ** -0.5 * LOG2E
        qg = jnp.tile(q_norm_g[i] * scale, 2)[None, :]
        kg = jnp.tile(k_norm_g[i], 2)[None, :]
        qt_aug, k_aug, vt = _qkv_proj(x, attn_norm_g[i][None, :], w_qkvf, bf3, qg, kg, _k_select_matrix(),
                                      tm=TM_QKV)
        y_att = _attention(qt_aug, k_aug, vt, tq=TQ_ATT, tk=TK_ATT)

        w_r = jnp.concatenate(
            [w_re[i], w_rg[i], jnp.zeros((d, LANES - N_EXPERTS - N_GROUPS), F32)], axis=1).astype(BF16)
        b_r = jnp.concatenate(
            [b_re[i], b_rg[i], jnp.zeros((LANES - N_EXPERTS - N_GROUPS,), F32)])[None, :]
        g_ffn = ffn_norm_g[i][None, :]
        xe, counts = _mix(x, y_att, attn_norm_g[i][None, :], w_cg, conv_w[i],
                          w_out_att[i].astype(BF16), w_out_conv[i].astype(BF16),
                          w_o[i].astype(BF16), g_ffn, w_r, b_r, tm=TM_MIX)
        tcnt = counts[:, 0, N_EXPERTS:N_EXPERTS + N_GROUPS].astype(jnp.int32).reshape(-1)

        w_dn = w_down[i].reshape(N_GROUPS, GROUP_WIDTH, d).astype(BF16)
        x = _moe(tcnt, xe.reshape(t, XE_WIDTH), p[i].reshape(t, PLE_DIM), g_ffn,
                 w_gate[i].astype(BF16), w_up[i].astype(BF16), w_dn,
                 ple_norm_g[i][None, :], w_pg[i].astype(BF16), w_ple[i].astype(BF16),
                 tm=TM_MOE).reshape(b, s, d)
    return x
```

```python
import functools

import numpy as np
import jax
import jax.numpy as jnp
from jax import lax
from jax.experimental import pallas as pl
from jax.experimental.pallas import tpu as pltpu

D_MODEL = 1024
N_HEADS = 8
HEAD_DIM = 64
ATT_WIDTH = N_HEADS * HEAD_DIM
CONV_WIDTH = 512
CONV_K = 3
N_GROUPS = 4
EXPERTS_PER_GROUP = 4
N_EXPERTS = 16
D_EXPERT = 256
GROUP_WIDTH = EXPERTS_PER_GROUP * D_EXPERT
PLE_DIM = 256
EPS = 1e-6
NEG_INF = -1e30
LOG2E = 1.4426950408889634

LANES = 128
BF16_ROWS = 16
C_PIECES = 3
QC_LANE = HEAD_DIM
KC_LANE = HEAD_DIM + C_PIECES
V_ROWS = HEAD_DIM + BF16_ROWS
QKV_WIDTH = 3 * ATT_WIDTH + LANES
GID_LANE = N_EXPERTS
ROUTE_PITCH = 32
XE_WIDTH = D_MODEL + LANES
VMEM_LIMIT = 56 * 1024 * 1024

TM_QKV = 512
TQ_ATT = 512
TK_ATT = 256
ATT_HEADS = 2
TM_MIX = 512
TM_MOE = TM_MIX
MOE_SORTED = -(-(TM_MOE + N_GROUPS * BF16_ROWS) // LANES) * LANES
MOE_CHUNK = 144
MOE_ROWS = MOE_SORTED + MOE_CHUNK

F32 = jnp.float32
BF16 = jnp.bfloat16


def _rms(xf, g):
    return xf * lax.rsqrt(jnp.mean(xf * xf, axis=-1, keepdims=True) + EPS) * g


def _log_sigmoid(z):
    return jnp.minimum(z, 0.0) - jnp.log1p(jnp.exp(-jnp.abs(z)))


def _bdot(a, b):
    return jnp.dot(a, b, preferred_element_type=F32)


def _bdot_nt(a, b):
    return lax.dot_general(a, b, (((1,), (1,)), ((), ())), preferred_element_type=F32)


def _bf16_pieces(x):
    hi = x.astype(BF16)
    r1 = x - hi.astype(F32)
    mid = r1.astype(BF16)
    lo = (r1 - mid.astype(F32)).astype(BF16)
    return hi, mid, lo


def _qkv_kernel(x0_ref, xa_ref, xb_ref, g_ref, w_ref, bf_ref, qg_ref, kg_ref, pk_ref,
                qt_ref, k_ref, vt_ref, pa_ref, pb_ref, carry_ref, *, tm, tiles_per_seq):
    step = pl.program_id(0)

    def project(x_ref, dst_ref):
        dst_ref[...] = _bdot_nt(_rms(x_ref[...], g_ref[...]).astype(BF16), w_ref[...])

    @pl.when(step == 0)
    def _():
        carry_ref[...] = jnp.zeros_like(carry_ref)
        project(x0_ref, pa_ref)

    lane = lax.broadcasted_iota(jnp.int32, (tm, LANES), 1)
    row = lax.broadcasted_iota(jnp.int32, (tm, LANES), 0)
    k_ones = jnp.where((lane >= QC_LANE) & (lane < QC_LANE + C_PIECES), 1.0, 0.0)
    low = lane < HEAD_DIM
    ext_row = lax.broadcasted_iota(jnp.int32, (V_ROWS - HEAD_DIM, tm), 0)
    v_ext = jnp.where(ext_row == 0, 1.0, 0.0)

    def finish(p_ref, tile, half):
        rows = slice(half * tm, (half + 1) * tm)
        for j in range(N_HEADS // 2):
            pair_t = p_ref[:, 2 * ATT_WIDTH + LANES * j: 2 * ATT_WIDTH + LANES * (j + 1)].T
            for hh in range(2):
                vt_ref[0, 2 * j + hh, :, rows] = jnp.concatenate(
                    [pair_t[HEAD_DIM * hh: HEAD_DIM * (hh + 1)], v_ext], axis=0).astype(BF16)

        f3 = p_ref[:, 3 * ATT_WIDTH:QKV_WIDTH] + bf_ref[...]
        c = jnp.where(lane < C_PIECES * N_HEADS, _log_sigmoid(f3), 0.0)
        sh = 1
        while sh < tm:
            c = c + jnp.where(row >= sh, pltpu.roll(c, sh, axis=0), 0.0)
            sh *= 2
        c = c + jnp.where(tile % tiles_per_seq == 0, 0.0, carry_ref[...])
        carry_ref[...] = c[tm - 1:tm, :]
        hi, mid, lo = _bf16_pieces(c * LOG2E)
        piece = jnp.where(lane < N_HEADS, hi, jnp.where(lane < 2 * N_HEADS, mid, lo))

        def rms_scales(pair):
            sq = pair * pair
            ss_lo = jnp.sum(jnp.where(low, sq, 0.0), axis=1, keepdims=True)
            ss_hi = jnp.sum(jnp.where(low, 0.0, sq), axis=1, keepdims=True)
            return (lax.rsqrt(ss_lo * (1.0 / HEAD_DIM) + EPS), lax.rsqrt(ss_hi * (1.0 / HEAD_DIM) + EPS))

        piece_t = piece.astype(F32).T
        erow = lax.broadcasted_iota(jnp.int32, (8, tm), 0)
        pad_t = jnp.zeros((LANES - HEAD_DIM - 8, tm), F32)
        qgain = qg_ref[...]
        for j in range(N_HEADS // 2):
            pair = p_ref[:, LANES * j: LANES * (j + 1)]
            r_lo, r_hi = rms_scales(pair)
            qn_t = (pair * jnp.where(low, r_lo, r_hi) * qgain).T
            for hh in range(2):
                h = 2 * j + hh
                c_t = jnp.where(erow >= C_PIECES, jnp.where(erow < 2 * C_PIECES, 1.0, 0.0), 0.0)
                for idx in range(C_PIECES):
                    c_t = jnp.where(erow == idx, piece_t[idx * N_HEADS + h: idx * N_HEADS + h + 1], c_t)
                qt_ref[0, h, :, rows] = jnp.concatenate(
                    [qn_t[HEAD_DIM * hh: HEAD_DIM * (hh + 1)], c_t, pad_t], axis=0).astype(BF16)

        kc = _bdot(piece, pk_ref[...])
        kgain = kg_ref[...]
        for j in range(N_HEADS // 2):
            pair = p_ref[:, ATT_WIDTH + LANES * j: ATT_WIDTH + LANES * (j + 1)]
            r_lo, r_hi = rms_scales(pair)
            n_lo = pair * r_lo * kgain
            n_hi = pltpu.roll(pair, HEAD_DIM, axis=1) * r_hi * kgain
            for h, nrm in ((2 * j, n_lo), (2 * j + 1, n_hi)):
                aug = jnp.where(low, nrm, 0.0) + kc[:, LANES * h: LANES * (h + 1)] + k_ones
                k_ref[0, h, rows, :] = aug.astype(BF16)

    finish(pa_ref, 2 * step, 0)
    project(xa_ref, pb_ref)
    finish(pb_ref, 2 * step + 1, 1)
    project(xb_ref, pa_ref)


def _qkv_proj(x, g, w, bf3, qg, kg, pk, *, tm):
    b, s, d = x.shape
    x2 = x.reshape(b * s, d)
    n_tiles = b * s // tm
    per_seq = s // (2 * tm)
    full = lambda shape: pl.BlockSpec(shape, lambda i: (0,) * len(shape))
    return pl.pallas_call(
        functools.partial(_qkv_kernel, tm=tm, tiles_per_seq=s // tm),
        grid=(n_tiles // 2,),
        in_specs=[
            pl.BlockSpec((tm, d), lambda i: (0, 0)),
            pl.BlockSpec((tm, d), lambda i: (2 * i + 1, 0)),
            pl.BlockSpec((tm, d), lambda i: (jnp.minimum(2 * i + 2, n_tiles - 1), 0)),
            full(g.shape), full(w.shape), full(bf3.shape), full(qg.shape), full(kg.shape),
            full(pk.shape),
        ],
        out_specs=[pl.BlockSpec((1, N_HEADS, LANES, 2 * tm), lambda i: (i // per_seq, 0, 0, i % per_seq)),
                   pl.BlockSpec((1, N_HEADS, 2 * tm, LANES), lambda i: (i // per_seq, 0, i % per_seq, 0)),
                   pl.BlockSpec((1, N_HEADS, V_ROWS, 2 * tm), lambda i: (i // per_seq, 0, 0, i % per_seq))],
        out_shape=[
            jax.ShapeDtypeStruct((b, N_HEADS, LANES, s), BF16),
            jax.ShapeDtypeStruct((b, N_HEADS, s, LANES), BF16),
            jax.ShapeDtypeStruct((b, N_HEADS, V_ROWS, s), BF16),
        ],
        scratch_shapes=[pltpu.VMEM((tm, QKV_WIDTH), F32), pltpu.VMEM((tm, QKV_WIDTH), F32),
                        pltpu.VMEM((1, LANES), F32)],
        compiler_params=pltpu.CompilerParams(
            dimension_semantics=("arbitrary",), vmem_limit_bytes=VMEM_LIMIT),
        name="qkv_proj",
    )(x2, x2, x2, g, w, bf3, qg, kg, pk)


def _attn_kernel(qt_ref, qn_ref, k_ref, vt_ref, bias_ref, o_ref, sa_ref, sb_ref, *, tq, tk, n_q):
    qi = pl.program_id(2)
    n_sub = tq // tk
    bufs = (sa_ref, sb_ref)

    def produce(j, s_ref, lo=0, q_ref=qt_ref):
        start = j * tk
        for hh in range(ATT_HEADS):
            s_ref[hh, :, lo:tq] = _bdot(k_ref[0, hh, pl.ds(start, tk), :], q_ref[0, hh, :, lo:tq])

    def consume(j, s_ref, carry, lo=0, masked=False):
        start = j * tk
        out = []
        for hh in range(ATT_HEADS):
            m, acc = carry[hh]

            def read():
                if not masked:
                    return s_ref[hh, :, lo:tq]
                tri = s_ref[hh, :, lo:lo + tk] + bias_ref[...]
                return tri if lo + tk == tq else jnp.concatenate([tri, s_ref[hh, :, lo + tk:tq]], axis=1)

            m_new = jnp.maximum(m[:, lo:tq], jnp.max(read(), axis=0, keepdims=True))
            alpha = jnp.exp2(m[:, lo:tq] - m_new)
            p = jnp.exp2(read() - m_new).astype(BF16)
            acc_new = alpha * acc[:, lo:tq] + _bdot(vt_ref[0, hh, :, pl.ds(start, tk)], p)
            if lo:
                m_new = jnp.concatenate([m[:, 0:lo], m_new], axis=1)
                acc_new = jnp.concatenate([acc[:, 0:lo], acc_new], axis=1)
            out.append((m_new, acc_new))
        return tuple(out)

    def trip(r, carry):
        for t in range(n_sub):
            j = r * n_sub + t
            produce(j + 1, bufs[(t + 1) % 2])
            carry = consume(j, bufs[t % 2], carry)
        return carry

    def query_tile(q):
        if q == 0:
            produce(0, sa_ref)
        carry = tuple((jnp.full((1, tq), NEG_INF, F32), jnp.zeros((V_ROWS, tq), F32))
                      for _ in range(ATT_HEADS))
        for r in range(q):
            carry = trip(r, carry)
        for t in range(n_sub):
            j = q * n_sub + t
            if t + 1 < n_sub:
                produce(j + 1, bufs[(t + 1) % 2], lo=(t + 1) * tk)
            elif q + 1 < n_q:
                produce(0, bufs[(t + 1) % 2], q_ref=qn_ref)
            carry = consume(j, bufs[t % 2], carry, lo=t * tk, masked=True)
        y_t = jnp.concatenate([acc[0:HEAD_DIM] / acc[HEAD_DIM:HEAD_DIM + 1] for _, acc in carry], axis=0)
        o_ref[0] = y_t.T.astype(BF16)

    for q in range(n_q):
        pl.when(qi == q)(functools.partial(query_tile, q))


def _attention(qt_aug, k_aug, vt, *, tq, tk):
    b, nh, s, _ = k_aug.shape
    assert tq % (2 * tk) == 0 and s % tq == 0
    future = np.arange(tk)[:, None] > np.arange(tk)[None, :]
    bias = jnp.asarray(np.where(future, np.float32(NEG_INF), np.float32(0.0)))
    nh_step = ATT_HEADS
    grid = (b, nh // nh_step, s // tq)
    return pl.pallas_call(
        functools.partial(_attn_kernel, tq=tq, tk=tk, n_q=s // tq),
        grid=grid,
        in_specs=[
            pl.BlockSpec((1, nh_step, LANES, tq), lambda i, h, j: (i, h, 0, j)),
            pl.BlockSpec((1, nh_step, LANES, tq), lambda i, h, j: (i, h, 0, jnp.minimum(j + 1, s // tq - 1))),
            pl.BlockSpec((1, nh_step, s, LANES), lambda i, h, j: (i, h, 0, 0)),
            pl.BlockSpec((1, nh_step, V_ROWS, s), lambda i, h, j: (i, h, 0, 0)),
            pl.BlockSpec((tk, tk), lambda i, h, j: (0, 0)),
        ],
        out_specs=pl.BlockSpec((1, tq, nh_step * HEAD_DIM), lambda i, h, j: (i, j, h)),
        out_shape=jax.ShapeDtypeStruct((b, s, ATT_WIDTH), BF16),
        scratch_shapes=[pltpu.VMEM((nh_step, tk, tq), F32)] * 2,
        compiler_params=pltpu.CompilerParams(
            dimension_semantics=("arbitrary", "arbitrary", "arbitrary"),
            vmem_limit_bytes=VMEM_LIMIT),
        name="fox_attention",
    )(qt_aug, qt_aug, k_aug, vt, bias)


def _mix_kernel(x0_ref, xa_ref, xb_ref, y0_ref, ya_ref, yb_ref, g_ref, wcg_ref, cw_ref, wa_ref, wb_ref,
                wo_ref, g2_ref, wr_ref, br_ref, xe_ref, cnt_ref, ra_ref, rb_ref, carry_ref, wcs_ref,
                *, tm, tiles_per_seq):
    step = pl.program_id(0)

    def residual(x_ref, y_ref, tile, dst_ref):
        x = x_ref[...]
        h = _rms(x, g_ref[...]).astype(BF16)
        pc = _bdot(h, wcs_ref[...])
        cw = CONV_WIDTH
        cb = pc[:, 0:cw]
        prod = pc[:, cw:2 * cw] * pc[:, 2 * cw:3 * cw]
        ga = pc[:, 3 * cw:3 * cw + D_MODEL]
        gb = pc[:, 3 * cw + D_MODEL:3 * cw + 2 * D_MODEL]

        prev = jnp.where(tile % tiles_per_seq == 0, 0.0, carry_ref[...])
        crow = lax.broadcasted_iota(jnp.int32, (tm, cw), 0)
        m1 = jnp.where(crow == 0, prev[7:8, :], pltpu.roll(prod, 1, axis=0))
        m2 = jnp.where(crow == 0, prev[6:7, :],
                       jnp.where(crow == 1, prev[7:8, :], pltpu.roll(prod, 2, axis=0)))
        carry_ref[...] = prod[tm - 8:tm, :]
        w = cw_ref[...]
        y_conv = cb * (w[0:1, :] * m2 + w[1:2, :] * m1 + w[2:3, :] * prod)

        a = _bdot(y_ref[...], wa_ref[...])
        bb = _bdot(y_conv.astype(BF16), wb_ref[...])
        merged = jax.nn.sigmoid(ga) * a + jax.nn.sigmoid(gb) * bb
        dst_ref[...] = x + _bdot(merged.astype(BF16), wo_ref[...])

    @pl.when(step == 0)
    def _():
        carry_ref[...] = jnp.zeros_like(carry_ref)
        for r in range(0, wcs_ref.shape[1], tm):
            wcs_ref[:, r:r + tm] = wcg_ref[r:r + tm, :].astype(F32).T.astype(BF16)
        residual(x0_ref, y0_ref, 0, ra_ref)

    _route(ra_ref, g2_ref, wr_ref, br_ref, xe_ref, cnt_ref, 0, tm)
    residual(xa_ref, ya_ref, 2 * step + 1, rb_ref)
    _route(rb_ref, g2_ref, wr_ref, br_ref, xe_ref, cnt_ref, 1, tm)
    residual(xb_ref, yb_ref, 2 * step + 2, ra_ref)


def _route(x1_ref, g2_ref, wr_ref, br_ref, xe_ref, cnt_ref, half, tm):
    rows = slice(half * tm, (half + 1) * tm)
    x1 = x1_ref[...]
    xe_ref[rows, 0:D_MODEL] = x1
    h2 = _rms(x1, g2_ref[...]).astype(BF16)

    logits = _bdot(h2, wr_ref[...]) + br_ref[...]
    lane = lax.broadcasted_iota(jnp.int32, (tm, LANES), 1)
    lanef = lane.astype(F32)
    is_g = (lane >= N_EXPERTS) & (lane < N_EXPERTS + N_GROUPS)
    gl = jnp.where(is_g, logits, NEG_INF)
    gmax = jnp.max(gl, axis=1, keepdims=True)
    gsum = jnp.sum(jnp.exp(gl - gmax), axis=1, keepdims=True)
    g_val = 1.0 / gsum
    g_lane = jnp.min(jnp.where(gl == gmax, lanef, float(LANES)), axis=1, keepdims=True)
    e_lo = (g_lane - float(N_EXPERTS)) * float(EXPERTS_PER_GROUP)
    in_grp = (lanef >= e_lo) & (lanef < e_lo + float(EXPERTS_PER_GROUP))
    el = jnp.where(in_grp, logits, NEG_INF)
    t1 = jnp.max(el, axis=1, keepdims=True)
    i1 = jnp.min(jnp.where(el == t1, lanef, float(LANES)), axis=1, keepdims=True)
    el2 = jnp.where(lanef == i1, NEG_INF, el)
    t2 = jnp.max(el2, axis=1, keepdims=True)
    i2 = jnp.min(jnp.where(el2 == t2, lanef, float(LANES)), axis=1, keepdims=True)
    e2 = jnp.exp(t2 - t1)
    w1 = g_val / (1.0 + e2)
    w2 = g_val * e2 / (1.0 + e2)
    comb = jnp.where(lanef == i1, w1, 0.0) + jnp.where(lanef == i2, w2, 0.0)

    cnt_ref[half] = jnp.sum(jnp.where(is_g & (lanef == g_lane), 1.0, 0.0), axis=0, keepdims=True)
    xe_ref[rows, D_MODEL:XE_WIDTH] = comb + jnp.where(lane == GID_LANE, g_lane - float(N_EXPERTS), 0.0)


def _mix(x, y_att, g, wcg, conv_w, wa, wb, wo, g2, wr, br, *, tm):
    b, s, d = x.shape
    t = b * s
    n_tiles = t // tm
    x2 = x.reshape(t, d)
    y2 = y_att.reshape(t, ATT_WIDTH)
    once = lambda a: pl.BlockSpec(a.shape, lambda i: (0,) * a.ndim, pipeline_mode=pl.Buffered(1))

    def tiles(width):
        nxt = lambda i: (jnp.minimum(2 * i + 2, n_tiles - 1), 0)
        return [pl.BlockSpec((tm, width), lambda i: (0, 0), pipeline_mode=pl.Buffered(1)),
                pl.BlockSpec((tm, width), lambda i: (2 * i + 1, 0)),
                pl.BlockSpec((tm, width), nxt)]

    return pl.pallas_call(
        functools.partial(_mix_kernel, tm=tm, tiles_per_seq=s // tm),
        grid=(n_tiles // 2,),
        in_specs=tiles(d) + tiles(ATT_WIDTH) + [once(g), once(wcg), once(conv_w), once(wa), once(wb),
                                                once(wo), once(g2), once(wr), once(br)],
        out_specs=[pl.BlockSpec((2 * tm, XE_WIDTH), lambda i: (i, 0)),
                   pl.BlockSpec((2, 1, LANES), lambda i: (i, 0, 0))],
        out_shape=[
            jax.ShapeDtypeStruct((t, XE_WIDTH), F32),
            jax.ShapeDtypeStruct((n_tiles, 1, LANES), F32),
        ],
        scratch_shapes=[pltpu.VMEM((tm, d), F32), pltpu.VMEM((tm, d), F32),
                        pltpu.VMEM((8, CONV_WIDTH), F32),
                        pltpu.VMEM(wcg.shape[::-1], BF16)],
        compiler_params=pltpu.CompilerParams(
            dimension_semantics=("arbitrary",), vmem_limit_bytes=VMEM_LIMIT),
        name="mix",
    )(x2, x2, x2, y2, y2, y2, g, wcg, conv_w, wa, wb, wo, g2, wr, br)


def _moe_kernel(tcnt_ref, xe_ref, p_ref, g2_ref, wg_ref, wu_ref, wd_ref, g3_ref, wpg_ref, wple_ref,
                earlier_ref, o_ref, hs_ref, rs_ref, ys_ref, *, tm):
    i = pl.program_id(0)
    n = [tcnt_ref[i * N_GROUPS + k] for k in range(N_GROUPS)]
    starts = [jnp.int32(0)]
    for k in range(N_GROUPS - 1):
        starts.append(starts[-1] + pl.cdiv(n[k], BF16_ROWS) * BF16_ROWS)

    x1 = xe_ref[:, 0:D_MODEL]
    route = xe_ref[:, D_MODEL:XE_WIDTH]
    h2 = _rms(x1, g2_ref[...]).astype(BF16)

    lane = lax.broadcasted_iota(jnp.int32, (tm, LANES), 1)
    lanef = lane.astype(F32)
    gid = jnp.sum(jnp.where(lane == GID_LANE, route, 0.0), axis=1, keepdims=True)
    onehot = jnp.where((lanef == gid) & (lane < N_GROUPS), 1.0, 0.0)
    rank = jnp.sum(onehot * _bdot(earlier_ref[...], onehot.astype(BF16)), axis=1, keepdims=True)
    base = starts[N_GROUPS - 1].astype(F32)
    for k in range(N_GROUPS - 2, -1, -1):
        base = jnp.where(gid == float(k), starts[k].astype(F32), base)
    pos = base + rank

    unsort = (pos == lax.broadcasted_iota(jnp.int32, (tm, MOE_SORTED), 1).astype(F32)).astype(BF16)
    digit_hi = jnp.floor(pos * (1.0 / 32.0))
    digits = jnp.where(lane == 0, digit_hi, jnp.where(lane == 1, pos - 32.0 * digit_hi, 0.0))
    sel_lane = lax.broadcasted_iota(jnp.int32, (BF16_ROWS, LANES), 1)
    sel = jnp.where(sel_lane == 0, 32.0, jnp.where(sel_lane == 1, 1.0, 0.0)).astype(BF16)
    pos_row = _bdot_nt(sel, digits.astype(BF16))[0:1, :]
    sort = (lax.broadcasted_iota(jnp.int32, (MOE_SORTED, tm), 0).astype(F32) == pos_row).astype(BF16)

    hs_ref[0:MOE_SORTED] = _bdot(sort, h2).astype(BF16)
    r_hi, r_mid, r_lo = _bf16_pieces(route)
    packed = (r_hi.astype(F32) + pltpu.roll(r_mid.astype(F32), ROUTE_PITCH, axis=1)
              + pltpu.roll(r_lo.astype(F32), 2 * ROUTE_PITCH, axis=1)).astype(BF16)
    rsorted = _bdot(sort, packed)
    rs_ref[0:MOE_SORTED] = (rsorted + pltpu.roll(rsorted, LANES - ROUTE_PITCH, axis=1)
                            + pltpu.roll(rsorted, LANES - 2 * ROUTE_PITCH, axis=1))
    hs_ref[MOE_SORTED:MOE_ROWS] = jnp.zeros((MOE_CHUNK, D_MODEL), BF16)
    rs_ref[MOE_SORTED:MOE_ROWS] = jnp.zeros((MOE_CHUNK, LANES), F32)
    ys_ref[...] = jnp.zeros_like(ys_ref)

    clane = lax.broadcasted_iota(jnp.int32, (MOE_CHUNK, LANES), 1)

    def chunk(g, c):
        r0 = pl.multiple_of(starts[g] + c * MOE_CHUNK, BF16_ROWS)
        hrows = hs_ref[pl.ds(r0, MOE_CHUNK), :]
        rt = rs_ref[pl.ds(r0, MOE_CHUNK), :]
        parts = []
        for e in range(EXPERTS_PER_GROUP):
            ex = g * EXPERTS_PER_GROUP + e
            a = _bdot(hrows, wg_ref[ex])
            u = _bdot(hrows, wu_ref[ex])
            ce = jnp.sum(jnp.where(clane == ex, rt, 0.0), axis=1, keepdims=True)
            parts.append(((a * jax.nn.sigmoid(a)) * u * ce).astype(BF16))
        out = _bdot(jnp.concatenate(parts, axis=1), wd_ref[g])
        ys_ref[pl.ds(r0, MOE_CHUNK), :] = (ys_ref[pl.ds(r0, MOE_CHUNK), :].astype(F32) + out).astype(BF16)

    for g in range(N_GROUPS):
        def more(c, _, g=g):
            chunk(g, c)
            return 0
        lax.fori_loop(1, pl.cdiv(n[g], MOE_CHUNK), more, 0)
    for g in range(N_GROUPS):
        chunk(g, 0)

    x2 = x1 + _bdot(unsort, ys_ref[0:MOE_SORTED])
    h3 = _rms(x2, g3_ref[...]).astype(BF16)
    gate = jax.nn.sigmoid(_bdot(h3, wpg_ref[...]))
    emb = _bdot(p_ref[...].astype(BF16), wple_ref[...])
    o_ref[...] = x2 + gate * emb


def _moe(tcnt, xe, p, g2, wg, wu, wd, g3, wpg, wple, *, tm):
    t = xe.shape[0]
    d = D_MODEL
    once = lambda a: pl.BlockSpec(a.shape, lambda i, c: (0,) * a.ndim, pipeline_mode=pl.Buffered(1))
    row = lambda width: pl.BlockSpec((tm, width), lambda i, c: (i, 0))
    earlier = jnp.asarray(np.tril(np.ones((tm, tm), np.float32), -1), BF16)
    return pl.pallas_call(
        functools.partial(_moe_kernel, tm=tm),
        grid_spec=pltpu.PrefetchScalarGridSpec(
            num_scalar_prefetch=1,
            grid=(t // tm,),
            in_specs=[row(XE_WIDTH), row(PLE_DIM), once(g2), once(wg), once(wu), once(wd), once(g3),
                      once(wpg), once(wple), once(earlier)],
            out_specs=row(d),
            scratch_shapes=[pltpu.VMEM((MOE_ROWS, d), BF16), pltpu.VMEM((MOE_ROWS, LANES), F32),
                            pltpu.VMEM((MOE_ROWS, d), BF16)],
        ),
        out_shape=jax.ShapeDtypeStruct((t, d), F32),
        compiler_params=pltpu.CompilerParams(
            dimension_semantics=("arbitrary",), vmem_limit_bytes=VMEM_LIMIT),
        name="moe",
    )(tcnt, xe, p, g2, wg, wu, wd, g3, wpg, wple, earlier)


def _k_select_matrix():
    pk = np.zeros((LANES, N_HEADS * LANES), np.float32)
    for idx in range(C_PIECES):
        for h in range(N_HEADS):
            pk[idx * N_HEADS + h, h * LANES + KC_LANE + idx] = -1.0
    return jnp.asarray(pk, BF16)


def kernel(x, p, attn_norm_g, w_in, b_f, q_norm_g, k_norm_g, conv_w, w_out_att, w_out_conv, w_o,
           ffn_norm_g, w_rg, b_rg, w_re, b_re, w_gate, w_up, w_down, ple_norm_g, w_pg, w_ple):
    b, s, d = x.shape
    assert d == D_MODEL and conv_w.shape[1:] == (CONV_K, CONV_WIDTH) and s % (2 * TM_MIX) == 0
    t = b * s
    aw = ATT_WIDTH
    for i in range(w_in.shape[0]):
        wt = w_in[i].T
        wf = wt[3 * aw:3 * aw + N_HEADS]
        w_qkvf = jnp.concatenate(
            [wt[:3 * aw], wf, wf, wf, jnp.zeros((LANES - C_PIECES * N_HEADS, d), F32)],
            axis=0).astype(BF16)
        w_cg = wt[3 * aw + N_HEADS:].astype(BF16)
        bf3 = jnp.concatenate([b_f[i]] * C_PIECES + [jnp.zeros((LANES - C_PIECES * N_HEADS,), F32)])[None, :]
        scale = HEAD_DIM ** -0.5 * LOG2E
        qg = jnp.tile(q_norm_g[i] * scale, 2)[None, :]
        kg = jnp.tile(k_norm_g[i], 2)[None, :]
        qt_aug, k_aug, vt = _qkv_proj(x, attn_norm_g[i][None, :], w_qkvf, bf3, qg, kg, _k_select_matrix(),
                                      tm=TM_QKV)
        y_att = _attention(qt_aug, k_aug, vt, tq=TQ_ATT, tk=TK_ATT)

        w_r = jnp.concatenate(
            [w_re[i], w_rg[i], jnp.zeros((d, LANES - N_EXPERTS - N_GROUPS), F32)], axis=1).astype(BF16)
        b_r = jnp.concatenate(
            [b_re[i], b_rg[i], jnp.zeros((LANES - N_EXPERTS - N_GROUPS,), F32)])[None, :]
        g_ffn = ffn_norm_g[i][None, :]
        xe, counts = _mix(x, y_att, attn_norm_g[i][None, :], w_cg, conv_w[i],
                          w_out_att[i].astype(BF16), w_out_conv[i].astype(BF16),
                          w_o[i].astype(BF16), g_ffn, w_r, b_r, tm=TM_MIX)
        tcnt = counts[:, 0, N_EXPERTS:N_EXPERTS + N_GROUPS].astype(jnp.int32).reshape(-1)

        w_dn = w_down[i].reshape(N_GROUPS, GROUP_WIDTH, d).astype(BF16)
        x = _moe(tcnt, xe.reshape(t, XE_WIDTH), p[i].reshape(t, PLE_DIM), g_ffn,
                 w_gate[i].astype(BF16), w_up[i].astype(BF16), w_dn,
                 ple_norm_g[i][None, :], w_pg[i].astype(BF16), w_ple[i].astype(BF16),
                 tm=TM_MOE).reshape(b, s, d)
    return x
```

```python
import functools

import numpy as np
import jax
import jax.numpy as jnp
from jax import lax
from jax.experimental import pallas as pl
from jax.experimental.pallas import tpu as pltpu

D_MODEL = 1024
N_HEADS = 8
HEAD_DIM = 64
ATT_WIDTH = N_HEADS * HEAD_DIM
CONV_WIDTH = 512
CONV_K = 3
N_GROUPS = 4
EXPERTS_PER_GROUP = 4
N_EXPERTS = 16
D_EXPERT = 256
GROUP_WIDTH = EXPERTS_PER_GROUP * D_EXPERT
PLE_DIM = 256
EPS = 1e-6
NEG_INF = -1e30
LOG2E = 1.4426950408889634

LANES = 128
BF16_ROWS = 16
C_PIECES = 3
QC_LANE = HEAD_DIM
KC_LANE = HEAD_DIM + C_PIECES
V_ROWS = HEAD_DIM + BF16_ROWS
QKV_WIDTH = 3 * ATT_WIDTH + LANES
GID_LANE = N_EXPERTS
ROUTE_PITCH = 32
XE_WIDTH = D_MODEL + LANES
VMEM_LIMIT = 56 * 1024 * 1024

TM_QKV = 512
TQ_ATT = 512
TK_ATT = 256
ATT_HEADS = 2
TM_MIX = 512
TM_MOE = TM_MIX
MOE_SORTED = -(-(TM_MOE + N_GROUPS * BF16_ROWS) // LANES) * LANES
MOE_CHUNK = 144
MOE_ROWS = MOE_SORTED + MOE_CHUNK

F32 = jnp.float32
BF16 = jnp.bfloat16


def _rms(xf, g):
    return xf * lax.rsqrt(jnp.mean(xf * xf, axis=-1, keepdims=True) + EPS) * g


def _log_sigmoid(z):
    return jnp.minimum(z, 0.0) - jnp.log1p(jnp.exp(-jnp.abs(z)))


def _bdot(a, b):
    return jnp.dot(a, b, preferred_element_type=F32)


def _bdot_nt(a, b):
    return lax.dot_general(a, b, (((1,), (1,)), ((), ())), preferred_element_type=F32)


def _bf16_pieces(x):
    hi = x.astype(BF16)
    r1 = x - hi.astype(F32)
    mid = r1.astype(BF16)
    lo = (r1 - mid.astype(F32)).astype(BF16)
    return hi, mid, lo


def _qkv_kernel(x0_ref, xa_ref, xb_ref, g_ref, w_ref, bf_ref, qg_ref, kg_ref, pk_ref,
                qt_ref, k_ref, vt_ref, pa_ref, pb_ref, carry_ref, *, tm, tiles_per_seq):
    step = pl.program_id(0)

    def project(x_ref, dst_ref):
        dst_ref[...] = _bdot_nt(_rms(x_ref[...], g_ref[...]).astype(BF16), w_ref[...])

    @pl.when(step == 0)
    def _():
        carry_ref[...] = jnp.zeros_like(carry_ref)
        project(x0_ref, pa_ref)

    lane = lax.broadcasted_iota(jnp.int32, (tm, LANES), 1)
    row = lax.broadcasted_iota(jnp.int32, (tm, LANES), 0)
    k_ones = jnp.where((lane >= QC_LANE) & (lane < QC_LANE + C_PIECES), 1.0, 0.0)
    low = lane < HEAD_DIM
    ext_row = lax.broadcasted_iota(jnp.int32, (V_ROWS - HEAD_DIM, tm), 0)
    v_ext = jnp.where(ext_row == 0, 1.0, 0.0)

    def finish(p_ref, tile, half):
        rows = slice(half * tm, (half + 1) * tm)
        for j in range(N_HEADS // 2):
            pair_t = p_ref[:, 2 * ATT_WIDTH + LANES * j: 2 * ATT_WIDTH + LANES * (j + 1)].T
            for hh in range(2):
                vt_ref[0, 2 * j + hh, :, rows] = jnp.concatenate(
                    [pair_t[HEAD_DIM * hh: HEAD_DIM * (hh + 1)], v_ext], axis=0).astype(BF16)

        f3 = p_ref[:, 3 * ATT_WIDTH:QKV_WIDTH] + bf_ref[...]
        c = jnp.where(lane < C_PIECES * N_HEADS, _log_sigmoid(f3), 0.0)
        sh = 1
        while sh < tm:
            c = c + jnp.where(row >= sh, pltpu.roll(c, sh, axis=0), 0.0)
            sh *= 2
        c = c + jnp.where(tile % tiles_per_seq == 0, 0.0, carry_ref[...])
        carry_ref[...] = c[tm - 1:tm, :]
        hi, mid, lo = _bf16_pieces(c * LOG2E)
        piece = jnp.where(lane < N_HEADS, hi, jnp.where(lane < 2 * N_HEADS, mid, lo))

        def rms_scales(pair):
            sq = pair * pair
            ss_lo = jnp.sum(jnp.where(low, sq, 0.0), axis=1, keepdims=True)
            ss_hi = jnp.sum(jnp.where(low, 0.0, sq), axis=1, keepdims=True)
            return (lax.rsqrt(ss_lo * (1.0 / HEAD_DIM) + EPS), lax.rsqrt(ss_hi * (1.0 / HEAD_DIM) + EPS))

        piece_t = piece.astype(F32).T
        erow = lax.broadcasted_iota(jnp.int32, (8, tm), 0)
        pad_t = jnp.zeros((LANES - HEAD_DIM - 8, tm), F32)
        qgain = qg_ref[...]
        for j in range(N_HEADS // 2):
            pair = p_ref[:, LANES * j: LANES * (j + 1)]
            r_lo, r_hi = rms_scales(pair)
            qn_t = (pair * jnp.where(low, r_lo, r_hi) * qgain).T
            for hh in range(2):
                h = 2 * j + hh
                c_t = jnp.where(erow >= C_PIECES, jnp.where(erow < 2 * C_PIECES, 1.0, 0.0), 0.0)
                for idx in range(C_PIECES):
                    c_t = jnp.where(erow == idx, piece_t[idx * N_HEADS + h: idx * N_HEADS + h + 1], c_t)
                qt_ref[0, h, :, rows] = jnp.concatenate(
                    [qn_t[HEAD_DIM * hh: HEAD_DIM * (hh + 1)], c_t, pad_t], axis=0).astype(BF16)

        kc = _bdot(piece, pk_ref[...])
        kgain = kg_ref[...]
        for j in range(N_HEADS // 2):
            pair = p_ref[:, ATT_WIDTH + LANES * j: ATT_WIDTH + LANES * (j + 1)]
            r_lo, r_hi = rms_scales(pair)
            n_lo = pair * r_lo * kgain
            n_hi = pltpu.roll(pair, HEAD_DIM, axis=1) * r_hi * kgain
            for h, nrm in ((2 * j, n_lo), (2 * j + 1, n_hi)):
                aug = jnp.where(low, nrm, 0.0) + kc[:, LANES * h: LANES * (h + 1)] + k_ones
                k_ref[0, h, rows, :] = aug.astype(BF16)

    finish(pa_ref, 2 * step, 0)
    project(xa_ref, pb_ref)
    finish(pb_ref, 2 * step + 1, 1)
    project(xb_ref, pa_ref)


def _qkv_proj(x, g, w, bf3, qg, kg, pk, *, tm):
    b, s, d = x.shape
    x2 = x.reshape(b * s, d)
    n_tiles = b * s // tm
    per_seq = s // (2 * tm)
    full = lambda shape: pl.BlockSpec(shape, lambda i: (0,) * len(shape))
    return pl.pallas_call(
        functools.partial(_qkv_kernel, tm=tm, tiles_per_seq=s // tm),
        grid=(n_tiles // 2,),
        in_specs=[
            pl.BlockSpec((tm, d), lambda i: (0, 0)),
            pl.BlockSpec((tm, d), lambda i: (2 * i + 1, 0)),
            pl.BlockSpec((tm, d), lambda i: (jnp.minimum(2 * i + 2, n_tiles - 1), 0)),
            full(g.shape), full(w.shape), full(bf3.shape), full(qg.shape), full(kg.shape),
            full(pk.shape),
        ],
        out_specs=[pl.BlockSpec((1, N_HEADS, LANES, 2 * tm), lambda i: (i // per_seq, 0, 0, i % per_seq)),
                   pl.BlockSpec((1, N_HEADS, 2 * tm, LANES), lambda i: (i // per_seq, 0, i % per_seq, 0)),
                   pl.BlockSpec((1, N_HEADS, V_ROWS, 2 * tm), lambda i: (i // per_seq, 0, 0, i % per_seq))],
        out_shape=[
            jax.ShapeDtypeStruct((b, N_HEADS, LANES, s), BF16),
            jax.ShapeDtypeStruct((b, N_HEADS, s, LANES), BF16),
            jax.ShapeDtypeStruct((b, N_HEADS, V_ROWS, s), BF16),
        ],
        scratch_shapes=[pltpu.VMEM((tm, QKV_WIDTH), F32), pltpu.VMEM((tm, QKV_WIDTH), F32),
                        pltpu.VMEM((1, LANES), F32)],
        compiler_params=pltpu.CompilerParams(
            dimension_semantics=("arbitrary",), vmem_limit_bytes=VMEM_LIMIT),
        name="qkv_proj",
    )(x2, x2, x2, g, w, bf3, qg, kg, pk)


def _attn_kernel(qt_ref, qn_ref, k_ref, vt_ref, bias_ref, o_ref, sa_ref, sb_ref, *, tq, tk, n_q):
    qi = pl.program_id(2)
    n_sub = tq // tk
    bufs = (sa_ref, sb_ref)

    def produce(j, s_ref, lo=0, q_ref=qt_ref):
        start = j * tk
        for hh in range(ATT_HEADS):
            s_ref[hh, :, lo:tq] = _bdot(k_ref[0, hh, pl.ds(start, tk), :], q_ref[0, hh, :, lo:tq])

    def consume(j, s_ref, carry, lo=0, masked=False):
        start = j * tk
        out = []
        for hh in range(ATT_HEADS):
            m, acc = carry[hh]

            def read():
                if not masked:
                    return s_ref[hh, :, lo:tq]
                tri = s_ref[hh, :, lo:lo + tk] + bias_ref[...]
                return tri if lo + tk == tq else jnp.concatenate([tri, s_ref[hh, :, lo + tk:tq]], axis=1)

            m_new = jnp.maximum(m[:, lo:tq], jnp.max(read(), axis=0, keepdims=True))
            alpha = jnp.exp2(m[:, lo:tq] - m_new)
            p = jnp.exp2(read() - m_new).astype(BF16)
            acc_new = alpha * acc[:, lo:tq] + _bdot(vt_ref[0, hh, :, pl.ds(start, tk)], p)
            if lo:
                m_new = jnp.concatenate([m[:, 0:lo], m_new], axis=1)
                acc_new = jnp.concatenate([acc[:, 0:lo], acc_new], axis=1)
            out.append((m_new, acc_new))
        return tuple(out)

    def trip(r, carry):
        for t in range(n_sub):
            j = r * n_sub + t
            produce(j + 1, bufs[(t + 1) % 2])
            carry = consume(j, bufs[t % 2], carry)
        return carry

    def query_tile(q):
        if q == 0:
            produce(0, sa_ref)
        carry = tuple((jnp.full((1, tq), NEG_INF, F32), jnp.zeros((V_ROWS, tq), F32))
                      for _ in range(ATT_HEADS))
        for r in range(q):
            carry = trip(r, carry)
        for t in range(n_sub):
            j = q * n_sub + t
            if t + 1 < n_sub:
                produce(j + 1, bufs[(t + 1) % 2], lo=(t + 1) * tk)
            elif q + 1 < n_q:
                produce(0, bufs[(t + 1) % 2], q_ref=qn_ref)
            carry = consume(j, bufs[t % 2], carry, lo=t * tk, masked=True)
        y_t = jnp.concatenate([acc[0:HEAD_DIM] / acc[HEAD_DIM:HEAD_DIM + 1] for _, acc in carry], axis=0)
        o_ref[0] = y_t.T.astype(BF16)

    for q in range(n_q):
        pl.when(qi == q)(functools.partial(query_tile, q))


def _attention(qt_aug, k_aug, vt, *, tq, tk):
    b, nh, s, _ = k_aug.shape
    assert tq % (2 * tk) == 0 and s % tq == 0
    future = np.arange(tk)[:, None] > np.arange(tk)[None, :]
    bias = jnp.asarray(np.where(future, np.float32(NEG_INF), np.float32(0.0)))
    nh_step = ATT_HEADS
    grid = (b, nh // nh_step, s // tq)
    return pl.pallas_call(
        functools.partial(_attn_kernel, tq=tq, tk=tk, n_q=s // tq),
        grid=grid,
        in_specs=[
            pl.BlockSpec((1, nh_step, LANES, tq), lambda i, h, j: (i, h, 0, j)),
            pl.BlockSpec((1, nh_step, LANES, tq), lambda i, h, j: (i, h, 0, jnp.minimum(j + 1, s // tq - 1))),
            pl.BlockSpec((1, nh_step, s, LANES), lambda i, h, j: (i, h, 0, 0)),
            pl.BlockSpec((1, nh_step, V_ROWS, s), lambda i, h, j: (i, h, 0, 0)),
            pl.BlockSpec((tk, tk), lambda i, h, j: (0, 0)),
        ],
        out_specs=pl.BlockSpec((1, tq, nh_step * HEAD_DIM), lambda i, h, j: (i, j, h)),
        out_shape=jax.ShapeDtypeStruct((b, s, ATT_WIDTH), BF16),
        scratch_shapes=[pltpu.VMEM((nh_step, tk, tq), F32)] * 2,
        compiler_params=pltpu.CompilerParams(
            dimension_semantics=("arbitrary", "arbitrary", "arbitrary"),
            vmem_limit_bytes=VMEM_LIMIT),
        name="fox_attention",
    )(qt_aug, qt_aug, k_aug, vt, bias)


def _mix_kernel(x0_ref, xa_ref, xb_ref, y0_ref, ya_ref, yb_ref, g_ref, wcg_ref, cw_ref, wa_ref, wb_ref,
                wo_ref, g2_ref, wr_ref, br_ref, xe_ref, cnt_ref, ra_ref, rb_ref, carry_ref,
                *, tm, tiles_per_seq):
    step = pl.program_id(0)

    def residual(x_ref, y_ref, tile, dst_ref):
        x = x_ref[...]
        h = _rms(x, g_ref[...]).astype(BF16)
        pc = _bdot(h, wcg_ref[...])
        cw = CONV_WIDTH
        cb = pc[:, 0:cw]
        prod = pc[:, cw:2 * cw] * pc[:, 2 * cw:3 * cw]
        ga = pc[:, 3 * cw:3 * cw + D_MODEL]
        gb = pc[:, 3 * cw + D_MODEL:3 * cw + 2 * D_MODEL]

        prev = jnp.where(tile % tiles_per_seq == 0, 0.0, carry_ref[...])
        crow = lax.broadcasted_iota(jnp.int32, (tm, cw), 0)
        m1 = jnp.where(crow == 0, prev[7:8, :], pltpu.roll(prod, 1, axis=0))
        m2 = jnp.where(crow == 0, prev[6:7, :],
                       jnp.where(crow == 1, prev[7:8, :], pltpu.roll(prod, 2, axis=0)))
        carry_ref[...] = prod[tm - 8:tm, :]
        w = cw_ref[...]
        y_conv = cb * (w[0:1, :] * m2 + w[1:2, :] * m1 + w[2:3, :] * prod)

        a = _bdot(y_ref[...], wa_ref[...])
        bb = _bdot(y_conv.astype(BF16), wb_ref[...])
        merged = jax.nn.sigmoid(ga) * a + jax.nn.sigmoid(gb) * bb
        dst_ref[...] = x + _bdot(merged.astype(BF16), wo_ref[...])

    @pl.when(step == 0)
    def _():
        carry_ref[...] = jnp.zeros_like(carry_ref)
        residual(x0_ref, y0_ref, 0, ra_ref)

    _route(ra_ref, g2_ref, wr_ref, br_ref, xe_ref, cnt_ref, 0, tm)
    residual(xa_ref, ya_ref, 2 * step + 1, rb_ref)
    _route(rb_ref, g2_ref, wr_ref, br_ref, xe_ref, cnt_ref, 1, tm)
    residual(xb_ref, yb_ref, 2 * step + 2, ra_ref)


def _route(x1_ref, g2_ref, wr_ref, br_ref, xe_ref, cnt_ref, half, tm):
    rows = slice(half * tm, (half + 1) * tm)
    x1 = x1_ref[...]
    xe_ref[rows, 0:D_MODEL] = x1
    h2 = _rms(x1, g2_ref[...]).astype(BF16)

    logits = _bdot(h2, wr_ref[...]) + br_ref[...]
    lane = lax.broadcasted_iota(jnp.int32, (tm, LANES), 1)
    lanef = lane.astype(F32)
    is_g = (lane >= N_EXPERTS) & (lane < N_EXPERTS + N_GROUPS)
    gl = jnp.where(is_g, logits, NEG_INF)
    gmax = jnp.max(gl, axis=1, keepdims=True)
    gsum = jnp.sum(jnp.exp(gl - gmax), axis=1, keepdims=True)
    g_val = 1.0 / gsum
    g_lane = jnp.min(jnp.where(gl == gmax, lanef, float(LANES)), axis=1, keepdims=True)
    e_lo = (g_lane - float(N_EXPERTS)) * float(EXPERTS_PER_GROUP)
    in_grp = (lanef >= e_lo) & (lanef < e_lo + float(EXPERTS_PER_GROUP))
    el = jnp.where(in_grp, logits, NEG_INF)
    t1 = jnp.max(el, axis=1, keepdims=True)
    i1 = jnp.min(jnp.where(el == t1, lanef, float(LANES)), axis=1, keepdims=True)
    el2 = jnp.where(lanef == i1, NEG_INF, el)
    t2 = jnp.max(el2, axis=1, keepdims=True)
    i2 = jnp.min(jnp.where(el2 == t2, lanef, float(LANES)), axis=1, keepdims=True)
    e2 = jnp.exp(t2 - t1)
    w1 = g_val / (1.0 + e2)
    w2 = g_val * e2 / (1.0 + e2)
    comb = jnp.where(lanef == i1, w1, 0.0) + jnp.where(lanef == i2, w2, 0.0)

    cnt_ref[half] = jnp.sum(jnp.where(is_g & (lanef == g_lane), 1.0, 0.0), axis=0, keepdims=True)
    xe_ref[rows, D_MODEL:XE_WIDTH] = comb + jnp.where(lane == GID_LANE, g_lane - float(N_EXPERTS), 0.0)


def _mix(x, y_att, g, wcg, conv_w, wa, wb, wo, g2, wr, br, *, tm):
    b, s, d = x.shape
    t = b * s
    n_tiles = t // tm
    x2 = x.reshape(t, d)
    y2 = y_att.reshape(t, ATT_WIDTH)
    once = lambda a: pl.BlockSpec(a.shape, lambda i: (0,) * a.ndim, pipeline_mode=pl.Buffered(1))

    def tiles(width):
        nxt = lambda i: (jnp.minimum(2 * i + 2, n_tiles - 1), 0)
        return [pl.BlockSpec((tm, width), lambda i: (0, 0), pipeline_mode=pl.Buffered(1)),
                pl.BlockSpec((tm, width), lambda i: (2 * i + 1, 0)),
                pl.BlockSpec((tm, width), nxt)]

    return pl.pallas_call(
        functools.partial(_mix_kernel, tm=tm, tiles_per_seq=s // tm),
        grid=(n_tiles // 2,),
        in_specs=tiles(d) + tiles(ATT_WIDTH) + [once(g), once(wcg), once(conv_w), once(wa), once(wb),
                                                once(wo), once(g2), once(wr), once(br)],
        out_specs=[pl.BlockSpec((2 * tm, XE_WIDTH), lambda i: (i, 0)),
                   pl.BlockSpec((2, 1, LANES), lambda i: (i, 0, 0))],
        out_shape=[
            jax.ShapeDtypeStruct((t, XE_WIDTH), F32),
            jax.ShapeDtypeStruct((n_tiles, 1, LANES), F32),
        ],
        scratch_shapes=[pltpu.VMEM((tm, d), F32), pltpu.VMEM((tm, d), F32),
                        pltpu.VMEM((8, CONV_WIDTH), F32)],
        compiler_params=pltpu.CompilerParams(
            dimension_semantics=("arbitrary",), vmem_limit_bytes=VMEM_LIMIT),
        name="mix",
    )(x2, x2, x2, y2, y2, y2, g, wcg, conv_w, wa, wb, wo, g2, wr, br)


def _moe_kernel(tcnt_ref, xe_ref, p_ref, g2_ref, wg_ref, wu_ref, wd_ref, g3_ref, wpg_ref, wple_ref,
                earlier_ref, o_ref, hs_ref, rs_ref, ys_ref, *, tm):
    i = pl.program_id(0)
    n = [tcnt_ref[i * N_GROUPS + k] for k in range(N_GROUPS)]
    starts = [jnp.int32(0)]
    for k in range(N_GROUPS - 1):
        starts.append(starts[-1] + pl.cdiv(n[k], BF16_ROWS) * BF16_ROWS)

    x1 = xe_ref[:, 0:D_MODEL]
    route = xe_ref[:, D_MODEL:XE_WIDTH]
    h2 = _rms(x1, g2_ref[...]).astype(BF16)

    lane = lax.broadcasted_iota(jnp.int32, (tm, LANES), 1)
    lanef = lane.astype(F32)
    gid = jnp.sum(jnp.where(lane == GID_LANE, route, 0.0), axis=1, keepdims=True)
    onehot = jnp.where((lanef == gid) & (lane < N_GROUPS), 1.0, 0.0)
    rank = jnp.sum(onehot * _bdot(earlier_ref[...], onehot.astype(BF16)), axis=1, keepdims=True)
    base = starts[N_GROUPS - 1].astype(F32)
    for k in range(N_GROUPS - 2, -1, -1):
        base = jnp.where(gid == float(k), starts[k].astype(F32), base)
    pos = base + rank

    unsort = (pos == lax.broadcasted_iota(jnp.int32, (tm, MOE_SORTED), 1).astype(F32)).astype(BF16)
    digit_hi = jnp.floor(pos * (1.0 / 32.0))
    digits = jnp.where(lane == 0, digit_hi, jnp.where(lane == 1, pos - 32.0 * digit_hi, 0.0))
    sel_lane = lax.broadcasted_iota(jnp.int32, (BF16_ROWS, LANES), 1)
    sel = jnp.where(sel_lane == 0, 32.0, jnp.where(sel_lane == 1, 1.0, 0.0)).astype(BF16)
    pos_row = _bdot_nt(sel, digits.astype(BF16))[0:1, :]
    sort = (lax.broadcasted_iota(jnp.int32, (MOE_SORTED, tm), 0).astype(F32) == pos_row).astype(BF16)

    hs_ref[0:MOE_SORTED] = _bdot(sort, h2).astype(BF16)
    r_hi, r_mid, r_lo = _bf16_pieces(route)
    packed = (r_hi.astype(F32) + pltpu.roll(r_mid.astype(F32), ROUTE_PITCH, axis=1)
              + pltpu.roll(r_lo.astype(F32), 2 * ROUTE_PITCH, axis=1)).astype(BF16)
    rsorted = _bdot(sort, packed)
    rs_ref[0:MOE_SORTED] = (rsorted + pltpu.roll(rsorted, LANES - ROUTE_PITCH, axis=1)
                            + pltpu.roll(rsorted, LANES - 2 * ROUTE_PITCH, axis=1))
    hs_ref[MOE_SORTED:MOE_ROWS] = jnp.zeros((MOE_CHUNK, D_MODEL), BF16)
    rs_ref[MOE_SORTED:MOE_ROWS] = jnp.zeros((MOE_CHUNK, LANES), F32)
    ys_ref[...] = jnp.zeros_like(ys_ref)

    clane = lax.broadcasted_iota(jnp.int32, (MOE_CHUNK, LANES), 1)

    def chunk(g, c):
        r0 = pl.multiple_of(starts[g] + c * MOE_CHUNK, BF16_ROWS)
        hrows = hs_ref[pl.ds(r0, MOE_CHUNK), :]
        rt = rs_ref[pl.ds(r0, MOE_CHUNK), :]
        parts = []
        for e in range(EXPERTS_PER_GROUP):
            ex = g * EXPERTS_PER_GROUP + e
            a = _bdot(hrows, wg_ref[ex])
            u = _bdot(hrows, wu_ref[ex])
            ce = jnp.sum(jnp.where(clane == ex, rt, 0.0), axis=1, keepdims=True)
            parts.append(((a * jax.nn.sigmoid(a)) * u * ce).astype(BF16))
        out = _bdot(jnp.concatenate(parts, axis=1), wd_ref[g])
        ys_ref[pl.ds(r0, MOE_CHUNK), :] = (ys_ref[pl.ds(r0, MOE_CHUNK), :].astype(F32) + out).astype(BF16)

    for g in range(N_GROUPS):
        def more(c, _, g=g):
            chunk(g, c)
            return 0
        lax.fori_loop(1, pl.cdiv(n[g], MOE_CHUNK), more, 0)
    for g in range(N_GROUPS):
        chunk(g, 0)

    x2 = x1 + _bdot(unsort, ys_ref[0:MOE_SORTED])
    h3 = _rms(x2, g3_ref[...]).astype(BF16)
    gate = jax.nn.sigmoid(_bdot(h3, wpg_ref[...]))
    emb = _bdot(p_ref[...].astype(BF16), wple_ref[...])
    o_ref[...] = x2 + gate * emb


def _moe(tcnt, xe, p, g2, wg, wu, wd, g3, wpg, wple, *, tm):
    t = xe.shape[0]
    d = D_MODEL
    once = lambda a: pl.BlockSpec(a.shape, lambda i, c: (0,) * a.ndim, pipeline_mode=pl.Buffered(1))
    row = lambda width: pl.BlockSpec((tm, width), lambda i, c: (i, 0))
    earlier = jnp.asarray(np.tril(np.ones((tm, tm), np.float32), -1), BF16)
    return pl.pallas_call(
        functools.partial(_moe_kernel, tm=tm),
        grid_spec=pltpu.PrefetchScalarGridSpec(
            num_scalar_prefetch=1,
            grid=(t // tm,),
            in_specs=[row(XE_WIDTH), row(PLE_DIM), once(g2), once(wg), once(wu), once(wd), once(g3),
                      once(wpg), once(wple), once(earlier)],
            out_specs=row(d),
            scratch_shapes=[pltpu.VMEM((MOE_ROWS, d), BF16), pltpu.VMEM((MOE_ROWS, LANES), F32),
                            pltpu.VMEM((MOE_ROWS, d), BF16)],
        ),
        out_shape=jax.ShapeDtypeStruct((t, d), F32),
        compiler_params=pltpu.CompilerParams(
            dimension_semantics=("arbitrary",), vmem_limit_bytes=VMEM_LIMIT),
        name="moe",
    )(tcnt, xe, p, g2, wg, wu, wd, g3, wpg, wple, earlier)


def _k_select_matrix():
    pk = np.zeros((LANES, N_HEADS * LANES), np.float32)
    for idx in range(C_PIECES):
        for h in range(N_HEADS):
            pk[idx * N_HEADS + h, h * LANES + KC_LANE + idx] = -1.0
    return jnp.asarray(pk, BF16)


def kernel(x, p, attn_norm_g, w_in, b_f, q_norm_g, k_norm_g, conv_w, w_out_att, w_out_conv, w_o,
           ffn_norm_g, w_rg, b_rg, w_re, b_re, w_gate, w_up, w_down, ple_norm_g, w_pg, w_ple):
    b, s, d = x.shape
    assert d == D_MODEL and conv_w.shape[1:] == (CONV_K, CONV_WIDTH) and s % (2 * TM_MIX) == 0
    t = b * s
    aw = ATT_WIDTH
    for i in range(w_in.shape[0]):
        wt = w_in[i].T
        wf = wt[3 * aw:3 * aw + N_HEADS]
        w_qkvf = jnp.concatenate(
            [wt[:3 * aw], wf, wf, wf, jnp.zeros((LANES - C_PIECES * N_HEADS, d), F32)],
            axis=0).astype(BF16)
        w_cg = wt[3 * aw + N_HEADS:].astype(BF16).T
        bf3 = jnp.concatenate([b_f[i]] * C_PIECES + [jnp.zeros((LANES - C_PIECES * N_HEADS,), F32)])[None, :]
        scale = HEAD_DIM ** -0.5 * LOG2E
        qg = jnp.tile(q_norm_g[i] * scale, 2)[None, :]
        kg = jnp.tile(k_norm_g[i], 2)[None, :]
        qt_aug, k_aug, vt = _qkv_proj(x, attn_norm_g[i][None, :], w_qkvf, bf3, qg, kg, _k_select_matrix(),
                                      tm=TM_QKV)
        y_att = _attention(qt_aug, k_aug, vt, tq=TQ_ATT, tk=TK_ATT)

        w_r = jnp.concatenate(
            [w_re[i], w_rg[i], jnp.zeros((d, LANES - N_EXPERTS - N_GROUPS), F32)], axis=1).astype(BF16)
        b_r = jnp.concatenate(
            [b_re[i], b_rg[i], jnp.zeros((LANES - N_EXPERTS - N_GROUPS,), F32)])[None, :]
        g_ffn = ffn_norm_g[i][None, :]
        xe, counts = _mix(x, y_att, attn_norm_g[i][None, :], w_cg, conv_w[i],
                          w_out_att[i].astype(BF16), w_out_conv[i].astype(BF16),
                          w_o[i].astype(BF16), g_ffn, w_r, b_r, tm=TM_MIX)
        tcnt = counts[:, 0, N_EXPERTS:N_EXPERTS + N_GROUPS].astype(jnp.int32).reshape(-1)

        w_dn = w_down[i].reshape(N_GROUPS, GROUP_WIDTH, d).astype(BF16)
        x = _moe(tcnt, xe.reshape(t, XE_WIDTH), p[i].reshape(t, PLE_DIM), g_ffn,
                 w_gate[i].astype(BF16), w_up[i].astype(BF16), w_dn,
                 ple_norm_g[i][None, :], w_pg[i].astype(BF16), w_ple[i].astype(BF16),
                 tm=TM_MOE).reshape(b, s, d)
    return x
```

```python
import functools

import numpy as np
import jax
import jax.numpy as jnp
from jax import lax
from jax.experimental import pallas as pl
from jax.experimental.pallas import tpu as pltpu

D_MODEL = 1024
N_HEADS = 8
HEAD_DIM = 64
ATT_WIDTH = N_HEADS * HEAD_DIM
CONV_WIDTH = 512
CONV_K = 3
N_GROUPS = 4
EXPERTS_PER_GROUP = 4
N_EXPERTS = 16
D_EXPERT = 256
GROUP_WIDTH = EXPERTS_PER_GROUP * D_EXPERT
PLE_DIM = 256
EPS = 1e-6
NEG_INF = -1e30
LOG2E = 1.4426950408889634

LANES = 128
BF16_ROWS = 16
C_PIECES = 3
QC_LANE = HEAD_DIM
KC_LANE = HEAD_DIM + C_PIECES
V_ROWS = HEAD_DIM + BF16_ROWS
QKV_WIDTH = 3 * ATT_WIDTH + LANES
GID_LANE = N_EXPERTS
ROUTE_PITCH = 32
XE_WIDTH = D_MODEL + LANES
VMEM_LIMIT = 56 * 1024 * 1024

TM_QKV = 512
TQ_ATT = 512
TK_ATT = 256
ATT_HEADS = 2
TM_MIX = 512
TM_MOE = TM_MIX
MOE_SORTED = -(-(TM_MOE + N_GROUPS * BF16_ROWS) // LANES) * LANES
MOE_CHUNK = 144
MOE_ROWS = MOE_SORTED + MOE_CHUNK

F32 = jnp.float32
BF16 = jnp.bfloat16


def _rms(xf, g):
    return xf * lax.rsqrt(jnp.mean(xf * xf, axis=-1, keepdims=True) + EPS) * g


def _log_sigmoid(z):
    return jnp.minimum(z, 0.0) - jnp.log1p(jnp.exp(-jnp.abs(z)))


def _bdot(a, b):
    return jnp.dot(a, b, preferred_element_type=F32)


def _bdot_nt(a, b):
    return lax.dot_general(a, b, (((1,), (1,)), ((), ())), preferred_element_type=F32)


def _bf16_pieces(x):
    hi = x.astype(BF16)
    r1 = x - hi.astype(F32)
    mid = r1.astype(BF16)
    lo = (r1 - mid.astype(F32)).astype(BF16)
    return hi, mid, lo


def _qkv_kernel(x0_ref, xa_ref, xb_ref, g_ref, w_ref, bf_ref, qg_ref, kg_ref, pk_ref,
                qt_ref, k_ref, vt_ref, pa_ref, pb_ref, carry_ref, *, tm, tiles_per_seq):
    step = pl.program_id(0)

    def project(x_ref, dst_ref):
        dst_ref[...] = _bdot_nt(_rms(x_ref[...], g_ref[...]).astype(BF16), w_ref[...])

    @pl.when(step == 0)
    def _():
        carry_ref[...] = jnp.zeros_like(carry_ref)
        project(x0_ref, pa_ref)

    lane = lax.broadcasted_iota(jnp.int32, (tm, LANES), 1)
    row = lax.broadcasted_iota(jnp.int32, (tm, LANES), 0)
    k_ones = jnp.where((lane >= QC_LANE) & (lane < QC_LANE + C_PIECES), 1.0, 0.0)
    low = lane < HEAD_DIM
    ext_row = lax.broadcasted_iota(jnp.int32, (V_ROWS - HEAD_DIM, tm), 0)
    v_ext = jnp.where(ext_row == 0, 1.0, 0.0)

    def finish(p_ref, tile, half):
        rows = slice(half * tm, (half + 1) * tm)
        for j in range(N_HEADS // 2):
            pair_t = p_ref[:, 2 * ATT_WIDTH + LANES * j: 2 * ATT_WIDTH + LANES * (j + 1)].T
            for hh in range(2):
                vt_ref[0, 2 * j + hh, :, rows] = jnp.concatenate(
                    [pair_t[HEAD_DIM * hh: HEAD_DIM * (hh + 1)], v_ext], axis=0).astype(BF16)

        f3 = p_ref[:, 3 * ATT_WIDTH:QKV_WIDTH] + bf_ref[...]
        c = jnp.where(lane < C_PIECES * N_HEADS, _log_sigmoid(f3), 0.0)
        sh = 1
        while sh < tm:
            c = c + jnp.where(row >= sh, pltpu.roll(c, sh, axis=0), 0.0)
            sh *= 2
        c = c + jnp.where(tile % tiles_per_seq == 0, 0.0, carry_ref[...])
        carry_ref[...] = c[tm - 1:tm, :]
        hi, mid, lo = _bf16_pieces(c * LOG2E)
        piece = jnp.where(lane < N_HEADS, hi, jnp.where(lane < 2 * N_HEADS, mid, lo))

        def rms_scales(pair):
            sq = pair * pair
            ss_lo = jnp.sum(jnp.where(low, sq, 0.0), axis=1, keepdims=True)
            ss_hi = jnp.sum(jnp.where(low, 0.0, sq), axis=1, keepdims=True)
            return (lax.rsqrt(ss_lo * (1.0 / HEAD_DIM) + EPS), lax.rsqrt(ss_hi * (1.0 / HEAD_DIM) + EPS))

        piece_t = piece.astype(F32).T
        erow = lax.broadcasted_iota(jnp.int32, (8, tm), 0)
        pad_t = jnp.zeros((LANES - HEAD_DIM - 8, tm), F32)
        qgain = qg_ref[...]
        for j in range(N_HEADS // 2):
            pair = p_ref[:, LANES * j: LANES * (j + 1)]
            r_lo, r_hi = rms_scales(pair)
            qn_t = (pair * jnp.where(low, r_lo, r_hi) * qgain).T
            for hh in range(2):
                h = 2 * j + hh
                c_t = jnp.where(erow >= C_PIECES, jnp.where(erow < 2 * C_PIECES, 1.0, 0.0), 0.0)
                for idx in range(C_PIECES):
                    c_t = jnp.where(erow == idx, piece_t[idx * N_HEADS + h: idx * N_HEADS + h + 1], c_t)
                qt_ref[0, h, :, rows] = jnp.concatenate(
                    [qn_t[HEAD_DIM * hh: HEAD_DIM * (hh + 1)], c_t, pad_t], axis=0).astype(BF16)

        kc = _bdot(piece, pk_ref[...])
        kgain = kg_ref[...]
        for j in range(N_HEADS // 2):
            pair = p_ref[:, ATT_WIDTH + LANES * j: ATT_WIDTH + LANES * (j + 1)]
            r_lo, r_hi = rms_scales(pair)
            n_lo = pair * r_lo * kgain
            n_hi = pltpu.roll(pair, HEAD_DIM, axis=1) * r_hi * kgain
            for h, nrm in ((2 * j, n_lo), (2 * j + 1, n_hi)):
                aug = jnp.where(low, nrm, 0.0) + kc[:, LANES * h: LANES * (h + 1)] + k_ones
                k_ref[0, h, rows, :] = aug.astype(BF16)

    finish(pa_ref, 2 * step, 0)
    project(xa_ref, pb_ref)
    finish(pb_ref, 2 * step + 1, 1)
    project(xb_ref, pa_ref)


def _qkv_proj(x, g, w, bf3, qg, kg, pk, *, tm):
    b, s, d = x.shape
    x2 = x.reshape(b * s, d)
    n_tiles = b * s // tm
    per_seq = s // (2 * tm)
    full = lambda shape: pl.BlockSpec(shape, lambda i: (0,) * len(shape))
    return pl.pallas_call(
        functools.partial(_qkv_kernel, tm=tm, tiles_per_seq=s // tm),
        grid=(n_tiles // 2,),
        in_specs=[
            pl.BlockSpec((tm, d), lambda i: (0, 0)),
            pl.BlockSpec((tm, d), lambda i: (2 * i + 1, 0)),
            pl.BlockSpec((tm, d), lambda i: (jnp.minimum(2 * i + 2, n_tiles - 1), 0)),
            full(g.shape), full(w.shape), full(bf3.shape), full(qg.shape), full(kg.shape),
            full(pk.shape),
        ],
        out_specs=[pl.BlockSpec((1, N_HEADS, LANES, 2 * tm), lambda i: (i // per_seq, 0, 0, i % per_seq)),
                   pl.BlockSpec((1, N_HEADS, 2 * tm, LANES), lambda i: (i // per_seq, 0, i % per_seq, 0)),
                   pl.BlockSpec((1, N_HEADS, V_ROWS, 2 * tm), lambda i: (i // per_seq, 0, 0, i % per_seq))],
        out_shape=[
            jax.ShapeDtypeStruct((b, N_HEADS, LANES, s), BF16),
            jax.ShapeDtypeStruct((b, N_HEADS, s, LANES), BF16),
            jax.ShapeDtypeStruct((b, N_HEADS, V_ROWS, s), BF16),
        ],
        scratch_shapes=[pltpu.VMEM((tm, QKV_WIDTH), F32), pltpu.VMEM((tm, QKV_WIDTH), F32),
                        pltpu.VMEM((1, LANES), F32)],
        compiler_params=pltpu.CompilerParams(
            dimension_semantics=("arbitrary",), vmem_limit_bytes=VMEM_LIMIT),
        name="qkv_proj",
    )(x2, x2, x2, g, w, bf3, qg, kg, pk)


def _attn_kernel(qt_ref, qn_ref, k_ref, vt_ref, bias_ref, o_ref, sa_ref, sb_ref, *, tq, tk, n_q):
    qi = pl.program_id(2)
    n_sub = tq // tk
    bufs = (sa_ref, sb_ref)

    def produce(j, s_ref, lo=0, q_ref=qt_ref):
        start = j * tk
        for hh in range(ATT_HEADS):
            s_ref[hh, :, lo:tq] = _bdot(k_ref[0, hh, pl.ds(start, tk), :], q_ref[0, hh, :, lo:tq])

    def consume(j, s_ref, carry, lo=0, masked=False):
        start = j * tk
        out = []
        for hh in range(ATT_HEADS):
            m, acc = carry[hh]

            def read():
                if not masked:
                    return s_ref[hh, :, lo:tq]
                tri = s_ref[hh, :, lo:lo + tk] + bias_ref[...]
                return tri if lo + tk == tq else jnp.concatenate([tri, s_ref[hh, :, lo + tk:tq]], axis=1)

            m_new = jnp.maximum(m[:, lo:tq], jnp.max(read(), axis=0, keepdims=True))
            alpha = jnp.exp2(m[:, lo:tq] - m_new)
            p = jnp.exp2(read() - m_new).astype(BF16)
            acc_new = alpha * acc[:, lo:tq] + _bdot(vt_ref[0, hh, :, pl.ds(start, tk)], p)
            if lo:
                m_new = jnp.concatenate([m[:, 0:lo], m_new], axis=1)
                acc_new = jnp.concatenate([acc[:, 0:lo], acc_new], axis=1)
            out.append((m_new, acc_new))
        return tuple(out)

    def trip(r, carry):
        for t in range(n_sub):
            j = r * n_sub + t
            produce(j + 1, bufs[(t + 1) % 2])
            carry = consume(j, bufs[t % 2], carry)
        return carry

    def query_tile(q):
        if q == 0:
            produce(0, sa_ref)
        carry = tuple((jnp.full((1, tq), NEG_INF, F32), jnp.zeros((V_ROWS, tq), F32))
                      for _ in range(ATT_HEADS))
        for r in range(q):
            carry = trip(r, carry)
        for t in range(n_sub):
            j = q * n_sub + t
            if t + 1 < n_sub:
                produce(j + 1, bufs[(t + 1) % 2], lo=(t + 1) * tk)
            elif q + 1 < n_q:
                produce(0, bufs[(t + 1) % 2], q_ref=qn_ref)
            carry = consume(j, bufs[t % 2], carry, lo=t * tk, masked=True)
        y_t = jnp.concatenate([acc[0:HEAD_DIM] / acc[HEAD_DIM:HEAD_DIM + 1] for _, acc in carry], axis=0)
        o_ref[0] = y_t.T.astype(BF16)

    for q in range(n_q):
        pl.when(qi == q)(functools.partial(query_tile, q))


def _attention(qt_aug, k_aug, vt, *, tq, tk):
    b, nh, s, _ = k_aug.shape
    assert tq % (2 * tk) == 0 and s % tq == 0
    future = np.arange(tk)[:, None] > np.arange(tk)[None, :]
    bias = jnp.asarray(np.where(future, np.float32(NEG_INF), np.float32(0.0)))
    nh_step = ATT_HEADS
    grid = (b, nh // nh_step, s // tq)
    return pl.pallas_call(
        functools.partial(_attn_kernel, tq=tq, tk=tk, n_q=s // tq),
        grid=grid,
        in_specs=[
            pl.BlockSpec((1, nh_step, LANES, tq), lambda i, h, j: (i, h, 0, j)),
            pl.BlockSpec((1, nh_step, LANES, tq), lambda i, h, j: (i, h, 0, jnp.minimum(j + 1, s // tq - 1))),
            pl.BlockSpec((1, nh_step, s, LANES), lambda i, h, j: (i, h, 0, 0)),
            pl.BlockSpec((1, nh_step, V_ROWS, s), lambda i, h, j: (i, h, 0, 0)),
            pl.BlockSpec((tk, tk), lambda i, h, j: (0, 0)),
        ],
        out_specs=pl.BlockSpec((1, tq, nh_step * HEAD_DIM), lambda i, h, j: (i, j, h)),
        out_shape=jax.ShapeDtypeStruct((b, s, ATT_WIDTH), BF16),
        scratch_shapes=[pltpu.VMEM((nh_step, tk, tq), F32)] * 2,
        compiler_params=pltpu.CompilerParams(
            dimension_semantics=("arbitrary", "arbitrary", "arbitrary"),
            vmem_limit_bytes=VMEM_LIMIT),
        name="fox_attention",
    )(qt_aug, qt_aug, k_aug, vt, bias)


def _mix_kernel(x0_ref, xa_ref, xb_ref, y0_ref, ya_ref, yb_ref, g_ref, wcg_ref, cw_ref, wa_ref, wb_ref,
                wo_ref, g2_ref, wr_ref, br_ref, xe_ref, cnt_ref, ra_ref, rb_ref, carry_ref,
                *, tm, tiles_per_seq):
    step = pl.program_id(0)

    def residual(x_ref, y_ref, tile, dst_ref):
        x = x_ref[...]
        h = _rms(x, g_ref[...]).astype(BF16)
        pc = _bdot(h, wcg_ref[...])
        cw = CONV_WIDTH
        cb = pc[:, 0:cw]
        prod = pc[:, cw:2 * cw] * pc[:, 2 * cw:3 * cw]
        ga = pc[:, 3 * cw:3 * cw + D_MODEL]
        gb = pc[:, 3 * cw + D_MODEL:3 * cw + 2 * D_MODEL]

        prev = jnp.where(tile % tiles_per_seq == 0, 0.0, carry_ref[...])
        crow = lax.broadcasted_iota(jnp.int32, (tm, cw), 0)
        m1 = jnp.where(crow == 0, prev[7:8, :], pltpu.roll(prod, 1, axis=0))
        m2 = jnp.where(crow == 0, prev[6:7, :],
                       jnp.where(crow == 1, prev[7:8, :], pltpu.roll(prod, 2, axis=0)))
        carry_ref[...] = prod[tm - 8:tm, :]
        w = cw_ref[...]
        y_conv = cb * (w[0:1, :] * m2 + w[1:2, :] * m1 + w[2:3, :] * prod)

        a = _bdot(y_ref[...], wa_ref[...])
        bb = _bdot(y_conv.astype(BF16), wb_ref[...])
        merged = jax.nn.sigmoid(ga) * a + jax.nn.sigmoid(gb) * bb
        dst_ref[...] = x + _bdot(merged.astype(BF16), wo_ref[...])

    @pl.when(step == 0)
    def _():
        carry_ref[...] = jnp.zeros_like(carry_ref)
        residual(x0_ref, y0_ref, 0, ra_ref)

    _route(ra_ref, g2_ref, wr_ref, br_ref, xe_ref, cnt_ref, 0, tm)
    residual(xa_ref, ya_ref, 2 * step + 1, rb_ref)
    _route(rb_ref, g2_ref, wr_ref, br_ref, xe_ref, cnt_ref, 1, tm)
    residual(xb_ref, yb_ref, 2 * step + 2, ra_ref)


def _route(x1_ref, g2_ref, wr_ref, br_ref, xe_ref, cnt_ref, half, tm):
    rows = slice(half * tm, (half + 1) * tm)
    x1 = x1_ref[...]
    xe_ref[rows, 0:D_MODEL] = x1
    h2 = _rms(x1, g2_ref[...]).astype(BF16)

    logits = _bdot(h2, wr_ref[...]) + br_ref[...]
    lane = lax.broadcasted_iota(jnp.int32, (tm, LANES), 1)
    lanef = lane.astype(F32)
    is_g = (lane >= N_EXPERTS) & (lane < N_EXPERTS + N_GROUPS)
    gl = jnp.where(is_g, logits, NEG_INF)
    gmax = jnp.max(gl, axis=1, keepdims=True)
    gsum = jnp.sum(jnp.exp(gl - gmax), axis=1, keepdims=True)
    g_val = 1.0 / gsum
    g_lane = jnp.min(jnp.where(gl == gmax, lanef, float(LANES)), axis=1, keepdims=True)
    e_lo = (g_lane - float(N_EXPERTS)) * float(EXPERTS_PER_GROUP)
    in_grp = (lanef >= e_lo) & (lanef < e_lo + float(EXPERTS_PER_GROUP))
    el = jnp.where(in_grp, logits, NEG_INF)
    t1 = jnp.max(el, axis=1, keepdims=True)
    i1 = jnp.min(jnp.where(el == t1, lanef, float(LANES)), axis=1, keepdims=True)
    el2 = jnp.where(lanef == i1, NEG_INF, el)
    t2 = jnp.max(el2, axis=1, keepdims=True)
    i2 = jnp.min(jnp.where(el2 == t2, lanef, float(LANES)), axis=1, keepdims=True)
    e2 = jnp.exp(t2 - t1)
    w1 = g_val / (1.0 + e2)
    w2 = g_val * e2 / (1.0 + e2)
    comb = jnp.where(lanef == i1, w1, 0.0) + jnp.where(lanef == i2, w2, 0.0)

    cnt_ref[half] = jnp.sum(jnp.where(is_g & (lanef == g_lane), 1.0, 0.0), axis=0, keepdims=True)
    xe_ref[rows, D_MODEL:XE_WIDTH] = comb + jnp.where(lane == GID_LANE, g_lane - float(N_EXPERTS), 0.0)


def _transpose_kernel(wt_ref, w_ref):
    w_ref[...] = wt_ref[...].astype(F32).T.astype(BF16)


def _transposed(wt, *, rows):
    n, k = wt.shape
    assert n % rows == 0
    return pl.pallas_call(
        _transpose_kernel,
        grid=(n // rows,),
        in_specs=[pl.BlockSpec((rows, k), lambda i: (i, 0))],
        out_specs=pl.BlockSpec((k, rows), lambda i: (0, i)),
        out_shape=jax.ShapeDtypeStruct((k, n), BF16),
        compiler_params=pltpu.CompilerParams(dimension_semantics=("arbitrary",)),
        name="transpose_w",
    )(wt)


def _mix(x, y_att, g, wcg, conv_w, wa, wb, wo, g2, wr, br, *, tm):
    b, s, d = x.shape
    t = b * s
    n_tiles = t // tm
    x2 = x.reshape(t, d)
    y2 = y_att.reshape(t, ATT_WIDTH)
    once = lambda a: pl.BlockSpec(a.shape, lambda i: (0,) * a.ndim, pipeline_mode=pl.Buffered(1))

    def tiles(width):
        nxt = lambda i: (jnp.minimum(2 * i + 2, n_tiles - 1), 0)
        return [pl.BlockSpec((tm, width), lambda i: (0, 0), pipeline_mode=pl.Buffered(1)),
                pl.BlockSpec((tm, width), lambda i: (2 * i + 1, 0)),
                pl.BlockSpec((tm, width), nxt)]

    return pl.pallas_call(
        functools.partial(_mix_kernel, tm=tm, tiles_per_seq=s // tm),
        grid=(n_tiles // 2,),
        in_specs=tiles(d) + tiles(ATT_WIDTH) + [once(g), once(wcg), once(conv_w), once(wa), once(wb),
                                                once(wo), once(g2), once(wr), once(br)],
        out_specs=[pl.BlockSpec((2 * tm, XE_WIDTH), lambda i: (i, 0)),
                   pl.BlockSpec((2, 1, LANES), lambda i: (i, 0, 0))],
        out_shape=[
            jax.ShapeDtypeStruct((t, XE_WIDTH), F32),
            jax.ShapeDtypeStruct((n_tiles, 1, LANES), F32),
        ],
        scratch_shapes=[pltpu.VMEM((tm, d), F32), pltpu.VMEM((tm, d), F32),
                        pltpu.VMEM((8, CONV_WIDTH), F32)],
        compiler_params=pltpu.CompilerParams(
            dimension_semantics=("arbitrary",), vmem_limit_bytes=VMEM_LIMIT),
        name="mix",
    )(x2, x2, x2, y2, y2, y2, g, wcg, conv_w, wa, wb, wo, g2, wr, br)


def _moe_kernel(tcnt_ref, xe_ref, p_ref, g2_ref, wg_ref, wu_ref, wd_ref, g3_ref, wpg_ref, wple_ref,
                earlier_ref, o_ref, hs_ref, rs_ref, ys_ref, *, tm):
    i = pl.program_id(0)
    n = [tcnt_ref[i * N_GROUPS + k] for k in range(N_GROUPS)]
    starts = [jnp.int32(0)]
    for k in range(N_GROUPS - 1):
        starts.append(starts[-1] + pl.cdiv(n[k], BF16_ROWS) * BF16_ROWS)

    x1 = xe_ref[:, 0:D_MODEL]
    route = xe_ref[:, D_MODEL:XE_WIDTH]
    h2 = _rms(x1, g2_ref[...]).astype(BF16)

    lane = lax.broadcasted_iota(jnp.int32, (tm, LANES), 1)
    lanef = lane.astype(F32)
    gid = jnp.sum(jnp.where(lane == GID_LANE, route, 0.0), axis=1, keepdims=True)
    onehot = jnp.where((lanef == gid) & (lane < N_GROUPS), 1.0, 0.0)
    rank = jnp.sum(onehot * _bdot(earlier_ref[...], onehot.astype(BF16)), axis=1, keepdims=True)
    base = starts[N_GROUPS - 1].astype(F32)
    for k in range(N_GROUPS - 2, -1, -1):
        base = jnp.where(gid == float(k), starts[k].astype(F32), base)
    pos = base + rank

    unsort = (pos == lax.broadcasted_iota(jnp.int32, (tm, MOE_SORTED), 1).astype(F32)).astype(BF16)
    digit_hi = jnp.floor(pos * (1.0 / 32.0))
    digits = jnp.where(lane == 0, digit_hi, jnp.where(lane == 1, pos - 32.0 * digit_hi, 0.0))
    sel_lane = lax.broadcasted_iota(jnp.int32, (BF16_ROWS, LANES), 1)
    sel = jnp.where(sel_lane == 0, 32.0, jnp.where(sel_lane == 1, 1.0, 0.0)).astype(BF16)
    pos_row = _bdot_nt(sel, digits.astype(BF16))[0:1, :]
    sort = (lax.broadcasted_iota(jnp.int32, (MOE_SORTED, tm), 0).astype(F32) == pos_row).astype(BF16)

    hs_ref[0:MOE_SORTED] = _bdot(sort, h2).astype(BF16)
    r_hi, r_mid, r_lo = _bf16_pieces(route)
    packed = (r_hi.astype(F32) + pltpu.roll(r_mid.astype(F32), ROUTE_PITCH, axis=1)
              + pltpu.roll(r_lo.astype(F32), 2 * ROUTE_PITCH, axis=1)).astype(BF16)
    rsorted = _bdot(sort, packed)
    rs_ref[0:MOE_SORTED] = (rsorted + pltpu.roll(rsorted, LANES - ROUTE_PITCH, axis=1)
                            + pltpu.roll(rsorted, LANES - 2 * ROUTE_PITCH, axis=1))
    hs_ref[MOE_SORTED:MOE_ROWS] = jnp.zeros((MOE_CHUNK, D_MODEL), BF16)
    rs_ref[MOE_SORTED:MOE_ROWS] = jnp.zeros((MOE_CHUNK, LANES), F32)
    ys_ref[...] = jnp.zeros_like(ys_ref)

    clane = lax.broadcasted_iota(jnp.int32, (MOE_CHUNK, LANES), 1)

    def chunk(g, c):
        r0 = pl.multiple_of(starts[g] + c * MOE_CHUNK, BF16_ROWS)
        hrows = hs_ref[pl.ds(r0, MOE_CHUNK), :]
        rt = rs_ref[pl.ds(r0, MOE_CHUNK), :]
        parts = []
        for e in range(EXPERTS_PER_GROUP):
            ex = g * EXPERTS_PER_GROUP + e
            a = _bdot(hrows, wg_ref[ex])
            u = _bdot(hrows, wu_ref[ex])
            ce = jnp.sum(jnp.where(clane == ex, rt, 0.0), axis=1, keepdims=True)
            parts.append(((a * jax.nn.sigmoid(a)) * u * ce).astype(BF16))
        out = _bdot(jnp.concatenate(parts, axis=1), wd_ref[g])
        ys_ref[pl.ds(r0, MOE_CHUNK), :] = (ys_ref[pl.ds(r0, MOE_CHUNK), :].astype(F32) + out).astype(BF16)

    for g in range(N_GROUPS):
        def more(c, _, g=g):
            chunk(g, c)
            return 0
        lax.fori_loop(1, pl.cdiv(n[g], MOE_CHUNK), more, 0)
    for g in range(N_GROUPS):
        chunk(g, 0)

    x2 = x1 + _bdot(unsort, ys_ref[0:MOE_SORTED])
    h3 = _rms(x2, g3_ref[...]).astype(BF16)
    gate = jax.nn.sigmoid(_bdot(h3, wpg_ref[...]))
    emb = _bdot(p_ref[...].astype(BF16), wple_ref[...])
    o_ref[...] = x2 + gate * emb


def _moe(tcnt, xe, p, g2, wg, wu, wd, g3, wpg, wple, *, tm):
    t = xe.shape[0]
    d = D_MODEL
    once = lambda a: pl.BlockSpec(a.shape, lambda i, c: (0,) * a.ndim, pipeline_mode=pl.Buffered(1))
    row = lambda width: pl.BlockSpec((tm, width), lambda i, c: (i, 0))
    earlier = jnp.asarray(np.tril(np.ones((tm, tm), np.float32), -1), BF16)
    return pl.pallas_call(
        functools.partial(_moe_kernel, tm=tm),
        grid_spec=pltpu.PrefetchScalarGridSpec(
            num_scalar_prefetch=1,
            grid=(t // tm,),
            in_specs=[row(XE_WIDTH), row(PLE_DIM), once(g2), once(wg), once(wu), once(wd), once(g3),
                      once(wpg), once(wple), once(earlier)],
            out_specs=row(d),
            scratch_shapes=[pltpu.VMEM((MOE_ROWS, d), BF16), pltpu.VMEM((MOE_ROWS, LANES), F32),
                            pltpu.VMEM((MOE_ROWS, d), BF16)],
        ),
        out_shape=jax.ShapeDtypeStruct((t, d), F32),
        compiler_params=pltpu.CompilerParams(
            dimension_semantics=("arbitrary",), vmem_limit_bytes=VMEM_LIMIT),
        name="moe",
    )(tcnt, xe, p, g2, wg, wu, wd, g3, wpg, wple, earlier)


def _k_select_matrix():
    pk = np.zeros((LANES, N_HEADS * LANES), np.float32)
    for idx in range(C_PIECES):
        for h in range(N_HEADS):
            pk[idx * N_HEADS + h, h * LANES + KC_LANE + idx] = -1.0
    return jnp.asarray(pk, BF16)


def kernel(x, p, attn_norm_g, w_in, b_f, q_norm_g, k_norm_g, conv_w, w_out_att, w_out_conv, w_o,
           ffn_norm_g, w_rg, b_rg, w_re, b_re, w_gate, w_up, w_down, ple_norm_g, w_pg, w_ple):
    b, s, d = x.shape
    assert d == D_MODEL and conv_w.shape[1:] == (CONV_K, CONV_WIDTH) and s % (2 * TM_MIX) == 0
    t = b * s
    aw = ATT_WIDTH
    for i in range(w_in.shape[0]):
        wt = w_in[i].T
        wf = wt[3 * aw:3 * aw + N_HEADS]
        w_qkvf = jnp.concatenate(
            [wt[:3 * aw], wf, wf, wf, jnp.zeros((LANES - C_PIECES * N_HEADS, d), F32)],
            axis=0).astype(BF16)
        w_cg = _transposed(wt[3 * aw + N_HEADS:].astype(BF16), rows=TM_MIX)
        bf3 = jnp.concatenate([b_f[i]] * C_PIECES + [jnp.zeros((LANES - C_PIECES * N_HEADS,), F32)])[None, :]
        scale = HEAD_DIM ** -0.5 * LOG2E
        qg = jnp.tile(q_norm_g[i] * scale, 2)[None, :]
        kg = jnp.tile(k_norm_g[i], 2)[None, :]
        qt_aug, k_aug, vt = _qkv_proj(x, attn_norm_g[i][None, :], w_qkvf, bf3, qg, kg, _k_select_matrix(),
                                      tm=TM_QKV)
        y_att = _attention(qt_aug, k_aug, vt, tq=TQ_ATT, tk=TK_ATT)

        w_r = jnp.concatenate(
            [w_re[i], w_rg[i], jnp.zeros((d, LANES - N_EXPERTS - N_GROUPS), F32)], axis=1).astype(BF16)
        b_r = jnp.concatenate(
            [b_re[i], b_rg[i], jnp.zeros((LANES - N_EXPERTS - N_GROUPS,), F32)])[None, :]
        g_ffn = ffn_norm_g[i][None, :]
        xe, counts = _mix(x, y_att, attn_norm_g[i][None, :], w_cg, conv_w[i],
                          w_out_att[i].astype(BF16), w_out_conv[i].astype(BF16),
                          w_o[i].astype(BF16), g_ffn, w_r, b_r, tm=TM_MIX)
        tcnt = counts[:, 0, N_EXPERTS:N_EXPERTS + N_GROUPS].astype(jnp.int32).reshape(-1)

        w_dn = w_down[i].reshape(N_GROUPS, GROUP_WIDTH, d).astype(BF16)
        x = _moe(tcnt, xe.reshape(t, XE_WIDTH), p[i].reshape(t, PLE_DIM), g_ffn,
                 w_gate[i].astype(BF16), w_up[i].astype(BF16), w_dn,
                 ple_norm_g[i][None, :], w_pg[i].astype(BF16), w_ple[i].astype(BF16),
                 tm=TM_MOE).reshape(b, s, d)
    return x
```

```python
import functools

import numpy as np
import jax
import jax.numpy as jnp
from jax import lax
from jax.experimental import pallas as pl
from jax.experimental.pallas import tpu as pltpu

D_MODEL = 1024
N_HEADS = 8
HEAD_DIM = 64
ATT_WIDTH = N_HEADS * HEAD_DIM
CONV_WIDTH = 512
CONV_K = 3
N_GROUPS = 4
EXPERTS_PER_GROUP = 4
N_EXPERTS = 16
D_EXPERT = 256
GROUP_WIDTH = EXPERTS_PER_GROUP * D_EXPERT
PLE_DIM = 256
EPS = 1e-6
NEG_INF = -1e30
LOG2E = 1.4426950408889634

LANES = 128
BF16_ROWS = 16
C_PIECES = 3
QC_LANE = HEAD_DIM
KC_LANE = HEAD_DIM + C_PIECES
V_ROWS = HEAD_DIM + BF16_ROWS
QKV_WIDTH = 3 * ATT_WIDTH + LANES
GID_LANE = N_EXPERTS
ROUTE_PITCH = 32
XE_WIDTH = D_MODEL + LANES
VMEM_LIMIT = 56 * 1024 * 1024

TM_QKV = 512
TQ_ATT = 512
TK_ATT = 256
ATT_HEADS = 2
TM_MIX = 512
TM_MOE = TM_MIX
MOE_SORTED = -(-(TM_MOE + N_GROUPS * BF16_ROWS) // LANES) * LANES
MOE_CHUNK = 144
MOE_ROWS = MOE_SORTED + MOE_CHUNK

F32 = jnp.float32
BF16 = jnp.bfloat16


def _rms(xf, g):
    return xf * lax.rsqrt(jnp.mean(xf * xf, axis=-1, keepdims=True) + EPS) * g


def _log_sigmoid(z):
    return jnp.minimum(z, 0.0) - jnp.log1p(jnp.exp(-jnp.abs(z)))


def _bdot(a, b):
    return jnp.dot(a, b, preferred_element_type=F32)


def _bdot_nt(a, b):
    return lax.dot_general(a, b, (((1,), (1,)), ((), ())), preferred_element_type=F32)


def _bf16_pieces(x):
    hi = x.astype(BF16)
    r1 = x - hi.astype(F32)
    mid = r1.astype(BF16)
    lo = (r1 - mid.astype(F32)).astype(BF16)
    return hi, mid, lo


def _qkv_kernel(x0_ref, xa_ref, xb_ref, g_ref, w_ref, bf_ref, qg_ref, kg_ref, pk_ref,
                qt_ref, k_ref, vt_ref, pa_ref, pb_ref, carry_ref, *, tm, tiles_per_seq):
    step = pl.program_id(0)

    def project(x_ref, dst_ref):
        dst_ref[...] = _bdot_nt(_rms(x_ref[...], g_ref[...]).astype(BF16), w_ref[...])

    @pl.when(step == 0)
    def _():
        carry_ref[...] = jnp.zeros_like(carry_ref)
        project(x0_ref, pa_ref)

    lane = lax.broadcasted_iota(jnp.int32, (tm, LANES), 1)
    row = lax.broadcasted_iota(jnp.int32, (tm, LANES), 0)
    k_ones = jnp.where((lane >= QC_LANE) & (lane < QC_LANE + C_PIECES), 1.0, 0.0)
    low = lane < HEAD_DIM
    ext_row = lax.broadcasted_iota(jnp.int32, (V_ROWS - HEAD_DIM, tm), 0)
    v_ext = jnp.where(ext_row == 0, 1.0, 0.0)

    def finish(p_ref, tile, half):
        rows = slice(half * tm, (half + 1) * tm)
        for j in range(N_HEADS // 2):
            pair_t = p_ref[:, 2 * ATT_WIDTH + LANES * j: 2 * ATT_WIDTH + LANES * (j + 1)].T
            for hh in range(2):
                vt_ref[0, 2 * j + hh, :, rows] = jnp.concatenate(
                    [pair_t[HEAD_DIM * hh: HEAD_DIM * (hh + 1)], v_ext], axis=0).astype(BF16)

        f3 = p_ref[:, 3 * ATT_WIDTH:QKV_WIDTH] + bf_ref[...]
        c = jnp.where(lane < C_PIECES * N_HEADS, _log_sigmoid(f3), 0.0)
        sh = 1
        while sh < tm:
            c = c + jnp.where(row >= sh, pltpu.roll(c, sh, axis=0), 0.0)
            sh *= 2
        c = c + jnp.where(tile % tiles_per_seq == 0, 0.0, carry_ref[...])
        carry_ref[...] = c[tm - 1:tm, :]
        hi, mid, lo = _bf16_pieces(c * LOG2E)
        piece = jnp.where(lane < N_HEADS, hi, jnp.where(lane < 2 * N_HEADS, mid, lo))

        def rms_scales(pair):
            sq = pair * pair
            ss_lo = jnp.sum(jnp.where(low, sq, 0.0), axis=1, keepdims=True)
            ss_hi = jnp.sum(jnp.where(low, 0.0, sq), axis=1, keepdims=True)
            return (lax.rsqrt(ss_lo * (1.0 / HEAD_DIM) + EPS), lax.rsqrt(ss_hi * (1.0 / HEAD_DIM) + EPS))

        piece_t = piece.astype(F32).T
        erow = lax.broadcasted_iota(jnp.int32, (8, tm), 0)
        pad_t = jnp.zeros((LANES - HEAD_DIM - 8, tm), F32)
        qgain = qg_ref[...]
        for j in range(N_HEADS // 2):
            pair = p_ref[:, LANES * j: LANES * (j + 1)]
            r_lo, r_hi = rms_scales(pair)
            qn_t = (pair * jnp.where(low, r_lo, r_hi) * qgain).T
            for hh in range(2):
                h = 2 * j + hh
                c_t = jnp.where(erow >= C_PIECES, jnp.where(erow < 2 * C_PIECES, 1.0, 0.0), 0.0)
                for idx in range(C_PIECES):
                    c_t = jnp.where(erow == idx, piece_t[idx * N_HEADS + h: idx * N_HEADS + h + 1], c_t)
                qt_ref[0, h, :, rows] = jnp.concatenate(
                    [qn_t[HEAD_DIM * hh: HEAD_DIM * (hh + 1)], c_t, pad_t], axis=0).astype(BF16)

        kc = _bdot(piece, pk_ref[...])
        kgain = kg_ref[...]
        for j in range(N_HEADS // 2):
            pair = p_ref[:, ATT_WIDTH + LANES * j: ATT_WIDTH + LANES * (j + 1)]
            r_lo, r_hi = rms_scales(pair)
            n_lo = pair * r_lo * kgain
            n_hi = pltpu.roll(pair, HEAD_DIM, axis=1) * r_hi * kgain
            for h, nrm in ((2 * j, n_lo), (2 * j + 1, n_hi)):
                aug = jnp.where(low, nrm, 0.0) + kc[:, LANES * h: LANES * (h + 1)] + k_ones
                k_ref[0, h, rows, :] = aug.astype(BF16)

    finish(pa_ref, 2 * step, 0)
    project(xa_ref, pb_ref)
    finish(pb_ref, 2 * step + 1, 1)
    project(xb_ref, pa_ref)


def _qkv_proj(x, g, w, bf3, qg, kg, pk, *, tm):
    b, s, d = x.shape
    x2 = x.reshape(b * s, d)
    n_tiles = b * s // tm
    per_seq = s // (2 * tm)
    full = lambda shape: pl.BlockSpec(shape, lambda i: (0,) * len(shape))
    return pl.pallas_call(
        functools.partial(_qkv_kernel, tm=tm, tiles_per_seq=s // tm),
        grid=(n_tiles // 2,),
        in_specs=[
            pl.BlockSpec((tm, d), lambda i: (0, 0)),
            pl.BlockSpec((tm, d), lambda i: (2 * i + 1, 0)),
            pl.BlockSpec((tm, d), lambda i: (jnp.minimum(2 * i + 2, n_tiles - 1), 0)),
            full(g.shape), full(w.shape), full(bf3.shape), full(qg.shape), full(kg.shape),
            full(pk.shape),
        ],
        out_specs=[pl.BlockSpec((1, N_HEADS, LANES, 2 * tm), lambda i: (i // per_seq, 0, 0, i % per_seq)),
                   pl.BlockSpec((1, N_HEADS, 2 * tm, LANES), lambda i: (i // per_seq, 0, i % per_seq, 0)),
                   pl.BlockSpec((1, N_HEADS, V_ROWS, 2 * tm), lambda i: (i // per_seq, 0, 0, i % per_seq))],
        out_shape=[
            jax.ShapeDtypeStruct((b, N_HEADS, LANES, s), BF16),
            jax.ShapeDtypeStruct((b, N_HEADS, s, LANES), BF16),
            jax.ShapeDtypeStruct((b, N_HEADS, V_ROWS, s), BF16),
        ],
        scratch_shapes=[pltpu.VMEM((tm, QKV_WIDTH), F32), pltpu.VMEM((tm, QKV_WIDTH), F32),
                        pltpu.VMEM((1, LANES), F32)],
        compiler_params=pltpu.CompilerParams(
            dimension_semantics=("arbitrary",), vmem_limit_bytes=VMEM_LIMIT),
        name="qkv_proj",
    )(x2, x2, x2, g, w, bf3, qg, kg, pk)


def _attn_kernel(qt_ref, qn_ref, k_ref, vt_ref, bias_ref, o_ref, sa_ref, sb_ref, *, tq, tk, n_q):
    qi = pl.program_id(2)
    n_sub = tq // tk
    bufs = (sa_ref, sb_ref)

    def produce(j, s_ref, lo=0, q_ref=qt_ref):
        start = j * tk
        for hh in range(ATT_HEADS):
            s_ref[hh, :, lo:tq] = _bdot(k_ref[0, hh, pl.ds(start, tk), :], q_ref[0, hh, :, lo:tq])

    def consume(j, s_ref, carry, lo=0, masked=False):
        start = j * tk
        out = []
        for hh in range(ATT_HEADS):
            m, acc = carry[hh]

            def read():
                if not masked:
                    return s_ref[hh, :, lo:tq]
                tri = s_ref[hh, :, lo:lo + tk] + bias_ref[...]
                return tri if lo + tk == tq else jnp.concatenate([tri, s_ref[hh, :, lo + tk:tq]], axis=1)

            m_new = jnp.maximum(m[:, lo:tq], jnp.max(read(), axis=0, keepdims=True))
            alpha = jnp.exp2(m[:, lo:tq] - m_new)
            p = jnp.exp2(read() - m_new).astype(BF16)
            acc_new = alpha * acc[:, lo:tq] + _bdot(vt_ref[0, hh, :, pl.ds(start, tk)], p)
            if lo:
                m_new = jnp.concatenate([m[:, 0:lo], m_new], axis=1)
                acc_new = jnp.concatenate([acc[:, 0:lo], acc_new], axis=1)
            out.append((m_new, acc_new))
        return tuple(out)

    def trip(r, carry):
        for t in range(n_sub):
            j = r * n_sub + t
            produce(j + 1, bufs[(t + 1) % 2])
            carry = consume(j, bufs[t % 2], carry)
        return carry

    def query_tile(q):
        if q == 0:
            produce(0, sa_ref)
        carry = tuple((jnp.full((1, tq), NEG_INF, F32), jnp.zeros((V_ROWS, tq), F32))
                      for _ in range(ATT_HEADS))
        for r in range(q):
            carry = trip(r, carry)
        for t in range(n_sub):
            j = q * n_sub + t
            if t + 1 < n_sub:
                produce(j + 1, bufs[(t + 1) % 2], lo=(t + 1) * tk)
            elif q + 1 < n_q:
                produce(0, bufs[(t + 1) % 2], q_ref=qn_ref)
            carry = consume(j, bufs[t % 2], carry, lo=t * tk, masked=True)
        y_t = jnp.concatenate([acc[0:HEAD_DIM] / acc[HEAD_DIM:HEAD_DIM + 1] for _, acc in carry], axis=0)
        o_ref[0] = y_t.T.astype(BF16)

    for q in range(n_q):
        pl.when(qi == q)(functools.partial(query_tile, q))


def _attention(qt_aug, k_aug, vt, *, tq, tk):
    b, nh, s, _ = k_aug.shape
    assert tq % (2 * tk) == 0 and s % tq == 0
    future = np.arange(tk)[:, None] > np.arange(tk)[None, :]
    bias = jnp.asarray(np.where(future, np.float32(NEG_INF), np.float32(0.0)))
    nh_step = ATT_HEADS
    grid = (b, nh // nh_step, s // tq)
    in_specs = [
        pl.BlockSpec((1, nh_step, LANES, tq), lambda i, h, j: (i, h, 0, j)),
        pl.BlockSpec((1, nh_step, LANES, tq), lambda i, h, j: (i, h, 0, jnp.minimum(j + 1, s // tq - 1))),
        pl.BlockSpec((1, nh_step, s, LANES), lambda i, h, j: (i, h, 0, 0)),
        pl.BlockSpec((1, nh_step, V_ROWS, s), lambda i, h, j: (i, h, 0, 0)),
        pl.BlockSpec((tk, tk), lambda i, h, j: (0, 0)),
    ]
    out_spec = pl.BlockSpec((1, tq, nh_step * HEAD_DIM), lambda i, h, j: (i, j, h))
    step = functools.partial(_attn_kernel, tq=tq, tk=tk, n_q=s // tq)

    def pipelined(*refs):
        hbm, (sa_ref, sb_ref) = refs[:-2], refs[-2:]
        pltpu.emit_pipeline(
            lambda *blocks: step(*blocks, sa_ref, sb_ref),
            grid=grid, in_specs=in_specs, out_specs=[out_spec])(*hbm)

    return pl.pallas_call(
        pipelined,
        in_specs=[pl.BlockSpec(memory_space=pl.ANY)] * len(in_specs),
        out_specs=pl.BlockSpec(memory_space=pl.ANY),
        out_shape=jax.ShapeDtypeStruct((b, s, ATT_WIDTH), BF16),
        scratch_shapes=[pltpu.VMEM((nh_step, tk, tq), F32)] * 2,
        compiler_params=pltpu.CompilerParams(vmem_limit_bytes=VMEM_LIMIT),
        name="fox_attention",
    )(qt_aug, qt_aug, k_aug, vt, bias)


def _mix_kernel(x0_ref, xa_ref, xb_ref, y0_ref, ya_ref, yb_ref, g_ref, wcg_ref, cw_ref, wa_ref, wb_ref,
                wo_ref, g2_ref, wr_ref, br_ref, xe_ref, cnt_ref, ra_ref, rb_ref, carry_ref, wcs_ref,
                *, tm, tiles_per_seq):
    step = pl.program_id(0)

    def residual(x_ref, y_ref, tile, dst_ref):
        x = x_ref[...]
        h = _rms(x, g_ref[...]).astype(BF16)
        pc = _bdot(h, wcs_ref[...])
        cw = CONV_WIDTH
        cb = pc[:, 0:cw]
        prod = pc[:, cw:2 * cw] * pc[:, 2 * cw:3 * cw]
        ga = pc[:, 3 * cw:3 * cw + D_MODEL]
        gb = pc[:, 3 * cw + D_MODEL:3 * cw + 2 * D_MODEL]

        prev = jnp.where(tile % tiles_per_seq == 0, 0.0, carry_ref[...])
        crow = lax.broadcasted_iota(jnp.int32, (tm, cw), 0)
        m1 = jnp.where(crow == 0, prev[7:8, :], pltpu.roll(prod, 1, axis=0))
        m2 = jnp.where(crow == 0, prev[6:7, :],
                       jnp.where(crow == 1, prev[7:8, :], pltpu.roll(prod, 2, axis=0)))
        carry_ref[...] = prod[tm - 8:tm, :]
        w = cw_ref[...]
        y_conv = cb * (w[0:1, :] * m2 + w[1:2, :] * m1 + w[2:3, :] * prod)

        a = _bdot(y_ref[...], wa_ref[...])
        bb = _bdot(y_conv.astype(BF16), wb_ref[...])
        merged = jax.nn.sigmoid(ga) * a + jax.nn.sigmoid(gb) * bb
        dst_ref[...] = x + _bdot(merged.astype(BF16), wo_ref[...])

    @pl.when(step == 0)
    def _():
        carry_ref[...] = jnp.zeros_like(carry_ref)
        for r in range(0, wcs_ref.shape[1], tm):
            wcs_ref[:, r:r + tm] = wcg_ref[r:r + tm, :].astype(F32).T.astype(BF16)
        residual(x0_ref, y0_ref, 0, ra_ref)

    _route(ra_ref, g2_ref, wr_ref, br_ref, xe_ref, cnt_ref, 0, tm)
    residual(xa_ref, ya_ref, 2 * step + 1, rb_ref)
    _route(rb_ref, g2_ref, wr_ref, br_ref, xe_ref, cnt_ref, 1, tm)
    residual(xb_ref, yb_ref, 2 * step + 2, ra_ref)


def _route(x1_ref, g2_ref, wr_ref, br_ref, xe_ref, cnt_ref, half, tm):
    rows = slice(half * tm, (half + 1) * tm)
    x1 = x1_ref[...]
    xe_ref[rows, 0:D_MODEL] = x1
    h2 = _rms(x1, g2_ref[...]).astype(BF16)

    logits = _bdot(h2, wr_ref[...]) + br_ref[...]
    lane = lax.broadcasted_iota(jnp.int32, (tm, LANES), 1)
    lanef = lane.astype(F32)
    is_g = (lane >= N_EXPERTS) & (lane < N_EXPERTS + N_GROUPS)
    gl = jnp.where(is_g, logits, NEG_INF)
    gmax = jnp.max(gl, axis=1, keepdims=True)
    gsum = jnp.sum(jnp.exp(gl - gmax), axis=1, keepdims=True)
    g_val = 1.0 / gsum
    g_lane = jnp.min(jnp.where(gl == gmax, lanef, float(LANES)), axis=1, keepdims=True)
    e_lo = (g_lane - float(N_EXPERTS)) * float(EXPERTS_PER_GROUP)
    in_grp = (lanef >= e_lo) & (lanef < e_lo + float(EXPERTS_PER_GROUP))
    el = jnp.where(in_grp, logits, NEG_INF)
    t1 = jnp.max(el, axis=1, keepdims=True)
    i1 = jnp.min(jnp.where(el == t1, lanef, float(LANES)), axis=1, keepdims=True)
    el2 = jnp.where(lanef == i1, NEG_INF, el)
    t2 = jnp.max(el2, axis=1, keepdims=True)
    i2 = jnp.min(jnp.where(el2 == t2, lanef, float(LANES)), axis=1, keepdims=True)
    e2 = jnp.exp(t2 - t1)
    w1 = g_val / (1.0 + e2)
    w2 = g_val * e2 / (1.0 + e2)
    comb = jnp.where(lanef == i1, w1, 0.0) + jnp.where(lanef == i2, w2, 0.0)

    cnt_ref[half] = jnp.sum(jnp.where(is_g & (lanef == g_lane), 1.0, 0.0), axis=0, keepdims=True)
    xe_ref[rows, D_MODEL:XE_WIDTH] = comb + jnp.where(lane == GID_LANE, g_lane - float(N_EXPERTS), 0.0)


def _mix(x, y_att, g, wcg, conv_w, wa, wb, wo, g2, wr, br, *, tm):
    b, s, d = x.shape
    t = b * s
    n_tiles = t // tm
    x2 = x.reshape(t, d)
    y2 = y_att.reshape(t, ATT_WIDTH)
    once = lambda a: pl.BlockSpec(a.shape, lambda i: (0,) * a.ndim, pipeline_mode=pl.Buffered(1))

    def tiles(width):
        nxt = lambda i: (jnp.minimum(2 * i + 2, n_tiles - 1), 0)
        return [pl.BlockSpec((tm, width), lambda i: (0, 0), pipeline_mode=pl.Buffered(1)),
                pl.BlockSpec((tm, width), lambda i: (2 * i + 1, 0)),
                pl.BlockSpec((tm, width), nxt)]

    return pl.pallas_call(
        functools.partial(_mix_kernel, tm=tm, tiles_per_seq=s // tm),
        grid=(n_tiles // 2,),
        in_specs=tiles(d) + tiles(ATT_WIDTH) + [once(g), once(wcg), once(conv_w), once(wa), once(wb),
                                                once(wo), once(g2), once(wr), once(br)],
        out_specs=[pl.BlockSpec((2 * tm, XE_WIDTH), lambda i: (i, 0)),
                   pl.BlockSpec((2, 1, LANES), lambda i: (i, 0, 0))],
        out_shape=[
            jax.ShapeDtypeStruct((t, XE_WIDTH), F32),
            jax.ShapeDtypeStruct((n_tiles, 1, LANES), F32),
        ],
        scratch_shapes=[pltpu.VMEM((tm, d), F32), pltpu.VMEM((tm, d), F32),
                        pltpu.VMEM((8, CONV_WIDTH), F32),
                        pltpu.VMEM(wcg.shape[::-1], BF16)],
        compiler_params=pltpu.CompilerParams(
            dimension_semantics=("arbitrary",), vmem_limit_bytes=VMEM_LIMIT),
        name="mix",
    )(x2, x2, x2, y2, y2, y2, g, wcg, conv_w, wa, wb, wo, g2, wr, br)


def _moe_kernel(tcnt_ref, xe_ref, p_ref, g2_ref, wg_ref, wu_ref, wd_ref, g3_ref, wpg_ref, wple_ref,
                earlier_ref, o_ref, hs_ref, rs_ref, ys_ref, *, tm):
    i = pl.program_id(0)
    n = [tcnt_ref[i * N_GROUPS + k] for k in range(N_GROUPS)]
    starts = [jnp.int32(0)]
    for k in range(N_GROUPS - 1):
        starts.append(starts[-1] + pl.cdiv(n[k], BF16_ROWS) * BF16_ROWS)

    x1 = xe_ref[:, 0:D_MODEL]
    route = xe_ref[:, D_MODEL:XE_WIDTH]
    h2 = _rms(x1, g2_ref[...]).astype(BF16)

    lane = lax.broadcasted_iota(jnp.int32, (tm, LANES), 1)
    lanef = lane.astype(F32)
    gid = jnp.sum(jnp.where(lane == GID_LANE, route, 0.0), axis=1, keepdims=True)
    onehot = jnp.where((lanef == gid) & (lane < N_GROUPS), 1.0, 0.0)
    rank = jnp.sum(onehot * _bdot(earlier_ref[...], onehot.astype(BF16)), axis=1, keepdims=True)
    base = starts[N_GROUPS - 1].astype(F32)
    for k in range(N_GROUPS - 2, -1, -1):
        base = jnp.where(gid == float(k), starts[k].astype(F32), base)
    pos = base + rank

    unsort = (pos == lax.broadcasted_iota(jnp.int32, (tm, MOE_SORTED), 1).astype(F32)).astype(BF16)
    digit_hi = jnp.floor(pos * (1.0 / 32.0))
    digits = jnp.where(lane == 0, digit_hi, jnp.where(lane == 1, pos - 32.0 * digit_hi, 0.0))
    sel_lane = lax.broadcasted_iota(jnp.int32, (BF16_ROWS, LANES), 1)
    sel = jnp.where(sel_lane == 0, 32.0, jnp.where(sel_lane == 1, 1.0, 0.0)).astype(BF16)
    pos_row = _bdot_nt(sel, digits.astype(BF16))[0:1, :]
    sort = (lax.broadcasted_iota(jnp.int32, (MOE_SORTED, tm), 0).astype(F32) == pos_row).astype(BF16)

    hs_ref[0:MOE_SORTED] = _bdot(sort, h2).astype(BF16)
    r_hi, r_mid, r_lo = _bf16_pieces(route)
    packed = (r_hi.astype(F32) + pltpu.roll(r_mid.astype(F32), ROUTE_PITCH, axis=1)
              + pltpu.roll(r_lo.astype(F32), 2 * ROUTE_PITCH, axis=1)).astype(BF16)
    rsorted = _bdot(sort, packed)
    rs_ref[0:MOE_SORTED] = (rsorted + pltpu.roll(rsorted, LANES - ROUTE_PITCH, axis=1)
                            + pltpu.roll(rsorted, LANES - 2 * ROUTE_PITCH, axis=1))
    hs_ref[MOE_SORTED:MOE_ROWS] = jnp.zeros((MOE_CHUNK, D_MODEL), BF16)
    rs_ref[MOE_SORTED:MOE_ROWS] = jnp.zeros((MOE_CHUNK, LANES), F32)
    ys_ref[...] = jnp.zeros_like(ys_ref)

    clane = lax.broadcasted_iota(jnp.int32, (MOE_CHUNK, LANES), 1)

    def chunk(g, c):
        r0 = pl.multiple_of(starts[g] + c * MOE_CHUNK, BF16_ROWS)
        hrows = hs_ref[pl.ds(r0, MOE_CHUNK), :]
        rt = rs_ref[pl.ds(r0, MOE_CHUNK), :]
        parts = []
        for e in range(EXPERTS_PER_GROUP):
            ex = g * EXPERTS_PER_GROUP + e
            a = _bdot(hrows, wg_ref[ex])
            u = _bdot(hrows, wu_ref[ex])
            ce = jnp.sum(jnp.where(clane == ex, rt, 0.0), axis=1, keepdims=True)
            parts.append(((a * jax.nn.sigmoid(a)) * u * ce).astype(BF16))
        out = _bdot(jnp.concatenate(parts, axis=1), wd_ref[g])
        ys_ref[pl.ds(r0, MOE_CHUNK), :] = (ys_ref[pl.ds(r0, MOE_CHUNK), :].astype(F32) + out).astype(BF16)

    for g in range(N_GROUPS):
        def more(c, _, g=g):
            chunk(g, c)
            return 0
        lax.fori_loop(1, pl.cdiv(n[g], MOE_CHUNK), more, 0)
    for g in range(N_GROUPS):
        chunk(g, 0)

    x2 = x1 + _bdot(unsort, ys_ref[0:MOE_SORTED])
    h3 = _rms(x2, g3_ref[...]).astype(BF16)
    gate = jax.nn.sigmoid(_bdot(h3, wpg_ref[...]))
    emb = _bdot(p_ref[...].astype(BF16), wple_ref[...])
    o_ref[...] = x2 + gate * emb


def _moe(tcnt, xe, p, g2, wg, wu, wd, g3, wpg, wple, *, tm):
    t = xe.shape[0]
    d = D_MODEL
    once = lambda a: pl.BlockSpec(a.shape, lambda i, c: (0,) * a.ndim, pipeline_mode=pl.Buffered(1))
    row = lambda width: pl.BlockSpec((tm, width), lambda i, c: (i, 0))
    earlier = jnp.asarray(np.tril(np.ones((tm, tm), np.float32), -1), BF16)
    return pl.pallas_call(
        functools.partial(_moe_kernel, tm=tm),
        grid_spec=pltpu.PrefetchScalarGridSpec(
            num_scalar_prefetch=1,
            grid=(t // tm,),
            in_specs=[row(XE_WIDTH), row(PLE_DIM), once(g2), once(wg), once(wu), once(wd), once(g3),
                      once(wpg), once(wple), once(earlier)],
            out_specs=row(d),
            scratch_shapes=[pltpu.VMEM((MOE_ROWS, d), BF16), pltpu.VMEM((MOE_ROWS, LANES), F32),
                            pltpu.VMEM((MOE_ROWS, d), BF16)],
        ),
        out_shape=jax.ShapeDtypeStruct((t, d), F32),
        compiler_params=pltpu.CompilerParams(
            dimension_semantics=("arbitrary",), vmem_limit_bytes=VMEM_LIMIT),
        name="moe",
    )(tcnt, xe, p, g2, wg, wu, wd, g3, wpg, wple, earlier)


def _k_select_matrix():
    pk = np.zeros((LANES, N_HEADS * LANES), np.float32)
    for idx in range(C_PIECES):
        for h in range(N_HEADS):
            pk[idx * N_HEADS + h, h * LANES + KC_LANE + idx] = -1.0
    return jnp.asarray(pk, BF16)


def kernel(x, p, attn_norm_g, w_in, b_f, q_norm_g, k_norm_g, conv_w, w_out_att, w_out_conv, w_o,
           ffn_norm_g, w_rg, b_rg, w_re, b_re, w_gate, w_up, w_down, ple_norm_g, w_pg, w_ple):
    b, s, d = x.shape
    assert d == D_MODEL and conv_w.shape[1:] == (CONV_K, CONV_WIDTH) and s % (2 * TM_MIX) == 0
    t = b * s
    aw = ATT_WIDTH
    for i in range(w_in.shape[0]):
        wt = w_in[i].T
        wf = wt[3 * aw:3 * aw + N_HEADS]
        w_qkvf = jnp.concatenate(
            [wt[:3 * aw], wf, wf, wf, jnp.zeros((LANES - C_PIECES * N_HEADS, d), F32)],
            axis=0).astype(BF16)
        w_cg = wt[3 * aw + N_HEADS:].astype(BF16)
        bf3 = jnp.concatenate([b_f[i]] * C_PIECES + [jnp.zeros((LANES - C_PIECES * N_HEADS,), F32)])[None, :]
        scale = HEAD_DIM ** -0.5 * LOG2E
        qg = jnp.tile(q_norm_g[i] * scale, 2)[None, :]
        kg = jnp.tile(k_norm_g[i], 2)[None, :]
        qt_aug, k_aug, vt = _qkv_proj(x, attn_norm_g[i][None, :], w_qkvf, bf3, qg, kg, _k_select_matrix(),
                                      tm=TM_QKV)
        y_att = _attention(qt_aug, k_aug, vt, tq=TQ_ATT, tk=TK_ATT)

        w_r = jnp.concatenate(
            [w_re[i], w_rg[i], jnp.zeros((d, LANES - N_EXPERTS - N_GROUPS), F32)], axis=1).astype(BF16)
        b_r = jnp.concatenate(
            [b_re[i], b_rg[i], jnp.zeros((LANES - N_EXPERTS - N_GROUPS,), F32)])[None, :]
        g_ffn = ffn_norm_g[i][None, :]
        xe, counts = _mix(x, y_att, attn_norm_g[i][None, :], w_cg, conv_w[i],
                          w_out_att[i].astype(BF16), w_out_conv[i].astype(BF16),
                          w_o[i].astype(BF16), g_ffn, w_r, b_r, tm=TM_MIX)
        tcnt = counts[:, 0, N_EXPERTS:N_EXPERTS + N_GROUPS].astype(jnp.int32).reshape(-1)

        w_dn = w_down[i].reshape(N_GROUPS, GROUP_WIDTH, d).astype(BF16)
        x = _moe(tcnt, xe.reshape(t, XE_WIDTH), p[i].reshape(t, PLE_DIM), g_ffn,
                 w_gate[i].astype(BF16), w_up[i].astype(BF16), w_dn,
                 ple_norm_g[i][None, :], w_pg[i].astype(BF16), w_ple[i].astype(BF16),
                 tm=TM_MOE).reshape(b, s, d)
    return x
```
